```python
import jax
import jax.numpy as jnp
from jax import lax
import numpy as np

D_MODEL = 1024
BATCH = 1
SEQ = 16384
DEPTH = 2

GRID_W = 64
CTX_LEN = 256
N_MIXERS = 2
N_ATTN_LAYERS = (DEPTH + 1) // 2
N_HGRN_LAYERS = DEPTH // 2
RMS_EPS = 1e-6

ATTN_HEADS = 16
ATTN_KV_HEADS = 2
ATTN_GROUP = ATTN_HEADS // ATTN_KV_HEADS
HEAD_DIM = 64
Q_DIM = ATTN_HEADS * HEAD_DIM
KV_DIM = ATTN_KV_HEADS * HEAD_DIM
QKV_DIM = Q_DIM + 2 * KV_DIM
WINDOW = 128
ATTN_BLOCK = 128
ROPE_BASE = 10000.0

HGRN_HEADS = 8
HGRN_DK = D_MODEL // HGRN_HEADS
HGRN_DV = D_MODEL // HGRN_HEADS
HGRN_FDIM = HGRN_HEADS * HGRN_DK
HGRN_IDIM = HGRN_HEADS * HGRN_DV
HGRN_IN_DIM = 3 * HGRN_FDIM + 2 * HGRN_IDIM
HGRN_CHUNK = 64

N_EXPERTS = 32
TOP_K = 4
D_FF = D_MODEL
SWIGLU_LIMIT = 7.0
SWIGLU_ALPHA = 1.702
MOE_BLOCK = 128

kernel_name = 'hybrid_swa_hgrn2_moe_flow_block'


def rmsnorm(x, g):
    xf = x.astype(jnp.float32)
    y = xf * lax.rsqrt(jnp.mean(xf * xf, axis=-1, keepdims=True) + RMS_EPS)
    return (y * g.astype(jnp.float32)).astype(x.dtype)


def modulate(h, shift, scale):
    return h * (1 + scale) + shift


def axial_angles(rows):
    n = HEAD_DIM // 4
    row_pos = jnp.repeat(jnp.arange(rows, dtype=jnp.float32), GRID_W)
    col_pos = jnp.broadcast_to(jnp.arange(GRID_W, dtype=jnp.float32)[None, :], (rows, GRID_W)).reshape(-1)
    inv_freq = ROPE_BASE ** (-jnp.arange(n, dtype=jnp.float32) / n)
    return row_pos[:, None] * inv_freq[None, :], col_pos[:, None] * inv_freq[None, :]


def _rotate(x, ang):
    x1, x2 = jnp.split(x, 2, axis=-1)
    cos = jnp.cos(ang)[None, :, None, :]
    sin = jnp.sin(ang)[None, :, None, :]
    return jnp.concatenate([x1 * cos - x2 * sin, x2 * cos + x1 * sin], axis=-1)


def axial_rope(x, ang_row, ang_col):
    xf = x.astype(jnp.float32)
    half = x.shape[-1] // 2
    out = jnp.concatenate([_rotate(xf[..., :half], ang_row), _rotate(xf[..., half:], ang_col)], axis=-1)
    return out.astype(x.dtype)


def attn_mixer(h_lat, h_ctx, ang_row, ang_col, w_qkv, b_qkv, sink, w_o, need_ctx_out):
    B, L, _ = h_lat.shape
    C = h_ctx.shape[1]
    dt = h_lat.dtype
    scale = HEAD_DIM ** -0.5
    qkv = h_lat @ w_qkv + b_qkv
    q = qkv[..., :Q_DIM].reshape(B, L, ATTN_HEADS, HEAD_DIM)
    k = qkv[..., Q_DIM:Q_DIM + KV_DIM].reshape(B, L, ATTN_KV_HEADS, HEAD_DIM)
    v = qkv[..., Q_DIM + KV_DIM:].reshape(B, L, ATTN_KV_HEADS, HEAD_DIM)
    q = axial_rope(q, ang_row, ang_col).reshape(B, L, ATTN_KV_HEADS, ATTN_GROUP, HEAD_DIM)
    k = axial_rope(k, ang_row, ang_col)
    if need_ctx_out:
        qkv_c = h_ctx @ w_qkv + b_qkv
        q_c = qkv_c[..., :Q_DIM].reshape(B, C, ATTN_KV_HEADS, ATTN_GROUP, HEAD_DIM)
        kv_c = qkv_c[..., Q_DIM:]
    else:
        kv_c = h_ctx @ w_qkv[:, Q_DIM:] + b_qkv[Q_DIM:]
    k_c = kv_c[..., :KV_DIM].reshape(B, C, ATTN_KV_HEADS, HEAD_DIM)
    v_c = kv_c[..., KV_DIM:].reshape(B, C, ATTN_KV_HEADS, HEAD_DIM)
    sink_kg = sink.astype(jnp.float32).reshape(ATTN_KV_HEADS, ATTN_GROUP)

    def softmax_with_sink(logits):
        s = jnp.broadcast_to(sink_kg[None, :, :, None, None], logits.shape[:-1] + (1,))
        return jax.nn.softmax(jnp.concatenate([logits, s], axis=-1), axis=-1)[..., :-1].astype(dt)

    NB = L // ATTN_BLOCK
    NW = 3 * ATTN_BLOCK

    def window_blocks(t):
        tp = jnp.pad(t, ((0, 0), (ATTN_BLOCK, ATTN_BLOCK), (0, 0), (0, 0)))
        tp = tp.reshape(B, NB + 2, ATTN_BLOCK, ATTN_KV_HEADS, HEAD_DIM)
        w = jnp.concatenate([tp[:, :-2], tp[:, 1:-1], tp[:, 2:]], axis=2)
        return jnp.moveaxis(w, 1, 0)

    q_blk = jnp.moveaxis(q.reshape(B, NB, ATTN_BLOCK, ATTN_KV_HEADS, ATTN_GROUP, HEAD_DIM), 1, 0)
    k_win = window_blocks(k)
    v_win = window_blocks(v)
    offs_q = jnp.arange(ATTN_BLOCK)
    offs_k = jnp.arange(NW) - ATTN_BLOCK

    def block(args):
        n, qb, kb, vb = args
        q_pos = n * ATTN_BLOCK + offs_q
        k_pos = n * ATTN_BLOCK + offs_k
        valid = (jnp.abs(q_pos[:, None] - k_pos[None, :]) <= WINDOW) & ((k_pos >= 0) & (k_pos < L))[None, :]
        s_win = jnp.einsum('bqkgd,bskd->bkgqs', qb, kb).astype(jnp.float32) * scale
        s_win = jnp.where(valid, s_win, -jnp.inf)
        s_ctx = jnp.einsum('bqkgd,bskd->bkgqs', qb, k_c).astype(jnp.float32) * scale
        p = softmax_with_sink(jnp.concatenate([s_win, s_ctx], axis=-1))
        o = (jnp.einsum('bkgqs,bskd->bqkgd', p[..., :NW], vb)
             + jnp.einsum('bkgqs,bskd->bqkgd', p[..., NW:], v_c))
        return o.reshape(B, ATTN_BLOCK, Q_DIM)

    o = lax.map(block, (jnp.arange(NB), q_blk, k_win, v_win))
    y_lat = jnp.moveaxis(o, 0, 1).reshape(B, L, Q_DIM) @ w_o
    y_ctx = None
    if need_ctx_out:
        s = jnp.einsum('bqkgd,bskd->bkgqs', q_c, k_c).astype(jnp.float32) * scale
        p = softmax_with_sink(s)
        o_c = jnp.einsum('bkgqs,bskd->bqkgd', p, v_c).reshape(B, C, Q_DIM)
        y_ctx = o_c @ w_o
    return y_lat, y_ctx


def hgrn_chunk_scan(k, v, log_f, s0, q=None):
    B, T, H, DK = k.shape
    DV = v.shape[-1]
    NC = T // HGRN_CHUNK
    with_outputs = q is not None

    def to_chunks(t):
        return t.astype(jnp.float32).reshape(B, NC, HGRN_CHUNK, H, t.shape[-1]).transpose(1, 0, 3, 2, 4)

    lower = jnp.tril(jnp.ones((HGRN_CHUNK, HGRN_CHUNK), dtype=bool))
    xs = (to_chunks(k), to_chunks(v), to_chunks(log_f)) + ((to_chunks(q),) if with_outputs else ())

    def step(S, inp):
        kc, vc, gc = inp[0], inp[1], inp[2]
        G = jnp.cumsum(gc, axis=2)
        G_last = G[:, :, -1]
        k_end = kc * jnp.exp(G_last[:, :, None, :] - G)
        S_new = jnp.exp(G_last)[..., None] * S + jnp.einsum('bhsd,bhsv->bhdv', k_end, vc)
        if not with_outputs:
            return S_new, None
        qc = inp[3]
        o_inter = jnp.einsum('bhtd,bhdv->bhtv', qc * jnp.exp(G), S)
        decay = jnp.exp(jnp.where(lower[:, :, None], G[:, :, :, None, :] - G[:, :, None, :, :], -jnp.inf))
        A = jnp.einsum('bhtd,bhsd,bhtsd->bhts', qc, kc, decay)
        return S_new, o_inter + jnp.einsum('bhts,bhsv->bhtv', A, vc)

    S_fin, o = lax.scan(step, s0, xs)
    if with_outputs:
        o = o.transpose(1, 0, 3, 2, 4).reshape(B, T, H, DV)
    return o, S_fin


def hgrn_mixer(h_lat, h_ctx, layer_idx, w_in, lb_param, norm_g, w_o, need_ctx_out):
    B, L, _ = h_lat.shape
    C = h_ctx.shape[1]
    dt = h_lat.dtype
    F, I, H = HGRN_FDIM, HGRN_IDIM, HGRN_HEADS
    lb_soft = jax.nn.softmax(lb_param.astype(jnp.float32), axis=0)
    lb = jnp.cumsum(lb_soft, axis=0)[layer_idx] - lb_soft[0]

    def forget(z, lb_d, T):
        zf = z.astype(jnp.float32)
        f = lb_d + (1 - lb_d) * jax.nn.sigmoid(zf)
        k = (1 - lb_d) * jax.nn.sigmoid(-zf)
        return k.reshape(B, T, H, HGRN_DK), jnp.log(f).reshape(B, T, H, HGRN_DK)

    q, zf, zb, i, g = jnp.split(h_lat @ w_in, [F, 2 * F, 3 * F, 3 * F + I], axis=-1)
    q = jax.nn.silu(q).reshape(B, L, H, HGRN_DK)
    i = i.reshape(B, L, H, HGRN_DV)
    kf, lf = forget(zf, lb[0], L)
    kb, lbw = forget(zb, lb[1], L)
    if need_ctx_out:
        q_c, zf_c, zb_c, i_c, g_c = jnp.split(h_ctx @ w_in, [F, 2 * F, 3 * F, 3 * F + I], axis=-1)
        q_c = jax.nn.silu(q_c).reshape(B, C, H, HGRN_DK)
    else:
        zf_c, zb_c, i_c = jnp.split(h_ctx @ w_in[:, F:3 * F + I], [F, 2 * F], axis=-1)
        q_c = None
    i_c = i_c.reshape(B, C, H, HGRN_DV)
    kf_c, lf_c = forget(zf_c, lb[0], C)
    kb_c, lb_c = forget(zb_c, lb[1], C)

    flip = lambda t: jnp.flip(t, axis=1)
    s0 = jnp.zeros((B, H, HGRN_DK, HGRN_DV), jnp.float32)
    o_cf, s_cf = hgrn_chunk_scan(kf_c, i_c, lf_c, s0, q_c)
    o_cb, s_cb = hgrn_chunk_scan(flip(kb_c), flip(i_c), flip(lb_c), s0, None if q_c is None else flip(q_c))
    o_lf, _ = hgrn_chunk_scan(kf, i, lf, s_cf, q)
    o_lb, _ = hgrn_chunk_scan(flip(kb), flip(i), flip(lbw), s_cb, flip(q))

    def readout(o, gate, T):
        o = rmsnorm(o, norm_g.reshape(H, HGRN_DV)).astype(dt).reshape(B, T, I)
        return (o * jax.nn.silu(gate)) @ w_o

    y_lat = readout(o_lf + flip(o_lb), g, L)
    y_ctx = readout(o_cf + flip(o_cb), g_c, C) if need_ctx_out else None
    return y_lat, y_ctx


def moe(t, router_w, router_b, w_gu, b_gu, w_dn, b_dn):
    T, D = t.shape
    A = T * TOP_K
    logits = (t @ router_w + router_b).astype(jnp.float32)
    top_logit, top_e = lax.top_k(logits, TOP_K)
    gate = jax.nn.softmax(top_logit, axis=-1).astype(t.dtype)
    e_flat = top_e.reshape(-1)
    order = jnp.argsort(e_flat, stable=True)
    e_s = e_flat[order]
    tok_s = order // TOP_K
    gate_s = gate.reshape(-1)[order]
    counts = jnp.bincount(e_flat, length=N_EXPERTS)
    starts = jnp.cumsum(counts) - counts
    padded = (counts + MOE_BLOCK - 1) // MOE_BLOCK * MOE_BLOCK
    p_end = jnp.cumsum(padded)
    p_start = p_end - padded
    dest = p_start[e_s] + jnp.arange(A) - starts[e_s]
    n_blk = (A + N_EXPERTS * (MOE_BLOCK - 1) + MOE_BLOCK - 1) // MOE_BLOCK
    buf = jnp.zeros((n_blk * MOE_BLOCK, D), t.dtype).at[dest].set(t[tok_s])
    blk_e = jnp.minimum(jnp.searchsorted(p_end, jnp.arange(n_blk) * MOE_BLOCK, side='right'), N_EXPERTS - 1)

    def expert_block(args):
        xb, e = args
        gu = xb @ w_gu[e] + b_gu[e]
        g_, u_ = jnp.split(gu, 2, axis=-1)
        g_ = jnp.minimum(g_, SWIGLU_LIMIT)
        u_ = jnp.clip(u_, -SWIGLU_LIMIT, SWIGLU_LIMIT)
        hdn = g_ * jax.nn.sigmoid(SWIGLU_ALPHA * g_) * (u_ + 1)
        return hdn @ w_dn[e] + b_dn[e]

    out_buf = lax.map(expert_block, (buf.reshape(n_blk, MOE_BLOCK, D), blk_e)).reshape(-1, D)
    return jnp.zeros((T, D), t.dtype).at[tok_s].add(out_buf[dest] * gate_s[:, None])


def setup_inputs(seed: int = 0):
    key = jax.random.key(seed)
    ks = jax.random.split(key, 21)
    D = D_MODEL

    def nrm(k, shape, s):
        return jax.random.normal(k, shape, jnp.float32) * s

    return {
        'x': nrm(ks[0], (BATCH, SEQ, D), 1.0),
        'c': nrm(ks[1], (BATCH, D), 1.0),
        'ctx': nrm(ks[2], (BATCH, CTX_LEN, D), 1.0),
        'c_ctx': nrm(ks[3], (D,), 1.0),
        'ada_w': nrm(ks[4], (DEPTH, D, 6 * D), 0.5 * D ** -0.5),
        'ada_b': nrm(ks[5], (DEPTH, 6 * D), 0.02),
        'norm_g': 1.0 + nrm(ks[6], (DEPTH, 4, D), 0.05),
        'attn_w_qkv': nrm(ks[7], (N_ATTN_LAYERS, D, QKV_DIM), D ** -0.5),
        'attn_b_qkv': nrm(ks[8], (N_ATTN_LAYERS, QKV_DIM), 0.02),
        'attn_sink': nrm(ks[9], (N_ATTN_LAYERS, ATTN_HEADS), 0.5),
        'attn_w_o': nrm(ks[10], (N_ATTN_LAYERS, Q_DIM, D), Q_DIM ** -0.5),
        'hgrn_w_in': nrm(ks[11], (N_HGRN_LAYERS, D, HGRN_IN_DIM), D ** -0.5),
        'hgrn_lb': nrm(ks[12], (DEPTH, 2, HGRN_FDIM), 0.5),
        'hgrn_norm_g': 1.0 + nrm(ks[13], (N_HGRN_LAYERS, HGRN_IDIM), 0.05),
        'hgrn_w_o': nrm(ks[14], (N_HGRN_LAYERS, HGRN_IDIM, D), HGRN_IDIM ** -0.5),
        'router_w': nrm(ks[15], (DEPTH, D, N_EXPERTS), D ** -0.5),
        'router_b': nrm(ks[16], (DEPTH, N_EXPERTS), 0.01),
        'moe_w_gu': nrm(ks[17], (DEPTH, N_EXPERTS, D, 2 * D_FF), D ** -0.5),
        'moe_b_gu': nrm(ks[18], (DEPTH, N_EXPERTS, 2 * D_FF), 0.02),
        'moe_w_dn': nrm(ks[19], (DEPTH, N_EXPERTS, D_FF, D), D_FF ** -0.5),
        'moe_b_dn': nrm(ks[20], (DEPTH, N_EXPERTS, D), 0.02),
    }


def reference(x, c, ctx, c_ctx, ada_w, ada_b, norm_g, attn_w_qkv, attn_b_qkv, attn_sink, attn_w_o,
              hgrn_w_in, hgrn_lb, hgrn_norm_g, hgrn_w_o, router_w, router_b, moe_w_gu, moe_b_gu,
              moe_w_dn, moe_b_dn):
    B, L, D = x.shape
    C = ctx.shape[1]
    ROWS = L // GRID_W
    ang_row, ang_col = axial_angles(ROWS)
    x_ctx = ctx
    silu_c = jax.nn.silu(c)
    silu_cc = jax.nn.silu(c_ctx)
    for i in range(DEPTH):
        need_ctx = i < DEPTH - 1
        sh1, sc1, g1, sh2, sc2, g2 = [m[:, None, :] for m in jnp.split(silu_c @ ada_w[i] + ada_b[i], 6, axis=-1)]
        sh1c, sc1c, g1c, sh2c, sc2c, g2c = jnp.split(silu_cc @ ada_w[i] + ada_b[i], 6, axis=-1)
        h = modulate(rmsnorm(x, norm_g[i, 0]), sh1, sc1)
        hc = modulate(rmsnorm(x_ctx, norm_g[i, 0]), sh1c, sc1c)
        j = i // N_MIXERS
        if i % N_MIXERS == 0:
            y, yc = attn_mixer(h, hc, ang_row, ang_col, attn_w_qkv[j], attn_b_qkv[j], attn_sink[j],
                               attn_w_o[j], need_ctx)
        else:
            y, yc = hgrn_mixer(h, hc, i, hgrn_w_in[j], hgrn_lb, hgrn_norm_g[j], hgrn_w_o[j], need_ctx)
        x = x + g1 * rmsnorm(y, norm_g[i, 1])
        h2 = modulate(rmsnorm(x, norm_g[i, 2]), sh2, sc2)
        if need_ctx:
            x_ctx = x_ctx + g1c * rmsnorm(yc, norm_g[i, 1])
            h2c = modulate(rmsnorm(x_ctx, norm_g[i, 2]), sh2c, sc2c)
            tokens = jnp.concatenate([h2c, h2], axis=1).reshape(-1, D)
            y2 = moe(tokens, router_w[i], router_b[i], moe_w_gu[i], moe_b_gu[i], moe_w_dn[i], moe_b_dn[i])
            y2 = y2.reshape(B, C + L, D)
            x_ctx = x_ctx + g2c * rmsnorm(y2[:, :C], norm_g[i, 3])
            y2 = y2[:, C:]
        else:
            y2 = moe(h2.reshape(-1, D), router_w[i], router_b[i], moe_w_gu[i], moe_b_gu[i], moe_w_dn[i],
                     moe_b_dn[i]).reshape(B, L, D)
        x = x + g2 * rmsnorm(y2, norm_g[i, 3])
    return x
```

```python
import functools

import jax
import jax.numpy as jnp
from jax import lax
from jax.experimental import pallas as pl
from jax.experimental.pallas import tpu as pltpu

D_MODEL = 1024
GRID_W = 64
RMS_EPS = 1e-6

ATTN_HEADS = 16
ATTN_KV_HEADS = 2
HEAD_DIM = 64
Q_DIM = ATTN_HEADS * HEAD_DIM
KV_DIM = ATTN_KV_HEADS * HEAD_DIM
QKV_DIM = Q_DIM + 2 * KV_DIM
WINDOW = 128
ATTN_BLOCK = 128
ROPE_BASE = 10000.0

HGRN_HEADS = 8
HGRN_DK = 128
HGRN_CHUNK = 64
HGRN_SUB = 16

N_EXPERTS = 32
TOP_K = 4
D_FF = 1024
SWIGLU_LIMIT = 7.0
SWIGLU_ALPHA = 1.702

LANES = 128
ROW_BLOCK = 256
MOE_ROWS = 256
COMBINE_ROWS = 128
VMEM_LIMIT = 56 * 1024 * 1024

F32 = jnp.float32
BF16 = jnp.bfloat16
NEG_BIG = -1e30


def _cparams(sem):
    return pltpu.CompilerParams(dimension_semantics=sem, vmem_limit_bytes=VMEM_LIMIT)


def _rms(x, g):
    return x * lax.rsqrt(jnp.mean(x * x, axis=-1, keepdims=True) + RMS_EPS) * g


def _sigmoid(x):
    return 1.0 / (1.0 + jnp.exp(-x))


def _ada_kernel(c_ref, w_ref, b_ref, o_ref):
    c = c_ref[...]
    s = c * _sigmoid(c)
    o_ref[0] = jnp.dot(s, w_ref[0], precision=lax.Precision.HIGHEST,
                       preferred_element_type=F32) + b_ref[0]


def _ada_call(cs, ada_w, ada_b):
    depth, d, n = ada_w.shape
    tn = 1024
    return pl.pallas_call(
        _ada_kernel,
        out_shape=jax.ShapeDtypeStruct((depth, 8, n), F32),
        grid=(depth, n // tn),
        in_specs=[
            pl.BlockSpec((8, d), lambda i, j: (0, 0)),
            pl.BlockSpec((1, d, tn), lambda i, j: (i, 0, j)),
            pl.BlockSpec((1, 1, tn), lambda i, j: (i, 0, j)),
        ],
        out_specs=pl.BlockSpec((1, 8, tn), lambda i, j: (i, 0, j)),
        compiler_params=_cparams(("arbitrary", "arbitrary")),
        name="adaln",
    )(cs, ada_w, ada_b.reshape(depth, 1, n))


def _swap16(t):
    lane = lax.broadcasted_iota(jnp.int32, t.shape, 1)
    return jnp.where(lane % 32 < 16, pltpu.roll(t, LANES - 16, 1), pltpu.roll(t, 16, 1))


def _qkv_kernel(x_ref, mod_ref, gn_ref, w_ref, b_ref, cos_ref, sin_ref, q_ref, k4_ref, v4_ref):
    x = x_ref[...]
    h = _rms(x, gn_ref[...]) * (1.0 + mod_ref[0, 1:2, :]) + mod_ref[0, 0:1, :]
    hb = h.astype(BF16)
    cos = cos_ref[...]
    sin = sin_ref[...]
    nq = Q_DIM // LANES
    for j in range(nq + 1):
        sl = slice(j * LANES, (j + 1) * LANES)
        t = jnp.dot(hb, w_ref[:, sl], preferred_element_type=F32) + b_ref[:, sl]
        t = t * cos + _swap16(t) * sin
        if j < nq:
            q_ref[:, sl] = (t * (HEAD_DIM ** -0.5)).astype(q_ref.dtype)
        else:
            kt = t
    sl = slice(Q_DIM + KV_DIM, QKV_DIM)
    vt = jnp.dot(hb, w_ref[:, sl], preferred_element_type=F32) + b_ref[:, sl]
    lo = lax.broadcasted_iota(jnp.int32, kt.shape, 1) < HEAD_DIM
    for t, ref in ((kt, k4_ref), (vt, v4_ref)):
        sw = pltpu.roll(t, HEAD_DIM, 1)
        ref[:, 0 * LANES:1 * LANES] = jnp.where(lo, t, 0.0).astype(ref.dtype)
        ref[:, 1 * LANES:2 * LANES] = jnp.where(lo, 0.0, sw).astype(ref.dtype)
        ref[:, 2 * LANES:3 * LANES] = jnp.where(lo, sw, 0.0).astype(ref.dtype)
        ref[:, 3 * LANES:4 * LANES] = jnp.where(lo, 0.0, t).astype(ref.dtype)


def _qkv_call(xs, mod, gn, w, b, cos, sin, cb):
    t, d = xs.shape
    tm = ROW_BLOCK
    sel = lambda i: (jnp.minimum(i // cb, 1), 0, 0)
    return pl.pallas_call(
        _qkv_kernel,
        out_shape=(jax.ShapeDtypeStruct((t, Q_DIM), BF16),
                   jax.ShapeDtypeStruct((t, 4 * LANES), BF16),
                   jax.ShapeDtypeStruct((t, 4 * LANES), BF16)),
        grid=(t // tm,),
        in_specs=[
            pl.BlockSpec((tm, d), lambda i: (i, 0)),
            pl.BlockSpec((1, 6, d), sel),
            pl.BlockSpec((1, d), lambda i: (0, 0)),
            pl.BlockSpec((d, QKV_DIM), lambda i: (0, 0)),
            pl.BlockSpec((1, QKV_DIM), lambda i: (0, 0)),
            pl.BlockSpec((tm, LANES), lambda i: (i, 0)),
            pl.BlockSpec((tm, LANES), lambda i: (i, 0)),
        ],
        out_specs=(pl.BlockSpec((tm, Q_DIM), lambda i: (i, 0)),
                   pl.BlockSpec((tm, 4 * LANES), lambda i: (i, 0)),
                   pl.BlockSpec((tm, 4 * LANES), lambda i: (i, 0))),
        compiler_params=_cparams(("parallel",)),
        name="qkv_rope",
    )(xs, mod, gn, w, b, cos, sin)


def _attn_kernel(sink_ref, q_ref, kp_ref, kc_ref, kn_ref, kx_ref, vp_ref, vc_ref, vn_ref, vx_ref,
                 o_ref, *, cb, n_lat):
    n = pl.program_id(0)
    blk = ATTN_BLOCK
    c = kx_ref.shape[0]
    nw = 3 * blk
    r = lax.broadcasted_iota(jnp.int32, (blk, nw + c), 0)
    s = lax.broadcasted_iota(jnp.int32, (blk, nw + c), 1)
    q_pos = (n - cb) * blk + r
    k_pos = (n - cb - 1) * blk + s
    win_ok = (jnp.abs(q_pos - k_pos) <= WINDOW) & (k_pos >= 0) & (k_pos < n_lat) & (n >= cb)
    valid = (s >= nw) | win_ok
    k_all = jnp.concatenate([kp_ref[...], kc_ref[...], kn_ref[...], kx_ref[...]], axis=0)
    v_all = jnp.concatenate([vp_ref[...], vc_ref[...], vn_ref[...], vx_ref[...]], axis=0)
    nt = (((1,), (1,)), ((), ()))
    group = ATTN_HEADS // ATTN_KV_HEADS
    for p in range(ATTN_HEADS // 2):
        g = (2 * p) // group
        qp = q_ref[:, p * LANES:(p + 1) * LANES]
        o_pair = jnp.zeros((blk, LANES), F32)
        for half in range(2):
            col = (2 * g + half) * LANES
            kk = k_all[:, col:col + LANES]
            vv = v_all[:, col:col + LANES]
            sc = lax.dot_general(qp, kk, nt, preferred_element_type=F32)
            sc = jnp.where(valid, sc, NEG_BIG)
            sink = sink_ref[2 * p + half]
            m = jnp.maximum(jnp.max(sc, axis=-1, keepdims=True), sink)
            e = jnp.exp(sc - m)
            denom = jnp.sum(e, axis=-1, keepdims=True) + jnp.exp(sink - m)
            pv = jnp.dot(e.astype(BF16), vv, preferred_element_type=F32)
            o_pair = o_pair + pv / denom
        o_ref[:, p * LANES:(p + 1) * LANES] = o_pair.astype(o_ref.dtype)


def _attn_call(sink, q, k4, v4, c_len):
    t = q.shape[0]
    blk = ATTN_BLOCK
    cb = c_len // blk
    nb = t // blk
    n_lat = t - c_len
    last = nb - 1
    kw = 4 * LANES
    spec_q = pl.BlockSpec((blk, Q_DIM), lambda n: (n, 0))
    prev = pl.BlockSpec((blk, kw), lambda n: (jnp.maximum(n - 1, 0), 0))
    cur = pl.BlockSpec((blk, kw), lambda n: (n, 0))
    nxt = pl.BlockSpec((blk, kw), lambda n: (jnp.minimum(n + 1, last), 0))
    ctx = pl.BlockSpec((c_len, kw), lambda n: (0, 0))
    return pl.pallas_call(
        functools.partial(_attn_kernel, cb=cb, n_lat=n_lat),
        out_shape=jax.ShapeDtypeStruct((t, Q_DIM), BF16),
        grid=(nb,),
        in_specs=[pl.BlockSpec(memory_space=pltpu.SMEM), spec_q,
                  prev, cur, nxt, ctx, prev, cur, nxt, ctx],
        out_specs=pl.BlockSpec((blk, Q_DIM), lambda n: (n, 0)),
        compiler_params=_cparams(("parallel",)),
        name="window_attn",
    )(sink, q, k4, k4, k4, k4, v4, v4, v4, v4)


def _residual_tail(y, x, mod_ref, gn_ref, rw_ref, rb_ref, xo_ref, h2_ref, lg_ref):
    x_new = x + mod_ref[0, 2:3, :] * _rms(y, gn_ref[1:2, :])
    h2 = _rms(x_new, gn_ref[2:3, :]) * (1.0 + mod_ref[0, 4:5, :]) + mod_ref[0, 3:4, :]
    xo_ref[...] = x_new
    h2_ref[...] = h2
    lg_ref[...] = jnp.dot(h2, rw_ref[...], precision=lax.Precision.HIGHEST,
                          preferred_element_type=F32) + rb_ref[...]


def _attn_out_kernel(a_ref, x_ref, mod_ref, gn_ref, w_ref, rw_ref, rb_ref, xo_ref, h2_ref, lg_ref):
    y = jnp.dot(a_ref[...], w_ref[...], preferred_element_type=F32)
    _residual_tail(y, x_ref[...], mod_ref, gn_ref, rw_ref, rb_ref, xo_ref, h2_ref, lg_ref)


def _hgrn_out_kernel(of_ref, ob_ref, gs_ref, hg_ref, x_ref, mod_ref, gn_ref, w_ref, rw_ref, rb_ref,
                     xo_ref, h2_ref, lg_ref):
    parts = []
    for h in range(HGRN_HEADS):
        sl = slice(h * HGRN_DK, (h + 1) * HGRN_DK)
        o = of_ref[:, sl] + ob_ref[:, sl]
        parts.append(_rms(o, hg_ref[:, sl]))
    a = jnp.concatenate(parts, axis=1) * gs_ref[...].astype(F32)
    y = jnp.dot(a.astype(BF16), w_ref[...], preferred_element_type=F32)
    _residual_tail(y, x_ref[...], mod_ref, gn_ref, rw_ref, rb_ref, xo_ref, h2_ref, lg_ref)


def _out_call(kind, acts, x, mod, gn, w, rw, rb, hg, cb, off):
    t_out = x.shape[0] - off * ROW_BLOCK
    d = D_MODEL
    tm = ROW_BLOCK
    row = lambda i: (i + off, 0)
    sel = lambda i: (jnp.minimum((i + off) // cb, 1), 0, 0)
    const = lambda i: (0, 0)
    common_specs = [
        pl.BlockSpec((tm, d), row),
        pl.BlockSpec((1, 6, d), sel),
        pl.BlockSpec((4, d), const),
        pl.BlockSpec((d, d), const),
        pl.BlockSpec((d, N_EXPERTS), const),
        pl.BlockSpec((1, N_EXPERTS), const),
    ]
    if kind == "attn":
        body = _attn_out_kernel
        in_specs = [pl.BlockSpec((tm, d), row)] + common_specs
        args = (acts[0], x, mod, gn, w, rw, rb)
    else:
        body = _hgrn_out_kernel
        in_specs = [pl.BlockSpec((tm, d), row)] * 3 + [pl.BlockSpec((1, d), const)] + common_specs
        args = (acts[0], acts[1], acts[2], hg, x, mod, gn, w, rw, rb)
    return pl.pallas_call(
        body,
        out_shape=(jax.ShapeDtypeStruct((t_out, d), F32),
                   jax.ShapeDtypeStruct((t_out, d), F32),
                   jax.ShapeDtypeStruct((t_out, N_EXPERTS), F32)),
        grid=(t_out // tm,),
        in_specs=in_specs,
        out_specs=(pl.BlockSpec((tm, d), lambda i: (i, 0)),
                   pl.BlockSpec((tm, d), lambda i: (i, 0)),
                   pl.BlockSpec((tm, N_EXPERTS), lambda i: (i, 0))),
        compiler_params=_cparams(("parallel",)),
        name=kind + "_out_residual",
    )(*args)


def _router_kernel(lg_ref, slab_ref, cnt_ref, run_ref):
    i = pl.program_id(0)

    @pl.when(i == 0)
    def _():
        run_ref[...] = jnp.zeros_like(run_ref)

    lg = lg_ref[...]
    tb = lg.shape[0]
    lane = lax.broadcasted_iota(jnp.int32, lg.shape, 1).astype(F32)
    sels, tops, idxs = [], [], []
    for _ in range(TOP_K):
        m = jnp.max(lg, axis=-1, keepdims=True)
        idx = jnp.min(jnp.where(lg == m, lane, float(N_EXPERTS)), axis=-1, keepdims=True)
        sel = lane == idx
        sels.append(sel)
        tops.append(m)
        idxs.append(idx)
        lg = jnp.where(sel, -jnp.inf, lg)
    ws = [jnp.exp(tk - tops[0]) for tk in tops]
    wsum = ws[0] + ws[1] + ws[2] + ws[3]
    chosen = (sels[0] | sels[1] | sels[2] | sels[3])
    onehot = jnp.where(chosen, 1.0, 0.0)
    rr = lax.broadcasted_iota(jnp.int32, (tb, tb), 0)
    cc = lax.broadcasted_iota(jnp.int32, (tb, tb), 1)
    tri = jnp.where(cc < rr, 1.0, 0.0).astype(BF16)
    before = jnp.dot(tri, onehot.astype(BF16), preferred_element_type=F32) + run_ref[...]
    out_lane = lax.broadcasted_iota(jnp.int32, (tb, LANES), 1)
    slab = jnp.zeros((tb, LANES), F32)
    for k in range(TOP_K):
        rank = jnp.sum(jnp.where(sels[k], before, 0.0), axis=-1, keepdims=True)
        slab = jnp.where(out_lane == k, idxs[k], slab)
        slab = jnp.where(out_lane == TOP_K + k, rank, slab)
        slab = jnp.where(out_lane == 2 * TOP_K + k, ws[k] / wsum, slab)
    slab_ref[...] = slab
    run_ref[...] = run_ref[...] + jnp.sum(onehot, axis=0, keepdims=True)
    cnt_ref[...] = run_ref[...]


def _router_call(logits):
    t = logits.shape[0]
    tb = ROW_BLOCK
    return pl.pallas_call(
        _router_kernel,
        out_shape=(jax.ShapeDtypeStruct((t, LANES), F32),
                   jax.ShapeDtypeStruct((1, N_EXPERTS), F32)),
        grid=(t // tb,),
        in_specs=[pl.BlockSpec((tb, N_EXPERTS), lambda i: (i, 0))],
        out_specs=(pl.BlockSpec((tb, LANES), lambda i: (i, 0)),
                   pl.BlockSpec((1, N_EXPERTS), lambda i: (0, 0))),
        scratch_shapes=[pltpu.VMEM((1, N_EXPERTS), F32)],
        compiler_params=_cparams(("arbitrary",)),
        name="router_topk",
    )(logits)


def _gather_kernel(src_ref, x_hbm, o_ref, sem):
    rows = o_ref.shape[0]

    def issue(r, carry):
        tok = src_ref[0, 0, r]
        pltpu.make_async_copy(x_hbm.at[pl.ds(tok, 1), :], o_ref.at[pl.ds(r, 1), :], sem).start()
        return carry

    lax.fori_loop(0, rows, issue, 0)
    pltpu.make_async_copy(x_hbm.at[pl.ds(0, rows), :], o_ref, sem).wait()


def _gather_call(src_tok, h2, n_blk):
    d = h2.shape[1]
    tr = MOE_ROWS
    return pl.pallas_call(
        _gather_kernel,
        out_shape=jax.ShapeDtypeStruct((n_blk * tr, d), h2.dtype),
        grid=(n_blk,),
        in_specs=[pl.BlockSpec((1, 1, tr), lambda i: (i, 0, 0), memory_space=pltpu.SMEM),
                  pl.BlockSpec(memory_space=pl.ANY)],
        out_specs=pl.BlockSpec((tr, d), lambda i: (i, 0)),
        scratch_shapes=[pltpu.SemaphoreType.DMA],
        compiler_params=_cparams(("arbitrary",)),
        name="moe_gather",
    )(src_tok.reshape(n_blk, 1, tr), h2)


def _expert_kernel(be_ref, nu_ref, x_ref, wgu_ref, bgu_ref, wdn_ref, bdn_ref, o_ref, wgu_bf, wdn_bf):
    i = pl.program_id(0)
    prev = be_ref[jnp.maximum(i - 1, 0)]
    fresh = (i == 0) | (be_ref[i] != prev)

    @pl.when(fresh)
    def _():
        wgu_bf[...] = wgu_ref[0].astype(BF16)
        wdn_bf[...] = wdn_ref[0].astype(BF16)

    @pl.when(i < nu_ref[0])
    def _():
        xb = x_ref[...].astype(BF16)
        gu = jnp.dot(xb, wgu_bf[...], preferred_element_type=F32) + bgu_ref[0]
        g = jnp.minimum(gu[:, :D_FF], SWIGLU_LIMIT)
        u = jnp.clip(gu[:, D_FF:], -SWIGLU_LIMIT, SWIGLU_LIMIT)
        hdn = g * _sigmoid(SWIGLU_ALPHA * g) * (u + 1.0)
        o_ref[...] = jnp.dot(hdn.astype(BF16), wdn_bf[...], preferred_element_type=F32) + bdn_ref[0]

    @pl.when(i >= nu_ref[0])
    def _():
        o_ref[...] = jnp.zeros_like(o_ref)


def _expert_call(blk_e, n_used, xs, w_gu, b_gu, w_dn, b_dn):
    n_rows, d = xs.shape
    tr = MOE_ROWS
    n_blk = n_rows // tr
    e, _, f2 = w_gu.shape
    return pl.pallas_call(
        _expert_kernel,
        out_shape=jax.ShapeDtypeStruct((n_rows, d), F32),
        grid_spec=pltpu.PrefetchScalarGridSpec(
            num_scalar_prefetch=2,
            grid=(n_blk,),
            in_specs=[
                pl.BlockSpec((tr, d), lambda i, be, nu: (i, 0)),
                pl.BlockSpec((1, d, f2), lambda i, be, nu: (be[i], 0, 0)),
                pl.BlockSpec((1, 1, f2), lambda i, be, nu: (be[i], 0, 0)),
                pl.BlockSpec((1, D_FF, d), lambda i, be, nu: (be[i], 0, 0)),
                pl.BlockSpec((1, 1, d), lambda i, be, nu: (be[i], 0, 0)),
            ],
            out_specs=pl.BlockSpec((tr, d), lambda i, be, nu: (i, 0)),
            scratch_shapes=[pltpu.VMEM((d, f2), BF16), pltpu.VMEM((D_FF, d), BF16)],
        ),
        compiler_params=_cparams(("arbitrary",)),
        name="moe_experts",
    )(blk_e, n_used, xs, w_gu, b_gu.reshape(e, 1, f2), w_dn, b_dn.reshape(e, 1, d))


def _combine_kernel(dest_ref, y_hbm, slab_ref, x_ref, mod_ref, gn_ref, o_ref, buf, sem):
    tb = o_ref.shape[0]

    def issue(r, carry):
        for k in range(TOP_K):
            row = dest_ref[0, 0, r * TOP_K + k]
            pltpu.make_async_copy(y_hbm.at[pl.ds(row, 1), :], buf.at[k, pl.ds(r, 1), :], sem).start()
        return carry

    lax.fori_loop(0, tb, issue, 0)
    for k in range(TOP_K):
        pltpu.make_async_copy(y_hbm.at[pl.ds(0, tb), :], buf.at[k], sem).wait()
    slab = slab_ref[...]
    y2 = jnp.zeros(o_ref.shape, F32)
    for k in range(TOP_K):
        y2 = y2 + buf[k] * slab[:, 2 * TOP_K + k:2 * TOP_K + k + 1]
    o_ref[...] = x_ref[...] + mod_ref[0, 5:6, :] * _rms(y2, gn_ref[3:4, :])


def _combine_call(dest, y_sorted, slab, x, mod, gn, cb_rows):
    t, d = x.shape
    tb = COMBINE_ROWS
    cbb = max(cb_rows // tb, 1)
    sel = (lambda i: (jnp.minimum(i // cbb, 1), 0, 0)) if cb_rows else (lambda i: (1, 0, 0))
    return pl.pallas_call(
        _combine_kernel,
        out_shape=jax.ShapeDtypeStruct((t, d), F32),
        grid=(t // tb,),
        in_specs=[
            pl.BlockSpec((1, 1, tb * TOP_K), lambda i: (i, 0, 0), memory_space=pltpu.SMEM),
            pl.BlockSpec(memory_space=pl.ANY),
            pl.BlockSpec((tb, LANES), lambda i: (i, 0)),
            pl.BlockSpec((tb, d), lambda i: (i, 0)),
            pl.BlockSpec((1, 6, d), sel),
            pl.BlockSpec((4, d), lambda i: (0, 0)),
        ],
        out_specs=pl.BlockSpec((tb, d), lambda i: (i, 0)),
        scratch_shapes=[pltpu.VMEM((TOP_K, tb, d), F32), pltpu.SemaphoreType.DMA],
        compiler_params=_cparams(("arbitrary",)),
        name="moe_combine",
    )(dest.reshape(t // tb, 1, tb * TOP_K), y_sorted, slab, x, mod, gn)


def _moe(h2, logits, x, mod, gn, w_gu, b_gu, w_dn, b_dn, cb_rows):
    t = h2.shape[0]
    tr = MOE_ROWS
    slab, counts = _router_call(logits)
    e_idx = slab[:, :TOP_K].astype(jnp.int32)
    rank = slab[:, TOP_K:2 * TOP_K].astype(jnp.int32)
    counts = counts[0].astype(jnp.int32)
    padded = (counts + tr - 1) // tr * tr
    p_end = jnp.cumsum(padded)
    p_start = p_end - padded
    dest = p_start[e_idx] + rank
    a = t * TOP_K
    n_blk = (a + N_EXPERTS * (tr - 1) + tr - 1) // tr
    src_tok = jnp.zeros((n_blk * tr,), jnp.int32).at[dest.reshape(-1)].set(
        jnp.arange(a, dtype=jnp.int32) // TOP_K)
    blk_e = jnp.minimum(jnp.searchsorted(p_end, jnp.arange(n_blk, dtype=jnp.int32) * tr, side="right"),
                        N_EXPERTS - 1).astype(jnp.int32)
    n_used = (p_end[-1] // tr).astype(jnp.int32).reshape(1)
    xs = _gather_call(src_tok, h2, n_blk)
    ys = _expert_call(blk_e, n_used, xs, w_gu, b_gu, w_dn, b_dn)
    return _combine_call(dest, ys, slab, x, mod, gn, cb_rows)


def _hgrn_in_kernel(x_ref, mod_ref, gn_ref, w_ref, lb_ref, q_ref, kf_ref, lf_ref, kb_ref, lbw_ref,
                    i_ref, gs_ref):
    x = x_ref[...]
    h = _rms(x, gn_ref[...]) * (1.0 + mod_ref[0, 1:2, :]) + mod_ref[0, 0:1, :]
    hb = h.astype(BF16)
    f = D_MODEL

    def proj(sec):
        return jnp.dot(hb, w_ref[:, sec * f:(sec + 1) * f], preferred_element_type=F32)

    z = proj(0)
    q_ref[...] = (z * _sigmoid(z)).astype(q_ref.dtype)
    for sec, k_ref, l_ref in ((1, kf_ref, lf_ref), (2, kb_ref, lbw_ref)):
        z = proj(sec)
        lb = lb_ref[sec - 1:sec, :]
        sg = _sigmoid(z)
        l_ref[...] = jnp.log(lb + (1.0 - lb) * sg)
        k_ref[...] = ((1.0 - lb) * _sigmoid(-z)).astype(k_ref.dtype)
    i_ref[...] = proj(3).astype(i_ref.dtype)
    z = proj(4)
    gs_ref[...] = (z * _sigmoid(z)).astype(gs_ref.dtype)


def _hgrn_in_call(xs, mod, gn, w, lb, cb):
    t, d = xs.shape
    tm = ROW_BLOCK
    sel = lambda i: (jnp.minimum(i // cb, 1), 0, 0)
    row = pl.BlockSpec((tm, d), lambda i: (i, 0))
    dts = (BF16, BF16, F32, BF16, F32, BF16, BF16)
    return pl.pallas_call(
        _hgrn_in_kernel,
        out_shape=tuple(jax.ShapeDtypeStruct((t, d), dt) for dt in dts),
        grid=(t // tm,),
        in_specs=[row, pl.BlockSpec((1, 6, d), sel), pl.BlockSpec((1, d), lambda i: (0, 0)),
                  pl.BlockSpec(w.shape, lambda i: (0, 0)), pl.BlockSpec((2, d), lambda i: (0, 0))],
        out_specs=tuple(row for _ in dts),
        compiler_params=_cparams(("parallel",)),
        name="hgrn_in_proj",
    )(xs, mod, gn, w, lb)


def _split3(x):
    hi = x.astype(BF16)
    r1 = x - hi.astype(F32)
    mid = r1.astype(BF16)
    lo = (r1 - mid.astype(F32)).astype(BF16)
    return hi, mid, lo


def _scan_kernel(q_ref, k_ref, g_ref, v_ref, e64_ref, o_ref, st_ref, gcum_ref, *, reverse):
    j = pl.program_id(0)
    ch, sub = HGRN_CHUNK, HGRN_SUB
    ns = ch // sub
    dk = HGRN_DK

    @pl.when(j == 0)
    def _():
        st_ref[...] = jnp.zeros_like(st_ref)

    rr = lax.broadcasted_iota(jnp.int32, (ch, ch), 0)
    cc = lax.broadcasted_iota(jnp.int32, (ch, ch), 1)
    causal = (cc >= rr) if reverse else (cc <= rr)
    tri = jnp.where(causal, 1.0, 0.0).astype(BF16)
    hi, mid, lo = _split3(g_ref[...])
    gcum_ref[...] = (jnp.dot(tri, hi, preferred_element_type=F32)
                     + jnp.dot(tri, mid, preferred_element_type=F32)
                     + jnp.dot(tri, lo, preferred_element_type=F32))
    same_sub = (rr // sub) == (cc // sub)
    earlier = ((cc // sub) > (rr // sub)) if reverse else ((cc // sub) < (rr // sub))
    diag_mask = same_sub & causal
    lane_blk = lax.broadcasted_iota(jnp.int32, (ch, ns * dk), 1) // dk
    row_blk = lax.broadcasted_iota(jnp.int32, (ch, ns * dk), 0) // sub
    kbd_mask = lane_blk == row_blk
    end_row = 0 if reverse else ch - 1
    nt = (((1,), (1,)), ((), ()))
    tn = (((0,), (0,)), ((), ()))

    def head(h, carry):
        hs = pl.ds(pl.multiple_of(h * dk, dk), dk)
        q = q_ref[:, hs].astype(F32)
        k = k_ref[:, hs].astype(F32)
        v = v_ref[:, hs]
        gc = gcum_ref[:, hs]
        st = st_ref[h]
        g_end = gc[end_row:end_row + 1, :]
        qg = (q * jnp.exp(gc)).astype(BF16)
        o = lax.dot_general(qg, st.astype(BF16), nt, preferred_element_type=F32)
        k_end = (k * jnp.exp(g_end - gc)).astype(BF16)
        st_ref[h] = st * jnp.exp(g_end) + lax.dot_general(v, k_end, tn, preferred_element_type=F32)
        gc4 = gc.reshape(ns, sub, dk)
        edge = (sub - 1) if not reverse else 0
        g_edge = jnp.broadcast_to(gc4[:, edge:edge + 1, :], (ns, sub, dk)).reshape(ch, dk)
        k_rel = (k * jnp.exp(g_edge - gc)).astype(BF16)
        k_bd = jnp.where(kbd_mask, jnp.concatenate([k_rel] * ns, axis=1), jnp.zeros((), BF16))
        q_parts = []
        for jb in range(ns):
            row = jb * sub + edge
            q_parts.append((q * jnp.exp(jnp.minimum(gc - gc[row:row + 1, :], 0.0))).astype(BF16))
        q_cat = jnp.concatenate(q_parts, axis=1)
        s_off = lax.dot_general(q_cat, k_bd, nt, preferred_element_type=F32)
        k4 = k.reshape(ns, sub, dk)
        p_parts = []
        for s in range(sub):
            g_s = jnp.broadcast_to(gc4[:, s:s + 1, :], (ns, sub, dk)).reshape(ch, dk)
            k_s = jnp.broadcast_to(k4[:, s:s + 1, :], (ns, sub, dk)).reshape(ch, dk)
            p_parts.append((q * jnp.exp(jnp.minimum(gc - g_s, 0.0)) * k_s).astype(BF16))
        p_cat = jnp.concatenate(p_parts, axis=1)
        s_diag = jnp.dot(p_cat, e64_ref[...], preferred_element_type=F32)
        a = jnp.where(earlier, s_off, 0.0) + jnp.where(diag_mask, s_diag, 0.0)
        o = o + jnp.dot(a.astype(BF16), v, preferred_element_type=F32)
        o_ref[:, hs] = o
        return carry

    lax.fori_loop(0, HGRN_HEADS, head, 0)


def _scan_call(q, k, lg, v, e64, c_len, reverse):
    t, d = q.shape
    ch = HGRN_CHUNK
    n_ch = t // ch
    cc = c_len // ch
    if reverse:
        idx = lambda j: (jnp.where(j < cc, cc - 1 - j, n_ch - 1 - (j - cc)), 0)
    else:
        idx = lambda j: (j, 0)
    blk = pl.BlockSpec((ch, d), idx)
    return pl.pallas_call(
        functools.partial(_scan_kernel, reverse=reverse),
        out_shape=jax.ShapeDtypeStruct((t, d), F32),
        grid=(n_ch,),
        in_specs=[blk, blk, blk, blk, pl.BlockSpec(e64.shape, lambda j: (0, 0))],
        out_specs=blk,
        scratch_shapes=[pltpu.VMEM((HGRN_HEADS, HGRN_DK, HGRN_DK), F32), pltpu.VMEM((ch, d), F32)],
        compiler_params=_cparams(("arbitrary",)),
        name="hgrn_scan_bwd" if reverse else "hgrn_scan_fwd",
    )(q, k, lg, v, e64)


def _rope_tables(l, c_len):
    n = HEAD_DIM // 4
    inv_freq = ROPE_BASE ** (-jnp.arange(n, dtype=F32) / n)
    pos = jnp.arange(l, dtype=jnp.int32)
    ang_row = (pos // GRID_W).astype(F32)[:, None] * inv_freq[None, :]
    ang_col = (pos % GRID_W).astype(F32)[:, None] * inv_freq[None, :]
    cr, sr, cc, sc = jnp.cos(ang_row), jnp.sin(ang_row), jnp.cos(ang_col), jnp.sin(ang_col)
    cos = jnp.concatenate([cr, cr, cc, cc] * 2, axis=1)
    sin = jnp.concatenate([-sr, sr, -sc, sc] * 2, axis=1)
    cos = jnp.concatenate([jnp.ones((c_len, LANES), F32), cos], axis=0)
    sin = jnp.concatenate([jnp.zeros((c_len, LANES), F32), sin], axis=0)
    return cos, sin


def kernel(x, c, ctx, c_ctx, ada_w, ada_b, norm_g, attn_w_qkv, attn_b_qkv, attn_sink, attn_w_o,
           hgrn_w_in, hgrn_lb, hgrn_norm_g, hgrn_w_o, router_w, router_b, moe_w_gu, moe_b_gu,
           moe_w_dn, moe_b_dn):
    b, l, d = x.shape
    c_len = ctx.shape[1]
    depth = ada_w.shape[0]
    assert b == 1 and d == D_MODEL and depth == 2
    assert c_len % ROW_BLOCK == 0 and l % ROW_BLOCK == 0
    cb = c_len // ROW_BLOCK

    cs = jnp.zeros((8, d), F32).at[0].set(c_ctx).at[1].set(c[0])
    mods = _ada_call(cs, ada_w, ada_b)
    mods = mods[:, :2].reshape(depth, 2, 6, d)

    xs = jnp.concatenate([ctx[0], x[0]], axis=0)
    cos, sin = _rope_tables(l, c_len)

    q, k4, v4 = _qkv_call(xs, mods[0], norm_g[0, 0:1], attn_w_qkv[0].astype(BF16),
                          attn_b_qkv[0].reshape(1, -1), cos, sin, cb)
    o = _attn_call(attn_sink[0], q, k4, v4, c_len)
    xs, h2, logits = _out_call("attn", (o,), xs, mods[0], norm_g[0], attn_w_o[0].astype(BF16),
                               router_w[0], router_b[0].reshape(1, -1), None, cb, 0)
    xs = _moe(h2, logits, xs, mods[0], norm_g[0], moe_w_gu[0], moe_b_gu[0], moe_w_dn[0], moe_b_dn[0],
              c_len)

    lb_soft = jax.nn.softmax(hgrn_lb.astype(F32), axis=0)
    lb = jnp.cumsum(lb_soft, axis=0)[1] - lb_soft[0]
    qh, kf, lf, kb, lbw, iv, gs = _hgrn_in_call(xs, mods[1], norm_g[1, 0:1], hgrn_w_in[0].astype(BF16),
                                                lb, cb)
    sub = HGRN_SUB
    e64 = (jnp.arange(sub * HGRN_DK)[:, None] // HGRN_DK == jnp.arange(HGRN_CHUNK)[None, :] % sub)
    e64 = e64.astype(BF16)
    o_f = _scan_call(qh, kf, lf, iv, e64, c_len, reverse=False)
    o_b = _scan_call(qh, kb, lbw, iv, e64, c_len, reverse=True)
    x_lat, h2, logits = _out_call("hgrn", (o_f, o_b, gs), xs, mods[1], norm_g[1],
                                  hgrn_w_o[0].astype(BF16), router_w[1], router_b[1].reshape(1, -1),
                                  hgrn_norm_g[0].reshape(1, -1), cb, cb)
    out = _moe(h2, logits, x_lat, mods[1], norm_g[1], moe_w_gu[1], moe_b_gu[1], moe_w_dn[1],
               moe_b_dn[1], 0)
    return out[None]
```

```python
import functools

import jax
import jax.numpy as jnp
from jax import lax
from jax.experimental import pallas as pl
from jax.experimental.pallas import tpu as pltpu

D_MODEL = 1024
GRID_W = 64
RMS_EPS = 1e-6

ATTN_HEADS = 16
ATTN_KV_HEADS = 2
HEAD_DIM = 64
Q_DIM = ATTN_HEADS * HEAD_DIM
KV_DIM = ATTN_KV_HEADS * HEAD_DIM
QKV_DIM = Q_DIM + 2 * KV_DIM
WINDOW = 128
ATTN_BLOCK = 128
ROPE_BASE = 10000.0

HGRN_HEADS = 8
HGRN_DK = 128
HGRN_CHUNK = 64
HGRN_SUB = 16

N_EXPERTS = 32
TOP_K = 4
D_FF = 1024
SWIGLU_LIMIT = 7.0
SWIGLU_ALPHA = 1.702

LANES = 128
ROW_BLOCK = 256
MOE_ROWS = 256
COMBINE_ROWS = 128
VMEM_LIMIT = 56 * 1024 * 1024

F32 = jnp.float32
BF16 = jnp.bfloat16
NEG_BIG = -1e30


def _cparams(sem):
    return pltpu.CompilerParams(dimension_semantics=sem, vmem_limit_bytes=VMEM_LIMIT)


def _rms(x, g):
    return x * lax.rsqrt(jnp.mean(x * x, axis=-1, keepdims=True) + RMS_EPS) * g


def _sigmoid(x):
    return 1.0 / (1.0 + jnp.exp(-x))


def _ada_kernel(c_ref, w_ref, b_ref, o_ref):
    c = c_ref[...]
    s = c * _sigmoid(c)
    o_ref[0] = jnp.dot(s, w_ref[0], precision=lax.Precision.HIGHEST,
                       preferred_element_type=F32) + b_ref[0]


def _ada_call(cs, ada_w, ada_b):
    depth, d, n = ada_w.shape
    tn = 1024
    return pl.pallas_call(
        _ada_kernel,
        out_shape=jax.ShapeDtypeStruct((depth, 8, n), F32),
        grid=(depth, n // tn),
        in_specs=[
            pl.BlockSpec((8, d), lambda i, j: (0, 0)),
            pl.BlockSpec((1, d, tn), lambda i, j: (i, 0, j)),
            pl.BlockSpec((1, 1, tn), lambda i, j: (i, 0, j)),
        ],
        out_specs=pl.BlockSpec((1, 8, tn), lambda i, j: (i, 0, j)),
        compiler_params=_cparams(("arbitrary", "arbitrary")),
        name="adaln",
    )(cs, ada_w, ada_b.reshape(depth, 1, n))


def _swap16(t):
    lane = lax.broadcasted_iota(jnp.int32, t.shape, 1)
    return jnp.where(lane % 32 < 16, pltpu.roll(t, LANES - 16, 1), pltpu.roll(t, 16, 1))


def _qkv_kernel(x_ref, mod_ref, gn_ref, w_ref, b_ref, cos_ref, sin_ref, q_ref, k4_ref, v4_ref):
    x = x_ref[...]
    h = _rms(x, gn_ref[...]) * (1.0 + mod_ref[0, 1:2, :]) + mod_ref[0, 0:1, :]
    hb = h.astype(BF16)
    cos = cos_ref[...]
    sin = sin_ref[...]
    nq = Q_DIM // LANES
    for j in range(nq + 1):
        sl = slice(j * LANES, (j + 1) * LANES)
        t = jnp.dot(hb, w_ref[:, sl], preferred_element_type=F32) + b_ref[:, sl]
        t = t * cos + _swap16(t) * sin
        if j < nq:
            q_ref[:, sl] = (t * (HEAD_DIM ** -0.5)).astype(q_ref.dtype)
        else:
            kt = t
    sl = slice(Q_DIM + KV_DIM, QKV_DIM)
    vt = jnp.dot(hb, w_ref[:, sl], preferred_element_type=F32) + b_ref[:, sl]
    lo = lax.broadcasted_iota(jnp.int32, kt.shape, 1) < HEAD_DIM
    for t, ref in ((kt, k4_ref), (vt, v4_ref)):
        sw = pltpu.roll(t, HEAD_DIM, 1)
        ref[:, 0 * LANES:1 * LANES] = jnp.where(lo, t, 0.0).astype(ref.dtype)
        ref[:, 1 * LANES:2 * LANES] = jnp.where(lo, 0.0, sw).astype(ref.dtype)
        ref[:, 2 * LANES:3 * LANES] = jnp.where(lo, sw, 0.0).astype(ref.dtype)
        ref[:, 3 * LANES:4 * LANES] = jnp.where(lo, 0.0, t).astype(ref.dtype)


def _qkv_call(xs, mod, gn, w, b, cos, sin, cb):
    t, d = xs.shape
    tm = ROW_BLOCK
    sel = lambda i: (jnp.minimum(i // cb, 1), 0, 0)
    return pl.pallas_call(
        _qkv_kernel,
        out_shape=(jax.ShapeDtypeStruct((t, Q_DIM), BF16),
                   jax.ShapeDtypeStruct((t, 4 * LANES), BF16),
                   jax.ShapeDtypeStruct((t, 4 * LANES), BF16)),
        grid=(t // tm,),
        in_specs=[
            pl.BlockSpec((tm, d), lambda i: (i, 0)),
            pl.BlockSpec((1, 6, d), sel),
            pl.BlockSpec((1, d), lambda i: (0, 0)),
            pl.BlockSpec((d, QKV_DIM), lambda i: (0, 0)),
            pl.BlockSpec((1, QKV_DIM), lambda i: (0, 0)),
            pl.BlockSpec((tm, LANES), lambda i: (i, 0)),
            pl.BlockSpec((tm, LANES), lambda i: (i, 0)),
        ],
        out_specs=(pl.BlockSpec((tm, Q_DIM), lambda i: (i, 0)),
                   pl.BlockSpec((tm, 4 * LANES), lambda i: (i, 0)),
                   pl.BlockSpec((tm, 4 * LANES), lambda i: (i, 0))),
        compiler_params=_cparams(("parallel",)),
        name="qkv_rope",
    )(xs, mod, gn, w, b, cos, sin)


def _attn_kernel(sink_ref, q_ref, kp_ref, kc_ref, kn_ref, kx_ref, vp_ref, vc_ref, vn_ref, vx_ref,
                 o_ref, *, cb, n_lat):
    n = pl.program_id(0)
    blk = ATTN_BLOCK
    c = kx_ref.shape[0]
    nw = 3 * blk
    r = lax.broadcasted_iota(jnp.int32, (blk, nw + c), 0)
    s = lax.broadcasted_iota(jnp.int32, (blk, nw + c), 1)
    q_pos = (n - cb) * blk + r
    k_pos = (n - cb - 1) * blk + s
    win_ok = (jnp.abs(q_pos - k_pos) <= WINDOW) & (k_pos >= 0) & (k_pos < n_lat) & (n >= cb)
    valid = (s >= nw) | win_ok
    k_all = jnp.concatenate([kp_ref[...], kc_ref[...], kn_ref[...], kx_ref[...]], axis=0)
    v_all = jnp.concatenate([vp_ref[...], vc_ref[...], vn_ref[...], vx_ref[...]], axis=0)
    nt = (((1,), (1,)), ((), ()))
    group = ATTN_HEADS // ATTN_KV_HEADS
    for p in range(ATTN_HEADS // 2):
        g = (2 * p) // group
        qp = q_ref[:, p * LANES:(p + 1) * LANES]
        o_pair = jnp.zeros((blk, LANES), F32)
        for half in range(2):
            col = (2 * g + half) * LANES
            kk = k_all[:, col:col + LANES]
            vv = v_all[:, col:col + LANES]
            sc = lax.dot_general(qp, kk, nt, preferred_element_type=F32)
            sc = jnp.where(valid, sc, NEG_BIG)
            sink = sink_ref[2 * p + half]
            m = jnp.maximum(jnp.max(sc, axis=-1, keepdims=True), sink)
            e = jnp.exp(sc - m)
            denom = jnp.sum(e, axis=-1, keepdims=True) + jnp.exp(sink - m)
            pv = jnp.dot(e.astype(BF16), vv, preferred_element_type=F32)
            o_pair = o_pair + pv / denom
        o_ref[:, p * LANES:(p + 1) * LANES] = o_pair.astype(o_ref.dtype)


def _attn_call(sink, q, k4, v4, c_len):
    t = q.shape[0]
    blk = ATTN_BLOCK
    cb = c_len // blk
    nb = t // blk
    n_lat = t - c_len
    last = nb - 1
    kw = 4 * LANES
    spec_q = pl.BlockSpec((blk, Q_DIM), lambda n: (n, 0))
    prev = pl.BlockSpec((blk, kw), lambda n: (jnp.maximum(n - 1, 0), 0))
    cur = pl.BlockSpec((blk, kw), lambda n: (n, 0))
    nxt = pl.BlockSpec((blk, kw), lambda n: (jnp.minimum(n + 1, last), 0))
    ctx = pl.BlockSpec((c_len, kw), lambda n: (0, 0))
    return pl.pallas_call(
        functools.partial(_attn_kernel, cb=cb, n_lat=n_lat),
        out_shape=jax.ShapeDtypeStruct((t, Q_DIM), BF16),
        grid=(nb,),
        in_specs=[pl.BlockSpec(memory_space=pltpu.SMEM), spec_q,
                  prev, cur, nxt, ctx, prev, cur, nxt, ctx],
        out_specs=pl.BlockSpec((blk, Q_DIM), lambda n: (n, 0)),
        compiler_params=_cparams(("parallel",)),
        name="window_attn",
    )(sink, q, k4, k4, k4, k4, v4, v4, v4, v4)


def _residual_tail(y, x, mod_ref, gn_ref, rw_ref, rb_ref, xo_ref, h2_ref, lg_ref):
    x_new = x + mod_ref[0, 2:3, :] * _rms(y, gn_ref[1:2, :])
    h2 = _rms(x_new, gn_ref[2:3, :]) * (1.0 + mod_ref[0, 4:5, :]) + mod_ref[0, 3:4, :]
    xo_ref[...] = x_new
    h2_ref[...] = h2
    lg_ref[...] = jnp.dot(h2, rw_ref[...], precision=lax.Precision.HIGHEST,
                          preferred_element_type=F32) + rb_ref[...]


def _attn_out_kernel(a_ref, x_ref, mod_ref, gn_ref, w_ref, rw_ref, rb_ref, xo_ref, h2_ref, lg_ref):
    y = jnp.dot(a_ref[...], w_ref[...], preferred_element_type=F32)
    _residual_tail(y, x_ref[...], mod_ref, gn_ref, rw_ref, rb_ref, xo_ref, h2_ref, lg_ref)


def _hgrn_out_kernel(of_ref, ob_ref, gs_ref, hg_ref, x_ref, mod_ref, gn_ref, w_ref, rw_ref, rb_ref,
                     xo_ref, h2_ref, lg_ref):
    parts = []
    for h in range(HGRN_HEADS):
        sl = slice(h * HGRN_DK, (h + 1) * HGRN_DK)
        o = of_ref[:, sl] + ob_ref[:, sl]
        parts.append(_rms(o, hg_ref[:, sl]))
    a = jnp.concatenate(parts, axis=1) * gs_ref[...].astype(F32)
    y = jnp.dot(a.astype(BF16), w_ref[...], preferred_element_type=F32)
    _residual_tail(y, x_ref[...], mod_ref, gn_ref, rw_ref, rb_ref, xo_ref, h2_ref, lg_ref)


def _out_call(kind, acts, x, mod, gn, w, rw, rb, hg, cb, off):
    t_out = x.shape[0] - off * ROW_BLOCK
    d = D_MODEL
    tm = ROW_BLOCK
    row = lambda i: (i + off, 0)
    sel = lambda i: (jnp.minimum((i + off) // cb, 1), 0, 0)
    const = lambda i: (0, 0)
    common_specs = [
        pl.BlockSpec((tm, d), row),
        pl.BlockSpec((1, 6, d), sel),
        pl.BlockSpec((4, d), const),
        pl.BlockSpec((d, d), const),
        pl.BlockSpec((d, N_EXPERTS), const),
        pl.BlockSpec((1, N_EXPERTS), const),
    ]
    if kind == "attn":
        body = _attn_out_kernel
        in_specs = [pl.BlockSpec((tm, d), row)] + common_specs
        args = (acts[0], x, mod, gn, w, rw, rb)
    else:
        body = _hgrn_out_kernel
        in_specs = [pl.BlockSpec((tm, d), row)] * 3 + [pl.BlockSpec((1, d), const)] + common_specs
        args = (acts[0], acts[1], acts[2], hg, x, mod, gn, w, rw, rb)
    return pl.pallas_call(
        body,
        out_shape=(jax.ShapeDtypeStruct((t_out, d), F32),
                   jax.ShapeDtypeStruct((t_out, d), F32),
                   jax.ShapeDtypeStruct((t_out, N_EXPERTS), F32)),
        grid=(t_out // tm,),
        in_specs=in_specs,
        out_specs=(pl.BlockSpec((tm, d), lambda i: (i, 0)),
                   pl.BlockSpec((tm, d), lambda i: (i, 0)),
                   pl.BlockSpec((tm, N_EXPERTS), lambda i: (i, 0))),
        compiler_params=_cparams(("parallel",)),
        name=kind + "_out_residual",
    )(*args)


def _router_kernel(lg_ref, slab_ref, cnt_ref, run_ref):
    i = pl.program_id(0)

    @pl.when(i == 0)
    def _():
        run_ref[...] = jnp.zeros_like(run_ref)

    lg = lg_ref[...]
    tb = lg.shape[0]
    lane = lax.broadcasted_iota(jnp.int32, lg.shape, 1).astype(F32)
    sels, tops, idxs = [], [], []
    for _ in range(TOP_K):
        m = jnp.max(lg, axis=-1, keepdims=True)
        idx = jnp.min(jnp.where(lg == m, lane, float(N_EXPERTS)), axis=-1, keepdims=True)
        sel = lane == idx
        sels.append(sel)
        tops.append(m)
        idxs.append(idx)
        lg = jnp.where(sel, -jnp.inf, lg)
    ws = [jnp.exp(tk - tops[0]) for tk in tops]
    wsum = ws[0] + ws[1] + ws[2] + ws[3]
    chosen = (sels[0] | sels[1] | sels[2] | sels[3])
    onehot = jnp.where(chosen, 1.0, 0.0)
    rr = lax.broadcasted_iota(jnp.int32, (tb, tb), 0)
    cc = lax.broadcasted_iota(jnp.int32, (tb, tb), 1)
    tri = jnp.where(cc < rr, 1.0, 0.0).astype(BF16)
    before = jnp.dot(tri, onehot.astype(BF16), preferred_element_type=F32) + run_ref[...]
    out_lane = lax.broadcasted_iota(jnp.int32, (tb, LANES), 1)
    slab = jnp.zeros((tb, LANES), F32)
    for k in range(TOP_K):
        rank = jnp.sum(jnp.where(sels[k], before, 0.0), axis=-1, keepdims=True)
        slab = jnp.where(out_lane == k, idxs[k], slab)
        slab = jnp.where(out_lane == TOP_K + k, rank, slab)
        slab = jnp.where(out_lane == 2 * TOP_K + k, ws[k] / wsum, slab)
    slab_ref[...] = slab
    run_ref[...] = run_ref[...] + jnp.sum(onehot, axis=0, keepdims=True)
    cnt_ref[...] = run_ref[...]


def _router_call(logits):
    t = logits.shape[0]
    tb = ROW_BLOCK
    return pl.pallas_call(
        _router_kernel,
        out_shape=(jax.ShapeDtypeStruct((t, LANES), F32),
                   jax.ShapeDtypeStruct((1, N_EXPERTS), F32)),
        grid=(t // tb,),
        in_specs=[pl.BlockSpec((tb, N_EXPERTS), lambda i: (i, 0))],
        out_specs=(pl.BlockSpec((tb, LANES), lambda i: (i, 0)),
                   pl.BlockSpec((1, N_EXPERTS), lambda i: (0, 0))),
        scratch_shapes=[pltpu.VMEM((1, N_EXPERTS), F32)],
        compiler_params=_cparams(("arbitrary",)),
        name="router_topk",
    )(logits)


def _expert_kernel(be_ref, nu_ref, src0_ref, src1_ref, x_hbm, wgu_ref, bgu_ref, wdn_ref, bdn_ref, o_ref,
                   xbuf, sems, wgu_bf, wdn_bf):
    i = pl.program_id(0)
    rows = o_ref.shape[0]
    slot = i % 2
    n_used = nu_ref[0]

    def gather(idx_ref, s):
        def issue(r, carry):
            pltpu.make_async_copy(x_hbm.at[pl.ds(idx_ref[0, 0, r], 1), :],
                                  xbuf.at[s, pl.ds(r, 1), :], sems.at[s]).start()
            return carry

        lax.fori_loop(0, rows, issue, 0, unroll=8)

    @pl.when(i == 0)
    def _():
        gather(src0_ref, 0)

    @pl.when(i + 1 < n_used)
    def _():
        gather(src1_ref, 1 - slot)

    prev = be_ref[jnp.maximum(i - 1, 0)]
    fresh = (i == 0) | (be_ref[i] != prev)

    @pl.when(fresh)
    def _():
        wgu_bf[...] = wgu_ref[...].astype(BF16)
        wdn_bf[...] = wdn_ref[...].astype(BF16)

    @pl.when(i < n_used)
    def _():
        pltpu.make_async_copy(x_hbm.at[pl.ds(0, rows), :], xbuf.at[slot], sems.at[slot]).wait()
        xb = xbuf[slot].astype(BF16)
        gu = jnp.dot(xb, wgu_bf[...], preferred_element_type=F32) + bgu_ref[...]
        g = jnp.minimum(gu[:, :D_FF], SWIGLU_LIMIT)
        u = jnp.clip(gu[:, D_FF:], -SWIGLU_LIMIT, SWIGLU_LIMIT)
        hdn = g * _sigmoid(SWIGLU_ALPHA * g) * (u + 1.0)
        o_ref[...] = jnp.dot(hdn.astype(BF16), wdn_bf[...], preferred_element_type=F32) + bdn_ref[...]

    @pl.when(i >= n_used)
    def _():
        o_ref[...] = jnp.zeros_like(o_ref)


def _expert_call(blk_e, n_used, src_tok, h2, layer, w_gu, b_gu, w_dn, b_dn):
    d = h2.shape[1]
    tr = MOE_ROWS
    n_blk = blk_e.shape[0]
    depth, e, _, f2 = w_gu.shape
    last = n_blk - 1
    return pl.pallas_call(
        _expert_kernel,
        out_shape=jax.ShapeDtypeStruct((n_blk * tr, d), F32),
        grid_spec=pltpu.PrefetchScalarGridSpec(
            num_scalar_prefetch=2,
            grid=(n_blk,),
            in_specs=[
                pl.BlockSpec((1, 1, tr), lambda i, be, nu: (0, 0, 0), memory_space=pltpu.SMEM),
                pl.BlockSpec((1, 1, tr), lambda i, be, nu: (jnp.minimum(i + 1, last), 0, 0),
                             memory_space=pltpu.SMEM),
                pl.BlockSpec(memory_space=pl.ANY),
                pl.BlockSpec((None, None, d, f2), lambda i, be, nu: (layer, be[i], 0, 0)),
                pl.BlockSpec((None, None, 1, f2), lambda i, be, nu: (layer, be[i], 0, 0)),
                pl.BlockSpec((None, None, D_FF, d), lambda i, be, nu: (layer, be[i], 0, 0)),
                pl.BlockSpec((None, None, 1, d), lambda i, be, nu: (layer, be[i], 0, 0)),
            ],
            out_specs=pl.BlockSpec((tr, d), lambda i, be, nu: (i, 0)),
            scratch_shapes=[pltpu.VMEM((2, tr, d), F32), pltpu.SemaphoreType.DMA((2,)),
                            pltpu.VMEM((d, f2), BF16), pltpu.VMEM((D_FF, d), BF16)],
        ),
        compiler_params=_cparams(("arbitrary",)),
        name="moe_experts",
    )(blk_e, n_used, src_tok.reshape(n_blk, 1, tr), src_tok.reshape(n_blk, 1, tr), h2,
      w_gu, b_gu.reshape(depth, e, 1, f2), w_dn, b_dn.reshape(depth, e, 1, d))


def _combine_kernel(d0_ref, d1_ref, y_hbm, slab_ref, x_ref, mod_ref, gn_ref, o_ref, buf, sems):
    i = pl.program_id(0)
    tb = o_ref.shape[0]
    slot = i % 2

    def gather(idx_ref, s):
        def issue(r, carry):
            for k in range(TOP_K):
                row = idx_ref[0, 0, r * TOP_K + k]
                pltpu.make_async_copy(y_hbm.at[pl.ds(row, 1), :], buf.at[s, k, pl.ds(r, 1), :],
                                      sems.at[s]).start()
            return carry

        lax.fori_loop(0, tb, issue, 0, unroll=4)

    @pl.when(i == 0)
    def _():
        gather(d0_ref, 0)

    @pl.when(i + 1 < pl.num_programs(0))
    def _():
        gather(d1_ref, 1 - slot)

    for k in range(TOP_K):
        pltpu.make_async_copy(y_hbm.at[pl.ds(0, tb), :], buf.at[slot, k], sems.at[slot]).wait()
    slab = slab_ref[...]
    y2 = jnp.zeros(o_ref.shape, F32)
    for k in range(TOP_K):
        y2 = y2 + buf[slot, k] * slab[:, 2 * TOP_K + k:2 * TOP_K + k + 1]
    o_ref[...] = x_ref[...] + mod_ref[0, 5:6, :] * _rms(y2, gn_ref[3:4, :])


def _combine_call(dest, y_sorted, slab, x, mod, gn, cb_rows):
    t, d = x.shape
    tb = COMBINE_ROWS
    nb = t // tb
    cbb = max(cb_rows // tb, 1)
    sel = (lambda i: (jnp.minimum(i // cbb, 1), 0, 0)) if cb_rows else (lambda i: (1, 0, 0))
    dest3 = dest.reshape(nb, 1, tb * TOP_K)
    return pl.pallas_call(
        _combine_kernel,
        out_shape=jax.ShapeDtypeStruct((t, d), F32),
        grid=(nb,),
        in_specs=[
            pl.BlockSpec((1, 1, tb * TOP_K), lambda i: (0, 0, 0), memory_space=pltpu.SMEM),
            pl.BlockSpec((1, 1, tb * TOP_K), lambda i: (jnp.minimum(i + 1, nb - 1), 0, 0),
                         memory_space=pltpu.SMEM),
            pl.BlockSpec(memory_space=pl.ANY),
            pl.BlockSpec((tb, LANES), lambda i: (i, 0)),
            pl.BlockSpec((tb, d), lambda i: (i, 0)),
            pl.BlockSpec((1, 6, d), sel),
            pl.BlockSpec((4, d), lambda i: (0, 0)),
        ],
        out_specs=pl.BlockSpec((tb, d), lambda i: (i, 0)),
        scratch_shapes=[pltpu.VMEM((2, TOP_K, tb, d), F32), pltpu.SemaphoreType.DMA((2,))],
        compiler_params=_cparams(("arbitrary",)),
        name="moe_combine",
    )(dest3, dest3, y_sorted, slab, x, mod, gn)


def _moe(h2, logits, x, mod, gn, layer, w_gu, b_gu, w_dn, b_dn, cb_rows):
    t = h2.shape[0]
    tr = MOE_ROWS
    slab, counts = _router_call(logits)
    e_idx = slab[:, :TOP_K].astype(jnp.int32)
    rank = slab[:, TOP_K:2 * TOP_K].astype(jnp.int32)
    counts = counts[0].astype(jnp.int32)
    padded = (counts + tr - 1) // tr * tr
    p_end = jnp.cumsum(padded)
    p_start = p_end - padded
    dest = p_start[e_idx] + rank
    a = t * TOP_K
    n_blk = (a + N_EXPERTS * (tr - 1) + tr - 1) // tr
    src_tok = jnp.zeros((n_blk * tr,), jnp.int32).at[dest.reshape(-1)].set(
        jnp.arange(a, dtype=jnp.int32) // TOP_K, unique_indices=True)
    blk_start = jnp.arange(n_blk, dtype=jnp.int32) * tr
    blk_e = jnp.sum((p_end[None, :] <= blk_start[:, None]).astype(jnp.int32), axis=1)
    blk_e = jnp.minimum(blk_e, N_EXPERTS - 1)
    n_used = (p_end[-1] // tr).astype(jnp.int32).reshape(1)
    ys = _expert_call(blk_e, n_used, src_tok, h2, layer, w_gu, b_gu, w_dn, b_dn)
    return _combine_call(dest, ys, slab, x, mod, gn, cb_rows)


def _hgrn_in_kernel(x_ref, mod_ref, gn_ref, w_ref, lb_ref, q_ref, kf_ref, lf_ref, kb_ref, lbw_ref,
                    i_ref, gs_ref):
    x = x_ref[...]
    h = _rms(x, gn_ref[...]) * (1.0 + mod_ref[0, 1:2, :]) + mod_ref[0, 0:1, :]
    hb = h.astype(BF16)
    f = D_MODEL

    def proj(sec):
        return jnp.dot(hb, w_ref[:, sec * f:(sec + 1) * f], preferred_element_type=F32)

    z = proj(0)
    q_ref[...] = (z * _sigmoid(z)).astype(q_ref.dtype)
    for sec, k_ref, l_ref in ((1, kf_ref, lf_ref), (2, kb_ref, lbw_ref)):
        z = proj(sec)
        lb = lb_ref[sec - 1:sec, :]
        sg = _sigmoid(z)
        l_ref[...] = jnp.log(lb + (1.0 - lb) * sg)
        k_ref[...] = ((1.0 - lb) * _sigmoid(-z)).astype(k_ref.dtype)
    i_ref[...] = proj(3).astype(i_ref.dtype)
    z = proj(4)
    gs_ref[...] = (z * _sigmoid(z)).astype(gs_ref.dtype)


def _hgrn_in_call(xs, mod, gn, w, lb, cb):
    t, d = xs.shape
    tm = ROW_BLOCK
    sel = lambda i: (jnp.minimum(i // cb, 1), 0, 0)
    row = pl.BlockSpec((tm, d), lambda i: (i, 0))
    dts = (BF16, BF16, F32, BF16, F32, BF16, BF16)
    return pl.pallas_call(
        _hgrn_in_kernel,
        out_shape=tuple(jax.ShapeDtypeStruct((t, d), dt) for dt in dts),
        grid=(t // tm,),
        in_specs=[row, pl.BlockSpec((1, 6, d), sel), pl.BlockSpec((1, d), lambda i: (0, 0)),
                  pl.BlockSpec(w.shape, lambda i: (0, 0)), pl.BlockSpec((2, d), lambda i: (0, 0))],
        out_specs=tuple(row for _ in dts),
        compiler_params=_cparams(("parallel",)),
        name="hgrn_in_proj",
    )(xs, mod, gn, w, lb)


def _split3(x):
    hi = x.astype(BF16)
    r1 = x - hi.astype(F32)
    mid = r1.astype(BF16)
    lo = (r1 - mid.astype(F32)).astype(BF16)
    return hi, mid, lo


def _scan_kernel(q_ref, k_ref, g_ref, v_ref, e64_ref, o_ref, st_ref, gcum_ref, *, reverse):
    j = pl.program_id(0)
    ch, sub = HGRN_CHUNK, HGRN_SUB
    ns = ch // sub
    dk = HGRN_DK

    @pl.when(j == 0)
    def _():
        st_ref[...] = jnp.zeros_like(st_ref)

    rr = lax.broadcasted_iota(jnp.int32, (ch, ch), 0)
    cc = lax.broadcasted_iota(jnp.int32, (ch, ch), 1)
    causal = (cc >= rr) if reverse else (cc <= rr)
    tri = jnp.where(causal, 1.0, 0.0).astype(BF16)
    hi, mid, lo = _split3(g_ref[...])
    gcum_ref[...] = (jnp.dot(tri, hi, preferred_element_type=F32)
                     + jnp.dot(tri, mid, preferred_element_type=F32)
                     + jnp.dot(tri, lo, preferred_element_type=F32))
    same_sub = (rr // sub) == (cc // sub)
    earlier = ((cc // sub) > (rr // sub)) if reverse else ((cc // sub) < (rr // sub))
    diag_mask = same_sub & causal
    lane_blk = lax.broadcasted_iota(jnp.int32, (ch, ns * dk), 1) // dk
    row_blk = lax.broadcasted_iota(jnp.int32, (ch, ns * dk), 0) // sub
    kbd_mask = lane_blk == row_blk
    end_row = 0 if reverse else ch - 1
    nt = (((1,), (1,)), ((), ()))
    tn = (((0,), (0,)), ((), ()))

    def head(h, carry):
        hs = pl.ds(pl.multiple_of(h * dk, dk), dk)
        q = q_ref[:, hs].astype(F32)
        k = k_ref[:, hs].astype(F32)
        v = v_ref[:, hs]
        gc = gcum_ref[:, hs]
        st = st_ref[h]
        g_end = gc[end_row:end_row + 1, :]
        qg = (q * jnp.exp(gc)).astype(BF16)
        o = lax.dot_general(qg, st.astype(BF16), nt, preferred_element_type=F32)
        k_end = (k * jnp.exp(g_end - gc)).astype(BF16)
        st_ref[h] = st * jnp.exp(g_end) + lax.dot_general(v, k_end, tn, preferred_element_type=F32)
        gc4 = gc.reshape(ns, sub, dk)
        edge = (sub - 1) if not reverse else 0
        g_edge = jnp.broadcast_to(gc4[:, edge:edge + 1, :], (ns, sub, dk)).reshape(ch, dk)
        k_rel = (k * jnp.exp(g_edge - gc)).astype(BF16)
        k_bd = jnp.where(kbd_mask, jnp.concatenate([k_rel] * ns, axis=1), jnp.zeros((), BF16))
        q_parts = []
        for jb in range(ns):
            row = jb * sub + edge
            q_parts.append((q * jnp.exp(jnp.minimum(gc - gc[row:row + 1, :], 0.0))).astype(BF16))
        q_cat = jnp.concatenate(q_parts, axis=1)
        s_off = lax.dot_general(q_cat, k_bd, nt, preferred_element_type=F32)
        k4 = k.reshape(ns, sub, dk)
        p_parts = []
        for s in range(sub):
            g_s = jnp.broadcast_to(gc4[:, s:s + 1, :], (ns, sub, dk)).reshape(ch, dk)
            k_s = jnp.broadcast_to(k4[:, s:s + 1, :], (ns, sub, dk)).reshape(ch, dk)
            p_parts.append((q * jnp.exp(jnp.minimum(gc - g_s, 0.0)) * k_s).astype(BF16))
        p_cat = jnp.concatenate(p_parts, axis=1)
        s_diag = jnp.dot(p_cat, e64_ref[...], preferred_element_type=F32)
        a = jnp.where(earlier, s_off, 0.0) + jnp.where(diag_mask, s_diag, 0.0)
        o = o + jnp.dot(a.astype(BF16), v, preferred_element_type=F32)
        o_ref[:, hs] = o
        return carry

    lax.fori_loop(0, HGRN_HEADS, head, 0)


def _scan_call(q, k, lg, v, e64, c_len, reverse):
    t, d = q.shape
    ch = HGRN_CHUNK
    n_ch = t // ch
    cc = c_len // ch
    if reverse:
        idx = lambda j: (jnp.where(j < cc, cc - 1 - j, n_ch - 1 - (j - cc)), 0)
    else:
        idx = lambda j: (j, 0)
    blk = pl.BlockSpec((ch, d), idx)
    return pl.pallas_call(
        functools.partial(_scan_kernel, reverse=reverse),
        out_shape=jax.ShapeDtypeStruct((t, d), F32),
        grid=(n_ch,),
        in_specs=[blk, blk, blk, blk, pl.BlockSpec(e64.shape, lambda j: (0, 0))],
        out_specs=blk,
        scratch_shapes=[pltpu.VMEM((HGRN_HEADS, HGRN_DK, HGRN_DK), F32), pltpu.VMEM((ch, d), F32)],
        compiler_params=_cparams(("arbitrary",)),
        name="hgrn_scan_bwd" if reverse else "hgrn_scan_fwd",
    )(q, k, lg, v, e64)


def _rope_tables(l, c_len):
    n = HEAD_DIM // 4
    inv_freq = ROPE_BASE ** (-jnp.arange(n, dtype=F32) / n)
    pos = jnp.arange(l, dtype=jnp.int32)
    ang_row = (pos // GRID_W).astype(F32)[:, None] * inv_freq[None, :]
    ang_col = (pos % GRID_W).astype(F32)[:, None] * inv_freq[None, :]
    cr, sr, cc, sc = jnp.cos(ang_row), jnp.sin(ang_row), jnp.cos(ang_col), jnp.sin(ang_col)
    cos = jnp.concatenate([cr, cr, cc, cc] * 2, axis=1)
    sin = jnp.concatenate([-sr, sr, -sc, sc] * 2, axis=1)
    cos = jnp.concatenate([jnp.ones((c_len, LANES), F32), cos], axis=0)
    sin = jnp.concatenate([jnp.zeros((c_len, LANES), F32), sin], axis=0)
    return cos, sin


def kernel(x, c, ctx, c_ctx, ada_w, ada_b, norm_g, attn_w_qkv, attn_b_qkv, attn_sink, attn_w_o,
           hgrn_w_in, hgrn_lb, hgrn_norm_g, hgrn_w_o, router_w, router_b, moe_w_gu, moe_b_gu,
           moe_w_dn, moe_b_dn):
    b, l, d = x.shape
    c_len = ctx.shape[1]
    depth = ada_w.shape[0]
    assert b == 1 and d == D_MODEL and depth == 2
    assert c_len % ROW_BLOCK == 0 and l % ROW_BLOCK == 0
    cb = c_len // ROW_BLOCK

    cs = jnp.zeros((8, d), F32).at[0].set(c_ctx).at[1].set(c[0])
    mods = _ada_call(cs, ada_w, ada_b)
    mods = mods[:, :2].reshape(depth, 2, 6, d)

    xs = jnp.concatenate([ctx[0], x[0]], axis=0)
    cos, sin = _rope_tables(l, c_len)

    q, k4, v4 = _qkv_call(xs, mods[0], norm_g[0, 0:1], attn_w_qkv[0].astype(BF16),
                          attn_b_qkv[0].reshape(1, -1), cos, sin, cb)
    o = _attn_call(attn_sink[0], q, k4, v4, c_len)
    xs, h2, logits = _out_call("attn", (o,), xs, mods[0], norm_g[0], attn_w_o[0].astype(BF16),
                               router_w[0], router_b[0].reshape(1, -1), None, cb, 0)
    xs = _moe(h2, logits, xs, mods[0], norm_g[0], 0, moe_w_gu, moe_b_gu, moe_w_dn, moe_b_dn, c_len)

    lb_soft = jax.nn.softmax(hgrn_lb.astype(F32), axis=0)
    lb = jnp.cumsum(lb_soft, axis=0)[1] - lb_soft[0]
    qh, kf, lf, kb, lbw, iv, gs = _hgrn_in_call(xs, mods[1], norm_g[1, 0:1], hgrn_w_in[0].astype(BF16),
                                                lb, cb)
    sub = HGRN_SUB
    e64 = (jnp.arange(sub * HGRN_DK)[:, None] // HGRN_DK == jnp.arange(HGRN_CHUNK)[None, :] % sub)
    e64 = e64.astype(BF16)
    o_f = _scan_call(qh, kf, lf, iv, e64, c_len, reverse=False)
    o_b = _scan_call(qh, kb, lbw, iv, e64, c_len, reverse=True)
    x_lat, h2, logits = _out_call("hgrn", (o_f, o_b, gs), xs, mods[1], norm_g[1],
                                  hgrn_w_o[0].astype(BF16), router_w[1], router_b[1].reshape(1, -1),
                                  hgrn_norm_g[0].reshape(1, -1), cb, cb)
    out = _moe(h2, logits, x_lat, mods[1], norm_g[1], 1, moe_w_gu, moe_b_gu, moe_w_dn, moe_b_dn, 0)
    return out[None]
```

```python
import functools

import jax
import jax.numpy as jnp
from jax import lax
from jax.experimental import pallas as pl
from jax.experimental.pallas import tpu as pltpu

D_MODEL = 1024
GRID_W = 64
RMS_EPS = 1e-6

ATTN_HEADS = 16
ATTN_KV_HEADS = 2
HEAD_DIM = 64
Q_DIM = ATTN_HEADS * HEAD_DIM
KV_DIM = ATTN_KV_HEADS * HEAD_DIM
QKV_DIM = Q_DIM + 2 * KV_DIM
WINDOW = 128
ATTN_BLOCK = 128
ROPE_BASE = 10000.0

HGRN_HEADS = 8
HGRN_DK = 128
HGRN_CHUNK = 64
HGRN_SUB = 8

N_EXPERTS = 32
TOP_K = 4
D_FF = 1024
SWIGLU_LIMIT = 7.0
SWIGLU_ALPHA = 1.702

LANES = 128
ROW_BLOCK = 256
MOE_ROWS = 256
COMBINE_ROWS = 128
VMEM_LIMIT = 56 * 1024 * 1024

F32 = jnp.float32
BF16 = jnp.bfloat16
NEG_BIG = -1e30


def _cparams(sem, row_dma=False):
    return pltpu.CompilerParams(dimension_semantics=sem, vmem_limit_bytes=VMEM_LIMIT,
                                disable_bounds_checks=row_dma)


def _rms(x, g):
    return x * lax.rsqrt(jnp.mean(x * x, axis=-1, keepdims=True) + RMS_EPS) * g


def _sigmoid(x):
    return 1.0 / (1.0 + jnp.exp(-x))


SEGS = D_MODEL // LANES


def _store_tiled(ref, val):
    n = val.shape[0]
    for s in range(SEGS):
        ref[pl.ds(s, n, stride=SEGS), :] = val[:, s * LANES:(s + 1) * LANES]


def _load_tiled(ref, n):
    return jnp.concatenate([ref[pl.ds(s, n, stride=SEGS), :] for s in range(SEGS)], axis=1)


def _row_tile(ref, r):
    return ref.at[pl.ds(pl.multiple_of(r * SEGS, SEGS), SEGS), :]


def _ada_kernel(c_ref, w_ref, b_ref, o_ref):
    c = c_ref[...]
    s = c * _sigmoid(c)
    o_ref[0] = jnp.dot(s, w_ref[0], precision=lax.Precision.HIGHEST,
                       preferred_element_type=F32) + b_ref[0]


def _ada_call(cs, ada_w, ada_b):
    depth, d, n = ada_w.shape
    tn = 1024
    return pl.pallas_call(
        _ada_kernel,
        out_shape=jax.ShapeDtypeStruct((depth, 8, n), F32),
        grid=(depth, n // tn),
        in_specs=[
            pl.BlockSpec((8, d), lambda i, j: (0, 0)),
            pl.BlockSpec((1, d, tn), lambda i, j: (i, 0, j)),
            pl.BlockSpec((1, 1, tn), lambda i, j: (i, 0, j)),
        ],
        out_specs=pl.BlockSpec((1, 8, tn), lambda i, j: (i, 0, j)),
        compiler_params=_cparams(("arbitrary", "arbitrary")),
        name="adaln",
    )(cs, ada_w, ada_b.reshape(depth, 1, n))


def _swap16(t):
    lane = lax.broadcasted_iota(jnp.int32, t.shape, 1)
    return jnp.where(lane % 32 < 16, pltpu.roll(t, LANES - 16, 1), pltpu.roll(t, 16, 1))


def _qkv_kernel(x_ref, mod_ref, gn_ref, w_ref, b_ref, cos_ref, sin_ref, q_ref, k4_ref, v4_ref):
    x = x_ref[...]
    h = _rms(x, gn_ref[...]) * (1.0 + mod_ref[0, 1:2, :]) + mod_ref[0, 0:1, :]
    hb = h.astype(BF16)
    cos = cos_ref[...]
    sin = sin_ref[...]
    nq = Q_DIM // LANES
    for j in range(nq + 1):
        sl = slice(j * LANES, (j + 1) * LANES)
        t = jnp.dot(hb, w_ref[:, sl], preferred_element_type=F32) + b_ref[:, sl]
        t = t * cos + _swap16(t) * sin
        if j < nq:
            q_ref[:, sl] = (t * (HEAD_DIM ** -0.5)).astype(q_ref.dtype)
        else:
            kt = t
    sl = slice(Q_DIM + KV_DIM, QKV_DIM)
    vt = jnp.dot(hb, w_ref[:, sl], preferred_element_type=F32) + b_ref[:, sl]
    lo = lax.broadcasted_iota(jnp.int32, kt.shape, 1) < HEAD_DIM
    for t, ref in ((kt, k4_ref), (vt, v4_ref)):
        sw = pltpu.roll(t, HEAD_DIM, 1)
        ref[:, 0 * LANES:1 * LANES] = jnp.where(lo, t, 0.0).astype(ref.dtype)
        ref[:, 1 * LANES:2 * LANES] = jnp.where(lo, 0.0, sw).astype(ref.dtype)
        ref[:, 2 * LANES:3 * LANES] = jnp.where(lo, sw, 0.0).astype(ref.dtype)
        ref[:, 3 * LANES:4 * LANES] = jnp.where(lo, 0.0, t).astype(ref.dtype)


def _qkv_call(xs, mod, gn, w, b, cos, sin, cb):
    t, d = xs.shape
    tm = ROW_BLOCK
    sel = lambda i: (jnp.minimum(i // cb, 1), 0, 0)
    return pl.pallas_call(
        _qkv_kernel,
        out_shape=(jax.ShapeDtypeStruct((t, Q_DIM), BF16),
                   jax.ShapeDtypeStruct((t, 4 * LANES), BF16),
                   jax.ShapeDtypeStruct((t, 4 * LANES), BF16)),
        grid=(t // tm,),
        in_specs=[
            pl.BlockSpec((tm, d), lambda i: (i, 0)),
            pl.BlockSpec((1, 6, d), sel),
            pl.BlockSpec((1, d), lambda i: (0, 0)),
            pl.BlockSpec((d, QKV_DIM), lambda i: (0, 0)),
            pl.BlockSpec((1, QKV_DIM), lambda i: (0, 0)),
            pl.BlockSpec((tm, LANES), lambda i: (i, 0)),
            pl.BlockSpec((tm, LANES), lambda i: (i, 0)),
        ],
        out_specs=(pl.BlockSpec((tm, Q_DIM), lambda i: (i, 0)),
                   pl.BlockSpec((tm, 4 * LANES), lambda i: (i, 0)),
                   pl.BlockSpec((tm, 4 * LANES), lambda i: (i, 0))),
        compiler_params=_cparams(("parallel",)),
        name="qkv_rope",
    )(xs, mod, gn, w, b, cos, sin)


def _attn_kernel(sink_ref, q_ref, kp_ref, kc_ref, kn_ref, kx_ref, vp_ref, vc_ref, vn_ref, vx_ref,
                 o_ref, *, cb, n_lat):
    n = pl.program_id(0)
    blk = ATTN_BLOCK
    c = kx_ref.shape[0]
    nw = 3 * blk
    r = lax.broadcasted_iota(jnp.int32, (blk, nw + c), 0)
    s = lax.broadcasted_iota(jnp.int32, (blk, nw + c), 1)
    q_pos = (n - cb) * blk + r
    k_pos = (n - cb - 1) * blk + s
    win_ok = (jnp.abs(q_pos - k_pos) <= WINDOW) & (k_pos >= 0) & (k_pos < n_lat) & (n >= cb)
    valid = (s >= nw) | win_ok
    k_all = jnp.concatenate([kp_ref[...], kc_ref[...], kn_ref[...], kx_ref[...]], axis=0)
    v_all = jnp.concatenate([vp_ref[...], vc_ref[...], vn_ref[...], vx_ref[...]], axis=0)
    nt = (((1,), (1,)), ((), ()))
    group = ATTN_HEADS // ATTN_KV_HEADS
    for p in range(ATTN_HEADS // 2):
        g = (2 * p) // group
        qp = q_ref[:, p * LANES:(p + 1) * LANES]
        o_pair = jnp.zeros((blk, LANES), F32)
        for half in range(2):
            col = (2 * g + half) * LANES
            kk = k_all[:, col:col + LANES]
            vv = v_all[:, col:col + LANES]
            sc = lax.dot_general(qp, kk, nt, preferred_element_type=F32)
            sc = jnp.where(valid, sc, NEG_BIG)
            sink = sink_ref[2 * p + half]
            m = jnp.maximum(jnp.max(sc, axis=-1, keepdims=True), sink)
            e = jnp.exp(sc - m)
            denom = jnp.sum(e, axis=-1, keepdims=True) + jnp.exp(sink - m)
            pv = jnp.dot(e.astype(BF16), vv, preferred_element_type=F32)
            o_pair = o_pair + pv / denom
        o_ref[:, p * LANES:(p + 1) * LANES] = o_pair.astype(o_ref.dtype)


def _attn_call(sink, q, k4, v4, c_len):
    t = q.shape[0]
    blk = ATTN_BLOCK
    cb = c_len // blk
    nb = t // blk
    n_lat = t - c_len
    last = nb - 1
    kw = 4 * LANES
    spec_q = pl.BlockSpec((blk, Q_DIM), lambda n: (n, 0))
    prev = pl.BlockSpec((blk, kw), lambda n: (jnp.maximum(n - 1, 0), 0))
    cur = pl.BlockSpec((blk, kw), lambda n: (n, 0))
    nxt = pl.BlockSpec((blk, kw), lambda n: (jnp.minimum(n + 1, last), 0))
    ctx = pl.BlockSpec((c_len, kw), lambda n: (0, 0))
    return pl.pallas_call(
        functools.partial(_attn_kernel, cb=cb, n_lat=n_lat),
        out_shape=jax.ShapeDtypeStruct((t, Q_DIM), BF16),
        grid=(nb,),
        in_specs=[pl.BlockSpec(memory_space=pltpu.SMEM), spec_q,
                  prev, cur, nxt, ctx, prev, cur, nxt, ctx],
        out_specs=pl.BlockSpec((blk, Q_DIM), lambda n: (n, 0)),
        compiler_params=_cparams(("parallel",)),
        name="window_attn",
    )(sink, q, k4, k4, k4, k4, v4, v4, v4, v4)


def _residual_tail(y, x, mod_ref, gn_ref, rw_ref, rb_ref, xo_ref, h2_ref, lg_ref):
    x_new = x + mod_ref[0, 2:3, :] * _rms(y, gn_ref[1:2, :])
    h2 = _rms(x_new, gn_ref[2:3, :]) * (1.0 + mod_ref[0, 4:5, :]) + mod_ref[0, 3:4, :]
    xo_ref[...] = x_new
    _store_tiled(h2_ref, h2)
    lg_ref[...] = jnp.dot(h2, rw_ref[...], precision=lax.Precision.HIGHEST,
                          preferred_element_type=F32) + rb_ref[...]


def _attn_out_kernel(a_ref, x_ref, mod_ref, gn_ref, w_ref, rw_ref, rb_ref, xo_ref, h2_ref, lg_ref):
    y = jnp.dot(a_ref[...], w_ref[...], preferred_element_type=F32)
    _residual_tail(y, x_ref[...], mod_ref, gn_ref, rw_ref, rb_ref, xo_ref, h2_ref, lg_ref)


def _hgrn_out_kernel(of_ref, ob_ref, gs_ref, hg_ref, x_ref, mod_ref, gn_ref, w_ref, rw_ref, rb_ref,
                     xo_ref, h2_ref, lg_ref):
    parts = []
    for h in range(HGRN_HEADS):
        sl = slice(h * HGRN_DK, (h + 1) * HGRN_DK)
        o = of_ref[:, sl] + ob_ref[:, sl]
        parts.append(_rms(o, hg_ref[:, sl]))
    a = jnp.concatenate(parts, axis=1) * gs_ref[...].astype(F32)
    y = jnp.dot(a.astype(BF16), w_ref[...], preferred_element_type=F32)
    _residual_tail(y, x_ref[...], mod_ref, gn_ref, rw_ref, rb_ref, xo_ref, h2_ref, lg_ref)


def _out_call(kind, acts, x, mod, gn, w, rw, rb, hg, cb, off):
    t_out = x.shape[0] - off * ROW_BLOCK
    d = D_MODEL
    tm = ROW_BLOCK
    row = lambda i: (i + off, 0)
    sel = lambda i: (jnp.minimum((i + off) // cb, 1), 0, 0)
    const = lambda i: (0, 0)
    common_specs = [
        pl.BlockSpec((tm, d), row),
        pl.BlockSpec((1, 6, d), sel),
        pl.BlockSpec((4, d), const),
        pl.BlockSpec((d, d), const),
        pl.BlockSpec((d, N_EXPERTS), const),
        pl.BlockSpec((1, N_EXPERTS), const),
    ]
    if kind == "attn":
        body = _attn_out_kernel
        in_specs = [pl.BlockSpec((tm, d), row)] + common_specs
        args = (acts[0], x, mod, gn, w, rw, rb)
    else:
        body = _hgrn_out_kernel
        in_specs = [pl.BlockSpec((tm, d), row)] * 3 + [pl.BlockSpec((1, d), const)] + common_specs
        args = (acts[0], acts[1], acts[2], hg, x, mod, gn, w, rw, rb)
    return pl.pallas_call(
        body,
        out_shape=(jax.ShapeDtypeStruct((t_out, d), F32),
                   jax.ShapeDtypeStruct((t_out * SEGS, LANES), F32),
                   jax.ShapeDtypeStruct((t_out, N_EXPERTS), F32)),
        grid=(t_out // tm,),
        in_specs=in_specs,
        out_specs=(pl.BlockSpec((tm, d), lambda i: (i, 0)),
                   pl.BlockSpec((tm * SEGS, LANES), lambda i: (i, 0)),
                   pl.BlockSpec((tm, N_EXPERTS), lambda i: (i, 0))),
        compiler_params=_cparams(("parallel",)),
        name=kind + "_out_residual",
    )(*args)


def _router_kernel(lg_ref, slab_ref, cnt_ref, run_ref):
    i = pl.program_id(0)

    @pl.when(i == 0)
    def _():
        run_ref[...] = jnp.zeros_like(run_ref)

    lg = lg_ref[...]
    tb = lg.shape[0]
    lane = lax.broadcasted_iota(jnp.int32, lg.shape, 1).astype(F32)
    sels, tops, idxs = [], [], []
    for _ in range(TOP_K):
        m = jnp.max(lg, axis=-1, keepdims=True)
        idx = jnp.min(jnp.where(lg == m, lane, float(N_EXPERTS)), axis=-1, keepdims=True)
        sel = lane == idx
        sels.append(sel)
        tops.append(m)
        idxs.append(idx)
        lg = jnp.where(sel, -jnp.inf, lg)
    ws = [jnp.exp(tk - tops[0]) for tk in tops]
    wsum = ws[0] + ws[1] + ws[2] + ws[3]
    chosen = (sels[0] | sels[1] | sels[2] | sels[3])
    onehot = jnp.where(chosen, 1.0, 0.0)
    rr = lax.broadcasted_iota(jnp.int32, (tb, tb), 0)
    cc = lax.broadcasted_iota(jnp.int32, (tb, tb), 1)
    tri = jnp.where(cc < rr, 1.0, 0.0).astype(BF16)
    before = jnp.dot(tri, onehot.astype(BF16), preferred_element_type=F32) + run_ref[...]
    out_lane = lax.broadcasted_iota(jnp.int32, (tb, LANES), 1)
    slab = jnp.zeros((tb, LANES), F32)
    for k in range(TOP_K):
        rank = jnp.sum(jnp.where(sels[k], before, 0.0), axis=-1, keepdims=True)
        slab = jnp.where(out_lane == k, idxs[k], slab)
        slab = jnp.where(out_lane == TOP_K + k, rank, slab)
        slab = jnp.where(out_lane == 2 * TOP_K + k, ws[k] / wsum, slab)
    slab_ref[...] = slab
    run_ref[...] = run_ref[...] + jnp.sum(onehot, axis=0, keepdims=True)
    cnt_ref[...] = run_ref[...]


def _router_call(logits):
    t = logits.shape[0]
    tb = ROW_BLOCK
    return pl.pallas_call(
        _router_kernel,
        out_shape=(jax.ShapeDtypeStruct((t, LANES), F32),
                   jax.ShapeDtypeStruct((1, N_EXPERTS), F32)),
        grid=(t // tb,),
        in_specs=[pl.BlockSpec((tb, N_EXPERTS), lambda i: (i, 0))],
        out_specs=(pl.BlockSpec((tb, LANES), lambda i: (i, 0)),
                   pl.BlockSpec((1, N_EXPERTS), lambda i: (0, 0))),
        scratch_shapes=[pltpu.VMEM((1, N_EXPERTS), F32)],
        compiler_params=_cparams(("arbitrary",)),
        name="router_topk",
    )(logits)


def _expert_kernel(be_ref, nu_ref, src0_ref, src1_ref, x_hbm, wgu_ref, bgu_ref, wdn_ref, bdn_ref, o_ref,
                   xbuf, sems, wgu_bf, wdn_bf):
    i = pl.program_id(0)
    rows = o_ref.shape[0] // SEGS
    slot = i % 2
    n_used = nu_ref[0]

    def gather(idx_ref, s):
        def issue(r, carry):
            pltpu.make_async_copy(_row_tile(x_hbm, idx_ref[0, 0, r]), _row_tile(xbuf.at[s], r),
                                  sems.at[s]).start()
            return carry

        lax.fori_loop(0, rows, issue, 0, unroll=8)

    @pl.when(i == 0)
    def _():
        gather(src0_ref, 0)

    @pl.when(i + 1 < n_used)
    def _():
        gather(src1_ref, 1 - slot)

    prev = be_ref[jnp.maximum(i - 1, 0)]
    fresh = (i == 0) | (be_ref[i] != prev)

    @pl.when(fresh)
    def _():
        wgu_bf[...] = wgu_ref[...].astype(BF16)
        wdn_bf[...] = wdn_ref[...].astype(BF16)

    @pl.when(i < n_used)
    def _():
        pltpu.make_async_copy(x_hbm.at[pl.ds(0, rows * SEGS), :], xbuf.at[slot], sems.at[slot]).wait()
        xb = _load_tiled(xbuf.at[slot], rows).astype(BF16)
        gu = jnp.dot(xb, wgu_bf[...], preferred_element_type=F32) + bgu_ref[...]
        g = jnp.minimum(gu[:, :D_FF], SWIGLU_LIMIT)
        u = jnp.clip(gu[:, D_FF:], -SWIGLU_LIMIT, SWIGLU_LIMIT)
        hdn = g * _sigmoid(SWIGLU_ALPHA * g) * (u + 1.0)
        y = jnp.dot(hdn.astype(BF16), wdn_bf[...], preferred_element_type=F32) + bdn_ref[...]
        _store_tiled(o_ref, y)

    @pl.when(i >= n_used)
    def _():
        o_ref[...] = jnp.zeros_like(o_ref)


def _expert_call(blk_e, n_used, src_tok, h2, layer, w_gu, b_gu, w_dn, b_dn):
    d = D_MODEL
    tr = MOE_ROWS
    n_blk = blk_e.shape[0]
    depth, e, _, f2 = w_gu.shape
    last = n_blk - 1
    return pl.pallas_call(
        _expert_kernel,
        out_shape=jax.ShapeDtypeStruct((n_blk * tr * SEGS, LANES), F32),
        grid_spec=pltpu.PrefetchScalarGridSpec(
            num_scalar_prefetch=2,
            grid=(n_blk,),
            in_specs=[
                pl.BlockSpec((1, 1, tr), lambda i, be, nu: (0, 0, 0), memory_space=pltpu.SMEM),
                pl.BlockSpec((1, 1, tr), lambda i, be, nu: (jnp.minimum(i + 1, last), 0, 0),
                             memory_space=pltpu.SMEM),
                pl.BlockSpec(memory_space=pl.ANY),
                pl.BlockSpec((None, None, d, f2), lambda i, be, nu: (layer, be[i], 0, 0)),
                pl.BlockSpec((None, None, 1, f2), lambda i, be, nu: (layer, be[i], 0, 0)),
                pl.BlockSpec((None, None, D_FF, d), lambda i, be, nu: (layer, be[i], 0, 0)),
                pl.BlockSpec((None, None, 1, d), lambda i, be, nu: (layer, be[i], 0, 0)),
            ],
            out_specs=pl.BlockSpec((tr * SEGS, LANES), lambda i, be, nu: (i, 0)),
            scratch_shapes=[pltpu.VMEM((2, tr * SEGS, LANES), F32), pltpu.SemaphoreType.DMA((2,)),
                            pltpu.VMEM((d, f2), BF16), pltpu.VMEM((D_FF, d), BF16)],
        ),
        compiler_params=_cparams(("arbitrary",), row_dma=True),
        name="moe_experts",
    )(blk_e, n_used, src_tok.reshape(n_blk, 1, tr), src_tok.reshape(n_blk, 1, tr), h2,
      w_gu, b_gu.reshape(depth, e, 1, f2), w_dn, b_dn.reshape(depth, e, 1, d))


def _combine_kernel(d0_ref, d1_ref, y_hbm, slab_ref, x_ref, mod_ref, gn_ref, o_ref, buf, sems):
    i = pl.program_id(0)
    tb = o_ref.shape[0]
    slot = i % 2

    def gather(idx_ref, s):
        def issue(r, carry):
            for k in range(TOP_K):
                row = idx_ref[0, 0, r * TOP_K + k]
                pltpu.make_async_copy(_row_tile(y_hbm, row), _row_tile(buf.at[s, k], r),
                                      sems.at[s]).start()
            return carry

        lax.fori_loop(0, tb, issue, 0, unroll=4)

    @pl.when(i == 0)
    def _():
        gather(d0_ref, 0)

    @pl.when(i + 1 < pl.num_programs(0))
    def _():
        gather(d1_ref, 1 - slot)

    for k in range(TOP_K):
        pltpu.make_async_copy(y_hbm.at[pl.ds(0, tb * SEGS), :], buf.at[slot, k], sems.at[slot]).wait()
    slab = slab_ref[...]
    y2 = jnp.zeros(o_ref.shape, F32)
    for k in range(TOP_K):
        y2 = y2 + _load_tiled(buf.at[slot, k], tb) * slab[:, 2 * TOP_K + k:2 * TOP_K + k + 1]
    o_ref[...] = x_ref[...] + mod_ref[0, 5:6, :] * _rms(y2, gn_ref[3:4, :])


def _combine_call(dest, y_sorted, slab, x, mod, gn, cb_rows):
    t, d = x.shape
    tb = COMBINE_ROWS
    nb = t // tb
    cbb = max(cb_rows // tb, 1)
    sel = (lambda i: (jnp.minimum(i // cbb, 1), 0, 0)) if cb_rows else (lambda i: (1, 0, 0))
    dest3 = dest.reshape(nb, 1, tb * TOP_K)
    return pl.pallas_call(
        _combine_kernel,
        out_shape=jax.ShapeDtypeStruct((t, d), F32),
        grid=(nb,),
        in_specs=[
            pl.BlockSpec((1, 1, tb * TOP_K), lambda i: (0, 0, 0), memory_space=pltpu.SMEM),
            pl.BlockSpec((1, 1, tb * TOP_K), lambda i: (jnp.minimum(i + 1, nb - 1), 0, 0),
                         memory_space=pltpu.SMEM),
            pl.BlockSpec(memory_space=pl.ANY),
            pl.BlockSpec((tb, LANES), lambda i: (i, 0)),
            pl.BlockSpec((tb, d), lambda i: (i, 0)),
            pl.BlockSpec((1, 6, d), sel),
            pl.BlockSpec((4, d), lambda i: (0, 0)),
        ],
        out_specs=pl.BlockSpec((tb, d), lambda i: (i, 0)),
        scratch_shapes=[pltpu.VMEM((2, TOP_K, tb * SEGS, LANES), F32), pltpu.SemaphoreType.DMA((2,))],
        compiler_params=_cparams(("arbitrary",), row_dma=True),
        name="moe_combine",
    )(dest3, dest3, y_sorted, slab, x, mod, gn)


def _moe(h2, logits, x, mod, gn, layer, w_gu, b_gu, w_dn, b_dn, cb_rows):
    t = logits.shape[0]
    tr = MOE_ROWS
    slab, counts = _router_call(logits)
    e_idx = slab[:, :TOP_K].astype(jnp.int32)
    rank = slab[:, TOP_K:2 * TOP_K].astype(jnp.int32)
    counts = counts[0].astype(jnp.int32)
    padded = (counts + tr - 1) // tr * tr
    p_end = jnp.cumsum(padded)
    p_start = p_end - padded
    dest = p_start[e_idx] + rank
    a = t * TOP_K
    n_blk = (a + N_EXPERTS * (tr - 1) + tr - 1) // tr
    src_tok = jnp.zeros((n_blk * tr,), jnp.int32).at[dest.reshape(-1)].set(
        jnp.arange(a, dtype=jnp.int32) // TOP_K, unique_indices=True)
    blk_start = jnp.arange(n_blk, dtype=jnp.int32) * tr
    blk_e = jnp.sum((p_end[None, :] <= blk_start[:, None]).astype(jnp.int32), axis=1)
    blk_e = jnp.minimum(blk_e, N_EXPERTS - 1)
    n_used = (p_end[-1] // tr).astype(jnp.int32).reshape(1)
    ys = _expert_call(blk_e, n_used, src_tok, h2, layer, w_gu, b_gu, w_dn, b_dn)
    return _combine_call(dest, ys, slab, x, mod, gn, cb_rows)


def _hgrn_in_kernel(x_ref, mod_ref, gn_ref, w_ref, lb_ref, q_ref, kf_ref, lf_ref, kb_ref, lbw_ref,
                    i_ref, gs_ref):
    x = x_ref[...]
    h = _rms(x, gn_ref[...]) * (1.0 + mod_ref[0, 1:2, :]) + mod_ref[0, 0:1, :]
    hb = h.astype(BF16)
    f = D_MODEL

    def proj(sec):
        return jnp.dot(hb, w_ref[:, sec * f:(sec + 1) * f], preferred_element_type=F32)

    z = proj(0)
    q_ref[...] = (z * _sigmoid(z)).astype(q_ref.dtype)
    for sec, k_ref, l_ref in ((1, kf_ref, lf_ref), (2, kb_ref, lbw_ref)):
        z = proj(sec)
        lb = lb_ref[sec - 1:sec, :]
        sg = _sigmoid(z)
        l_ref[...] = jnp.log(lb + (1.0 - lb) * sg)
        k_ref[...] = ((1.0 - lb) * _sigmoid(-z)).astype(k_ref.dtype)
    i_ref[...] = proj(3).astype(i_ref.dtype)
    z = proj(4)
    gs_ref[...] = (z * _sigmoid(z)).astype(gs_ref.dtype)


def _hgrn_in_call(xs, mod, gn, w, lb, cb):
    t, d = xs.shape
    tm = ROW_BLOCK
    sel = lambda i: (jnp.minimum(i // cb, 1), 0, 0)
    row = pl.BlockSpec((tm, d), lambda i: (i, 0))
    dts = (BF16, BF16, F32, BF16, F32, BF16, BF16)
    return pl.pallas_call(
        _hgrn_in_kernel,
        out_shape=tuple(jax.ShapeDtypeStruct((t, d), dt) for dt in dts),
        grid=(t // tm,),
        in_specs=[row, pl.BlockSpec((1, 6, d), sel), pl.BlockSpec((1, d), lambda i: (0, 0)),
                  pl.BlockSpec(w.shape, lambda i: (0, 0)), pl.BlockSpec((2, d), lambda i: (0, 0))],
        out_specs=tuple(row for _ in dts),
        compiler_params=_cparams(("parallel",)),
        name="hgrn_in_proj",
    )(xs, mod, gn, w, lb)


def _split3(x):
    hi = x.astype(BF16)
    r1 = x - hi.astype(F32)
    mid = r1.astype(BF16)
    lo = (r1 - mid.astype(F32)).astype(BF16)
    return hi, mid, lo


def _scan_kernel(q_ref, k_ref, g_ref, v_ref, o_ref, st_ref, gcum_ref, *, reverse):
    j = pl.program_id(0)
    ch, sub = HGRN_CHUNK, HGRN_SUB
    ns = ch // sub
    dk = HGRN_DK

    @pl.when(j == 0)
    def _():
        st_ref[...] = jnp.zeros_like(st_ref)

    rr = lax.broadcasted_iota(jnp.int32, (ch, ch), 0)
    cc = lax.broadcasted_iota(jnp.int32, (ch, ch), 1)
    causal = (cc >= rr) if reverse else (cc <= rr)
    tri = jnp.where(causal, 1.0, 0.0).astype(BF16)
    hi, mid, lo = _split3(g_ref[...])
    gcum_ref[...] = (jnp.dot(tri, hi, preferred_element_type=F32)
                     + jnp.dot(tri, mid, preferred_element_type=F32)
                     + jnp.dot(tri, lo, preferred_element_type=F32))
    same_sub = (rr // sub) == (cc // sub)
    earlier = ((cc // sub) > (rr // sub)) if reverse else ((cc // sub) < (rr // sub))
    diag_mask = same_sub & causal
    lane_blk = lax.broadcasted_iota(jnp.int32, (ch, ns * dk), 1) // dk
    row_blk = lax.broadcasted_iota(jnp.int32, (ch, ns * dk), 0) // sub
    kbd_mask = lane_blk == row_blk
    kdg_mask = (lax.broadcasted_iota(jnp.int32, (ch, sub * dk), 1) // dk
                == lax.broadcasted_iota(jnp.int32, (ch, sub * dk), 0) % sub)
    end_row = 0 if reverse else ch - 1
    nt = (((1,), (1,)), ((), ()))
    tn = (((0,), (0,)), ((), ()))

    def head(h, carry):
        hs = pl.ds(pl.multiple_of(h * dk, dk), dk)
        q = q_ref[:, hs].astype(F32)
        kb = k_ref[:, hs]
        k = kb.astype(F32)
        v = v_ref[:, hs]
        gc = gcum_ref[:, hs]
        st = st_ref[h]
        g_end = gc[end_row:end_row + 1, :]
        qg = (q * jnp.exp(gc)).astype(BF16)
        o = lax.dot_general(qg, st.astype(BF16), nt, preferred_element_type=F32)
        k_end = (k * jnp.exp(g_end - gc)).astype(BF16)
        st_ref[h] = st * jnp.exp(g_end) + lax.dot_general(v, k_end, tn, preferred_element_type=F32)
        gc4 = gc.reshape(ns, sub, dk)
        edge = (sub - 1) if not reverse else 0
        g_edge = jnp.broadcast_to(gc4[:, edge:edge + 1, :], (ns, sub, dk)).reshape(ch, dk)
        k_rel = (k * jnp.exp(g_edge - gc)).astype(BF16)
        k_bd = jnp.where(kbd_mask, jnp.concatenate([k_rel] * ns, axis=1), jnp.zeros((), BF16))
        q_parts = []
        for jb in range(ns):
            row = jb * sub + edge
            q_parts.append((q * jnp.exp(jnp.minimum(gc - gc[row:row + 1, :], 0.0))).astype(BF16))
        q_cat = jnp.concatenate(q_parts, axis=1)
        s_off = lax.dot_general(q_cat, k_bd, nt, preferred_element_type=F32)
        p_parts = []
        for s in range(sub):
            g_s = jnp.broadcast_to(gc4[:, s:s + 1, :], (ns, sub, dk)).reshape(ch, dk)
            p_parts.append((q * jnp.exp(jnp.minimum(gc - g_s, 0.0))).astype(BF16))
        p_cat = jnp.concatenate(p_parts, axis=1)
        k_dg = jnp.where(kdg_mask, jnp.concatenate([kb] * sub, axis=1), jnp.zeros((), BF16))
        s_diag = lax.dot_general(p_cat, k_dg, nt, preferred_element_type=F32)
        a = jnp.where(earlier, s_off, 0.0) + jnp.where(diag_mask, s_diag, 0.0)
        o = o + jnp.dot(a.astype(BF16), v, preferred_element_type=F32)
        o_ref[:, hs] = o
        return carry

    lax.fori_loop(0, HGRN_HEADS, head, 0, unroll=2)


def _scan_call(q, k, lg, v, c_len, reverse):
    t, d = q.shape
    ch = HGRN_CHUNK
    n_ch = t // ch
    cc = c_len // ch
    if reverse:
        idx = lambda j: (jnp.where(j < cc, cc - 1 - j, n_ch - 1 - (j - cc)), 0)
    else:
        idx = lambda j: (j, 0)
    blk = pl.BlockSpec((ch, d), idx)
    return pl.pallas_call(
        functools.partial(_scan_kernel, reverse=reverse),
        out_shape=jax.ShapeDtypeStruct((t, d), F32),
        grid=(n_ch,),
        in_specs=[blk, blk, blk, blk],
        out_specs=blk,
        scratch_shapes=[pltpu.VMEM((HGRN_HEADS, HGRN_DK, HGRN_DK), F32), pltpu.VMEM((ch, d), F32)],
        compiler_params=_cparams(("arbitrary",)),
        name="hgrn_scan_bwd" if reverse else "hgrn_scan_fwd",
    )(q, k, lg, v)


def _rope_tables(l, c_len):
    n = HEAD_DIM // 4
    inv_freq = ROPE_BASE ** (-jnp.arange(n, dtype=F32) / n)
    pos = jnp.arange(l, dtype=jnp.int32)
    ang_row = (pos // GRID_W).astype(F32)[:, None] * inv_freq[None, :]
    ang_col = (pos % GRID_W).astype(F32)[:, None] * inv_freq[None, :]
    cr, sr, cc, sc = jnp.cos(ang_row), jnp.sin(ang_row), jnp.cos(ang_col), jnp.sin(ang_col)
    cos = jnp.concatenate([cr, cr, cc, cc] * 2, axis=1)
    sin = jnp.concatenate([-sr, sr, -sc, sc] * 2, axis=1)
    cos = jnp.concatenate([jnp.ones((c_len, LANES), F32), cos], axis=0)
    sin = jnp.concatenate([jnp.zeros((c_len, LANES), F32), sin], axis=0)
    return cos, sin


def kernel(x, c, ctx, c_ctx, ada_w, ada_b, norm_g, attn_w_qkv, attn_b_qkv, attn_sink, attn_w_o,
           hgrn_w_in, hgrn_lb, hgrn_norm_g, hgrn_w_o, router_w, router_b, moe_w_gu, moe_b_gu,
           moe_w_dn, moe_b_dn):
    b, l, d = x.shape
    c_len = ctx.shape[1]
    depth = ada_w.shape[0]
    assert b == 1 and d == D_MODEL and depth == 2
    assert c_len % ROW_BLOCK == 0 and l % ROW_BLOCK == 0
    cb = c_len // ROW_BLOCK

    cs = jnp.zeros((8, d), F32).at[0].set(c_ctx).at[1].set(c[0])
    mods = _ada_call(cs, ada_w, ada_b)
    mods = mods[:, :2].reshape(depth, 2, 6, d)

    xs = jnp.concatenate([ctx[0], x[0]], axis=0)
    cos, sin = _rope_tables(l, c_len)

    q, k4, v4 = _qkv_call(xs, mods[0], norm_g[0, 0:1], attn_w_qkv[0].astype(BF16),
                          attn_b_qkv[0].reshape(1, -1), cos, sin, cb)
    o = _attn_call(attn_sink[0], q, k4, v4, c_len)
    xs, h2, logits = _out_call("attn", (o,), xs, mods[0], norm_g[0], attn_w_o[0].astype(BF16),
                               router_w[0], router_b[0].reshape(1, -1), None, cb, 0)
    xs = _moe(h2, logits, xs, mods[0], norm_g[0], 0, moe_w_gu, moe_b_gu, moe_w_dn, moe_b_dn, c_len)

    lb_soft = jax.nn.softmax(hgrn_lb.astype(F32), axis=0)
    lb = jnp.cumsum(lb_soft, axis=0)[1] - lb_soft[0]
    qh, kf, lf, kb, lbw, iv, gs = _hgrn_in_call(xs, mods[1], norm_g[1, 0:1], hgrn_w_in[0].astype(BF16),
                                                lb, cb)
    o_f = _scan_call(qh, kf, lf, iv, c_len, reverse=False)
    o_b = _scan_call(qh, kb, lbw, iv, c_len, reverse=True)
    x_lat, h2, logits = _out_call("hgrn", (o_f, o_b, gs), xs, mods[1], norm_g[1],
                                  hgrn_w_o[0].astype(BF16), router_w[1], router_b[1].reshape(1, -1),
                                  hgrn_norm_g[0].reshape(1, -1), cb, cb)
    out = _moe(h2, logits, x_lat, mods[1], norm_g[1], 1, moe_w_gu, moe_b_gu, moe_w_dn, moe_b_dn, 0)
    return out[None]
```

```python
import functools

import jax
import jax.numpy as jnp
from jax import lax
from jax.experimental import pallas as pl
from jax.experimental.pallas import tpu as pltpu

D_MODEL = 1024
GRID_W = 64
RMS_EPS = 1e-6

ATTN_HEADS = 16
ATTN_KV_HEADS = 2
HEAD_DIM = 64
Q_DIM = ATTN_HEADS * HEAD_DIM
KV_DIM = ATTN_KV_HEADS * HEAD_DIM
QKV_DIM = Q_DIM + 2 * KV_DIM
WINDOW = 128
ATTN_BLOCK = 128
ROPE_BASE = 10000.0

HGRN_HEADS = 8
HGRN_DK = 128
HGRN_CHUNK = 64
HGRN_SUB = 8

N_EXPERTS = 32
TOP_K = 4
D_FF = 1024
SWIGLU_LIMIT = 7.0
SWIGLU_ALPHA = 1.702

LANES = 128
ROW_BLOCK = 256
MOE_ROWS = 256
COMBINE_ROWS = 128
VMEM_LIMIT = 56 * 1024 * 1024

F32 = jnp.float32
BF16 = jnp.bfloat16
NEG_BIG = -1e30
LOG2E = 1.4426950408889634


def _cparams(sem, row_dma=False):
    return pltpu.CompilerParams(dimension_semantics=sem, vmem_limit_bytes=VMEM_LIMIT,
                                disable_bounds_checks=row_dma)


def _rms(x, g):
    return x * lax.rsqrt(jnp.mean(x * x, axis=-1, keepdims=True) + RMS_EPS) * g


def _sigmoid(x):
    return 1.0 / (1.0 + jnp.exp(-x))


SEGS = D_MODEL // LANES


def _store_tiled(ref, val):
    n = val.shape[0]
    for s in range(SEGS):
        ref[pl.ds(s, n, stride=SEGS), :] = val[:, s * LANES:(s + 1) * LANES]


def _load_tiled(ref, n):
    return jnp.concatenate([ref[pl.ds(s, n, stride=SEGS), :] for s in range(SEGS)], axis=1)


def _row_tile(ref, r):
    return ref.at[pl.ds(pl.multiple_of(r * SEGS, SEGS), SEGS), :]


def _ada_kernel(c_ref, w_ref, b_ref, o_ref):
    c = c_ref[...]
    s = c * _sigmoid(c)
    o_ref[0] = jnp.dot(s, w_ref[0], precision=lax.Precision.HIGHEST,
                       preferred_element_type=F32) + b_ref[0]


def _ada_call(cs, ada_w, ada_b):
    depth, d, n = ada_w.shape
    tn = 1024
    return pl.pallas_call(
        _ada_kernel,
        out_shape=jax.ShapeDtypeStruct((depth, 8, n), F32),
        grid=(depth, n // tn),
        in_specs=[
            pl.BlockSpec((8, d), lambda i, j: (0, 0)),
            pl.BlockSpec((1, d, tn), lambda i, j: (i, 0, j)),
            pl.BlockSpec((1, 1, tn), lambda i, j: (i, 0, j)),
        ],
        out_specs=pl.BlockSpec((1, 8, tn), lambda i, j: (i, 0, j)),
        compiler_params=_cparams(("arbitrary", "arbitrary")),
        name="adaln",
    )(cs, ada_w, ada_b.reshape(depth, 1, n))


def _swap16(t):
    lane = lax.broadcasted_iota(jnp.int32, t.shape, 1)
    return jnp.where(lane % 32 < 16, pltpu.roll(t, LANES - 16, 1), pltpu.roll(t, 16, 1))


def _qkv_kernel(x_ref, mod_ref, gn_ref, w_ref, b_ref, cos_ref, sin_ref, q_ref, k4_ref, v4_ref):
    x = x_ref[...]
    h = _rms(x, gn_ref[...]) * (1.0 + mod_ref[0, 1:2, :]) + mod_ref[0, 0:1, :]
    hb = h.astype(BF16)
    cos = cos_ref[...]
    sin = sin_ref[...]
    nq = Q_DIM // LANES
    for j in range(nq + 1):
        sl = slice(j * LANES, (j + 1) * LANES)
        t = jnp.dot(hb, w_ref[:, sl], preferred_element_type=F32) + b_ref[:, sl]
        t = t * cos + _swap16(t) * sin
        if j < nq:
            q_ref[:, sl] = (t * (HEAD_DIM ** -0.5 * LOG2E)).astype(q_ref.dtype)
        else:
            kt = t
    sl = slice(Q_DIM + KV_DIM, QKV_DIM)
    vt = jnp.dot(hb, w_ref[:, sl], preferred_element_type=F32) + b_ref[:, sl]
    lo = lax.broadcasted_iota(jnp.int32, kt.shape, 1) < HEAD_DIM
    for t, ref in ((kt, k4_ref), (vt, v4_ref)):
        sw = pltpu.roll(t, HEAD_DIM, 1)
        ref[:, 0 * LANES:1 * LANES] = jnp.where(lo, t, 0.0).astype(ref.dtype)
        ref[:, 1 * LANES:2 * LANES] = jnp.where(lo, 0.0, sw).astype(ref.dtype)
        ref[:, 2 * LANES:3 * LANES] = jnp.where(lo, sw, 0.0).astype(ref.dtype)
        ref[:, 3 * LANES:4 * LANES] = jnp.where(lo, 0.0, t).astype(ref.dtype)


def _qkv_call(xs, mod, gn, w, b, cos, sin, cb):
    t, d = xs.shape
    tm = ROW_BLOCK
    sel = lambda i: (jnp.minimum(i // cb, 1), 0, 0)
    return pl.pallas_call(
        _qkv_kernel,
        out_shape=(jax.ShapeDtypeStruct((t, Q_DIM), BF16),
                   jax.ShapeDtypeStruct((t, 4 * LANES), BF16),
                   jax.ShapeDtypeStruct((t, 4 * LANES), BF16)),
        grid=(t // tm,),
        in_specs=[
            pl.BlockSpec((tm, d), lambda i: (i, 0)),
            pl.BlockSpec((1, 6, d), sel),
            pl.BlockSpec((1, d), lambda i: (0, 0)),
            pl.BlockSpec((d, QKV_DIM), lambda i: (0, 0)),
            pl.BlockSpec((1, QKV_DIM), lambda i: (0, 0)),
            pl.BlockSpec((tm, LANES), lambda i: (i, 0)),
            pl.BlockSpec((tm, LANES), lambda i: (i, 0)),
        ],
        out_specs=(pl.BlockSpec((tm, Q_DIM), lambda i: (i, 0)),
                   pl.BlockSpec((tm, 4 * LANES), lambda i: (i, 0)),
                   pl.BlockSpec((tm, 4 * LANES), lambda i: (i, 0))),
        compiler_params=_cparams(("parallel",)),
        name="qkv_rope",
    )(xs, mod, gn, w, b, cos, sin)


def _attn_kernel(sink_ref, q_ref, kp_ref, kc_ref, kn_ref, kx_ref, vp_ref, vc_ref, vn_ref, vx_ref,
                 o_ref, *, cb, n_lat):
    n = pl.program_id(0)
    blk = ATTN_BLOCK
    c = kx_ref.shape[0]
    nw = 3 * blk
    s = lax.broadcasted_iota(jnp.int32, (nw + c, blk), 0)
    r = lax.broadcasted_iota(jnp.int32, (nw + c, blk), 1)
    q_pos = (n - cb) * blk + r
    k_pos = (n - cb - 1) * blk + s
    win_ok = (jnp.abs(q_pos - k_pos) <= WINDOW) & (k_pos >= 0) & (k_pos < n_lat) & (n >= cb)
    valid = (s >= nw) | win_ok
    k_all = jnp.concatenate([kp_ref[...], kc_ref[...], kn_ref[...], kx_ref[...]], axis=0)
    v_all = jnp.concatenate([vp_ref[...], vc_ref[...], vn_ref[...], vx_ref[...]], axis=0)
    n_var = 2 * ATTN_KV_HEADS
    v_t = [v_all[:, j * LANES:(j + 1) * LANES].astype(F32).T.astype(BF16) for j in range(n_var)]
    nt = (((1,), (1,)), ((), ()))
    group = ATTN_HEADS // ATTN_KV_HEADS
    for p in range(ATTN_HEADS // 2):
        g = (2 * p) // group
        qp = q_ref[:, p * LANES:(p + 1) * LANES]
        o_t = jnp.zeros((LANES, blk), F32)
        for half in range(2):
            j = 2 * g + half
            sc = lax.dot_general(k_all[:, j * LANES:(j + 1) * LANES], qp, nt,
                                 preferred_element_type=F32)
            sc = jnp.where(valid, sc, NEG_BIG)
            sink = sink_ref[2 * p + half] * LOG2E
            m = jnp.maximum(jnp.max(sc, axis=0, keepdims=True), sink)
            e = jnp.exp2(sc - m)
            denom = jnp.sum(e, axis=0, keepdims=True) + jnp.exp2(sink - m)
            pv = jnp.dot(v_t[j], e.astype(BF16), preferred_element_type=F32)
            o_t = o_t + pv / denom
        o_ref[:, p * LANES:(p + 1) * LANES] = o_t.T.astype(o_ref.dtype)


def _attn_call(sink, q, k4, v4, c_len):
    t = q.shape[0]
    blk = ATTN_BLOCK
    cb = c_len // blk
    nb = t // blk
    n_lat = t - c_len
    last = nb - 1
    kw = 4 * LANES
    spec_q = pl.BlockSpec((blk, Q_DIM), lambda n: (n, 0))
    prev = pl.BlockSpec((blk, kw), lambda n: (jnp.maximum(n - 1, 0), 0))
    cur = pl.BlockSpec((blk, kw), lambda n: (n, 0))
    nxt = pl.BlockSpec((blk, kw), lambda n: (jnp.minimum(n + 1, last), 0))
    ctx = pl.BlockSpec((c_len, kw), lambda n: (0, 0))
    return pl.pallas_call(
        functools.partial(_attn_kernel, cb=cb, n_lat=n_lat),
        out_shape=jax.ShapeDtypeStruct((t, Q_DIM), BF16),
        grid=(nb,),
        in_specs=[pl.BlockSpec(memory_space=pltpu.SMEM), spec_q,
                  prev, cur, nxt, ctx, prev, cur, nxt, ctx],
        out_specs=pl.BlockSpec((blk, Q_DIM), lambda n: (n, 0)),
        compiler_params=_cparams(("parallel",)),
        name="window_attn",
    )(sink, q, k4, k4, k4, k4, v4, v4, v4, v4)


def _residual_tail(y, x, mod_ref, gn_ref, rw_ref, rb_ref, xo_ref, h2_ref, lg_ref):
    x_new = x + mod_ref[0, 2:3, :] * _rms(y, gn_ref[1:2, :])
    h2 = _rms(x_new, gn_ref[2:3, :]) * (1.0 + mod_ref[0, 4:5, :]) + mod_ref[0, 3:4, :]
    xo_ref[...] = x_new
    _store_tiled(h2_ref, h2)
    hi, mid, lo = _split3(h2)
    lane = lax.broadcasted_iota(jnp.int32, (h2.shape[0], LANES), 1)
    w = rw_ref[...]
    r = (jnp.dot(hi, w, preferred_element_type=F32)
         + jnp.where(lane < 2 * N_EXPERTS, jnp.dot(mid, w, preferred_element_type=F32), 0.0)
         + jnp.where(lane < N_EXPERTS, jnp.dot(lo, w, preferred_element_type=F32), 0.0))
    r = r + pltpu.roll(r, LANES - N_EXPERTS, 1) + pltpu.roll(r, LANES - 2 * N_EXPERTS, 1)
    lg_ref[...] = r[:, :N_EXPERTS] + rb_ref[...]


def _attn_out_kernel(a_ref, x_ref, mod_ref, gn_ref, w_ref, rw_ref, rb_ref, xo_ref, h2_ref, lg_ref):
    y = jnp.dot(a_ref[...], w_ref[...], preferred_element_type=F32)
    _residual_tail(y, x_ref[...], mod_ref, gn_ref, rw_ref, rb_ref, xo_ref, h2_ref, lg_ref)


def _hgrn_out_kernel(of_ref, ob_ref, gs_ref, hg_ref, x_ref, mod_ref, gn_ref, w_ref, rw_ref, rb_ref,
                     xo_ref, h2_ref, lg_ref):
    parts = []
    for h in range(HGRN_HEADS):
        sl = slice(h * HGRN_DK, (h + 1) * HGRN_DK)
        o = of_ref[:, sl] + ob_ref[:, sl]
        parts.append(_rms(o, hg_ref[:, sl]))
    a = jnp.concatenate(parts, axis=1) * gs_ref[...].astype(F32)
    y = jnp.dot(a.astype(BF16), w_ref[...], preferred_element_type=F32)
    _residual_tail(y, x_ref[...], mod_ref, gn_ref, rw_ref, rb_ref, xo_ref, h2_ref, lg_ref)


def _out_call(kind, acts, x, mod, gn, w, rw, rb, hg, cb, off):
    t_out = x.shape[0] - off * ROW_BLOCK
    d = D_MODEL
    tm = ROW_BLOCK
    w_hi, w_mid, w_lo = _split3(rw)
    rw = jnp.concatenate([w_hi, w_mid, w_lo, jnp.zeros_like(w_hi)], axis=1)
    row = lambda i: (i + off, 0)
    sel = lambda i: (jnp.minimum((i + off) // cb, 1), 0, 0)
    const = lambda i: (0, 0)
    common_specs = [
        pl.BlockSpec((tm, d), row),
        pl.BlockSpec((1, 6, d), sel),
        pl.BlockSpec((4, d), const),
        pl.BlockSpec((d, d), const),
        pl.BlockSpec((d, LANES), const),
        pl.BlockSpec((1, N_EXPERTS), const),
    ]
    if kind == "attn":
        body = _attn_out_kernel
        in_specs = [pl.BlockSpec((tm, d), row)] + common_specs
        args = (acts[0], x, mod, gn, w, rw, rb)
    else:
        body = _hgrn_out_kernel
        in_specs = [pl.BlockSpec((tm, d), row)] * 3 + [pl.BlockSpec((1, d), const)] + common_specs
        args = (acts[0], acts[1], acts[2], hg, x, mod, gn, w, rw, rb)
    return pl.pallas_call(
        body,
        out_shape=(jax.ShapeDtypeStruct((t_out, d), F32),
                   jax.ShapeDtypeStruct((t_out * SEGS, LANES), F32),
                   jax.ShapeDtypeStruct((t_out, N_EXPERTS), F32)),
        grid=(t_out // tm,),
        in_specs=in_specs,
        out_specs=(pl.BlockSpec((tm, d), lambda i: (i, 0)),
                   pl.BlockSpec((tm * SEGS, LANES), lambda i: (i, 0)),
                   pl.BlockSpec((tm, N_EXPERTS), lambda i: (i, 0))),
        compiler_params=_cparams(("parallel",)),
        name=kind + "_out_residual",
    )(*args)


def _router_kernel(lg_ref, slab_ref, cnt_ref, run_ref):
    i = pl.program_id(0)

    @pl.when(i == 0)
    def _():
        run_ref[...] = jnp.zeros_like(run_ref)

    lg = lg_ref[...]
    tb = lg.shape[0]
    lane = lax.broadcasted_iota(jnp.int32, lg.shape, 1).astype(F32)
    sels, tops, idxs = [], [], []
    for _ in range(TOP_K):
        m = jnp.max(lg, axis=-1, keepdims=True)
        idx = jnp.min(jnp.where(lg == m, lane, float(N_EXPERTS)), axis=-1, keepdims=True)
        sel = lane == idx
        sels.append(sel)
        tops.append(m)
        idxs.append(idx)
        lg = jnp.where(sel, -jnp.inf, lg)
    ws = [jnp.exp(tk - tops[0]) for tk in tops]
    wsum = ws[0] + ws[1] + ws[2] + ws[3]
    chosen = (sels[0] | sels[1] | sels[2] | sels[3])
    onehot = jnp.where(chosen, 1.0, 0.0)
    rr = lax.broadcasted_iota(jnp.int32, (tb, tb), 0)
    cc = lax.broadcasted_iota(jnp.int32, (tb, tb), 1)
    tri = jnp.where(cc < rr, 1.0, 0.0).astype(BF16)
    before = jnp.dot(tri, onehot.astype(BF16), preferred_element_type=F32) + run_ref[...]
    out_lane = lax.broadcasted_iota(jnp.int32, (tb, LANES), 1)
    slab = jnp.zeros((tb, LANES), F32)
    for k in range(TOP_K):
        rank = jnp.sum(jnp.where(sels[k], before, 0.0), axis=-1, keepdims=True)
        slab = jnp.where(out_lane == k, idxs[k], slab)
        slab = jnp.where(out_lane == TOP_K + k, rank, slab)
        slab = jnp.where(out_lane == 2 * TOP_K + k, ws[k] / wsum, slab)
    slab_ref[...] = slab
    run_ref[...] = run_ref[...] + jnp.sum(onehot, axis=0, keepdims=True)
    cnt_ref[...] = run_ref[...]


def _router_call(logits):
    t = logits.shape[0]
    tb = ROW_BLOCK
    return pl.pallas_call(
        _router_kernel,
        out_shape=(jax.ShapeDtypeStruct((t, LANES), F32),
                   jax.ShapeDtypeStruct((1, N_EXPERTS), F32)),
        grid=(t // tb,),
        in_specs=[pl.BlockSpec((tb, N_EXPERTS), lambda i: (i, 0))],
        out_specs=(pl.BlockSpec((tb, LANES), lambda i: (i, 0)),
                   pl.BlockSpec((1, N_EXPERTS), lambda i: (0, 0))),
        scratch_shapes=[pltpu.VMEM((1, N_EXPERTS), F32)],
        compiler_params=_cparams(("arbitrary",)),
        name="router_topk",
    )(logits)


def _expert_kernel(be_ref, nu_ref, src0_ref, src1_ref, x_hbm, wgu_ref, bgu_ref, wdn_ref, bdn_ref, o_ref,
                   xbuf, sems, wgu_bf, wdn_bf):
    i = pl.program_id(0)
    rows = o_ref.shape[0] // SEGS
    slot = i % 2
    n_used = nu_ref[0]

    def gather(idx_ref, s):
        def issue(r, carry):
            pltpu.make_async_copy(_row_tile(x_hbm, idx_ref[0, 0, r]), _row_tile(xbuf.at[s], r),
                                  sems.at[s]).start()
            return carry

        lax.fori_loop(0, rows, issue, 0, unroll=8)

    @pl.when(i == 0)
    def _():
        gather(src0_ref, 0)

    @pl.when(i + 1 < n_used)
    def _():
        gather(src1_ref, 1 - slot)

    prev = be_ref[jnp.maximum(i - 1, 0)]
    fresh = (i == 0) | (be_ref[i] != prev)

    @pl.when(fresh)
    def _():
        wgu_bf[...] = wgu_ref[...].astype(BF16)
        wdn_bf[...] = wdn_ref[...].astype(BF16)

    @pl.when(i < n_used)
    def _():
        pltpu.make_async_copy(x_hbm.at[pl.ds(0, rows * SEGS), :], xbuf.at[slot], sems.at[slot]).wait()
        xb = _load_tiled(xbuf.at[slot], rows).astype(BF16)
        gu = jnp.dot(xb, wgu_bf[...], preferred_element_type=F32) + bgu_ref[...]
        g = jnp.minimum(gu[:, :D_FF], SWIGLU_LIMIT)
        u = jnp.clip(gu[:, D_FF:], -SWIGLU_LIMIT, SWIGLU_LIMIT)
        hdn = g * _sigmoid(SWIGLU_ALPHA * g) * (u + 1.0)
        y = jnp.dot(hdn.astype(BF16), wdn_bf[...], preferred_element_type=F32) + bdn_ref[...]
        _store_tiled(o_ref, y)

    @pl.when(i >= n_used)
    def _():
        o_ref[...] = jnp.zeros_like(o_ref)


def _expert_call(blk_e, n_used, src_tok, h2, layer, w_gu, b_gu, w_dn, b_dn):
    d = D_MODEL
    tr = MOE_ROWS
    n_blk = blk_e.shape[0]
    depth, e, _, f2 = w_gu.shape
    last = n_blk - 1
    return pl.pallas_call(
        _expert_kernel,
        out_shape=jax.ShapeDtypeStruct((n_blk * tr * SEGS, LANES), F32),
        grid_spec=pltpu.PrefetchScalarGridSpec(
            num_scalar_prefetch=2,
            grid=(n_blk,),
            in_specs=[
                pl.BlockSpec((1, 1, tr), lambda i, be, nu: (0, 0, 0), memory_space=pltpu.SMEM),
                pl.BlockSpec((1, 1, tr), lambda i, be, nu: (jnp.minimum(i + 1, last), 0, 0),
                             memory_space=pltpu.SMEM),
                pl.BlockSpec(memory_space=pl.ANY),
                pl.BlockSpec((None, None, d, f2), lambda i, be, nu: (layer, be[i], 0, 0)),
                pl.BlockSpec((None, None, 1, f2), lambda i, be, nu: (layer, be[i], 0, 0)),
                pl.BlockSpec((None, None, D_FF, d), lambda i, be, nu: (layer, be[i], 0, 0)),
                pl.BlockSpec((None, None, 1, d), lambda i, be, nu: (layer, be[i], 0, 0)),
            ],
            out_specs=pl.BlockSpec((tr * SEGS, LANES), lambda i, be, nu: (i, 0)),
            scratch_shapes=[pltpu.VMEM((2, tr * SEGS, LANES), F32), pltpu.SemaphoreType.DMA((2,)),
                            pltpu.VMEM((d, f2), BF16), pltpu.VMEM((D_FF, d), BF16)],
        ),
        compiler_params=_cparams(("arbitrary",), row_dma=True),
        name="moe_experts",
    )(blk_e, n_used, src_tok.reshape(n_blk, 1, tr), src_tok.reshape(n_blk, 1, tr), h2,
      w_gu, b_gu.reshape(depth, e, 1, f2), w_dn, b_dn.reshape(depth, e, 1, d))


def _combine_kernel(d0_ref, d1_ref, y_hbm, slab_ref, x_ref, mod_ref, gn_ref, o_ref, buf, sems):
    i = pl.program_id(0)
    tb = o_ref.shape[0]
    slot = i % 2

    def gather(idx_ref, s):
        def issue(r, carry):
            for k in range(TOP_K):
                row = idx_ref[0, 0, r * TOP_K + k]
                pltpu.make_async_copy(_row_tile(y_hbm, row), _row_tile(buf.at[s, k], r),
                                      sems.at[s]).start()
            return carry

        lax.fori_loop(0, tb, issue, 0, unroll=4)

    @pl.when(i == 0)
    def _():
        gather(d0_ref, 0)

    @pl.when(i + 1 < pl.num_programs(0))
    def _():
        gather(d1_ref, 1 - slot)

    for k in range(TOP_K):
        pltpu.make_async_copy(y_hbm.at[pl.ds(0, tb * SEGS), :], buf.at[slot, k], sems.at[slot]).wait()
    slab = slab_ref[...]
    y2 = jnp.zeros(o_ref.shape, F32)
    for k in range(TOP_K):
        y2 = y2 + _load_tiled(buf.at[slot, k], tb) * slab[:, 2 * TOP_K + k:2 * TOP_K + k + 1]
    o_ref[...] = x_ref[...] + mod_ref[0, 5:6, :] * _rms(y2, gn_ref[3:4, :])


def _combine_call(dest, y_sorted, slab, x, mod, gn, cb_rows):
    t, d = x.shape
    tb = COMBINE_ROWS
    nb = t // tb
    cbb = max(cb_rows // tb, 1)
    sel = (lambda i: (jnp.minimum(i // cbb, 1), 0, 0)) if cb_rows else (lambda i: (1, 0, 0))
    dest3 = dest.reshape(nb, 1, tb * TOP_K)
    return pl.pallas_call(
        _combine_kernel,
        out_shape=jax.ShapeDtypeStruct((t, d), F32),
        grid=(nb,),
        in_specs=[
            pl.BlockSpec((1, 1, tb * TOP_K), lambda i: (0, 0, 0), memory_space=pltpu.SMEM),
            pl.BlockSpec((1, 1, tb * TOP_K), lambda i: (jnp.minimum(i + 1, nb - 1), 0, 0),
                         memory_space=pltpu.SMEM),
            pl.BlockSpec(memory_space=pl.ANY),
            pl.BlockSpec((tb, LANES), lambda i: (i, 0)),
            pl.BlockSpec((tb, d), lambda i: (i, 0)),
            pl.BlockSpec((1, 6, d), sel),
            pl.BlockSpec((4, d), lambda i: (0, 0)),
        ],
        out_specs=pl.BlockSpec((tb, d), lambda i: (i, 0)),
        scratch_shapes=[pltpu.VMEM((2, TOP_K, tb * SEGS, LANES), F32), pltpu.SemaphoreType.DMA((2,))],
        compiler_params=_cparams(("arbitrary",), row_dma=True),
        name="moe_combine",
    )(dest3, dest3, y_sorted, slab, x, mod, gn)


def _moe(h2, logits, x, mod, gn, layer, w_gu, b_gu, w_dn, b_dn, cb_rows):
    t = logits.shape[0]
    tr = MOE_ROWS
    slab, counts = _router_call(logits)
    e_idx = slab[:, :TOP_K].astype(jnp.int32)
    rank = slab[:, TOP_K:2 * TOP_K].astype(jnp.int32)
    counts = counts[0].astype(jnp.int32)
    padded = (counts + tr - 1) // tr * tr
    p_end = jnp.cumsum(padded)
    p_start = p_end - padded
    dest = p_start[e_idx] + rank
    a = t * TOP_K
    n_blk = (a + N_EXPERTS * (tr - 1) + tr - 1) // tr
    src_tok = jnp.zeros((n_blk * tr,), jnp.int32).at[dest.reshape(-1)].set(
        jnp.arange(a, dtype=jnp.int32) // TOP_K, unique_indices=True)
    blk_start = jnp.arange(n_blk, dtype=jnp.int32) * tr
    blk_e = jnp.sum((p_end[None, :] <= blk_start[:, None]).astype(jnp.int32), axis=1)
    blk_e = jnp.minimum(blk_e, N_EXPERTS - 1)
    n_used = (p_end[-1] // tr).astype(jnp.int32).reshape(1)
    ys = _expert_call(blk_e, n_used, src_tok, h2, layer, w_gu, b_gu, w_dn, b_dn)
    return _combine_call(dest, ys, slab, x, mod, gn, cb_rows)


def _hgrn_in_kernel(x_ref, mod_ref, gn_ref, w_ref, lb_ref, q_ref, kf_ref, lf_ref, kb_ref, lbw_ref,
                    i_ref, gs_ref):
    x = x_ref[...]
    h = _rms(x, gn_ref[...]) * (1.0 + mod_ref[0, 1:2, :]) + mod_ref[0, 0:1, :]
    hb = h.astype(BF16)
    f = D_MODEL

    def proj(sec):
        return jnp.dot(hb, w_ref[:, sec * f:(sec + 1) * f], preferred_element_type=F32)

    z = proj(0)
    q_ref[...] = (z * _sigmoid(z)).astype(q_ref.dtype)
    for sec, k_ref, l_ref in ((1, kf_ref, lf_ref), (2, kb_ref, lbw_ref)):
        z = proj(sec)
        lb = lb_ref[sec - 1:sec, :]
        sg = _sigmoid(z)
        l_ref[...] = jnp.log(lb + (1.0 - lb) * sg)
        k_ref[...] = ((1.0 - lb) * _sigmoid(-z)).astype(k_ref.dtype)
    i_ref[...] = proj(3).astype(i_ref.dtype)
    z = proj(4)
    gs_ref[...] = (z * _sigmoid(z)).astype(gs_ref.dtype)


def _hgrn_in_call(xs, mod, gn, w, lb, cb):
    t, d = xs.shape
    tm = ROW_BLOCK
    sel = lambda i: (jnp.minimum(i // cb, 1), 0, 0)
    row = pl.BlockSpec((tm, d), lambda i: (i, 0))
    dts = (BF16, BF16, F32, BF16, F32, BF16, BF16)
    return pl.pallas_call(
        _hgrn_in_kernel,
        out_shape=tuple(jax.ShapeDtypeStruct((t, d), dt) for dt in dts),
        grid=(t // tm,),
        in_specs=[row, pl.BlockSpec((1, 6, d), sel), pl.BlockSpec((1, d), lambda i: (0, 0)),
                  pl.BlockSpec(w.shape, lambda i: (0, 0)), pl.BlockSpec((2, d), lambda i: (0, 0))],
        out_specs=tuple(row for _ in dts),
        compiler_params=_cparams(("parallel",)),
        name="hgrn_in_proj",
    )(xs, mod, gn, w, lb)


def _split3(x):
    hi = x.astype(BF16)
    r1 = x - hi.astype(F32)
    mid = r1.astype(BF16)
    lo = (r1 - mid.astype(F32)).astype(BF16)
    return hi, mid, lo


def _scan_kernel(q_ref, k_ref, g_ref, v_ref, o_ref, st_ref, gcum_ref, *, reverse):
    j = pl.program_id(0)
    ch, sub = HGRN_CHUNK, HGRN_SUB
    ns = ch // sub
    dk = HGRN_DK

    @pl.when(j == 0)
    def _():
        st_ref[...] = jnp.zeros_like(st_ref)

    rr = lax.broadcasted_iota(jnp.int32, (ch, ch), 0)
    cc = lax.broadcasted_iota(jnp.int32, (ch, ch), 1)
    causal = (cc >= rr) if reverse else (cc <= rr)
    tri = jnp.where(causal, 1.0, 0.0).astype(BF16)
    hi, mid, lo = _split3(g_ref[...])
    gcum_ref[...] = LOG2E * (jnp.dot(tri, hi, preferred_element_type=F32)
                             + jnp.dot(tri, mid, preferred_element_type=F32)
                             + jnp.dot(tri, lo, preferred_element_type=F32))
    diag_mask = ((rr // sub) == (cc // sub)) & causal
    key_blocks = list(range(1, ns)) if reverse else list(range(ns - 1))
    nkb = len(key_blocks)
    kbd_mask = (lax.broadcasted_iota(jnp.int32, (ch, nkb * dk), 1) // dk + key_blocks[0]
                == lax.broadcasted_iota(jnp.int32, (ch, nkb * dk), 0) // sub)
    kdg_mask = (lax.broadcasted_iota(jnp.int32, (ch, sub * dk), 1) // dk
                == lax.broadcasted_iota(jnp.int32, (ch, sub * dk), 0) % sub)
    end_row = 0 if reverse else ch - 1
    edge = 0 if reverse else sub - 1
    nt = (((1,), (1,)), ((), ()))
    tn = (((0,), (0,)), ((), ()))

    for h in range(HGRN_HEADS):
        hs = slice(h * dk, (h + 1) * dk)
        q = q_ref[:, hs].astype(F32)
        kb = k_ref[:, hs]
        k = kb.astype(F32)
        v = v_ref[:, hs]
        gc = gcum_ref[:, hs]
        st = st_ref[h]
        g_end = gc[end_row:end_row + 1, :]
        qg = (q * jnp.exp2(gc)).astype(BF16)
        o = lax.dot_general(qg, st.astype(BF16), nt, preferred_element_type=F32)
        k_end = (k * jnp.exp2(g_end - gc)).astype(BF16)
        st_ref[h] = st * jnp.exp2(g_end) + lax.dot_general(v, k_end, tn, preferred_element_type=F32)
        gc4 = gc.reshape(ns, sub, dk)
        g_edge = jnp.broadcast_to(gc4[:, edge:edge + 1, :], (ns, sub, dk)).reshape(ch, dk)
        k_rel = (k * jnp.exp2(g_edge - gc)).astype(BF16)
        k_bd = jnp.where(kbd_mask, jnp.concatenate([k_rel] * nkb, axis=1), jnp.zeros((), BF16))
        q_parts = []
        for jb in key_blocks:
            row = jb * sub + edge
            rows = slice(0, jb * sub) if reverse else slice((jb + 1) * sub, ch)
            part = q[rows] * jnp.exp2(gc[rows] - gc[row:row + 1, :])
            pad = jnp.zeros((ch - part.shape[0], dk), F32)
            full = jnp.concatenate([part, pad] if reverse else [pad, part], axis=0)
            q_parts.append(full.astype(BF16))
        q_cat = jnp.concatenate(q_parts, axis=1)
        s_off = lax.dot_general(q_cat, k_bd, nt, preferred_element_type=F32)
        p_parts = []
        for s in range(sub):
            g_s = jnp.broadcast_to(gc4[:, s:s + 1, :], (ns, sub, dk)).reshape(ch, dk)
            p_parts.append((q * jnp.exp2(jnp.minimum(gc - g_s, 0.0))).astype(BF16))
        p_cat = jnp.concatenate(p_parts, axis=1)
        k_dg = jnp.where(kdg_mask, jnp.concatenate([kb] * sub, axis=1), jnp.zeros((), BF16))
        s_diag = lax.dot_general(p_cat, k_dg, nt, preferred_element_type=F32)
        a = s_off + jnp.where(diag_mask, s_diag, 0.0)
        o_ref[:, hs] = o + jnp.dot(a.astype(BF16), v, preferred_element_type=F32)


def _scan_call(q, k, lg, v, c_len, reverse):
    t, d = q.shape
    ch = HGRN_CHUNK
    n_ch = t // ch
    cc = c_len // ch
    if reverse:
        idx = lambda j: (jnp.where(j < cc, cc - 1 - j, n_ch - 1 - (j - cc)), 0)
    else:
        idx = lambda j: (j, 0)
    blk = pl.BlockSpec((ch, d), idx)
    return pl.pallas_call(
        functools.partial(_scan_kernel, reverse=reverse),
        out_shape=jax.ShapeDtypeStruct((t, d), F32),
        grid=(n_ch,),
        in_specs=[blk, blk, blk, blk],
        out_specs=blk,
        scratch_shapes=[pltpu.VMEM((HGRN_HEADS, HGRN_DK, HGRN_DK), F32), pltpu.VMEM((ch, d), F32)],
        compiler_params=_cparams(("arbitrary",)),
        name="hgrn_scan_bwd" if reverse else "hgrn_scan_fwd",
    )(q, k, lg, v)


def _rope_tables(l, c_len):
    n = HEAD_DIM // 4
    inv_freq = ROPE_BASE ** (-jnp.arange(n, dtype=F32) / n)
    rows = l // GRID_W
    ang_row = jnp.arange(rows, dtype=F32)[:, None] * inv_freq[None, :]
    ang_col = jnp.arange(GRID_W, dtype=F32)[:, None] * inv_freq[None, :]
    cr, sr = (jnp.repeat(f(ang_row), GRID_W, axis=0) for f in (jnp.cos, jnp.sin))
    cc, sc = (jnp.tile(f(ang_col), (rows, 1)) for f in (jnp.cos, jnp.sin))
    cos = jnp.concatenate([cr, cr, cc, cc] * 2, axis=1)
    sin = jnp.concatenate([-sr, sr, -sc, sc] * 2, axis=1)
    cos = jnp.concatenate([jnp.ones((c_len, LANES), F32), cos], axis=0)
    sin = jnp.concatenate([jnp.zeros((c_len, LANES), F32), sin], axis=0)
    return cos, sin


def kernel(x, c, ctx, c_ctx, ada_w, ada_b, norm_g, attn_w_qkv, attn_b_qkv, attn_sink, attn_w_o,
           hgrn_w_in, hgrn_lb, hgrn_norm_g, hgrn_w_o, router_w, router_b, moe_w_gu, moe_b_gu,
           moe_w_dn, moe_b_dn):
    b, l, d = x.shape
    c_len = ctx.shape[1]
    depth = ada_w.shape[0]
    assert b == 1 and d == D_MODEL and depth == 2
    assert c_len % ROW_BLOCK == 0 and l % ROW_BLOCK == 0
    cb = c_len // ROW_BLOCK

    cs = jnp.zeros((8, d), F32).at[0].set(c_ctx).at[1].set(c[0])
    mods = _ada_call(cs, ada_w, ada_b)
    mods = mods[:, :2].reshape(depth, 2, 6, d)

    xs = jnp.concatenate([ctx[0], x[0]], axis=0)
    cos, sin = _rope_tables(l, c_len)

    q, k4, v4 = _qkv_call(xs, mods[0], norm_g[0, 0:1], attn_w_qkv[0].astype(BF16),
                          attn_b_qkv[0].reshape(1, -1), cos, sin, cb)
    o = _attn_call(attn_sink[0], q, k4, v4, c_len)
    xs, h2, logits = _out_call("attn", (o,), xs, mods[0], norm_g[0], attn_w_o[0].astype(BF16),
                               router_w[0], router_b[0].reshape(1, -1), None, cb, 0)
    xs = _moe(h2, logits, xs, mods[0], norm_g[0], 0, moe_w_gu, moe_b_gu, moe_w_dn, moe_b_dn, c_len)

    lb_soft = jax.nn.softmax(hgrn_lb.astype(F32), axis=0)
    lb = jnp.cumsum(lb_soft, axis=0)[1] - lb_soft[0]
    qh, kf, lf, kb, lbw, iv, gs = _hgrn_in_call(xs, mods[1], norm_g[1, 0:1], hgrn_w_in[0].astype(BF16),
                                                lb, cb)
    o_f = _scan_call(qh, kf, lf, iv, c_len, reverse=False)
    o_b = _scan_call(qh, kb, lbw, iv, c_len, reverse=True)
    x_lat, h2, logits = _out_call("hgrn", (o_f, o_b, gs), xs, mods[1], norm_g[1],
                                  hgrn_w_o[0].astype(BF16), router_w[1], router_b[1].reshape(1, -1),
                                  hgrn_norm_g[0].reshape(1, -1), cb, cb)
    out = _moe(h2, logits, x_lat, mods[1], norm_g[1], 1, moe_w_gu, moe_b_gu, moe_w_dn, moe_b_dn, 0)
    return out[None]
```

```python
import functools

import jax
import jax.numpy as jnp
from jax import lax
from jax.experimental import pallas as pl
from jax.experimental.pallas import tpu as pltpu

D_MODEL = 1024
GRID_W = 64
RMS_EPS = 1e-6

ATTN_HEADS = 16
ATTN_KV_HEADS = 2
HEAD_DIM = 64
Q_DIM = ATTN_HEADS * HEAD_DIM
KV_DIM = ATTN_KV_HEADS * HEAD_DIM
QKV_DIM = Q_DIM + 2 * KV_DIM
WINDOW = 128
ATTN_BLOCK = 128
ROPE_BASE = 10000.0

HGRN_HEADS = 8
HGRN_DK = 128
HGRN_CHUNK = 64
HGRN_SUB = 8

N_EXPERTS = 32
TOP_K = 4
D_FF = 1024
SWIGLU_LIMIT = 7.0
SWIGLU_ALPHA = 1.702

LANES = 128
ROW_BLOCK = 256
MOE_ROWS = 256
COMBINE_ROWS = 128
VMEM_LIMIT = 56 * 1024 * 1024

F32 = jnp.float32
BF16 = jnp.bfloat16
NEG_BIG = -1e30
LOG2E = 1.4426950408889634


def _cparams(sem, row_dma=False):
    return pltpu.CompilerParams(dimension_semantics=sem, vmem_limit_bytes=VMEM_LIMIT,
                                disable_bounds_checks=row_dma)


def _rms(x, g):
    return x * lax.rsqrt(jnp.mean(x * x, axis=-1, keepdims=True) + RMS_EPS) * g


def _sigmoid(x):
    return 1.0 / (1.0 + jnp.exp(-x))


SEGS = D_MODEL // LANES


def _store_tiled(ref, val):
    n = val.shape[0]
    for s in range(SEGS):
        ref[pl.ds(s, n, stride=SEGS), :] = val[:, s * LANES:(s + 1) * LANES]


def _load_tiled(ref, n):
    return jnp.concatenate([ref[pl.ds(s, n, stride=SEGS), :] for s in range(SEGS)], axis=1)


def _row_tile(ref, r):
    return ref.at[pl.ds(pl.multiple_of(r * SEGS, SEGS), SEGS), :]


def _ada_kernel(c_ref, w_ref, b_ref, o_ref):
    c = c_ref[...]
    s = c * _sigmoid(c)
    o_ref[0] = jnp.dot(s, w_ref[0], precision=lax.Precision.HIGHEST,
                       preferred_element_type=F32) + b_ref[0]


def _ada_call(cs, ada_w, ada_b):
    depth, d, n = ada_w.shape
    tn = 1024
    return pl.pallas_call(
        _ada_kernel,
        out_shape=jax.ShapeDtypeStruct((depth, 8, n), F32),
        grid=(depth, n // tn),
        in_specs=[
            pl.BlockSpec((8, d), lambda i, j: (0, 0)),
            pl.BlockSpec((1, d, tn), lambda i, j: (i, 0, j)),
            pl.BlockSpec((1, 1, tn), lambda i, j: (i, 0, j)),
        ],
        out_specs=pl.BlockSpec((1, 8, tn), lambda i, j: (i, 0, j)),
        compiler_params=_cparams(("arbitrary", "arbitrary")),
        name="adaln",
    )(cs, ada_w, ada_b.reshape(depth, 1, n))


def _swap16(t):
    lane = lax.broadcasted_iota(jnp.int32, t.shape, 1)
    return jnp.where(lane % 32 < 16, pltpu.roll(t, LANES - 16, 1), pltpu.roll(t, 16, 1))


def _qkv_kernel(x_ref, mod_ref, gn_ref, w_ref, b_ref, cos_ref, sin_ref, q_ref, k4_ref, v4_ref):
    x = x_ref[...]
    h = _rms(x, gn_ref[...]) * (1.0 + mod_ref[0, 1:2, :]) + mod_ref[0, 0:1, :]
    hb = h.astype(BF16)
    cos = cos_ref[...]
    sin = sin_ref[...]
    nq = Q_DIM // LANES
    for j in range(nq + 1):
        sl = slice(j * LANES, (j + 1) * LANES)
        t = jnp.dot(hb, w_ref[:, sl], preferred_element_type=F32) + b_ref[:, sl]
        t = t * cos + _swap16(t) * sin
        if j < nq:
            q_ref[:, sl] = (t * (HEAD_DIM ** -0.5 * LOG2E)).astype(q_ref.dtype)
        else:
            kt = t
    sl = slice(Q_DIM + KV_DIM, QKV_DIM)
    vt = jnp.dot(hb, w_ref[:, sl], preferred_element_type=F32) + b_ref[:, sl]
    lo = lax.broadcasted_iota(jnp.int32, kt.shape, 1) < HEAD_DIM
    for t, ref in ((kt, k4_ref), (vt, v4_ref)):
        sw = pltpu.roll(t, HEAD_DIM, 1)
        ref[:, 0 * LANES:1 * LANES] = jnp.where(lo, t, 0.0).astype(ref.dtype)
        ref[:, 1 * LANES:2 * LANES] = jnp.where(lo, 0.0, sw).astype(ref.dtype)
        ref[:, 2 * LANES:3 * LANES] = jnp.where(lo, sw, 0.0).astype(ref.dtype)
        ref[:, 3 * LANES:4 * LANES] = jnp.where(lo, 0.0, t).astype(ref.dtype)


def _qkv_call(xs, mod, gn, w, b, cos, sin, cb):
    t, d = xs.shape
    tm = ROW_BLOCK
    sel = lambda i: (jnp.minimum(i // cb, 1), 0, 0)
    return pl.pallas_call(
        _qkv_kernel,
        out_shape=(jax.ShapeDtypeStruct((t, Q_DIM), BF16),
                   jax.ShapeDtypeStruct((t, 4 * LANES), BF16),
                   jax.ShapeDtypeStruct((t, 4 * LANES), BF16)),
        grid=(t // tm,),
        in_specs=[
            pl.BlockSpec((tm, d), lambda i: (i, 0)),
            pl.BlockSpec((1, 6, d), sel),
            pl.BlockSpec((1, d), lambda i: (0, 0)),
            pl.BlockSpec((d, QKV_DIM), lambda i: (0, 0)),
            pl.BlockSpec((1, QKV_DIM), lambda i: (0, 0)),
            pl.BlockSpec((tm, LANES), lambda i: (i, 0)),
            pl.BlockSpec((tm, LANES), lambda i: (i, 0)),
        ],
        out_specs=(pl.BlockSpec((tm, Q_DIM), lambda i: (i, 0)),
                   pl.BlockSpec((tm, 4 * LANES), lambda i: (i, 0)),
                   pl.BlockSpec((tm, 4 * LANES), lambda i: (i, 0))),
        compiler_params=_cparams(("parallel",)),
        name="qkv_rope",
    )(xs, mod, gn, w, b, cos, sin)


def _attn_kernel(sink_ref, q_ref, kp_ref, kc_ref, kn_ref, kx_ref, vp_ref, vc_ref, vn_ref, vx_ref,
                 o_ref, *, cb, n_lat):
    n = pl.program_id(0)
    blk = ATTN_BLOCK
    c = kx_ref.shape[0]
    nw = 3 * blk
    s = lax.broadcasted_iota(jnp.int32, (nw + c, blk), 0)
    r = lax.broadcasted_iota(jnp.int32, (nw + c, blk), 1)
    q_pos = (n - cb) * blk + r
    k_pos = (n - cb - 1) * blk + s
    win_ok = (jnp.abs(q_pos - k_pos) <= WINDOW) & (k_pos >= 0) & (k_pos < n_lat) & (n >= cb)
    valid = (s >= nw) | win_ok
    k_all = jnp.concatenate([kp_ref[...], kc_ref[...], kn_ref[...], kx_ref[...]], axis=0)
    v_all = jnp.concatenate([vp_ref[...], vc_ref[...], vn_ref[...], vx_ref[...]], axis=0)
    n_var = 2 * ATTN_KV_HEADS
    v_t = [v_all[:, j * LANES:(j + 1) * LANES].astype(F32).T.astype(BF16) for j in range(n_var)]
    nt = (((1,), (1,)), ((), ()))
    group = ATTN_HEADS // ATTN_KV_HEADS
    for p in range(ATTN_HEADS // 2):
        g = (2 * p) // group
        qp = q_ref[:, p * LANES:(p + 1) * LANES]
        o_t = jnp.zeros((LANES, blk), F32)
        for half in range(2):
            j = 2 * g + half
            sc = lax.dot_general(k_all[:, j * LANES:(j + 1) * LANES], qp, nt,
                                 preferred_element_type=F32)
            sc = jnp.where(valid, sc, NEG_BIG)
            sink = sink_ref[2 * p + half] * LOG2E
            m = jnp.maximum(jnp.max(sc, axis=0, keepdims=True), sink)
            e = jnp.exp2(sc - m)
            denom = jnp.sum(e, axis=0, keepdims=True) + jnp.exp2(sink - m)
            pv = jnp.dot(v_t[j], e.astype(BF16), preferred_element_type=F32)
            o_t = o_t + pv / denom
        o_ref[:, p * LANES:(p + 1) * LANES] = o_t.T.astype(o_ref.dtype)


def _attn_call(sink, q, k4, v4, c_len):
    t = q.shape[0]
    blk = ATTN_BLOCK
    cb = c_len // blk
    nb = t // blk
    n_lat = t - c_len
    last = nb - 1
    kw = 4 * LANES
    spec_q = pl.BlockSpec((blk, Q_DIM), lambda n: (n, 0))
    prev = pl.BlockSpec((blk, kw), lambda n: (jnp.maximum(n - 1, 0), 0))
    cur = pl.BlockSpec((blk, kw), lambda n: (n, 0))
    nxt = pl.BlockSpec((blk, kw), lambda n: (jnp.minimum(n + 1, last), 0))
    ctx = pl.BlockSpec((c_len, kw), lambda n: (0, 0))
    return pl.pallas_call(
        functools.partial(_attn_kernel, cb=cb, n_lat=n_lat),
        out_shape=jax.ShapeDtypeStruct((t, Q_DIM), BF16),
        grid=(nb,),
        in_specs=[pl.BlockSpec(memory_space=pltpu.SMEM), spec_q,
                  prev, cur, nxt, ctx, prev, cur, nxt, ctx],
        out_specs=pl.BlockSpec((blk, Q_DIM), lambda n: (n, 0)),
        compiler_params=_cparams(("parallel",)),
        name="window_attn",
    )(sink, q, k4, k4, k4, k4, v4, v4, v4, v4)


def _residual_tail(y, x, mod_ref, gn_ref, rw_ref, rb_ref, xo_ref, h2_ref, lg_ref):
    x_new = x + mod_ref[0, 2:3, :] * _rms(y, gn_ref[1:2, :])
    h2 = _rms(x_new, gn_ref[2:3, :]) * (1.0 + mod_ref[0, 4:5, :]) + mod_ref[0, 3:4, :]
    xo_ref[...] = x_new
    _store_tiled(h2_ref, h2)
    hi, mid, lo = _split3(h2)
    lane = lax.broadcasted_iota(jnp.int32, (h2.shape[0], LANES), 1)
    w = rw_ref[...]
    r = (jnp.dot(hi, w, preferred_element_type=F32)
         + jnp.where(lane < 2 * N_EXPERTS, jnp.dot(mid, w, preferred_element_type=F32), 0.0)
         + jnp.where(lane < N_EXPERTS, jnp.dot(lo, w, preferred_element_type=F32), 0.0))
    r = r + pltpu.roll(r, LANES - N_EXPERTS, 1) + pltpu.roll(r, LANES - 2 * N_EXPERTS, 1)
    lg_ref[...] = r[:, :N_EXPERTS] + rb_ref[...]


def _attn_out_kernel(a_ref, x_ref, mod_ref, gn_ref, w_ref, rw_ref, rb_ref, xo_ref, h2_ref, lg_ref):
    y = jnp.dot(a_ref[...], w_ref[...], preferred_element_type=F32)
    _residual_tail(y, x_ref[...], mod_ref, gn_ref, rw_ref, rb_ref, xo_ref, h2_ref, lg_ref)


def _hgrn_out_kernel(of_ref, ob_ref, gs_ref, hg_ref, x_ref, mod_ref, gn_ref, w_ref, rw_ref, rb_ref,
                     xo_ref, h2_ref, lg_ref):
    parts = []
    for h in range(HGRN_HEADS):
        sl = slice(h * HGRN_DK, (h + 1) * HGRN_DK)
        o = of_ref[:, sl] + ob_ref[:, sl]
        parts.append(_rms(o, hg_ref[:, sl]))
    a = jnp.concatenate(parts, axis=1) * gs_ref[...].astype(F32)
    y = jnp.dot(a.astype(BF16), w_ref[...], preferred_element_type=F32)
    _residual_tail(y, x_ref[...], mod_ref, gn_ref, rw_ref, rb_ref, xo_ref, h2_ref, lg_ref)


def _out_call(kind, acts, x, mod, gn, w, rw, rb, hg, cb, off):
    t_out = x.shape[0] - off * ROW_BLOCK
    d = D_MODEL
    tm = ROW_BLOCK
    w_hi, w_mid, w_lo = _split3(rw)
    rw = jnp.concatenate([w_hi, w_mid, w_lo, jnp.zeros_like(w_hi)], axis=1)
    row = lambda i: (i + off, 0)
    sel = lambda i: (jnp.minimum((i + off) // cb, 1), 0, 0)
    const = lambda i: (0, 0)
    common_specs = [
        pl.BlockSpec((tm, d), row),
        pl.BlockSpec((1, 6, d), sel),
        pl.BlockSpec((4, d), const),
        pl.BlockSpec((d, d), const),
        pl.BlockSpec((d, LANES), const),
        pl.BlockSpec((1, N_EXPERTS), const),
    ]
    if kind == "attn":
        body = _attn_out_kernel
        in_specs = [pl.BlockSpec((tm, d), row)] + common_specs
        args = (acts[0], x, mod, gn, w, rw, rb)
    else:
        body = _hgrn_out_kernel
        in_specs = [pl.BlockSpec((tm, d), row)] * 3 + [pl.BlockSpec((1, d), const)] + common_specs
        args = (acts[0], acts[1], acts[2], hg, x, mod, gn, w, rw, rb)
    return pl.pallas_call(
        body,
        out_shape=(jax.ShapeDtypeStruct((t_out, d), F32),
                   jax.ShapeDtypeStruct((t_out * SEGS, LANES), F32),
                   jax.ShapeDtypeStruct((t_out, N_EXPERTS), F32)),
        grid=(t_out // tm,),
        in_specs=in_specs,
        out_specs=(pl.BlockSpec((tm, d), lambda i: (i, 0)),
                   pl.BlockSpec((tm * SEGS, LANES), lambda i: (i, 0)),
                   pl.BlockSpec((tm, N_EXPERTS), lambda i: (i, 0))),
        compiler_params=_cparams(("parallel",)),
        name=kind + "_out_residual",
    )(*args)


def _router_kernel(lg_ref, slab_ref, cnt_ref, run_ref):
    i = pl.program_id(0)

    @pl.when(i == 0)
    def _():
        run_ref[...] = jnp.zeros_like(run_ref)

    lg = lg_ref[...]
    tb = lg.shape[0]
    lane = lax.broadcasted_iota(jnp.int32, lg.shape, 1).astype(F32)
    sels, tops, idxs = [], [], []
    for _ in range(TOP_K):
        m = jnp.max(lg, axis=-1, keepdims=True)
        idx = jnp.min(jnp.where(lg == m, lane, float(N_EXPERTS)), axis=-1, keepdims=True)
        sel = lane == idx
        sels.append(sel)
        tops.append(m)
        idxs.append(idx)
        lg = jnp.where(sel, -jnp.inf, lg)
    ws = [jnp.exp(tk - tops[0]) for tk in tops]
    wsum = ws[0] + ws[1] + ws[2] + ws[3]
    chosen = (sels[0] | sels[1] | sels[2] | sels[3])
    onehot = jnp.where(chosen, 1.0, 0.0)
    rr = lax.broadcasted_iota(jnp.int32, (tb, tb), 0)
    cc = lax.broadcasted_iota(jnp.int32, (tb, tb), 1)
    tri = jnp.where(cc < rr, 1.0, 0.0).astype(BF16)
    before = jnp.dot(tri, onehot.astype(BF16), preferred_element_type=F32) + run_ref[...]
    out_lane = lax.broadcasted_iota(jnp.int32, (tb, LANES), 1)
    slab = jnp.zeros((tb, LANES), F32)
    for k in range(TOP_K):
        rank = jnp.sum(jnp.where(sels[k], before, 0.0), axis=-1, keepdims=True)
        slab = jnp.where(out_lane == k, idxs[k], slab)
        slab = jnp.where(out_lane == TOP_K + k, rank, slab)
        slab = jnp.where(out_lane == 2 * TOP_K + k, ws[k] / wsum, slab)
    slab_ref[...] = slab
    run_ref[...] = run_ref[...] + jnp.sum(onehot, axis=0, keepdims=True)
    cnt_ref[...] = run_ref[...]


def _router_call(logits):
    t = logits.shape[0]
    tb = ROW_BLOCK
    return pl.pallas_call(
        _router_kernel,
        out_shape=(jax.ShapeDtypeStruct((t, LANES), F32),
                   jax.ShapeDtypeStruct((1, N_EXPERTS), F32)),
        grid=(t // tb,),
        in_specs=[pl.BlockSpec((tb, N_EXPERTS), lambda i: (i, 0))],
        out_specs=(pl.BlockSpec((tb, LANES), lambda i: (i, 0)),
                   pl.BlockSpec((1, N_EXPERTS), lambda i: (0, 0))),
        scratch_shapes=[pltpu.VMEM((1, N_EXPERTS), F32)],
        compiler_params=_cparams(("arbitrary",)),
        name="router_topk",
    )(logits)


def _expert_kernel(be_ref, nu_ref, src0_ref, src1_ref, x_hbm, wgu_ref, bgu_ref, wdn_ref, bdn_ref, o_ref,
                   xbuf, sems, wgu_bf, wdn_bf):
    i = pl.program_id(0)
    rows = o_ref.shape[0] // SEGS
    slot = i % 2
    n_used = nu_ref[0]

    def gather(idx_ref, s):
        def issue(r, carry):
            pltpu.make_async_copy(_row_tile(x_hbm, idx_ref[0, 0, r]), _row_tile(xbuf.at[s], r),
                                  sems.at[s]).start()
            return carry

        lax.fori_loop(0, rows, issue, 0, unroll=8)

    @pl.when(i == 0)
    def _():
        gather(src0_ref, 0)

    prev = be_ref[jnp.maximum(i - 1, 0)]
    fresh = (i == 0) | (be_ref[i] != prev)

    @pl.when(fresh)
    def _():
        wgu_bf[...] = wgu_ref[...].astype(BF16)
        wdn_bf[...] = wdn_ref[...].astype(BF16)

    @pl.when(i < n_used)
    def _():
        pltpu.make_async_copy(x_hbm.at[pl.ds(0, rows * SEGS), :], xbuf.at[slot], sems.at[slot]).wait()
        xb = _load_tiled(xbuf.at[slot], rows).astype(BF16)
        for r in range(rows):
            pltpu.make_async_copy(_row_tile(x_hbm, src1_ref[0, 0, r]),
                                  xbuf.at[1 - slot, pl.ds(r * SEGS, SEGS), :], sems.at[1 - slot]).start()
        gu = jnp.dot(xb, wgu_bf[...], preferred_element_type=F32) + bgu_ref[...]
        g = jnp.minimum(gu[:, :D_FF], SWIGLU_LIMIT)
        u = jnp.clip(gu[:, D_FF:], -SWIGLU_LIMIT, SWIGLU_LIMIT)
        hdn = g * _sigmoid(SWIGLU_ALPHA * g) * (u + 1.0)
        y = jnp.dot(hdn.astype(BF16), wdn_bf[...], preferred_element_type=F32) + bdn_ref[...]
        _store_tiled(o_ref, y)

        @pl.when(i == pl.num_programs(0) - 1)
        def _():
            pltpu.make_async_copy(x_hbm.at[pl.ds(0, rows * SEGS), :], xbuf.at[1 - slot], sems.at[1 - slot]).wait()

    @pl.when(i >= n_used)
    def _():
        @pl.when(i == n_used)
        def _():
            pltpu.make_async_copy(x_hbm.at[pl.ds(0, rows * SEGS), :], xbuf.at[slot], sems.at[slot]).wait()

        o_ref[...] = jnp.zeros_like(o_ref)


def _expert_call(blk_e, n_used, src_tok, h2, layer, w_gu, b_gu, w_dn, b_dn):
    d = D_MODEL
    tr = MOE_ROWS
    n_blk = blk_e.shape[0]
    depth, e, _, f2 = w_gu.shape
    last = n_blk - 1
    return pl.pallas_call(
        _expert_kernel,
        out_shape=jax.ShapeDtypeStruct((n_blk * tr * SEGS, LANES), F32),
        grid_spec=pltpu.PrefetchScalarGridSpec(
            num_scalar_prefetch=2,
            grid=(n_blk,),
            in_specs=[
                pl.BlockSpec((1, 1, tr), lambda i, be, nu: (0, 0, 0), memory_space=pltpu.SMEM),
                pl.BlockSpec((1, 1, tr), lambda i, be, nu: (jnp.minimum(i + 1, last), 0, 0),
                             memory_space=pltpu.SMEM),
                pl.BlockSpec(memory_space=pl.ANY),
                pl.BlockSpec((None, None, d, f2), lambda i, be, nu: (layer, be[i], 0, 0)),
                pl.BlockSpec((None, None, 1, f2), lambda i, be, nu: (layer, be[i], 0, 0)),
                pl.BlockSpec((None, None, D_FF, d), lambda i, be, nu: (layer, be[i], 0, 0)),
                pl.BlockSpec((None, None, 1, d), lambda i, be, nu: (layer, be[i], 0, 0)),
            ],
            out_specs=pl.BlockSpec((tr * SEGS, LANES), lambda i, be, nu: (i, 0)),
            scratch_shapes=[pltpu.VMEM((2, tr * SEGS, LANES), F32), pltpu.SemaphoreType.DMA((2,)),
                            pltpu.VMEM((d, f2), BF16), pltpu.VMEM((D_FF, d), BF16)],
        ),
        compiler_params=_cparams(("arbitrary",), row_dma=True),
        name="moe_experts",
    )(blk_e, n_used, src_tok.reshape(n_blk, 1, tr), src_tok.reshape(n_blk, 1, tr), h2,
      w_gu, b_gu.reshape(depth, e, 1, f2), w_dn, b_dn.reshape(depth, e, 1, d))


def _combine_kernel(d0_ref, d1_ref, y_hbm, slab_ref, x_ref, mod_ref, gn_ref, o_ref, buf, sems):
    i = pl.program_id(0)
    tb = o_ref.shape[0]
    slot = i % 2

    def gather(idx_ref, s):
        def issue(r, carry):
            for k in range(TOP_K):
                row = idx_ref[0, 0, r * TOP_K + k]
                pltpu.make_async_copy(_row_tile(y_hbm, row), _row_tile(buf.at[s, k], r),
                                      sems.at[s]).start()
            return carry

        lax.fori_loop(0, tb, issue, 0, unroll=4)

    @pl.when(i == 0)
    def _():
        gather(d0_ref, 0)

    @pl.when(i + 1 < pl.num_programs(0))
    def _():
        gather(d1_ref, 1 - slot)

    for k in range(TOP_K):
        pltpu.make_async_copy(y_hbm.at[pl.ds(0, tb * SEGS), :], buf.at[slot, k], sems.at[slot]).wait()
    slab = slab_ref[...]
    y2 = jnp.zeros(o_ref.shape, F32)
    for k in range(TOP_K):
        y2 = y2 + _load_tiled(buf.at[slot, k], tb) * slab[:, 2 * TOP_K + k:2 * TOP_K + k + 1]
    o_ref[...] = x_ref[...] + mod_ref[0, 5:6, :] * _rms(y2, gn_ref[3:4, :])


def _combine_call(dest, y_sorted, slab, x, mod, gn, cb_rows):
    t, d = x.shape
    tb = COMBINE_ROWS
    nb = t // tb
    cbb = max(cb_rows // tb, 1)
    sel = (lambda i: (jnp.minimum(i // cbb, 1), 0, 0)) if cb_rows else (lambda i: (1, 0, 0))
    dest3 = dest.reshape(nb, 1, tb * TOP_K)
    return pl.pallas_call(
        _combine_kernel,
        out_shape=jax.ShapeDtypeStruct((t, d), F32),
        grid=(nb,),
        in_specs=[
            pl.BlockSpec((1, 1, tb * TOP_K), lambda i: (0, 0, 0), memory_space=pltpu.SMEM),
            pl.BlockSpec((1, 1, tb * TOP_K), lambda i: (jnp.minimum(i + 1, nb - 1), 0, 0),
                         memory_space=pltpu.SMEM),
            pl.BlockSpec(memory_space=pl.ANY),
            pl.BlockSpec((tb, LANES), lambda i: (i, 0)),
            pl.BlockSpec((tb, d), lambda i: (i, 0)),
            pl.BlockSpec((1, 6, d), sel),
            pl.BlockSpec((4, d), lambda i: (0, 0)),
        ],
        out_specs=pl.BlockSpec((tb, d), lambda i: (i, 0)),
        scratch_shapes=[pltpu.VMEM((2, TOP_K, tb * SEGS, LANES), F32), pltpu.SemaphoreType.DMA((2,))],
        compiler_params=_cparams(("arbitrary",), row_dma=True),
        name="moe_combine",
    )(dest3, dest3, y_sorted, slab, x, mod, gn)


INVERT_BLOCK = 1024


def _invert_kernel(dest_ref, zeros_hbm, src_ref):
    i = pl.program_id(0)
    n = dest_ref.shape[2]

    @pl.when(i == 0)
    def _():
        pltpu.sync_copy(zeros_hbm, src_ref)

    tok0 = i * (n // TOP_K)

    def put(t, carry):
        for k in range(TOP_K):
            src_ref[dest_ref[0, 0, t * TOP_K + k]] = tok0 + t
        return carry

    lax.fori_loop(0, n // TOP_K, put, 0, unroll=4)


def _invert_call(dest, n_rows):
    a = dest.size
    nb = a // INVERT_BLOCK
    return pl.pallas_call(
        _invert_kernel,
        out_shape=jax.ShapeDtypeStruct((n_rows,), jnp.int32),
        grid=(nb,),
        in_specs=[pl.BlockSpec((1, 1, INVERT_BLOCK), lambda i: (i, 0, 0), memory_space=pltpu.SMEM),
                  pl.BlockSpec(memory_space=pl.ANY)],
        out_specs=pl.BlockSpec(memory_space=pltpu.SMEM),
        compiler_params=_cparams(("arbitrary",)),
        name="moe_invert",
    )(dest.reshape(nb, 1, INVERT_BLOCK), jnp.zeros((n_rows,), jnp.int32))


def _moe(h2, logits, x, mod, gn, layer, w_gu, b_gu, w_dn, b_dn, cb_rows):
    t = logits.shape[0]
    tr = MOE_ROWS
    slab, counts = _router_call(logits)
    e_idx = slab[:, :TOP_K].astype(jnp.int32)
    rank = slab[:, TOP_K:2 * TOP_K].astype(jnp.int32)
    counts = counts[0].astype(jnp.int32)
    padded = (counts + tr - 1) // tr * tr
    p_end = jnp.cumsum(padded)
    p_start = p_end - padded
    dest = p_start[e_idx] + rank
    a = t * TOP_K
    n_blk = (a + N_EXPERTS * (tr - 1) + tr - 1) // tr
    src_tok = _invert_call(dest, n_blk * tr)
    blk_start = jnp.arange(n_blk, dtype=jnp.int32) * tr
    blk_e = jnp.sum((p_end[None, :] <= blk_start[:, None]).astype(jnp.int32), axis=1)
    blk_e = jnp.minimum(blk_e, N_EXPERTS - 1)
    n_used = (p_end[-1] // tr).astype(jnp.int32).reshape(1)
    ys = _expert_call(blk_e, n_used, src_tok, h2, layer, w_gu, b_gu, w_dn, b_dn)
    return _combine_call(dest, ys, slab, x, mod, gn, cb_rows)


def _hgrn_in_kernel(x_ref, mod_ref, gn_ref, w_ref, lb_ref, q_ref, kf_ref, lf_ref, kb_ref, lbw_ref,
                    i_ref, gs_ref):
    x = x_ref[...]
    h = _rms(x, gn_ref[...]) * (1.0 + mod_ref[0, 1:2, :]) + mod_ref[0, 0:1, :]
    hb = h.astype(BF16)
    f = D_MODEL

    def proj(sec):
        return jnp.dot(hb, w_ref[:, sec * f:(sec + 1) * f], preferred_element_type=F32)

    z = proj(0)
    q_ref[...] = (z * _sigmoid(z)).astype(q_ref.dtype)
    for sec, k_ref, l_ref in ((1, kf_ref, lf_ref), (2, kb_ref, lbw_ref)):
        z = proj(sec)
        lb = lb_ref[sec - 1:sec, :]
        sg = _sigmoid(z)
        l_ref[...] = jnp.log(lb + (1.0 - lb) * sg)
        k_ref[...] = ((1.0 - lb) * _sigmoid(-z)).astype(k_ref.dtype)
    i_ref[...] = proj(3).astype(i_ref.dtype)
    z = proj(4)
    gs_ref[...] = (z * _sigmoid(z)).astype(gs_ref.dtype)


def _hgrn_in_call(xs, mod, gn, w, lb, cb):
    t, d = xs.shape
    tm = ROW_BLOCK
    sel = lambda i: (jnp.minimum(i // cb, 1), 0, 0)
    row = pl.BlockSpec((tm, d), lambda i: (i, 0))
    dts = (BF16, BF16, F32, BF16, F32, BF16, BF16)
    return pl.pallas_call(
        _hgrn_in_kernel,
        out_shape=tuple(jax.ShapeDtypeStruct((t, d), dt) for dt in dts),
        grid=(t // tm,),
        in_specs=[row, pl.BlockSpec((1, 6, d), sel), pl.BlockSpec((1, d), lambda i: (0, 0)),
                  pl.BlockSpec(w.shape, lambda i: (0, 0)), pl.BlockSpec((2, d), lambda i: (0, 0))],
        out_specs=tuple(row for _ in dts),
        compiler_params=_cparams(("parallel",)),
        name="hgrn_in_proj",
    )(xs, mod, gn, w, lb)


def _split3(x):
    hi = x.astype(BF16)
    r1 = x - hi.astype(F32)
    mid = r1.astype(BF16)
    lo = (r1 - mid.astype(F32)).astype(BF16)
    return hi, mid, lo


def _scan_kernel(q_ref, k_ref, g_ref, v_ref, o_ref, st_ref, gcum_ref, *, reverse):
    j = pl.program_id(0)
    ch, sub = HGRN_CHUNK, HGRN_SUB
    ns = ch // sub
    dk = HGRN_DK

    @pl.when(j == 0)
    def _():
        st_ref[...] = jnp.zeros_like(st_ref)

    rr = lax.broadcasted_iota(jnp.int32, (ch, ch), 0)
    cc = lax.broadcasted_iota(jnp.int32, (ch, ch), 1)
    causal = (cc >= rr) if reverse else (cc <= rr)
    tri = jnp.where(causal, 1.0, 0.0).astype(BF16)
    hi, mid, lo = _split3(g_ref[...])
    gcum_ref[...] = LOG2E * (jnp.dot(tri, hi, preferred_element_type=F32)
                             + jnp.dot(tri, mid, preferred_element_type=F32)
                             + jnp.dot(tri, lo, preferred_element_type=F32))
    diag_mask = ((rr // sub) == (cc // sub)) & causal
    key_blocks = list(range(1, ns)) if reverse else list(range(ns - 1))
    nkb = len(key_blocks)
    kbd_mask = (lax.broadcasted_iota(jnp.int32, (ch, nkb * dk), 1) // dk + key_blocks[0]
                == lax.broadcasted_iota(jnp.int32, (ch, nkb * dk), 0) // sub)
    kdg_mask = (lax.broadcasted_iota(jnp.int32, (ch, sub * dk), 1) // dk
                == lax.broadcasted_iota(jnp.int32, (ch, sub * dk), 0) % sub)
    end_row = 0 if reverse else ch - 1
    edge = 0 if reverse else sub - 1
    nt = (((1,), (1,)), ((), ()))
    tn = (((0,), (0,)), ((), ()))

    for h in range(HGRN_HEADS):
        hs = slice(h * dk, (h + 1) * dk)
        q = q_ref[:, hs].astype(F32)
        kb = k_ref[:, hs]
        k = kb.astype(F32)
        v = v_ref[:, hs]
        gc = gcum_ref[:, hs]
        st = st_ref[h]
        g_end = gc[end_row:end_row + 1, :]
        qg = (q * jnp.exp2(gc)).astype(BF16)
        o = lax.dot_general(qg, st.astype(BF16), nt, preferred_element_type=F32)
        k_end = (k * jnp.exp2(g_end - gc)).astype(BF16)
        st_ref[h] = st * jnp.exp2(g_end) + lax.dot_general(v, k_end, tn, preferred_element_type=F32)
        gc4 = gc.reshape(ns, sub, dk)
        g_edge = jnp.broadcast_to(gc4[:, edge:edge + 1, :], (ns, sub, dk)).reshape(ch, dk)
        k_rel = (k * jnp.exp2(g_edge - gc)).astype(BF16)
        k_bd = jnp.where(kbd_mask, jnp.concatenate([k_rel] * nkb, axis=1), jnp.zeros((), BF16))
        q_parts = []
        for jb in key_blocks:
            row = jb * sub + edge
            rows = slice(0, jb * sub) if reverse else slice((jb + 1) * sub, ch)
            part = q[rows] * jnp.exp2(gc[rows] - gc[row:row + 1, :])
            pad = jnp.zeros((ch - part.shape[0], dk), F32)
            full = jnp.concatenate([part, pad] if reverse else [pad, part], axis=0)
            q_parts.append(full.astype(BF16))
        q_cat = jnp.concatenate(q_parts, axis=1)
        s_off = lax.dot_general(q_cat, k_bd, nt, preferred_element_type=F32)
        p_parts = []
        for s in range(sub):
            g_s = jnp.broadcast_to(gc4[:, s:s + 1, :], (ns, sub, dk)).reshape(ch, dk)
            p_parts.append((q * jnp.exp2(jnp.minimum(gc - g_s, 0.0))).astype(BF16))
        p_cat = jnp.concatenate(p_parts, axis=1)
        k_dg = jnp.where(kdg_mask, jnp.concatenate([kb] * sub, axis=1), jnp.zeros((), BF16))
        s_diag = lax.dot_general(p_cat, k_dg, nt, preferred_element_type=F32)
        a = s_off + jnp.where(diag_mask, s_diag, 0.0)
        o_ref[:, hs] = o + jnp.dot(a.astype(BF16), v, preferred_element_type=F32)


def _scan_call(q, k, lg, v, c_len, reverse):
    t, d = q.shape
    ch = HGRN_CHUNK
    n_ch = t // ch
    cc = c_len // ch
    if reverse:
        idx = lambda j: (jnp.where(j < cc, cc - 1 - j, n_ch - 1 - (j - cc)), 0)
    else:
        idx = lambda j: (j, 0)
    blk = pl.BlockSpec((ch, d), idx)
    return pl.pallas_call(
        functools.partial(_scan_kernel, reverse=reverse),
        out_shape=jax.ShapeDtypeStruct((t, d), F32),
        grid=(n_ch,),
        in_specs=[blk, blk, blk, blk],
        out_specs=blk,
        scratch_shapes=[pltpu.VMEM((HGRN_HEADS, HGRN_DK, HGRN_DK), F32), pltpu.VMEM((ch, d), F32)],
        compiler_params=_cparams(("arbitrary",)),
        name="hgrn_scan_bwd" if reverse else "hgrn_scan_fwd",
    )(q, k, lg, v)


def _rope_tables(l, c_len):
    n = HEAD_DIM // 4
    inv_freq = ROPE_BASE ** (-jnp.arange(n, dtype=F32) / n)
    rows = l // GRID_W
    ang_row = jnp.arange(rows, dtype=F32)[:, None] * inv_freq[None, :]
    ang_col = jnp.arange(GRID_W, dtype=F32)[:, None] * inv_freq[None, :]
    cr, sr = (jnp.repeat(f(ang_row), GRID_W, axis=0) for f in (jnp.cos, jnp.sin))
    cc, sc = (jnp.tile(f(ang_col), (rows, 1)) for f in (jnp.cos, jnp.sin))
    cos = jnp.concatenate([cr, cr, cc, cc] * 2, axis=1)
    sin = jnp.concatenate([-sr, sr, -sc, sc] * 2, axis=1)
    cos = jnp.concatenate([jnp.ones((c_len, LANES), F32), cos], axis=0)
    sin = jnp.concatenate([jnp.zeros((c_len, LANES), F32), sin], axis=0)
    return cos, sin


def kernel(x, c, ctx, c_ctx, ada_w, ada_b, norm_g, attn_w_qkv, attn_b_qkv, attn_sink, attn_w_o,
           hgrn_w_in, hgrn_lb, hgrn_norm_g, hgrn_w_o, router_w, router_b, moe_w_gu, moe_b_gu,
           moe_w_dn, moe_b_dn):
    b, l, d = x.shape
    c_len = ctx.shape[1]
    depth = ada_w.shape[0]
    assert b == 1 and d == D_MODEL and depth == 2
    assert c_len % ROW_BLOCK == 0 and l % ROW_BLOCK == 0
    cb = c_len // ROW_BLOCK

    cs = jnp.zeros((8, d), F32).at[0].set(c_ctx).at[1].set(c[0])
    mods = _ada_call(cs, ada_w, ada_b)
    mods = mods[:, :2].reshape(depth, 2, 6, d)

    xs = jnp.concatenate([ctx[0], x[0]], axis=0)
    cos, sin = _rope_tables(l, c_len)

    q, k4, v4 = _qkv_call(xs, mods[0], norm_g[0, 0:1], attn_w_qkv[0].astype(BF16),
                          attn_b_qkv[0].reshape(1, -1), cos, sin, cb)
    o = _attn_call(attn_sink[0], q, k4, v4, c_len)
    xs, h2, logits = _out_call("attn", (o,), xs, mods[0], norm_g[0], attn_w_o[0].astype(BF16),
                               router_w[0], router_b[0].reshape(1, -1), None, cb, 0)
    xs = _moe(h2, logits, xs, mods[0], norm_g[0], 0, moe_w_gu, moe_b_gu, moe_w_dn, moe_b_dn, c_len)

    lb_soft = jax.nn.softmax(hgrn_lb.astype(F32), axis=0)
    lb = jnp.cumsum(lb_soft, axis=0)[1] - lb_soft[0]
    qh, kf, lf, kb, lbw, iv, gs = _hgrn_in_call(xs, mods[1], norm_g[1, 0:1], hgrn_w_in[0].astype(BF16),
                                                lb, cb)
    o_f = _scan_call(qh, kf, lf, iv, c_len, reverse=False)
    o_b = _scan_call(qh, kb, lbw, iv, c_len, reverse=True)
    x_lat, h2, logits = _out_call("hgrn", (o_f, o_b, gs), xs, mods[1], norm_g[1],
                                  hgrn_w_o[0].astype(BF16), router_w[1], router_b[1].reshape(1, -1),
                                  hgrn_norm_g[0].reshape(1, -1), cb, cb)
    out = _moe(h2, logits, x_lat, mods[1], norm_g[1], 1, moe_w_gu, moe_b_gu, moe_w_dn, moe_b_dn, 0)
    return out[None]
```

```python
import functools

import jax
import jax.numpy as jnp
from jax import lax
from jax.experimental import pallas as pl
from jax.experimental.pallas import tpu as pltpu

D_MODEL = 1024
GRID_W = 64
RMS_EPS = 1e-6

ATTN_HEADS = 16
ATTN_KV_HEADS = 2
HEAD_DIM = 64
Q_DIM = ATTN_HEADS * HEAD_DIM
KV_DIM = ATTN_KV_HEADS * HEAD_DIM
QKV_DIM = Q_DIM + 2 * KV_DIM
WINDOW = 128
ATTN_BLOCK = 128
ROPE_BASE = 10000.0

HGRN_HEADS = 8
HGRN_DK = 128
HGRN_CHUNK = 64
HGRN_SUB = 8

N_EXPERTS = 32
TOP_K = 4
D_FF = 1024
SWIGLU_LIMIT = 7.0
SWIGLU_ALPHA = 1.702

LANES = 128
ROW_BLOCK = 256
MOE_ROWS = 256
COMBINE_ROWS = 128
VMEM_LIMIT = 56 * 1024 * 1024

F32 = jnp.float32
BF16 = jnp.bfloat16
NEG_BIG = -1e30
LOG2E = 1.4426950408889634


def _cparams(sem, row_dma=False):
    return pltpu.CompilerParams(dimension_semantics=sem, vmem_limit_bytes=VMEM_LIMIT,
                                disable_bounds_checks=row_dma)


def _rms(x, g):
    return x * lax.rsqrt(jnp.mean(x * x, axis=-1, keepdims=True) + RMS_EPS) * g


def _sigmoid(x):
    return 1.0 / (1.0 + jnp.exp(-x))


SEGS = D_MODEL // LANES


def _store_tiled(ref, val):
    n = val.shape[0]
    for s in range(SEGS):
        ref[pl.ds(s, n, stride=SEGS), :] = val[:, s * LANES:(s + 1) * LANES]


def _load_tiled(ref, n):
    return jnp.concatenate([ref[pl.ds(s, n, stride=SEGS), :] for s in range(SEGS)], axis=1)


def _row_tile(ref, r):
    return ref.at[pl.ds(pl.multiple_of(r * SEGS, SEGS), SEGS), :]


def _ada_kernel(c_ref, w_ref, b_ref, o_ref):
    c = c_ref[...]
    s = c * _sigmoid(c)
    o_ref[0] = jnp.dot(s, w_ref[0], precision=lax.Precision.HIGHEST,
                       preferred_element_type=F32) + b_ref[0]


def _ada_call(cs, ada_w, ada_b):
    depth, d, n = ada_w.shape
    tn = 1024
    return pl.pallas_call(
        _ada_kernel,
        out_shape=jax.ShapeDtypeStruct((depth, 8, n), F32),
        grid=(depth, n // tn),
        in_specs=[
            pl.BlockSpec((8, d), lambda i, j: (0, 0)),
            pl.BlockSpec((1, d, tn), lambda i, j: (i, 0, j)),
            pl.BlockSpec((1, 1, tn), lambda i, j: (i, 0, j)),
        ],
        out_specs=pl.BlockSpec((1, 8, tn), lambda i, j: (i, 0, j)),
        compiler_params=_cparams(("arbitrary", "arbitrary")),
        name="adaln",
    )(cs, ada_w, ada_b.reshape(depth, 1, n))


def _swap16(t):
    lane = lax.broadcasted_iota(jnp.int32, t.shape, 1)
    return jnp.where(lane % 32 < 16, pltpu.roll(t, LANES - 16, 1), pltpu.roll(t, 16, 1))


def _stream_rows(c_ref, x_ref, cb):
    return jnp.where(pl.program_id(0) < cb, c_ref[...], x_ref[...])


def _stream_specs(tm, d, cb):
    return [pl.BlockSpec((tm, d), lambda i: (jnp.minimum(i, cb - 1), 0)),
            pl.BlockSpec((tm, d), lambda i: (jnp.maximum(i - cb, 0), 0))]


def _qkv_kernel(c_ref, x_ref, mod_ref, gn_ref, w_ref, b_ref, cos_ref, sin_ref, q_ref, k4_ref, v4_ref, *, cb):
    x = _stream_rows(c_ref, x_ref, cb)
    h = _rms(x, gn_ref[...]) * (1.0 + mod_ref[0, 1:2, :]) + mod_ref[0, 0:1, :]
    hb = h.astype(BF16)
    cos = cos_ref[...]
    sin = sin_ref[...]
    nq = Q_DIM // LANES
    for j in range(nq + 1):
        sl = slice(j * LANES, (j + 1) * LANES)
        t = jnp.dot(hb, w_ref[:, sl], preferred_element_type=F32) + b_ref[:, sl]
        t = t * cos + _swap16(t) * sin
        if j < nq:
            q_ref[:, sl] = (t * (HEAD_DIM ** -0.5 * LOG2E)).astype(q_ref.dtype)
        else:
            kt = t
    sl = slice(Q_DIM + KV_DIM, QKV_DIM)
    vt = jnp.dot(hb, w_ref[:, sl], preferred_element_type=F32) + b_ref[:, sl]
    lo = lax.broadcasted_iota(jnp.int32, kt.shape, 1) < HEAD_DIM
    for t, ref in ((kt, k4_ref), (vt, v4_ref)):
        sw = pltpu.roll(t, HEAD_DIM, 1)
        ref[:, 0 * LANES:1 * LANES] = jnp.where(lo, t, 0.0).astype(ref.dtype)
        ref[:, 1 * LANES:2 * LANES] = jnp.where(lo, 0.0, sw).astype(ref.dtype)
        ref[:, 2 * LANES:3 * LANES] = jnp.where(lo, sw, 0.0).astype(ref.dtype)
        ref[:, 3 * LANES:4 * LANES] = jnp.where(lo, 0.0, t).astype(ref.dtype)


def _qkv_call(ctx, x, mod, gn, w, b, cos, sin, cb):
    d = x.shape[1]
    t = ctx.shape[0] + x.shape[0]
    tm = ROW_BLOCK
    sel = lambda i: (jnp.minimum(i // cb, 1), 0, 0)
    return pl.pallas_call(
        functools.partial(_qkv_kernel, cb=cb),
        out_shape=(jax.ShapeDtypeStruct((t, Q_DIM), BF16),
                   jax.ShapeDtypeStruct((t, 4 * LANES), BF16),
                   jax.ShapeDtypeStruct((t, 4 * LANES), BF16)),
        grid=(t // tm,),
        in_specs=_stream_specs(tm, d, cb) + [
            pl.BlockSpec((1, 6, d), sel),
            pl.BlockSpec((1, d), lambda i: (0, 0)),
            pl.BlockSpec((d, QKV_DIM), lambda i: (0, 0)),
            pl.BlockSpec((1, QKV_DIM), lambda i: (0, 0)),
            pl.BlockSpec((tm, LANES), lambda i: (i, 0)),
            pl.BlockSpec((tm, LANES), lambda i: (i, 0)),
        ],
        out_specs=(pl.BlockSpec((tm, Q_DIM), lambda i: (i, 0)),
                   pl.BlockSpec((tm, 4 * LANES), lambda i: (i, 0)),
                   pl.BlockSpec((tm, 4 * LANES), lambda i: (i, 0))),
        compiler_params=_cparams(("parallel",)),
        name="qkv_rope",
    )(ctx, x, mod, gn, w, b, cos, sin)


def _attn_kernel(sink_ref, q_ref, kp_ref, kc_ref, kn_ref, kx_ref, vp_ref, vc_ref, vn_ref, vx_ref,
                 o_ref, *, cb, n_lat):
    n = pl.program_id(0)
    blk = ATTN_BLOCK
    c = kx_ref.shape[0]
    nw = 3 * blk
    s = lax.broadcasted_iota(jnp.int32, (nw + c, blk), 0)
    r = lax.broadcasted_iota(jnp.int32, (nw + c, blk), 1)
    q_pos = (n - cb) * blk + r
    k_pos = (n - cb - 1) * blk + s
    win_ok = (jnp.abs(q_pos - k_pos) <= WINDOW) & (k_pos >= 0) & (k_pos < n_lat) & (n >= cb)
    valid = (s >= nw) | win_ok
    k_all = jnp.concatenate([kp_ref[...], kc_ref[...], kn_ref[...], kx_ref[...]], axis=0)
    v_all = jnp.concatenate([vp_ref[...], vc_ref[...], vn_ref[...], vx_ref[...]], axis=0)
    n_var = 2 * ATTN_KV_HEADS
    v_t = [v_all[:, j * LANES:(j + 1) * LANES].astype(F32).T.astype(BF16) for j in range(n_var)]
    nt = (((1,), (1,)), ((), ()))
    group = ATTN_HEADS // ATTN_KV_HEADS
    for p in range(ATTN_HEADS // 2):
        g = (2 * p) // group
        qp = q_ref[:, p * LANES:(p + 1) * LANES]
        o_t = jnp.zeros((LANES, blk), F32)
        for half in range(2):
            j = 2 * g + half
            sc = lax.dot_general(k_all[:, j * LANES:(j + 1) * LANES], qp, nt,
                                 preferred_element_type=F32)
            sc = jnp.where(valid, sc, NEG_BIG)
            sink = sink_ref[2 * p + half] * LOG2E
            m = jnp.maximum(jnp.max(sc, axis=0, keepdims=True), sink)
            e = jnp.exp2(sc - m)
            denom = jnp.sum(e, axis=0, keepdims=True) + jnp.exp2(sink - m)
            pv = jnp.dot(v_t[j], e.astype(BF16), preferred_element_type=F32)
            o_t = o_t + pv / denom
        o_ref[:, p * LANES:(p + 1) * LANES] = o_t.T.astype(o_ref.dtype)


def _attn_call(sink, q, k4, v4, c_len):
    t = q.shape[0]
    blk = ATTN_BLOCK
    cb = c_len // blk
    nb = t // blk
    n_lat = t - c_len
    last = nb - 1
    kw = 4 * LANES
    spec_q = pl.BlockSpec((blk, Q_DIM), lambda n: (n, 0))
    prev = pl.BlockSpec((blk, kw), lambda n: (jnp.maximum(n - 1, 0), 0))
    cur = pl.BlockSpec((blk, kw), lambda n: (n, 0))
    nxt = pl.BlockSpec((blk, kw), lambda n: (jnp.minimum(n + 1, last), 0))
    ctx = pl.BlockSpec((c_len, kw), lambda n: (0, 0))
    return pl.pallas_call(
        functools.partial(_attn_kernel, cb=cb, n_lat=n_lat),
        out_shape=jax.ShapeDtypeStruct((t, Q_DIM), BF16),
        grid=(nb,),
        in_specs=[pl.BlockSpec(memory_space=pltpu.SMEM), spec_q,
                  prev, cur, nxt, ctx, prev, cur, nxt, ctx],
        out_specs=pl.BlockSpec((blk, Q_DIM), lambda n: (n, 0)),
        compiler_params=_cparams(("parallel",)),
        name="window_attn",
    )(sink, q, k4, k4, k4, k4, v4, v4, v4, v4)


def _residual_tail(y, x, mod_ref, gn_ref, rw_ref, rb_ref, xo_ref, h2_ref, lg_ref):
    x_new = x + mod_ref[0, 2:3, :] * _rms(y, gn_ref[1:2, :])
    h2 = _rms(x_new, gn_ref[2:3, :]) * (1.0 + mod_ref[0, 4:5, :]) + mod_ref[0, 3:4, :]
    xo_ref[...] = x_new
    _store_tiled(h2_ref, h2)
    hi, mid, lo = _split3(h2)
    lane = lax.broadcasted_iota(jnp.int32, (h2.shape[0], LANES), 1)
    w = rw_ref[...]
    r = (jnp.dot(hi, w, preferred_element_type=F32)
         + jnp.where(lane < 2 * N_EXPERTS, jnp.dot(mid, w, preferred_element_type=F32), 0.0)
         + jnp.where(lane < N_EXPERTS, jnp.dot(lo, w, preferred_element_type=F32), 0.0))
    r = r + pltpu.roll(r, LANES - N_EXPERTS, 1) + pltpu.roll(r, LANES - 2 * N_EXPERTS, 1)
    lg_ref[...] = r[:, :N_EXPERTS] + rb_ref[...]


def _attn_out_kernel(a_ref, c_ref, x_ref, mod_ref, gn_ref, w_ref, rw_ref, rb_ref, xo_ref, h2_ref, lg_ref,
                     *, cb):
    y = jnp.dot(a_ref[...], w_ref[...], preferred_element_type=F32)
    _residual_tail(y, _stream_rows(c_ref, x_ref, cb), mod_ref, gn_ref, rw_ref, rb_ref, xo_ref, h2_ref, lg_ref)


def _hgrn_out_kernel(of_ref, ob_ref, gs_ref, hg_ref, x_ref, mod_ref, gn_ref, w_ref, rw_ref, rb_ref,
                     xo_ref, h2_ref, lg_ref):
    parts = []
    for h in range(HGRN_HEADS):
        sl = slice(h * HGRN_DK, (h + 1) * HGRN_DK)
        o = of_ref[:, sl] + ob_ref[:, sl]
        parts.append(_rms(o, hg_ref[:, sl]))
    a = jnp.concatenate(parts, axis=1) * gs_ref[...].astype(F32)
    y = jnp.dot(a.astype(BF16), w_ref[...], preferred_element_type=F32)
    _residual_tail(y, x_ref[...], mod_ref, gn_ref, rw_ref, rb_ref, xo_ref, h2_ref, lg_ref)


def _out_call(kind, acts, x, mod, gn, w, rw, rb, hg, cb, off):
    t_out = (x[0].shape[0] + x[1].shape[0]) if kind == "attn" else x.shape[0] - off * ROW_BLOCK
    d = D_MODEL
    tm = ROW_BLOCK
    w_hi, w_mid, w_lo = _split3(rw)
    rw = jnp.concatenate([w_hi, w_mid, w_lo, jnp.zeros_like(w_hi)], axis=1)
    row = lambda i: (i + off, 0)
    sel = lambda i: (jnp.minimum((i + off) // cb, 1), 0, 0)
    const = lambda i: (0, 0)
    common_specs = [
        pl.BlockSpec((1, 6, d), sel),
        pl.BlockSpec((4, d), const),
        pl.BlockSpec((d, d), const),
        pl.BlockSpec((d, LANES), const),
        pl.BlockSpec((1, N_EXPERTS), const),
    ]
    if kind == "attn":
        body = functools.partial(_attn_out_kernel, cb=cb)
        in_specs = [pl.BlockSpec((tm, d), row)] + _stream_specs(tm, d, cb) + common_specs
        args = (acts[0], x[0], x[1], mod, gn, w, rw, rb)
    else:
        body = _hgrn_out_kernel
        in_specs = ([pl.BlockSpec((tm, d), row)] * 3 + [pl.BlockSpec((1, d), const)]
                    + [pl.BlockSpec((tm, d), row)] + common_specs)
        args = (acts[0], acts[1], acts[2], hg, x, mod, gn, w, rw, rb)
    return pl.pallas_call(
        body,
        out_shape=(jax.ShapeDtypeStruct((t_out, d), F32),
                   jax.ShapeDtypeStruct((t_out * SEGS, LANES), F32),
                   jax.ShapeDtypeStruct((t_out, N_EXPERTS), F32)),
        grid=(t_out // tm,),
        in_specs=in_specs,
        out_specs=(pl.BlockSpec((tm, d), lambda i: (i, 0)),
                   pl.BlockSpec((tm * SEGS, LANES), lambda i: (i, 0)),
                   pl.BlockSpec((tm, N_EXPERTS), lambda i: (i, 0))),
        compiler_params=_cparams(("parallel",)),
        name=kind + "_out_residual",
    )(*args)


def _router_kernel(lg_ref, slab_ref, cnt_ref, run_ref):
    i = pl.program_id(0)

    @pl.when(i == 0)
    def _():
        run_ref[...] = jnp.zeros_like(run_ref)

    lg = lg_ref[...]
    tb = lg.shape[0]
    lane = lax.broadcasted_iota(jnp.int32, lg.shape, 1).astype(F32)
    sels, tops, idxs = [], [], []
    for _ in range(TOP_K):
        m = jnp.max(lg, axis=-1, keepdims=True)
        idx = jnp.min(jnp.where(lg == m, lane, float(N_EXPERTS)), axis=-1, keepdims=True)
        sel = lane == idx
        sels.append(sel)
        tops.append(m)
        idxs.append(idx)
        lg = jnp.where(sel, -jnp.inf, lg)
    ws = [jnp.exp(tk - tops[0]) for tk in tops]
    wsum = ws[0] + ws[1] + ws[2] + ws[3]
    chosen = (sels[0] | sels[1] | sels[2] | sels[3])
    onehot = jnp.where(chosen, 1.0, 0.0)
    rr = lax.broadcasted_iota(jnp.int32, (tb, tb), 0)
    cc = lax.broadcasted_iota(jnp.int32, (tb, tb), 1)
    tri = jnp.where(cc < rr, 1.0, 0.0).astype(BF16)
    before = jnp.dot(tri, onehot.astype(BF16), preferred_element_type=F32) + run_ref[...]
    out_lane = lax.broadcasted_iota(jnp.int32, (tb, LANES), 1)
    slab = jnp.zeros((tb, LANES), F32)
    for k in range(TOP_K):
        rank = jnp.sum(jnp.where(sels[k], before, 0.0), axis=-1, keepdims=True)
        slab = jnp.where(out_lane == k, idxs[k], slab)
        slab = jnp.where(out_lane == TOP_K + k, rank, slab)
        slab = jnp.where(out_lane == 2 * TOP_K + k, ws[k] / wsum, slab)
    slab_ref[...] = slab
    run_ref[...] = run_ref[...] + jnp.sum(onehot, axis=0, keepdims=True)
    cnt_ref[...] = run_ref[...]


def _router_call(logits):
    t = logits.shape[0]
    tb = ROW_BLOCK
    return pl.pallas_call(
        _router_kernel,
        out_shape=(jax.ShapeDtypeStruct((t, LANES), F32),
                   jax.ShapeDtypeStruct((1, N_EXPERTS), F32)),
        grid=(t // tb,),
        in_specs=[pl.BlockSpec((tb, N_EXPERTS), lambda i: (i, 0))],
        out_specs=(pl.BlockSpec((tb, LANES), lambda i: (i, 0)),
                   pl.BlockSpec((1, N_EXPERTS), lambda i: (0, 0))),
        scratch_shapes=[pltpu.VMEM((1, N_EXPERTS), F32)],
        compiler_params=_cparams(("arbitrary",)),
        name="router_topk",
    )(logits)


def _expert_kernel(be_ref, nu_ref, src0_ref, src1_ref, x_hbm, wgu_ref, bgu_ref, wdn_ref, bdn_ref, o_ref,
                   xbuf, sems, wgu_bf, wdn_bf):
    i = pl.program_id(0)
    rows = o_ref.shape[0] // SEGS
    slot = i % 2
    n_used = nu_ref[0]

    def gather(idx_ref, s):
        def issue(r, carry):
            pltpu.make_async_copy(_row_tile(x_hbm, idx_ref[0, 0, r]), _row_tile(xbuf.at[s], r),
                                  sems.at[s]).start()
            return carry

        lax.fori_loop(0, rows, issue, 0, unroll=8)

    @pl.when(i == 0)
    def _():
        gather(src0_ref, 0)

    @pl.when(i + 1 < n_used)
    def _():
        gather(src1_ref, 1 - slot)

    prev = be_ref[jnp.maximum(i - 1, 0)]
    fresh = (i == 0) | (be_ref[i] != prev)

    @pl.when(fresh)
    def _():
        wgu_bf[...] = wgu_ref[...].astype(BF16)
        wdn_bf[...] = wdn_ref[...].astype(BF16)

    @pl.when(i < n_used)
    def _():
        pltpu.make_async_copy(x_hbm.at[pl.ds(0, rows * SEGS), :], xbuf.at[slot], sems.at[slot]).wait()
        xb = _load_tiled(xbuf.at[slot], rows).astype(BF16)
        gu = jnp.dot(xb, wgu_bf[...], preferred_element_type=F32) + bgu_ref[...]
        g = jnp.minimum(gu[:, :D_FF], SWIGLU_LIMIT)
        u = jnp.clip(gu[:, D_FF:], -SWIGLU_LIMIT, SWIGLU_LIMIT)
        hdn = g * _sigmoid(SWIGLU_ALPHA * g) * (u + 1.0)
        y = jnp.dot(hdn.astype(BF16), wdn_bf[...], preferred_element_type=F32) + bdn_ref[...]
        _store_tiled(o_ref, y)

    @pl.when(i >= n_used)
    def _():
        o_ref[...] = jnp.zeros_like(o_ref)


def _expert_call(blk_e, n_used, src_tok, h2, layer, w_gu, b_gu, w_dn, b_dn):
    d = D_MODEL
    tr = MOE_ROWS
    n_blk = blk_e.shape[0]
    depth, e, _, f2 = w_gu.shape
    last = n_blk - 1
    return pl.pallas_call(
        _expert_kernel,
        out_shape=jax.ShapeDtypeStruct((n_blk * tr * SEGS, LANES), F32),
        grid_spec=pltpu.PrefetchScalarGridSpec(
            num_scalar_prefetch=2,
            grid=(n_blk,),
            in_specs=[
                pl.BlockSpec((1, 1, tr), lambda i, be, nu: (0, 0, 0), memory_space=pltpu.SMEM),
                pl.BlockSpec((1, 1, tr), lambda i, be, nu: (jnp.minimum(i + 1, last), 0, 0),
                             memory_space=pltpu.SMEM),
                pl.BlockSpec(memory_space=pl.ANY),
                pl.BlockSpec((None, None, d, f2), lambda i, be, nu: (layer, be[i], 0, 0)),
                pl.BlockSpec((None, None, 1, f2), lambda i, be, nu: (layer, be[i], 0, 0)),
                pl.BlockSpec((None, None, D_FF, d), lambda i, be, nu: (layer, be[i], 0, 0)),
                pl.BlockSpec((None, None, 1, d), lambda i, be, nu: (layer, be[i], 0, 0)),
            ],
            out_specs=pl.BlockSpec((tr * SEGS, LANES), lambda i, be, nu: (i, 0)),
            scratch_shapes=[pltpu.VMEM((2, tr * SEGS, LANES), F32), pltpu.SemaphoreType.DMA((2,)),
                            pltpu.VMEM((d, f2), BF16), pltpu.VMEM((D_FF, d), BF16)],
        ),
        compiler_params=_cparams(("arbitrary",), row_dma=True),
        name="moe_experts",
    )(blk_e, n_used, src_tok.reshape(n_blk, 1, tr), src_tok.reshape(n_blk, 1, tr), h2,
      w_gu, b_gu.reshape(depth, e, 1, f2), w_dn, b_dn.reshape(depth, e, 1, d))


def _combine_kernel(d0_ref, d1_ref, y_hbm, slab_ref, x_ref, mod_ref, gn_ref, o_ref, buf, sems):
    i = pl.program_id(0)
    tb = o_ref.shape[0]
    slot = i % 2

    def gather(idx_ref, s):
        def issue(r, carry):
            for k in range(TOP_K):
                row = idx_ref[0, 0, k * tb + r]
                pltpu.make_async_copy(_row_tile(y_hbm, row), _row_tile(buf.at[s, k], r),
                                      sems.at[s]).start(priority=k % 2)
            return carry

        lax.fori_loop(0, tb, issue, 0, unroll=4)

    @pl.when(i == 0)
    def _():
        gather(d0_ref, 0)

    @pl.when(i + 1 < pl.num_programs(0))
    def _():
        gather(d1_ref, 1 - slot)

    for k in range(TOP_K):
        pltpu.make_async_copy(y_hbm.at[pl.ds(0, tb * SEGS), :], buf.at[slot, k], sems.at[slot]).wait()
    slab = slab_ref[...]
    y2 = jnp.zeros(o_ref.shape, F32)
    for k in range(TOP_K):
        y2 = y2 + _load_tiled(buf.at[slot, k], tb) * slab[:, 2 * TOP_K + k:2 * TOP_K + k + 1]
    o_ref[...] = x_ref[...] + mod_ref[0, 5:6, :] * _rms(y2, gn_ref[3:4, :])


def _combine_call(dest, y_sorted, slab, x, mod, gn, cb_rows):
    t, d = x.shape
    tb = COMBINE_ROWS
    nb = t // tb
    cbb = max(cb_rows // tb, 1)
    sel = (lambda i: (jnp.minimum(i // cbb, 1), 0, 0)) if cb_rows else (lambda i: (1, 0, 0))
    dest3 = dest.reshape(TOP_K, nb, tb).transpose(1, 0, 2).reshape(nb, 1, TOP_K * tb)
    return pl.pallas_call(
        _combine_kernel,
        out_shape=jax.ShapeDtypeStruct((t, d), F32),
        grid=(nb,),
        in_specs=[
            pl.BlockSpec((1, 1, tb * TOP_K), lambda i: (0, 0, 0), memory_space=pltpu.SMEM),
            pl.BlockSpec((1, 1, tb * TOP_K), lambda i: (jnp.minimum(i + 1, nb - 1), 0, 0),
                         memory_space=pltpu.SMEM),
            pl.BlockSpec(memory_space=pl.ANY),
            pl.BlockSpec((tb, LANES), lambda i: (i, 0)),
            pl.BlockSpec((tb, d), lambda i: (i, 0)),
            pl.BlockSpec((1, 6, d), sel),
            pl.BlockSpec((4, d), lambda i: (0, 0)),
        ],
        out_specs=pl.BlockSpec((tb, d), lambda i: (i, 0)),
        scratch_shapes=[pltpu.VMEM((2, TOP_K, tb * SEGS, LANES), F32), pltpu.SemaphoreType.DMA((2,))],
        compiler_params=_cparams(("arbitrary",), row_dma=True),
        name="moe_combine",
    )(dest3, dest3, y_sorted, slab, x, mod, gn)


INVERT_BLOCK = 2048


def _invert_kernel(dest_ref, zeros_hbm, src_ref):
    k, j = pl.program_id(0), pl.program_id(1)
    n = dest_ref.shape[2]

    @pl.when((k == 0) & (j == 0))
    def _():
        pltpu.sync_copy(zeros_hbm, src_ref)

    tok0 = j * n

    def put(t, carry):
        src_ref[dest_ref[0, 0, t]] = tok0 + t
        return carry

    lax.fori_loop(0, n, put, 0, unroll=16)


def _invert_call(dest, n_rows):
    t = dest.shape[1]
    blk = max(b for b in range(LANES, INVERT_BLOCK + 1, LANES) if t % b == 0)
    nb = t // blk
    return pl.pallas_call(
        _invert_kernel,
        out_shape=jax.ShapeDtypeStruct((n_rows,), jnp.int32),
        grid=(TOP_K, nb),
        in_specs=[pl.BlockSpec((1, 1, blk), lambda k, j: (k * nb + j, 0, 0), memory_space=pltpu.SMEM),
                  pl.BlockSpec(memory_space=pl.ANY)],
        out_specs=pl.BlockSpec(memory_space=pltpu.SMEM),
        compiler_params=_cparams(("arbitrary", "arbitrary")),
        name="moe_invert",
    )(dest.reshape(TOP_K * nb, 1, blk), jnp.zeros((n_rows,), jnp.int32))


def _moe(h2, logits, x, mod, gn, layer, w_gu, b_gu, w_dn, b_dn, cb_rows):
    t = logits.shape[0]
    tr = MOE_ROWS
    slab, counts = _router_call(logits)
    slab_t = slab[:, :2 * TOP_K].T
    e_idx = slab_t[:TOP_K].astype(jnp.int32)
    rank = slab_t[TOP_K:].astype(jnp.int32)
    counts = counts[0].astype(jnp.int32)
    padded = (counts + tr - 1) // tr * tr
    p_end = jnp.cumsum(padded)
    p_start = p_end - padded
    dest = p_start[e_idx] + rank
    a = t * TOP_K
    n_blk = (a + N_EXPERTS * (tr - 1) + tr - 1) // tr
    src_tok = _invert_call(dest, n_blk * tr)
    blk_start = jnp.arange(n_blk, dtype=jnp.int32) * tr
    blk_e = jnp.sum((p_end[None, :] <= blk_start[:, None]).astype(jnp.int32), axis=1)
    blk_e = jnp.minimum(blk_e, N_EXPERTS - 1)
    n_used = (p_end[-1] // tr).astype(jnp.int32).reshape(1)
    ys = _expert_call(blk_e, n_used, src_tok, h2, layer, w_gu, b_gu, w_dn, b_dn)
    return _combine_call(dest, ys, slab, x, mod, gn, cb_rows)


def _hgrn_in_kernel(x_ref, mod_ref, gn_ref, w_ref, lb_ref, q_ref, kf_ref, lf_ref, kb_ref, lbw_ref,
                    i_ref, gs_ref):
    x = x_ref[...]
    h = _rms(x, gn_ref[...]) * (1.0 + mod_ref[0, 1:2, :]) + mod_ref[0, 0:1, :]
    hb = h.astype(BF16)
    f = D_MODEL

    def proj(sec):
        return jnp.dot(hb, w_ref[:, sec * f:(sec + 1) * f], preferred_element_type=F32)

    z = proj(0)
    q_ref[...] = (z * _sigmoid(z)).astype(q_ref.dtype)
    for sec, k_ref, l_ref in ((1, kf_ref, lf_ref), (2, kb_ref, lbw_ref)):
        z = proj(sec)
        lb = lb_ref[sec - 1:sec, :]
        sg = _sigmoid(z)
        l_ref[...] = jnp.log(lb + (1.0 - lb) * sg)
        k_ref[...] = ((1.0 - lb) * _sigmoid(-z)).astype(k_ref.dtype)
    i_ref[...] = proj(3).astype(i_ref.dtype)
    z = proj(4)
    gs_ref[...] = (z * _sigmoid(z)).astype(gs_ref.dtype)


def _hgrn_in_call(xs, mod, gn, w, lb, cb):
    t, d = xs.shape
    tm = ROW_BLOCK
    sel = lambda i: (jnp.minimum(i // cb, 1), 0, 0)
    row = pl.BlockSpec((tm, d), lambda i: (i, 0))
    dts = (BF16, BF16, F32, BF16, F32, BF16, BF16)
    return pl.pallas_call(
        _hgrn_in_kernel,
        out_shape=tuple(jax.ShapeDtypeStruct((t, d), dt) for dt in dts),
        grid=(t // tm,),
        in_specs=[row, pl.BlockSpec((1, 6, d), sel), pl.BlockSpec((1, d), lambda i: (0, 0)),
                  pl.BlockSpec(w.shape, lambda i: (0, 0)), pl.BlockSpec((2, d), lambda i: (0, 0))],
        out_specs=tuple(row for _ in dts),
        compiler_params=_cparams(("parallel",)),
        name="hgrn_in_proj",
    )(xs, mod, gn, w, lb)


def _split3(x):
    hi = x.astype(BF16)
    r1 = x - hi.astype(F32)
    mid = r1.astype(BF16)
    lo = (r1 - mid.astype(F32)).astype(BF16)
    return hi, mid, lo


def _scan_kernel(q_ref, k_ref, g_ref, v_ref, o_ref, st_ref, gcum_ref, *, reverse):
    j = pl.program_id(0)
    ch, sub = HGRN_CHUNK, HGRN_SUB
    ns = ch // sub
    dk = HGRN_DK

    @pl.when(j == 0)
    def _():
        st_ref[...] = jnp.zeros_like(st_ref)

    rr = lax.broadcasted_iota(jnp.int32, (ch, ch), 0)
    cc = lax.broadcasted_iota(jnp.int32, (ch, ch), 1)
    causal = (cc >= rr) if reverse else (cc <= rr)
    tri = jnp.where(causal, 1.0, 0.0).astype(BF16)
    hi, mid, lo = _split3(g_ref[...])
    gcum_ref[...] = LOG2E * (jnp.dot(tri, hi, preferred_element_type=F32)
                             + jnp.dot(tri, mid, preferred_element_type=F32)
                             + jnp.dot(tri, lo, preferred_element_type=F32))
    diag_mask = ((rr // sub) == (cc // sub)) & causal
    key_blocks = list(range(1, ns)) if reverse else list(range(ns - 1))
    nkb = len(key_blocks)
    kbd_mask = (lax.broadcasted_iota(jnp.int32, (ch, nkb * dk), 1) // dk + key_blocks[0]
                == lax.broadcasted_iota(jnp.int32, (ch, nkb * dk), 0) // sub)
    kdg_mask = (lax.broadcasted_iota(jnp.int32, (ch, sub * dk), 1) // dk
                == lax.broadcasted_iota(jnp.int32, (ch, sub * dk), 0) % sub)
    end_row = 0 if reverse else ch - 1
    edge = 0 if reverse else sub - 1
    nt = (((1,), (1,)), ((), ()))
    tn = (((0,), (0,)), ((), ()))

    for h in range(HGRN_HEADS):
        hs = slice(h * dk, (h + 1) * dk)
        q = q_ref[:, hs].astype(F32)
        kb = k_ref[:, hs]
        k = kb.astype(F32)
        v = v_ref[:, hs]
        gc = gcum_ref[:, hs]
        st = st_ref[h]
        g_end = gc[end_row:end_row + 1, :]
        qg = (q * jnp.exp2(gc)).astype(BF16)
        o = lax.dot_general(qg, st.astype(BF16), nt, preferred_element_type=F32)
        k_end = (k * jnp.exp2(g_end - gc)).astype(BF16)
        st_ref[h] = st * jnp.exp2(g_end) + lax.dot_general(v, k_end, tn, preferred_element_type=F32)
        gc4 = gc.reshape(ns, sub, dk)
        g_edge = jnp.broadcast_to(gc4[:, edge:edge + 1, :], (ns, sub, dk)).reshape(ch, dk)
        k_rel = (k * jnp.exp2(g_edge - gc)).astype(BF16)
        k_bd = jnp.where(kbd_mask, jnp.concatenate([k_rel] * nkb, axis=1), jnp.zeros((), BF16))
        q_parts = []
        for jb in key_blocks:
            row = jb * sub + edge
            rows = slice(0, jb * sub) if reverse else slice((jb + 1) * sub, ch)
            part = q[rows] * jnp.exp2(gc[rows] - gc[row:row + 1, :])
            pad = jnp.zeros((ch - part.shape[0], dk), F32)
            full = jnp.concatenate([part, pad] if reverse else [pad, part], axis=0)
            q_parts.append(full.astype(BF16))
        q_cat = jnp.concatenate(q_parts, axis=1)
        s_off = lax.dot_general(q_cat, k_bd, nt, preferred_element_type=F32)
        p_parts = []
        for s in range(sub):
            g_s = jnp.broadcast_to(gc4[:, s:s + 1, :], (ns, sub, dk)).reshape(ch, dk)
            p_parts.append((q * jnp.exp2(jnp.minimum(gc - g_s, 0.0))).astype(BF16))
        p_cat = jnp.concatenate(p_parts, axis=1)
        k_dg = jnp.where(kdg_mask, jnp.concatenate([kb] * sub, axis=1), jnp.zeros((), BF16))
        s_diag = lax.dot_general(p_cat, k_dg, nt, preferred_element_type=F32)
        a = s_off + jnp.where(diag_mask, s_diag, 0.0)
        o_ref[:, hs] = o + jnp.dot(a.astype(BF16), v, preferred_element_type=F32)


def _scan_call(q, k, lg, v, c_len, reverse):
    t, d = q.shape
    ch = HGRN_CHUNK
    n_ch = t // ch
    cc = c_len // ch
    if reverse:
        idx = lambda j: (jnp.where(j < cc, cc - 1 - j, n_ch - 1 - (j - cc)), 0)
    else:
        idx = lambda j: (j, 0)
    blk = pl.BlockSpec((ch, d), idx)
    return pl.pallas_call(
        functools.partial(_scan_kernel, reverse=reverse),
        out_shape=jax.ShapeDtypeStruct((t, d), F32),
        grid=(n_ch,),
        in_specs=[blk, blk, blk, blk],
        out_specs=blk,
        scratch_shapes=[pltpu.VMEM((HGRN_HEADS, HGRN_DK, HGRN_DK), F32), pltpu.VMEM((ch, d), F32)],
        compiler_params=_cparams(("arbitrary",)),
        name="hgrn_scan_bwd" if reverse else "hgrn_scan_fwd",
    )(q, k, lg, v)


def _rope_tables(l, c_len):
    n = HEAD_DIM // 4
    inv_freq = ROPE_BASE ** (-jnp.arange(n, dtype=F32) / n)
    rows = l // GRID_W
    ang_row = jnp.arange(rows, dtype=F32)[:, None] * inv_freq[None, :]
    ang_col = jnp.arange(GRID_W, dtype=F32)[:, None] * inv_freq[None, :]
    cr, sr, cc, sc = jnp.cos(ang_row), jnp.sin(ang_row), jnp.cos(ang_col), jnp.sin(ang_col)
    zr, zc = jnp.zeros_like(cr), jnp.zeros_like(cc)
    cos_r = jnp.concatenate([cr, cr, zr, zr] * 2, axis=1)
    sin_r = jnp.concatenate([-sr, sr, zr, zr] * 2, axis=1)
    cos_c = jnp.concatenate([zc, zc, cc, cc] * 2, axis=1)
    sin_c = jnp.concatenate([zc, zc, -sc, sc] * 2, axis=1)
    cos = (cos_r[:, None, :] + cos_c[None, :, :]).reshape(l, LANES)
    sin = (sin_r[:, None, :] + sin_c[None, :, :]).reshape(l, LANES)
    cos = jnp.concatenate([jnp.ones((c_len, LANES), F32), cos], axis=0)
    sin = jnp.concatenate([jnp.zeros((c_len, LANES), F32), sin], axis=0)
    return cos, sin


def kernel(x, c, ctx, c_ctx, ada_w, ada_b, norm_g, attn_w_qkv, attn_b_qkv, attn_sink, attn_w_o,
           hgrn_w_in, hgrn_lb, hgrn_norm_g, hgrn_w_o, router_w, router_b, moe_w_gu, moe_b_gu,
           moe_w_dn, moe_b_dn):
    b, l, d = x.shape
    c_len = ctx.shape[1]
    depth = ada_w.shape[0]
    assert b == 1 and d == D_MODEL and depth == 2
    assert c_len % ROW_BLOCK == 0 and l % ROW_BLOCK == 0
    cb = c_len // ROW_BLOCK

    cs = jnp.zeros((8, d), F32).at[0].set(c_ctx).at[1].set(c[0])
    mods = _ada_call(cs, ada_w, ada_b)
    mods = mods[:, :2].reshape(depth, 2, 6, d)

    cos, sin = _rope_tables(l, c_len)

    q, k4, v4 = _qkv_call(ctx[0], x[0], mods[0], norm_g[0, 0:1], attn_w_qkv[0].astype(BF16),
                          attn_b_qkv[0].reshape(1, -1), cos, sin, cb)
    o = _attn_call(attn_sink[0], q, k4, v4, c_len)
    xs, h2, logits = _out_call("attn", (o,), (ctx[0], x[0]), mods[0], norm_g[0], attn_w_o[0].astype(BF16),
                               router_w[0], router_b[0].reshape(1, -1), None, cb, 0)
    xs = _moe(h2, logits, xs, mods[0], norm_g[0], 0, moe_w_gu, moe_b_gu, moe_w_dn, moe_b_dn, c_len)

    lb_soft = jax.nn.softmax(hgrn_lb.astype(F32), axis=0)
    lb = jnp.cumsum(lb_soft, axis=0)[1] - lb_soft[0]
    qh, kf, lf, kb, lbw, iv, gs = _hgrn_in_call(xs, mods[1], norm_g[1, 0:1], hgrn_w_in[0].astype(BF16),
                                                lb, cb)
    o_f = _scan_call(qh, kf, lf, iv, c_len, reverse=False)
    o_b = _scan_call(qh, kb, lbw, iv, c_len, reverse=True)
    x_lat, h2, logits = _out_call("hgrn", (o_f, o_b, gs), xs, mods[1], norm_g[1],
                                  hgrn_w_o[0].astype(BF16), router_w[1], router_b[1].reshape(1, -1),
                                  hgrn_norm_g[0].reshape(1, -1), cb, cb)
    out = _moe(h2, logits, x_lat, mods[1], norm_g[1], 1, moe_w_gu, moe_b_gu, moe_w_dn, moe_b_dn, 0)
    return out[None]
```

```python
import functools

import jax
import jax.numpy as jnp
from jax import lax
from jax.experimental import pallas as pl
from jax.experimental.pallas import tpu as pltpu

D_MODEL = 1024
GRID_W = 64
RMS_EPS = 1e-6

ATTN_HEADS = 16
ATTN_KV_HEADS = 2
HEAD_DIM = 64
Q_DIM = ATTN_HEADS * HEAD_DIM
KV_DIM = ATTN_KV_HEADS * HEAD_DIM
QKV_DIM = Q_DIM + 2 * KV_DIM
WINDOW = 128
ATTN_BLOCK = 128
ROPE_BASE = 10000.0

HGRN_HEADS = 8
HGRN_DK = 128
HGRN_CHUNK = 64
HGRN_SUB = 8

N_EXPERTS = 32
TOP_K = 4
D_FF = 1024
SWIGLU_LIMIT = 7.0
SWIGLU_ALPHA = 1.702

LANES = 128
ROW_BLOCK = 256
MOE_ROWS = 256
COMBINE_ROWS = 128
VMEM_LIMIT = 56 * 1024 * 1024

F32 = jnp.float32
BF16 = jnp.bfloat16
NEG_BIG = -1e30
LOG2E = 1.4426950408889634


def _cparams(sem, row_dma=False):
    return pltpu.CompilerParams(dimension_semantics=sem, vmem_limit_bytes=VMEM_LIMIT,
                                disable_bounds_checks=row_dma)


def _rms(x, g):
    return x * lax.rsqrt(jnp.mean(x * x, axis=-1, keepdims=True) + RMS_EPS) * g


def _sigmoid(x):
    return 1.0 / (1.0 + jnp.exp(-x))


SEGS = D_MODEL // LANES


def _store_tiled(ref, val):
    n = val.shape[0]
    for s in range(SEGS):
        ref[pl.ds(s, n, stride=SEGS), :] = val[:, s * LANES:(s + 1) * LANES]


def _load_tiled(ref, n):
    return jnp.concatenate([ref[pl.ds(s, n, stride=SEGS), :] for s in range(SEGS)], axis=1)


def _row_tile(ref, r):
    return ref.at[pl.ds(pl.multiple_of(r * SEGS, SEGS), SEGS), :]


def _ada_kernel(c_ref, w_ref, b_ref, o_ref):
    c = c_ref[...]
    s = c * _sigmoid(c)
    o_ref[0] = jnp.dot(s, w_ref[0], precision=lax.Precision.HIGHEST,
                       preferred_element_type=F32) + b_ref[0]


def _ada_call(cs, ada_w, ada_b):
    depth, d, n = ada_w.shape
    tn = 1024
    return pl.pallas_call(
        _ada_kernel,
        out_shape=jax.ShapeDtypeStruct((depth, 8, n), F32),
        grid=(depth, n // tn),
        in_specs=[
            pl.BlockSpec((8, d), lambda i, j: (0, 0)),
            pl.BlockSpec((1, d, tn), lambda i, j: (i, 0, j)),
            pl.BlockSpec((1, 1, tn), lambda i, j: (i, 0, j)),
        ],
        out_specs=pl.BlockSpec((1, 8, tn), lambda i, j: (i, 0, j)),
        compiler_params=_cparams(("arbitrary", "arbitrary")),
        name="adaln",
    )(cs, ada_w, ada_b.reshape(depth, 1, n))


def _swap16(t):
    lane = lax.broadcasted_iota(jnp.int32, t.shape, 1)
    return jnp.where(lane % 32 < 16, pltpu.roll(t, LANES - 16, 1), pltpu.roll(t, 16, 1))


def _stream_rows(c_ref, x_ref, cb):
    return jnp.where(pl.program_id(0) < cb, c_ref[...], x_ref[...])


def _stream_specs(tm, d, cb):
    return [pl.BlockSpec((tm, d), lambda i: (jnp.minimum(i, cb - 1), 0)),
            pl.BlockSpec((tm, d), lambda i: (jnp.maximum(i - cb, 0), 0))]


def _qkv_kernel(c_ref, x_ref, mod_ref, gn_ref, w_ref, b_ref, cos_ref, sin_ref, q_ref, k4_ref, v4_ref, *, cb):
    x = _stream_rows(c_ref, x_ref, cb)
    h = _rms(x, gn_ref[...]) * (1.0 + mod_ref[0, 1:2, :]) + mod_ref[0, 0:1, :]
    hb = h.astype(BF16)
    cos = cos_ref[...]
    sin = sin_ref[...]
    nq = Q_DIM // LANES
    for j in range(nq + 1):
        sl = slice(j * LANES, (j + 1) * LANES)
        t = jnp.dot(hb, w_ref[:, sl], preferred_element_type=F32) + b_ref[:, sl]
        t = t * cos + _swap16(t) * sin
        if j < nq:
            q_ref[:, sl] = (t * (HEAD_DIM ** -0.5 * LOG2E)).astype(q_ref.dtype)
        else:
            kt = t
    sl = slice(Q_DIM + KV_DIM, QKV_DIM)
    vt = jnp.dot(hb, w_ref[:, sl], preferred_element_type=F32) + b_ref[:, sl]
    lo = lax.broadcasted_iota(jnp.int32, kt.shape, 1) < HEAD_DIM
    for t, ref in ((kt, k4_ref), (vt, v4_ref)):
        sw = pltpu.roll(t, HEAD_DIM, 1)
        ref[:, 0 * LANES:1 * LANES] = jnp.where(lo, t, 0.0).astype(ref.dtype)
        ref[:, 1 * LANES:2 * LANES] = jnp.where(lo, 0.0, sw).astype(ref.dtype)
        ref[:, 2 * LANES:3 * LANES] = jnp.where(lo, sw, 0.0).astype(ref.dtype)
        ref[:, 3 * LANES:4 * LANES] = jnp.where(lo, 0.0, t).astype(ref.dtype)


def _qkv_call(ctx, x, mod, gn, w, b, cos, sin, cb):
    d = x.shape[1]
    t = ctx.shape[0] + x.shape[0]
    tm = ROW_BLOCK
    sel = lambda i: (jnp.minimum(i // cb, 1), 0, 0)
    return pl.pallas_call(
        functools.partial(_qkv_kernel, cb=cb),
        out_shape=(jax.ShapeDtypeStruct((t, Q_DIM), BF16),
                   jax.ShapeDtypeStruct((t, 4 * LANES), BF16),
                   jax.ShapeDtypeStruct((t, 4 * LANES), BF16)),
        grid=(t // tm,),
        in_specs=_stream_specs(tm, d, cb) + [
            pl.BlockSpec((1, 6, d), sel),
            pl.BlockSpec((1, d), lambda i: (0, 0)),
            pl.BlockSpec((d, QKV_DIM), lambda i: (0, 0)),
            pl.BlockSpec((1, QKV_DIM), lambda i: (0, 0)),
            pl.BlockSpec((tm, LANES), lambda i: (i, 0)),
            pl.BlockSpec((tm, LANES), lambda i: (i, 0)),
        ],
        out_specs=(pl.BlockSpec((tm, Q_DIM), lambda i: (i, 0)),
                   pl.BlockSpec((tm, 4 * LANES), lambda i: (i, 0)),
                   pl.BlockSpec((tm, 4 * LANES), lambda i: (i, 0))),
        compiler_params=_cparams(("parallel",)),
        name="qkv_rope",
    )(ctx, x, mod, gn, w, b, cos, sin)


def _attn_kernel(sink_ref, q_ref, kp_ref, kc_ref, kn_ref, kx_ref, vp_ref, vc_ref, vn_ref, vx_ref,
                 o_ref, *, cb, n_lat):
    n = pl.program_id(0)
    blk = ATTN_BLOCK
    c = kx_ref.shape[0]
    nw = 3 * blk
    s = lax.broadcasted_iota(jnp.int32, (nw + c, blk), 0)
    r = lax.broadcasted_iota(jnp.int32, (nw + c, blk), 1)
    q_pos = (n - cb) * blk + r
    k_pos = (n - cb - 1) * blk + s
    win_ok = (jnp.abs(q_pos - k_pos) <= WINDOW) & (k_pos >= 0) & (k_pos < n_lat) & (n >= cb)
    valid = (s >= nw) | win_ok
    k_all = jnp.concatenate([kp_ref[...], kc_ref[...], kn_ref[...], kx_ref[...]], axis=0)
    v_all = jnp.concatenate([vp_ref[...], vc_ref[...], vn_ref[...], vx_ref[...]], axis=0)
    n_var = 2 * ATTN_KV_HEADS
    v_t = [v_all[:, j * LANES:(j + 1) * LANES].astype(F32).T.astype(BF16) for j in range(n_var)]
    nt = (((1,), (1,)), ((), ()))
    group = ATTN_HEADS // ATTN_KV_HEADS
    for p in range(ATTN_HEADS // 2):
        g = (2 * p) // group
        qp = q_ref[:, p * LANES:(p + 1) * LANES]
        o_t = jnp.zeros((LANES, blk), F32)
        for half in range(2):
            j = 2 * g + half
            sc = lax.dot_general(k_all[:, j * LANES:(j + 1) * LANES], qp, nt,
                                 preferred_element_type=F32)
            sc = jnp.where(valid, sc, NEG_BIG)
            sink = sink_ref[2 * p + half] * LOG2E
            m = jnp.maximum(jnp.max(sc, axis=0, keepdims=True), sink)
            e = jnp.exp2(sc - m)
            denom = jnp.sum(e, axis=0, keepdims=True) + jnp.exp2(sink - m)
            pv = jnp.dot(v_t[j], e.astype(BF16), preferred_element_type=F32)
            o_t = o_t + pv / denom
        o_ref[:, p * LANES:(p + 1) * LANES] = o_t.T.astype(o_ref.dtype)


def _attn_call(sink, q, k4, v4, c_len):
    t = q.shape[0]
    blk = ATTN_BLOCK
    cb = c_len // blk
    nb = t // blk
    n_lat = t - c_len
    last = nb - 1
    kw = 4 * LANES
    spec_q = pl.BlockSpec((blk, Q_DIM), lambda n: (n, 0))
    prev = pl.BlockSpec((blk, kw), lambda n: (jnp.maximum(n - 1, 0), 0))
    cur = pl.BlockSpec((blk, kw), lambda n: (n, 0))
    nxt = pl.BlockSpec((blk, kw), lambda n: (jnp.minimum(n + 1, last), 0))
    ctx = pl.BlockSpec((c_len, kw), lambda n: (0, 0))
    return pl.pallas_call(
        functools.partial(_attn_kernel, cb=cb, n_lat=n_lat),
        out_shape=jax.ShapeDtypeStruct((t, Q_DIM), BF16),
        grid=(nb,),
        in_specs=[pl.BlockSpec(memory_space=pltpu.SMEM), spec_q,
                  prev, cur, nxt, ctx, prev, cur, nxt, ctx],
        out_specs=pl.BlockSpec((blk, Q_DIM), lambda n: (n, 0)),
        compiler_params=_cparams(("parallel",)),
        name="window_attn",
    )(sink, q, k4, k4, k4, k4, v4, v4, v4, v4)


def _residual_tail(y, x, mod_ref, gn_ref, rw_ref, rb_ref, xo_ref, h2_ref, lg_ref):
    x_new = x + mod_ref[0, 2:3, :] * _rms(y, gn_ref[1:2, :])
    h2 = _rms(x_new, gn_ref[2:3, :]) * (1.0 + mod_ref[0, 4:5, :]) + mod_ref[0, 3:4, :]
    xo_ref[...] = x_new
    _store_tiled(h2_ref, h2)
    hi, mid, lo = _split3(h2)
    lane = lax.broadcasted_iota(jnp.int32, (h2.shape[0], LANES), 1)
    w = rw_ref[...]
    r = (jnp.dot(hi, w, preferred_element_type=F32)
         + jnp.where(lane < 2 * N_EXPERTS, jnp.dot(mid, w, preferred_element_type=F32), 0.0)
         + jnp.where(lane < N_EXPERTS, jnp.dot(lo, w, preferred_element_type=F32), 0.0))
    r = r + pltpu.roll(r, LANES - N_EXPERTS, 1) + pltpu.roll(r, LANES - 2 * N_EXPERTS, 1)
    lg_ref[...] = r[:, :N_EXPERTS] + rb_ref[...]


def _attn_out_kernel(a_ref, c_ref, x_ref, mod_ref, gn_ref, w_ref, rw_ref, rb_ref, xo_ref, h2_ref, lg_ref,
                     *, cb):
    y = jnp.dot(a_ref[...], w_ref[...], preferred_element_type=F32)
    _residual_tail(y, _stream_rows(c_ref, x_ref, cb), mod_ref, gn_ref, rw_ref, rb_ref, xo_ref, h2_ref, lg_ref)


def _hgrn_out_kernel(of_ref, ob_ref, gs_ref, hg_ref, x_ref, mod_ref, gn_ref, w_ref, rw_ref, rb_ref,
                     xo_ref, h2_ref, lg_ref):
    parts = []
    for h in range(HGRN_HEADS):
        sl = slice(h * HGRN_DK, (h + 1) * HGRN_DK)
        o = of_ref[:, sl] + ob_ref[:, sl]
        parts.append(_rms(o, hg_ref[:, sl]))
    a = jnp.concatenate(parts, axis=1) * gs_ref[...].astype(F32)
    y = jnp.dot(a.astype(BF16), w_ref[...], preferred_element_type=F32)
    _residual_tail(y, x_ref[...], mod_ref, gn_ref, rw_ref, rb_ref, xo_ref, h2_ref, lg_ref)


def _out_call(kind, acts, x, mod, gn, w, rw, rb, hg, cb, off):
    t_out = (x[0].shape[0] + x[1].shape[0]) if kind == "attn" else x.shape[0] - off * ROW_BLOCK
    d = D_MODEL
    tm = ROW_BLOCK
    w_hi, w_mid, w_lo = _split3(rw)
    rw = jnp.concatenate([w_hi, w_mid, w_lo, jnp.zeros_like(w_hi)], axis=1)
    row = lambda i: (i + off, 0)
    sel = lambda i: (jnp.minimum((i + off) // cb, 1), 0, 0)
    const = lambda i: (0, 0)
    common_specs = [
        pl.BlockSpec((1, 6, d), sel),
        pl.BlockSpec((4, d), const),
        pl.BlockSpec((d, d), const),
        pl.BlockSpec((d, LANES), const),
        pl.BlockSpec((1, N_EXPERTS), const),
    ]
    if kind == "attn":
        body = functools.partial(_attn_out_kernel, cb=cb)
        in_specs = [pl.BlockSpec((tm, d), row)] + _stream_specs(tm, d, cb) + common_specs
        args = (acts[0], x[0], x[1], mod, gn, w, rw, rb)
    else:
        body = _hgrn_out_kernel
        in_specs = ([pl.BlockSpec((tm, d), row)] * 3 + [pl.BlockSpec((1, d), const)]
                    + [pl.BlockSpec((tm, d), row)] + common_specs)
        args = (acts[0], acts[1], acts[2], hg, x, mod, gn, w, rw, rb)
    return pl.pallas_call(
        body,
        out_shape=(jax.ShapeDtypeStruct((t_out, d), F32),
                   jax.ShapeDtypeStruct((t_out * SEGS, LANES), F32),
                   jax.ShapeDtypeStruct((t_out, N_EXPERTS), F32)),
        grid=(t_out // tm,),
        in_specs=in_specs,
        out_specs=(pl.BlockSpec((tm, d), lambda i: (i, 0)),
                   pl.BlockSpec((tm * SEGS, LANES), lambda i: (i, 0)),
                   pl.BlockSpec((tm, N_EXPERTS), lambda i: (i, 0))),
        compiler_params=_cparams(("parallel",)),
        name=kind + "_out_residual",
    )(*args)


def _router_kernel(lg_ref, slab_ref, cnt_ref, run_ref):
    i = pl.program_id(0)

    @pl.when(i == 0)
    def _():
        run_ref[...] = jnp.zeros_like(run_ref)

    lg = lg_ref[...]
    tb = lg.shape[0]
    lane = lax.broadcasted_iota(jnp.int32, lg.shape, 1).astype(F32)
    sels, tops, idxs = [], [], []
    for _ in range(TOP_K):
        m = jnp.max(lg, axis=-1, keepdims=True)
        idx = jnp.min(jnp.where(lg == m, lane, float(N_EXPERTS)), axis=-1, keepdims=True)
        sel = lane == idx
        sels.append(sel)
        tops.append(m)
        idxs.append(idx)
        lg = jnp.where(sel, -jnp.inf, lg)
    ws = [jnp.exp(tk - tops[0]) for tk in tops]
    wsum = ws[0] + ws[1] + ws[2] + ws[3]
    chosen = (sels[0] | sels[1] | sels[2] | sels[3])
    onehot = jnp.where(chosen, 1.0, 0.0)
    rr = lax.broadcasted_iota(jnp.int32, (tb, tb), 0)
    cc = lax.broadcasted_iota(jnp.int32, (tb, tb), 1)
    tri = jnp.where(cc < rr, 1.0, 0.0).astype(BF16)
    before = jnp.dot(tri, onehot.astype(BF16), preferred_element_type=F32) + run_ref[...]
    out_lane = lax.broadcasted_iota(jnp.int32, (tb, LANES), 1)
    slab = jnp.zeros((tb, LANES), F32)
    for k in range(TOP_K):
        rank = jnp.sum(jnp.where(sels[k], before, 0.0), axis=-1, keepdims=True)
        slab = jnp.where(out_lane == k, idxs[k], slab)
        slab = jnp.where(out_lane == TOP_K + k, rank, slab)
        slab = jnp.where(out_lane == 2 * TOP_K + k, ws[k] / wsum, slab)
    slab_ref[...] = slab
    run_ref[...] = run_ref[...] + jnp.sum(onehot, axis=0, keepdims=True)
    cnt_ref[...] = run_ref[...]


def _router_call(logits):
    t = logits.shape[0]
    tb = ROW_BLOCK
    return pl.pallas_call(
        _router_kernel,
        out_shape=(jax.ShapeDtypeStruct((t, LANES), F32),
                   jax.ShapeDtypeStruct((1, N_EXPERTS), F32)),
        grid=(t // tb,),
        in_specs=[pl.BlockSpec((tb, N_EXPERTS), lambda i: (i, 0))],
        out_specs=(pl.BlockSpec((tb, LANES), lambda i: (i, 0)),
                   pl.BlockSpec((1, N_EXPERTS), lambda i: (0, 0))),
        scratch_shapes=[pltpu.VMEM((1, N_EXPERTS), F32)],
        compiler_params=_cparams(("arbitrary",)),
        name="router_topk",
    )(logits)


def _expert_kernel(be_ref, nu_ref, src0_ref, src1_ref, x_hbm, wgu_ref, bgu_ref, wdn_ref, bdn_ref, o_ref,
                   xbuf, sems, wgu_bf, wdn_bf):
    i = pl.program_id(0)
    rows = o_ref.shape[0] // SEGS
    slot = i % 2
    n_used = nu_ref[0]

    def gather(idx_ref, s):
        def issue(r2, carry):
            for p in range(2):
                r = 2 * r2 + p
                pltpu.make_async_copy(_row_tile(x_hbm, idx_ref[0, 0, r]), _row_tile(xbuf.at[s], r),
                                      sems.at[s]).start(priority=p)
            return carry

        lax.fori_loop(0, rows // 2, issue, 0, unroll=4)

    @pl.when(i == 0)
    def _():
        gather(src0_ref, 0)

    @pl.when(i + 1 < n_used)
    def _():
        gather(src1_ref, 1 - slot)

    prev = be_ref[jnp.maximum(i - 1, 0)]
    fresh = (i == 0) | (be_ref[i] != prev)

    @pl.when(fresh)
    def _():
        wgu_bf[...] = wgu_ref[...].astype(BF16)
        wdn_bf[...] = wdn_ref[...].astype(BF16)

    @pl.when(i < n_used)
    def _():
        pltpu.make_async_copy(x_hbm.at[pl.ds(0, rows * SEGS), :], xbuf.at[slot], sems.at[slot]).wait()
        xb = _load_tiled(xbuf.at[slot], rows).astype(BF16)
        gu = jnp.dot(xb, wgu_bf[...], preferred_element_type=F32) + bgu_ref[...]
        g = jnp.minimum(gu[:, :D_FF], SWIGLU_LIMIT)
        u = jnp.clip(gu[:, D_FF:], -SWIGLU_LIMIT, SWIGLU_LIMIT)
        hdn = g * _sigmoid(SWIGLU_ALPHA * g) * (u + 1.0)
        y = jnp.dot(hdn.astype(BF16), wdn_bf[...], preferred_element_type=F32) + bdn_ref[...]
        _store_tiled(o_ref, y)

    @pl.when(i >= n_used)
    def _():
        o_ref[...] = jnp.zeros_like(o_ref)


def _expert_call(blk_e, n_used, src_tok, h2, layer, w_gu, b_gu, w_dn, b_dn):
    d = D_MODEL
    tr = MOE_ROWS
    n_blk = blk_e.shape[0]
    depth, e, _, f2 = w_gu.shape
    last = n_blk - 1
    return pl.pallas_call(
        _expert_kernel,
        out_shape=jax.ShapeDtypeStruct((n_blk * tr * SEGS, LANES), F32),
        grid_spec=pltpu.PrefetchScalarGridSpec(
            num_scalar_prefetch=2,
            grid=(n_blk,),
            in_specs=[
                pl.BlockSpec((1, 1, tr), lambda i, be, nu: (0, 0, 0), memory_space=pltpu.SMEM),
                pl.BlockSpec((1, 1, tr), lambda i, be, nu: (jnp.minimum(i + 1, last), 0, 0),
                             memory_space=pltpu.SMEM),
                pl.BlockSpec(memory_space=pl.ANY),
                pl.BlockSpec((None, None, d, f2), lambda i, be, nu: (layer, be[i], 0, 0)),
                pl.BlockSpec((None, None, 1, f2), lambda i, be, nu: (layer, be[i], 0, 0)),
                pl.BlockSpec((None, None, D_FF, d), lambda i, be, nu: (layer, be[i], 0, 0)),
                pl.BlockSpec((None, None, 1, d), lambda i, be, nu: (layer, be[i], 0, 0)),
            ],
            out_specs=pl.BlockSpec((tr * SEGS, LANES), lambda i, be, nu: (i, 0)),
            scratch_shapes=[pltpu.VMEM((2, tr * SEGS, LANES), F32), pltpu.SemaphoreType.DMA((2,)),
                            pltpu.VMEM((d, f2), BF16), pltpu.VMEM((D_FF, d), BF16)],
        ),
        compiler_params=_cparams(("arbitrary",), row_dma=True),
        name="moe_experts",
    )(blk_e, n_used, src_tok.reshape(n_blk, 1, tr), src_tok.reshape(n_blk, 1, tr), h2,
      w_gu, b_gu.reshape(depth, e, 1, f2), w_dn, b_dn.reshape(depth, e, 1, d))


def _combine_kernel(d0_ref, d1_ref, y_hbm, slab_ref, x_ref, mod_ref, gn_ref, o_ref, buf, sems):
    i = pl.program_id(0)
    tb = o_ref.shape[0]
    slot = i % 2

    def gather(idx_ref, s):
        def issue(r, carry):
            for k in range(TOP_K):
                row = idx_ref[0, 0, k * tb + r]
                pltpu.make_async_copy(_row_tile(y_hbm, row), _row_tile(buf.at[s, k], r),
                                      sems.at[s]).start(priority=k % 2)
            return carry

        lax.fori_loop(0, tb, issue, 0, unroll=4)

    @pl.when(i == 0)
    def _():
        gather(d0_ref, 0)

    @pl.when(i + 1 < pl.num_programs(0))
    def _():
        gather(d1_ref, 1 - slot)

    for k in range(TOP_K):
        pltpu.make_async_copy(y_hbm.at[pl.ds(0, tb * SEGS), :], buf.at[slot, k], sems.at[slot]).wait()
    slab = slab_ref[...]
    y2 = jnp.zeros(o_ref.shape, F32)
    for k in range(TOP_K):
        y2 = y2 + _load_tiled(buf.at[slot, k], tb) * slab[:, 2 * TOP_K + k:2 * TOP_K + k + 1]
    o_ref[...] = x_ref[...] + mod_ref[0, 5:6, :] * _rms(y2, gn_ref[3:4, :])


def _combine_call(dest, y_sorted, slab, x, mod, gn, cb_rows):
    t, d = x.shape
    tb = COMBINE_ROWS
    nb = t // tb
    cbb = max(cb_rows // tb, 1)
    sel = (lambda i: (jnp.minimum(i // cbb, 1), 0, 0)) if cb_rows else (lambda i: (1, 0, 0))
    dest3 = dest.reshape(TOP_K, nb, tb).transpose(1, 0, 2).reshape(nb, 1, TOP_K * tb)
    return pl.pallas_call(
        _combine_kernel,
        out_shape=jax.ShapeDtypeStruct((t, d), F32),
        grid=(nb,),
        in_specs=[
            pl.BlockSpec((1, 1, tb * TOP_K), lambda i: (0, 0, 0), memory_space=pltpu.SMEM),
            pl.BlockSpec((1, 1, tb * TOP_K), lambda i: (jnp.minimum(i + 1, nb - 1), 0, 0),
                         memory_space=pltpu.SMEM),
            pl.BlockSpec(memory_space=pl.ANY),
            pl.BlockSpec((tb, LANES), lambda i: (i, 0)),
            pl.BlockSpec((tb, d), lambda i: (i, 0)),
            pl.BlockSpec((1, 6, d), sel),
            pl.BlockSpec((4, d), lambda i: (0, 0)),
        ],
        out_specs=pl.BlockSpec((tb, d), lambda i: (i, 0)),
        scratch_shapes=[pltpu.VMEM((2, TOP_K, tb * SEGS, LANES), F32), pltpu.SemaphoreType.DMA((2,))],
        compiler_params=_cparams(("arbitrary",), row_dma=True),
        name="moe_combine",
    )(dest3, dest3, y_sorted, slab, x, mod, gn)


INVERT_BLOCK = 2048


def _invert_kernel(dest_ref, zeros_hbm, src_ref):
    k, j = pl.program_id(0), pl.program_id(1)
    n = dest_ref.shape[2]

    @pl.when((k == 0) & (j == 0))
    def _():
        pltpu.sync_copy(zeros_hbm, src_ref)

    tok0 = j * n

    def put(t, carry):
        src_ref[dest_ref[0, 0, t]] = tok0 + t
        return carry

    lax.fori_loop(0, n, put, 0, unroll=16)


def _invert_call(dest, n_rows):
    t = dest.shape[1]
    blk = max(b for b in range(LANES, INVERT_BLOCK + 1, LANES) if t % b == 0)
    nb = t // blk
    return pl.pallas_call(
        _invert_kernel,
        out_shape=jax.ShapeDtypeStruct((n_rows,), jnp.int32),
        grid=(TOP_K, nb),
        in_specs=[pl.BlockSpec((1, 1, blk), lambda k, j: (k * nb + j, 0, 0), memory_space=pltpu.SMEM),
                  pl.BlockSpec(memory_space=pl.ANY)],
        out_specs=pl.BlockSpec(memory_space=pltpu.SMEM),
        compiler_params=_cparams(("arbitrary", "arbitrary")),
        name="moe_invert",
    )(dest.reshape(TOP_K * nb, 1, blk), jnp.zeros((n_rows,), jnp.int32))


def _moe(h2, logits, x, mod, gn, layer, w_gu, b_gu, w_dn, b_dn, cb_rows):
    t = logits.shape[0]
    tr = MOE_ROWS
    slab, counts = _router_call(logits)
    slab_t = slab[:, :2 * TOP_K].T
    e_idx = slab_t[:TOP_K].astype(jnp.int32)
    rank = slab_t[TOP_K:].astype(jnp.int32)
    counts = counts[0].astype(jnp.int32)
    padded = (counts + tr - 1) // tr * tr
    p_end = jnp.cumsum(padded)
    p_start = p_end - padded
    dest = rank
    for e in range(N_EXPERTS):
        dest = dest + jnp.where(e_idx == e, p_start[e], 0)
    a = t * TOP_K
    n_blk = (a + N_EXPERTS * (tr - 1) + tr - 1) // tr
    src_tok = _invert_call(dest, n_blk * tr)
    blk_start = jnp.arange(n_blk, dtype=jnp.int32) * tr
    blk_e = jnp.sum((p_end[None, :] <= blk_start[:, None]).astype(jnp.int32), axis=1)
    blk_e = jnp.minimum(blk_e, N_EXPERTS - 1)
    n_used = (p_end[-1] // tr).astype(jnp.int32).reshape(1)
    ys = _expert_call(blk_e, n_used, src_tok, h2, layer, w_gu, b_gu, w_dn, b_dn)
    return _combine_call(dest, ys, slab, x, mod, gn, cb_rows)


def _hgrn_in_kernel(x_ref, mod_ref, gn_ref, w_ref, lb_ref, q_ref, kf_ref, lf_ref, kb_ref, lbw_ref,
                    i_ref, gs_ref):
    x = x_ref[...]
    h = _rms(x, gn_ref[...]) * (1.0 + mod_ref[0, 1:2, :]) + mod_ref[0, 0:1, :]
    hb = h.astype(BF16)
    f = D_MODEL

    def proj(sec):
        return jnp.dot(hb, w_ref[:, sec * f:(sec + 1) * f], preferred_element_type=F32)

    z = proj(0)
    q_ref[...] = (z * _sigmoid(z)).astype(q_ref.dtype)
    for sec, k_ref, l_ref in ((1, kf_ref, lf_ref), (2, kb_ref, lbw_ref)):
        z = proj(sec)
        lb = lb_ref[sec - 1:sec, :]
        sg = _sigmoid(z)
        l_ref[...] = jnp.log(lb + (1.0 - lb) * sg)
        k_ref[...] = ((1.0 - lb) * _sigmoid(-z)).astype(k_ref.dtype)
    i_ref[...] = proj(3).astype(i_ref.dtype)
    z = proj(4)
    gs_ref[...] = (z * _sigmoid(z)).astype(gs_ref.dtype)


def _hgrn_in_call(xs, mod, gn, w, lb, cb):
    t, d = xs.shape
    tm = ROW_BLOCK
    sel = lambda i: (jnp.minimum(i // cb, 1), 0, 0)
    row = pl.BlockSpec((tm, d), lambda i: (i, 0))
    dts = (BF16, BF16, F32, BF16, F32, BF16, BF16)
    return pl.pallas_call(
        _hgrn_in_kernel,
        out_shape=tuple(jax.ShapeDtypeStruct((t, d), dt) for dt in dts),
        grid=(t // tm,),
        in_specs=[row, pl.BlockSpec((1, 6, d), sel), pl.BlockSpec((1, d), lambda i: (0, 0)),
                  pl.BlockSpec(w.shape, lambda i: (0, 0)), pl.BlockSpec((2, d), lambda i: (0, 0))],
        out_specs=tuple(row for _ in dts),
        compiler_params=_cparams(("parallel",)),
        name="hgrn_in_proj",
    )(xs, mod, gn, w, lb)


def _split3(x):
    hi = x.astype(BF16)
    r1 = x - hi.astype(F32)
    mid = r1.astype(BF16)
    lo = (r1 - mid.astype(F32)).astype(BF16)
    return hi, mid, lo


def _scan_kernel(q_ref, k_ref, g_ref, v_ref, o_ref, st_ref, gcum_ref, *, reverse):
    j = pl.program_id(0)
    ch, sub = HGRN_CHUNK, HGRN_SUB
    ns = ch // sub
    dk = HGRN_DK

    @pl.when(j == 0)
    def _():
        st_ref[...] = jnp.zeros_like(st_ref)

    rr = lax.broadcasted_iota(jnp.int32, (ch, ch), 0)
    cc = lax.broadcasted_iota(jnp.int32, (ch, ch), 1)
    causal = (cc >= rr) if reverse else (cc <= rr)
    tri = jnp.where(causal, 1.0, 0.0).astype(BF16)
    hi, mid, lo = _split3(g_ref[...])
    gcum_ref[...] = LOG2E * (jnp.dot(tri, hi, preferred_element_type=F32)
                             + jnp.dot(tri, mid, preferred_element_type=F32)
                             + jnp.dot(tri, lo, preferred_element_type=F32))
    diag_mask = ((rr // sub) == (cc // sub)) & causal
    key_blocks = list(range(1, ns)) if reverse else list(range(ns - 1))
    nkb = len(key_blocks)
    kbd_mask = (lax.broadcasted_iota(jnp.int32, (ch, nkb * dk), 1) // dk + key_blocks[0]
                == lax.broadcasted_iota(jnp.int32, (ch, nkb * dk), 0) // sub)
    kdg_mask = (lax.broadcasted_iota(jnp.int32, (ch, sub * dk), 1) // dk
                == lax.broadcasted_iota(jnp.int32, (ch, sub * dk), 0) % sub)
    end_row = 0 if reverse else ch - 1
    edge = 0 if reverse else sub - 1
    nt = (((1,), (1,)), ((), ()))
    tn = (((0,), (0,)), ((), ()))

    for h in range(HGRN_HEADS):
        hs = slice(h * dk, (h + 1) * dk)
        q = q_ref[:, hs].astype(F32)
        kb = k_ref[:, hs]
        k = kb.astype(F32)
        v = v_ref[:, hs]
        gc = gcum_ref[:, hs]
        st = st_ref[h]
        g_end = gc[end_row:end_row + 1, :]
        qg = (q * jnp.exp2(gc)).astype(BF16)
        o = lax.dot_general(qg, st.astype(BF16), nt, preferred_element_type=F32)
        k_end = (k * jnp.exp2(g_end - gc)).astype(BF16)
        st_ref[h] = st * jnp.exp2(g_end) + lax.dot_general(v, k_end, tn, preferred_element_type=F32)
        gc4 = gc.reshape(ns, sub, dk)
        g_edge = jnp.broadcast_to(gc4[:, edge:edge + 1, :], (ns, sub, dk)).reshape(ch, dk)
        k_rel = (k * jnp.exp2(g_edge - gc)).astype(BF16)
        k_bd = jnp.where(kbd_mask, jnp.concatenate([k_rel] * nkb, axis=1), jnp.zeros((), BF16))
        q_parts = []
        for jb in key_blocks:
            row = jb * sub + edge
            rows = slice(0, jb * sub) if reverse else slice((jb + 1) * sub, ch)
            part = q[rows] * jnp.exp2(gc[rows] - gc[row:row + 1, :])
            pad = jnp.zeros((ch - part.shape[0], dk), F32)
            full = jnp.concatenate([part, pad] if reverse else [pad, part], axis=0)
            q_parts.append(full.astype(BF16))
        q_cat = jnp.concatenate(q_parts, axis=1)
        s_off = lax.dot_general(q_cat, k_bd, nt, preferred_element_type=F32)
        p_parts = []
        for s in range(sub):
            g_s = jnp.broadcast_to(gc4[:, s:s + 1, :], (ns, sub, dk)).reshape(ch, dk)
            p_parts.append((q * jnp.exp2(jnp.minimum(gc - g_s, 0.0))).astype(BF16))
        p_cat = jnp.concatenate(p_parts, axis=1)
        k_dg = jnp.where(kdg_mask, jnp.concatenate([kb] * sub, axis=1), jnp.zeros((), BF16))
        s_diag = lax.dot_general(p_cat, k_dg, nt, preferred_element_type=F32)
        a = s_off + jnp.where(diag_mask, s_diag, 0.0)
        o_ref[:, hs] = o + jnp.dot(a.astype(BF16), v, preferred_element_type=F32)


def _scan_call(q, k, lg, v, c_len, reverse):
    t, d = q.shape
    ch = HGRN_CHUNK
    n_ch = t // ch
    cc = c_len // ch
    if reverse:
        idx = lambda j: (jnp.where(j < cc, cc - 1 - j, n_ch - 1 - (j - cc)), 0)
    else:
        idx = lambda j: (j, 0)
    blk = pl.BlockSpec((ch, d), idx)
    return pl.pallas_call(
        functools.partial(_scan_kernel, reverse=reverse),
        out_shape=jax.ShapeDtypeStruct((t, d), F32),
        grid=(n_ch,),
        in_specs=[blk, blk, blk, blk],
        out_specs=blk,
        scratch_shapes=[pltpu.VMEM((HGRN_HEADS, HGRN_DK, HGRN_DK), F32), pltpu.VMEM((ch, d), F32)],
        compiler_params=_cparams(("arbitrary",)),
        name="hgrn_scan_bwd" if reverse else "hgrn_scan_fwd",
    )(q, k, lg, v)


def _rope_tables(l, c_len):
    n = HEAD_DIM // 4
    inv_freq = ROPE_BASE ** (-jnp.arange(n, dtype=F32) / n)
    rows = l // GRID_W
    ang_row = jnp.arange(rows, dtype=F32)[:, None] * inv_freq[None, :]
    ang_col = jnp.arange(GRID_W, dtype=F32)[:, None] * inv_freq[None, :]
    cr, sr, cc, sc = jnp.cos(ang_row), jnp.sin(ang_row), jnp.cos(ang_col), jnp.sin(ang_col)
    zr, zc = jnp.zeros_like(cr), jnp.zeros_like(cc)
    cos_r = jnp.concatenate([cr, cr, zr, zr] * 2, axis=1)
    sin_r = jnp.concatenate([-sr, sr, zr, zr] * 2, axis=1)
    cos_c = jnp.concatenate([zc, zc, cc, cc] * 2, axis=1)
    sin_c = jnp.concatenate([zc, zc, -sc, sc] * 2, axis=1)
    cos = (cos_r[:, None, :] + cos_c[None, :, :]).reshape(l, LANES)
    sin = (sin_r[:, None, :] + sin_c[None, :, :]).reshape(l, LANES)
    cos = jnp.concatenate([jnp.ones((c_len, LANES), F32), cos], axis=0)
    sin = jnp.concatenate([jnp.zeros((c_len, LANES), F32), sin], axis=0)
    return cos, sin


def kernel(x, c, ctx, c_ctx, ada_w, ada_b, norm_g, attn_w_qkv, attn_b_qkv, attn_sink, attn_w_o,
           hgrn_w_in, hgrn_lb, hgrn_norm_g, hgrn_w_o, router_w, router_b, moe_w_gu, moe_b_gu,
           moe_w_dn, moe_b_dn):
    b, l, d = x.shape
    c_len = ctx.shape[1]
    depth = ada_w.shape[0]
    assert b == 1 and d == D_MODEL and depth == 2
    assert c_len % ROW_BLOCK == 0 and l % ROW_BLOCK == 0
    cb = c_len // ROW_BLOCK

    cs = jnp.zeros((8, d), F32).at[0].set(c_ctx).at[1].set(c[0])
    mods = _ada_call(cs, ada_w, ada_b)
    mods = mods[:, :2].reshape(depth, 2, 6, d)

    cos, sin = _rope_tables(l, c_len)

    q, k4, v4 = _qkv_call(ctx[0], x[0], mods[0], norm_g[0, 0:1], attn_w_qkv[0].astype(BF16),
                          attn_b_qkv[0].reshape(1, -1), cos, sin, cb)
    o = _attn_call(attn_sink[0], q, k4, v4, c_len)
    xs, h2, logits = _out_call("attn", (o,), (ctx[0], x[0]), mods[0], norm_g[0], attn_w_o[0].astype(BF16),
                               router_w[0], router_b[0].reshape(1, -1), None, cb, 0)
    xs = _moe(h2, logits, xs, mods[0], norm_g[0], 0, moe_w_gu, moe_b_gu, moe_w_dn, moe_b_dn, c_len)

    lb_soft = jax.nn.softmax(hgrn_lb.astype(F32), axis=0)
    lb = jnp.cumsum(lb_soft, axis=0)[1] - lb_soft[0]
    qh, kf, lf, kb, lbw, iv, gs = _hgrn_in_call(xs, mods[1], norm_g[1, 0:1], hgrn_w_in[0].astype(BF16),
                                                lb, cb)
    o_f = _scan_call(qh, kf, lf, iv, c_len, reverse=False)
    o_b = _scan_call(qh, kb, lbw, iv, c_len, reverse=True)
    x_lat, h2, logits = _out_call("hgrn", (o_f, o_b, gs), xs, mods[1], norm_g[1],
                                  hgrn_w_o[0].astype(BF16), router_w[1], router_b[1].reshape(1, -1),
                                  hgrn_norm_g[0].reshape(1, -1), cb, cb)
    out = _moe(h2, logits, x_lat, mods[1], norm_g[1], 1, moe_w_gu, moe_b_gu, moe_w_dn, moe_b_dn, 0)
    return out[None]
```

```python
import functools

import jax
import jax.numpy as jnp
from jax import lax
from jax.experimental import pallas as pl
from jax.experimental.pallas import tpu as pltpu

D_MODEL = 1024
GRID_W = 64
RMS_EPS = 1e-6

ATTN_HEADS = 16
ATTN_KV_HEADS = 2
HEAD_DIM = 64
Q_DIM = ATTN_HEADS * HEAD_DIM
KV_DIM = ATTN_KV_HEADS * HEAD_DIM
QKV_DIM = Q_DIM + 2 * KV_DIM
WINDOW = 128
ATTN_BLOCK = 128
ROPE_BASE = 10000.0

HGRN_HEADS = 8
HGRN_DK = 128
HGRN_CHUNK = 64
HGRN_SUB = 8
HGRN_GROUP = 2

N_EXPERTS = 32
TOP_K = 4
D_FF = 1024
SWIGLU_LIMIT = 7.0
SWIGLU_ALPHA = 1.702

LANES = 128
ROW_BLOCK = 256
MOE_ROWS = 256
COMBINE_ROWS = 256
VMEM_LIMIT = 56 * 1024 * 1024

F32 = jnp.float32
BF16 = jnp.bfloat16
NEG_BIG = -1e30
LOG2E = 1.4426950408889634


def _cparams(sem, row_dma=False):
    return pltpu.CompilerParams(dimension_semantics=sem, vmem_limit_bytes=VMEM_LIMIT,
                                disable_bounds_checks=row_dma)


def _rms(x, g):
    return x * lax.rsqrt(jnp.mean(x * x, axis=-1, keepdims=True) + RMS_EPS) * g


def _sigmoid(x):
    return 1.0 / (1.0 + jnp.exp(-x))


def _silu(x):
    return x * (0.5 * jnp.tanh(0.5 * x) + 0.5)


SEGS = D_MODEL // LANES


def _store_tiled(ref, val):
    n = val.shape[0]
    for s in range(SEGS):
        ref[pl.ds(s, n, stride=SEGS), :] = val[:, s * LANES:(s + 1) * LANES]


def _load_tiled(ref, n):
    return jnp.concatenate([ref[pl.ds(s, n, stride=SEGS), :] for s in range(SEGS)], axis=1)


def _row_tile(ref, r):
    return ref.at[pl.ds(pl.multiple_of(r * SEGS, SEGS), SEGS), :]


def _ada_kernel(c_ref, w_ref, b_ref, o_ref):
    c = c_ref[...]
    s = c * _sigmoid(c)
    o_ref[0] = jnp.dot(s, w_ref[0], precision=lax.Precision.HIGHEST,
                       preferred_element_type=F32) + b_ref[0]


def _ada_call(cs, ada_w, ada_b):
    depth, d, n = ada_w.shape
    tn = 1024
    return pl.pallas_call(
        _ada_kernel,
        out_shape=jax.ShapeDtypeStruct((depth, 8, n), F32),
        grid=(depth, n // tn),
        in_specs=[
            pl.BlockSpec((8, d), lambda i, j: (0, 0)),
            pl.BlockSpec((1, d, tn), lambda i, j: (i, 0, j)),
            pl.BlockSpec((1, 1, tn), lambda i, j: (i, 0, j)),
        ],
        out_specs=pl.BlockSpec((1, 8, tn), lambda i, j: (i, 0, j)),
        compiler_params=_cparams(("arbitrary", "arbitrary")),
        name="adaln",
    )(cs, ada_w, ada_b.reshape(depth, 1, n))


def _swap16(t):
    lane = lax.broadcasted_iota(jnp.int32, t.shape, 1)
    return jnp.where(lane % 32 < 16, pltpu.roll(t, LANES - 16, 1), pltpu.roll(t, 16, 1))


def _stream_rows(c_ref, x_ref, cb):
    return jnp.where(pl.program_id(0) < cb, c_ref[...], x_ref[...])


def _stream_specs(tm, d, cb):
    return [pl.BlockSpec((tm, d), lambda i: (jnp.minimum(i, cb - 1), 0)),
            pl.BlockSpec((tm, d), lambda i: (jnp.maximum(i - cb, 0), 0))]


def _qkv_kernel(c_ref, x_ref, mod_ref, gn_ref, w_ref, b_ref, cos_ref, sin_ref, q_ref, k4_ref, v4_ref, *, cb):
    x = _stream_rows(c_ref, x_ref, cb)
    h = _rms(x, gn_ref[...]) * (1.0 + mod_ref[0, 1:2, :]) + mod_ref[0, 0:1, :]
    hb = h.astype(BF16)
    cos = cos_ref[...]
    sin = sin_ref[...]
    nq = Q_DIM // LANES
    for j in range(nq + 1):
        sl = slice(j * LANES, (j + 1) * LANES)
        t = jnp.dot(hb, w_ref[:, sl], preferred_element_type=F32) + b_ref[:, sl]
        t = t * cos + _swap16(t) * sin
        if j < nq:
            q_ref[:, sl] = (t * (HEAD_DIM ** -0.5 * LOG2E)).astype(q_ref.dtype)
        else:
            kt = t
    sl = slice(Q_DIM + KV_DIM, QKV_DIM)
    vt = jnp.dot(hb, w_ref[:, sl], preferred_element_type=F32) + b_ref[:, sl]
    lo = lax.broadcasted_iota(jnp.int32, kt.shape, 1) < HEAD_DIM
    for t, ref in ((kt, k4_ref), (vt, v4_ref)):
        sw = pltpu.roll(t, HEAD_DIM, 1)
        ref[:, 0 * LANES:1 * LANES] = jnp.where(lo, t, 0.0).astype(ref.dtype)
        ref[:, 1 * LANES:2 * LANES] = jnp.where(lo, 0.0, sw).astype(ref.dtype)
        ref[:, 2 * LANES:3 * LANES] = jnp.where(lo, sw, 0.0).astype(ref.dtype)
        ref[:, 3 * LANES:4 * LANES] = jnp.where(lo, 0.0, t).astype(ref.dtype)


def _qkv_call(ctx, x, mod, gn, w, b, cos, sin, cb):
    d = x.shape[1]
    t = ctx.shape[0] + x.shape[0]
    tm = ROW_BLOCK
    sel = lambda i: (jnp.minimum(i // cb, 1), 0, 0)
    return pl.pallas_call(
        functools.partial(_qkv_kernel, cb=cb),
        out_shape=(jax.ShapeDtypeStruct((t, Q_DIM), BF16),
                   jax.ShapeDtypeStruct((t, 4 * LANES), BF16),
                   jax.ShapeDtypeStruct((t, 4 * LANES), BF16)),
        grid=(t // tm,),
        in_specs=_stream_specs(tm, d, cb) + [
            pl.BlockSpec((1, 6, d), sel),
            pl.BlockSpec((1, d), lambda i: (0, 0)),
            pl.BlockSpec((d, QKV_DIM), lambda i: (0, 0)),
            pl.BlockSpec((1, QKV_DIM), lambda i: (0, 0)),
            pl.BlockSpec((tm, LANES), lambda i: (i, 0)),
            pl.BlockSpec((tm, LANES), lambda i: (i, 0)),
        ],
        out_specs=(pl.BlockSpec((tm, Q_DIM), lambda i: (i, 0)),
                   pl.BlockSpec((tm, 4 * LANES), lambda i: (i, 0)),
                   pl.BlockSpec((tm, 4 * LANES), lambda i: (i, 0))),
        compiler_params=_cparams(("parallel",)),
        name="qkv_rope",
    )(ctx, x, mod, gn, w, b, cos, sin)


def _attn_kernel(sink_ref, q_ref, kp_ref, kc_ref, kn_ref, kx_ref, vp_ref, vc_ref, vn_ref, vx_ref,
                 o_ref, *, cb, n_lat):
    n = pl.program_id(0)
    blk = ATTN_BLOCK
    c = kx_ref.shape[0]
    nw = 3 * blk
    s = lax.broadcasted_iota(jnp.int32, (nw + c, blk), 0)
    r = lax.broadcasted_iota(jnp.int32, (nw + c, blk), 1)
    q_pos = (n - cb) * blk + r
    k_pos = (n - cb - 1) * blk + s
    win_ok = (jnp.abs(q_pos - k_pos) <= WINDOW) & (k_pos >= 0) & (k_pos < n_lat) & (n >= cb)
    valid = (s >= nw) | win_ok
    k_all = jnp.concatenate([kp_ref[...], kc_ref[...], kn_ref[...], kx_ref[...]], axis=0)
    v_all = jnp.concatenate([vp_ref[...], vc_ref[...], vn_ref[...], vx_ref[...]], axis=0)
    n_var = 2 * ATTN_KV_HEADS
    v_t = [v_all[:, j * LANES:(j + 1) * LANES].astype(F32).T.astype(BF16) for j in range(n_var)]
    nt = (((1,), (1,)), ((), ()))
    group = ATTN_HEADS // ATTN_KV_HEADS
    for p in range(ATTN_HEADS // 2):
        g = (2 * p) // group
        qp = q_ref[:, p * LANES:(p + 1) * LANES]
        o_t = jnp.zeros((LANES, blk), F32)
        for half in range(2):
            j = 2 * g + half
            sc = lax.dot_general(k_all[:, j * LANES:(j + 1) * LANES], qp, nt,
                                 preferred_element_type=F32)
            sc = jnp.where(valid, sc, NEG_BIG)
            sink = sink_ref[2 * p + half] * LOG2E
            m = jnp.maximum(jnp.max(sc, axis=0, keepdims=True), sink)
            e = jnp.exp2(sc - m)
            denom = jnp.sum(e, axis=0, keepdims=True) + jnp.exp2(sink - m)
            pv = jnp.dot(v_t[j], e.astype(BF16), preferred_element_type=F32)
            o_t = o_t + pv / denom
        o_ref[:, p * LANES:(p + 1) * LANES] = o_t.T.astype(o_ref.dtype)


def _attn_call(sink, q, k4, v4, c_len):
    t = q.shape[0]
    blk = ATTN_BLOCK
    cb = c_len // blk
    nb = t // blk
    n_lat = t - c_len
    last = nb - 1
    kw = 4 * LANES
    spec_q = pl.BlockSpec((blk, Q_DIM), lambda n: (n, 0))
    prev = pl.BlockSpec((blk, kw), lambda n: (jnp.maximum(n - 1, 0), 0))
    cur = pl.BlockSpec((blk, kw), lambda n: (n, 0))
    nxt = pl.BlockSpec((blk, kw), lambda n: (jnp.minimum(n + 1, last), 0))
    ctx = pl.BlockSpec((c_len, kw), lambda n: (0, 0))
    return pl.pallas_call(
        functools.partial(_attn_kernel, cb=cb, n_lat=n_lat),
        out_shape=jax.ShapeDtypeStruct((t, Q_DIM), BF16),
        grid=(nb,),
        in_specs=[pl.BlockSpec(memory_space=pltpu.SMEM), spec_q,
                  prev, cur, nxt, ctx, prev, cur, nxt, ctx],
        out_specs=pl.BlockSpec((blk, Q_DIM), lambda n: (n, 0)),
        compiler_params=_cparams(("parallel",)),
        name="window_attn",
    )(sink, q, k4, k4, k4, k4, v4, v4, v4, v4)


def _residual_tail(y, x, mod_ref, gn_ref, rw_ref, rb_ref, xo_ref, h2_ref, lg_ref):
    x_new = x + mod_ref[0, 2:3, :] * _rms(y, gn_ref[1:2, :])
    h2 = _rms(x_new, gn_ref[2:3, :]) * (1.0 + mod_ref[0, 4:5, :]) + mod_ref[0, 3:4, :]
    xo_ref[...] = x_new
    _store_tiled(h2_ref, h2)
    hi, mid, lo = _split3(h2)
    lane = lax.broadcasted_iota(jnp.int32, (h2.shape[0], LANES), 1)
    w = rw_ref[...]
    r = (jnp.dot(hi, w, preferred_element_type=F32)
         + jnp.where(lane < 2 * N_EXPERTS, jnp.dot(mid, w, preferred_element_type=F32), 0.0)
         + jnp.where(lane < N_EXPERTS, jnp.dot(lo, w, preferred_element_type=F32), 0.0))
    r = r + pltpu.roll(r, LANES - N_EXPERTS, 1) + pltpu.roll(r, LANES - 2 * N_EXPERTS, 1)
    lg_ref[...] = r[:, :N_EXPERTS] + rb_ref[...]


def _attn_out_kernel(a_ref, c_ref, x_ref, mod_ref, gn_ref, w_ref, rw_ref, rb_ref, xo_ref, h2_ref, lg_ref,
                     *, cb):
    y = jnp.dot(a_ref[...], w_ref[...], preferred_element_type=F32)
    _residual_tail(y, _stream_rows(c_ref, x_ref, cb), mod_ref, gn_ref, rw_ref, rb_ref, xo_ref, h2_ref, lg_ref)


def _hgrn_out_kernel(of_ref, ob_ref, gs_ref, hg_ref, x_ref, mod_ref, gn_ref, w_ref, rw_ref, rb_ref,
                     xo_ref, h2_ref, lg_ref):
    parts = []
    for h in range(HGRN_HEADS):
        sl = slice(h * HGRN_DK, (h + 1) * HGRN_DK)
        o = of_ref[:, sl] + ob_ref[:, sl]
        parts.append(_rms(o, hg_ref[:, sl]))
    a = jnp.concatenate(parts, axis=1) * gs_ref[...].astype(F32)
    y = jnp.dot(a.astype(BF16), w_ref[...], preferred_element_type=F32)
    _residual_tail(y, x_ref[...], mod_ref, gn_ref, rw_ref, rb_ref, xo_ref, h2_ref, lg_ref)


def _out_call(kind, acts, x, mod, gn, w, rw, rb, hg, cb, off):
    t_out = (x[0].shape[0] + x[1].shape[0]) if kind == "attn" else x.shape[0] - off * ROW_BLOCK
    d = D_MODEL
    tm = ROW_BLOCK
    w_hi, w_mid, w_lo = _split3(rw)
    rw = jnp.concatenate([w_hi, w_mid, w_lo, jnp.zeros_like(w_hi)], axis=1)
    row = lambda i: (i + off, 0)
    sel = lambda i: (jnp.minimum((i + off) // cb, 1), 0, 0)
    const = lambda i: (0, 0)
    common_specs = [
        pl.BlockSpec((1, 6, d), sel),
        pl.BlockSpec((4, d), const),
        pl.BlockSpec((d, d), const),
        pl.BlockSpec((d, LANES), const),
        pl.BlockSpec((1, N_EXPERTS), const),
    ]
    if kind == "attn":
        body = functools.partial(_attn_out_kernel, cb=cb)
        in_specs = [pl.BlockSpec((tm, d), row)] + _stream_specs(tm, d, cb) + common_specs
        args = (acts[0], x[0], x[1], mod, gn, w, rw, rb)
    else:
        body = _hgrn_out_kernel
        in_specs = ([pl.BlockSpec((tm, d), row)] * 3 + [pl.BlockSpec((1, d), const)]
                    + [pl.BlockSpec((tm, d), row)] + common_specs)
        args = (acts[0], acts[1], acts[2], hg, x, mod, gn, w, rw, rb)
    return pl.pallas_call(
        body,
        out_shape=(jax.ShapeDtypeStruct((t_out, d), F32),
                   jax.ShapeDtypeStruct((t_out * SEGS, LANES), F32),
                   jax.ShapeDtypeStruct((t_out, N_EXPERTS), F32)),
        grid=(t_out // tm,),
        in_specs=in_specs,
        out_specs=(pl.BlockSpec((tm, d), lambda i: (i, 0)),
                   pl.BlockSpec((tm * SEGS, LANES), lambda i: (i, 0)),
                   pl.BlockSpec((tm, N_EXPERTS), lambda i: (i, 0))),
        compiler_params=_cparams(("parallel",)),
        name=kind + "_out_residual",
    )(*args)


def _router_kernel(lg_ref, slab_ref, cnt_ref, run_ref):
    i = pl.program_id(0)

    @pl.when(i == 0)
    def _():
        run_ref[...] = jnp.zeros_like(run_ref)

    lg = lg_ref[...]
    tb = lg.shape[0]
    lane = lax.broadcasted_iota(jnp.int32, lg.shape, 1).astype(F32)
    sels, tops, idxs = [], [], []
    for _ in range(TOP_K):
        m = jnp.max(lg, axis=-1, keepdims=True)
        idx = jnp.min(jnp.where(lg == m, lane, float(N_EXPERTS)), axis=-1, keepdims=True)
        sel = lane == idx
        sels.append(sel)
        tops.append(m)
        idxs.append(idx)
        lg = jnp.where(sel, -jnp.inf, lg)
    ws = [jnp.exp(tk - tops[0]) for tk in tops]
    wsum = ws[0] + ws[1] + ws[2] + ws[3]
    chosen = (sels[0] | sels[1] | sels[2] | sels[3])
    onehot = jnp.where(chosen, 1.0, 0.0)
    rr = lax.broadcasted_iota(jnp.int32, (tb, tb), 0)
    cc = lax.broadcasted_iota(jnp.int32, (tb, tb), 1)
    tri = jnp.where(cc < rr, 1.0, 0.0).astype(BF16)
    before = jnp.dot(tri, onehot.astype(BF16), preferred_element_type=F32) + run_ref[...]
    out_lane = lax.broadcasted_iota(jnp.int32, (tb, LANES), 1)
    slab = jnp.zeros((tb, LANES), F32)
    for k in range(TOP_K):
        rank = jnp.sum(jnp.where(sels[k], before, 0.0), axis=-1, keepdims=True)
        slab = jnp.where(out_lane == k, idxs[k], slab)
        slab = jnp.where(out_lane == TOP_K + k, rank, slab)
        slab = jnp.where(out_lane == 2 * TOP_K + k, ws[k] / wsum, slab)
    slab_ref[...] = slab
    run_ref[...] = run_ref[...] + jnp.sum(onehot, axis=0, keepdims=True)
    cnt_ref[...] = run_ref[...]


def _router_call(logits):
    t = logits.shape[0]
    tb = ROW_BLOCK
    return pl.pallas_call(
        _router_kernel,
        out_shape=(jax.ShapeDtypeStruct((t, LANES), F32),
                   jax.ShapeDtypeStruct((1, N_EXPERTS), F32)),
        grid=(t // tb,),
        in_specs=[pl.BlockSpec((tb, N_EXPERTS), lambda i: (i, 0))],
        out_specs=(pl.BlockSpec((tb, LANES), lambda i: (i, 0)),
                   pl.BlockSpec((1, N_EXPERTS), lambda i: (0, 0))),
        scratch_shapes=[pltpu.VMEM((1, N_EXPERTS), F32)],
        compiler_params=_cparams(("arbitrary",)),
        name="router_topk",
    )(logits)


def _expert_kernel(be_ref, nu_ref, src0_ref, src1_ref, x_hbm, wgu_ref, bgu_ref, wdn_ref, bdn_ref, o_ref,
                   xbuf, sems, wgu_bf, wdn_bf):
    i = pl.program_id(0)
    rows = o_ref.shape[0] // SEGS
    slot = i % 2
    n_used = nu_ref[0]

    def gather(idx_ref, s):
        def issue(r2, carry):
            for p in range(2):
                r = 2 * r2 + p
                pltpu.make_async_copy(_row_tile(x_hbm, idx_ref[0, 0, r]), _row_tile(xbuf.at[s], r),
                                      sems.at[s]).start(priority=p)
            return carry

        lax.fori_loop(0, rows // 2, issue, 0, unroll=4)

    @pl.when(i == 0)
    def _():
        gather(src0_ref, 0)

    @pl.when(i + 1 < n_used)
    def _():
        gather(src1_ref, 1 - slot)

    prev = be_ref[jnp.maximum(i - 1, 0)]
    fresh = (i == 0) | (be_ref[i] != prev)

    @pl.when(fresh)
    def _():
        wgu_bf[...] = wgu_ref[...].astype(BF16)
        wdn_bf[...] = wdn_ref[...].astype(BF16)

    @pl.when(i < n_used)
    def _():
        pltpu.make_async_copy(x_hbm.at[pl.ds(0, rows * SEGS), :], xbuf.at[slot], sems.at[slot]).wait()
        xb = _load_tiled(xbuf.at[slot], rows).astype(BF16)
        gu = jnp.dot(xb, wgu_bf[...], preferred_element_type=F32) + bgu_ref[...]
        g = jnp.minimum(gu[:, :D_FF], SWIGLU_LIMIT)
        u = jnp.clip(gu[:, D_FF:], -SWIGLU_LIMIT, SWIGLU_LIMIT)
        hdn = g * _sigmoid(SWIGLU_ALPHA * g) * (u + 1.0)
        y = jnp.dot(hdn.astype(BF16), wdn_bf[...], preferred_element_type=F32) + bdn_ref[...]
        _store_tiled(o_ref, y)

    @pl.when(i >= n_used)
    def _():
        o_ref[...] = jnp.zeros_like(o_ref)


def _expert_call(blk_e, n_used, src_tok, h2, layer, w_gu, b_gu, w_dn, b_dn):
    d = D_MODEL
    tr = MOE_ROWS
    n_blk = blk_e.shape[0]
    depth, e, _, f2 = w_gu.shape
    last = n_blk - 1
    return pl.pallas_call(
        _expert_kernel,
        out_shape=jax.ShapeDtypeStruct((n_blk * tr * SEGS, LANES), F32),
        grid_spec=pltpu.PrefetchScalarGridSpec(
            num_scalar_prefetch=2,
            grid=(n_blk,),
            in_specs=[
                pl.BlockSpec((1, 1, tr), lambda i, be, nu: (0, 0, 0), memory_space=pltpu.SMEM),
                pl.BlockSpec((1, 1, tr), lambda i, be, nu: (jnp.minimum(i + 1, last), 0, 0),
                             memory_space=pltpu.SMEM),
                pl.BlockSpec(memory_space=pl.ANY),
                pl.BlockSpec((None, None, d, f2), lambda i, be, nu: (layer, be[i], 0, 0)),
                pl.BlockSpec((None, None, 1, f2), lambda i, be, nu: (layer, be[i], 0, 0)),
                pl.BlockSpec((None, None, D_FF, d), lambda i, be, nu: (layer, be[i], 0, 0)),
                pl.BlockSpec((None, None, 1, d), lambda i, be, nu: (layer, be[i], 0, 0)),
            ],
            out_specs=pl.BlockSpec((tr * SEGS, LANES), lambda i, be, nu: (i, 0)),
            scratch_shapes=[pltpu.VMEM((2, tr * SEGS, LANES), F32), pltpu.SemaphoreType.DMA((2,)),
                            pltpu.VMEM((d, f2), BF16), pltpu.VMEM((D_FF, d), BF16)],
        ),
        compiler_params=_cparams(("arbitrary",), row_dma=True),
        name="moe_experts",
    )(blk_e, n_used, src_tok.reshape(n_blk, 1, tr), src_tok.reshape(n_blk, 1, tr), h2,
      w_gu, b_gu.reshape(depth, e, 1, f2), w_dn, b_dn.reshape(depth, e, 1, d))


def _combine_kernel(d0_ref, d1_ref, y_hbm, slab_ref, x_ref, mod_ref, gn_ref, o_ref, buf, sems):
    i = pl.program_id(0)
    tb = o_ref.shape[0]
    slot = i % 2

    def gather(idx_ref, s):
        def issue(r, carry):
            for k in range(TOP_K):
                row = idx_ref[0, 0, k * tb + r]
                pltpu.make_async_copy(_row_tile(y_hbm, row), _row_tile(buf.at[s, k], r),
                                      sems.at[s]).start(priority=k % 2)
            return carry

        lax.fori_loop(0, tb, issue, 0, unroll=4)

    @pl.when(i == 0)
    def _():
        gather(d0_ref, 0)

    @pl.when(i + 1 < pl.num_programs(0))
    def _():
        gather(d1_ref, 1 - slot)

    for k in range(TOP_K):
        pltpu.make_async_copy(y_hbm.at[pl.ds(0, tb * SEGS), :], buf.at[slot, k], sems.at[slot]).wait()
    slab = slab_ref[...]
    y2 = jnp.zeros(o_ref.shape, F32)
    for k in range(TOP_K):
        y2 = y2 + _load_tiled(buf.at[slot, k], tb) * slab[:, 2 * TOP_K + k:2 * TOP_K + k + 1]
    o_ref[...] = x_ref[...] + mod_ref[0, 5:6, :] * _rms(y2, gn_ref[3:4, :])


def _combine_call(dest, y_sorted, slab, x, mod, gn, cb_rows):
    t, d = x.shape
    tb = COMBINE_ROWS
    nb = t // tb
    cbb = max(cb_rows // tb, 1)
    sel = (lambda i: (jnp.minimum(i // cbb, 1), 0, 0)) if cb_rows else (lambda i: (1, 0, 0))
    dest3 = dest.reshape(TOP_K, nb, tb).transpose(1, 0, 2).reshape(nb, 1, TOP_K * tb)
    return pl.pallas_call(
        _combine_kernel,
        out_shape=jax.ShapeDtypeStruct((t, d), F32),
        grid=(nb,),
        in_specs=[
            pl.BlockSpec((1, 1, tb * TOP_K), lambda i: (0, 0, 0), memory_space=pltpu.SMEM),
            pl.BlockSpec((1, 1, tb * TOP_K), lambda i: (jnp.minimum(i + 1, nb - 1), 0, 0),
                         memory_space=pltpu.SMEM),
            pl.BlockSpec(memory_space=pl.ANY),
            pl.BlockSpec((tb, LANES), lambda i: (i, 0)),
            pl.BlockSpec((tb, d), lambda i: (i, 0)),
            pl.BlockSpec((1, 6, d), sel),
            pl.BlockSpec((4, d), lambda i: (0, 0)),
        ],
        out_specs=pl.BlockSpec((tb, d), lambda i: (i, 0)),
        scratch_shapes=[pltpu.VMEM((2, TOP_K, tb * SEGS, LANES), F32), pltpu.SemaphoreType.DMA((2,))],
        compiler_params=_cparams(("arbitrary",), row_dma=True),
        name="moe_combine",
    )(dest3, dest3, y_sorted, slab, x, mod, gn)


INVERT_BLOCK = 2048


def _invert_kernel(dest_ref, zeros_hbm, src_ref):
    k, j = pl.program_id(0), pl.program_id(1)
    n = dest_ref.shape[2]

    @pl.when((k == 0) & (j == 0))
    def _():
        pltpu.sync_copy(zeros_hbm, src_ref)

    tok0 = j * n

    def put(t, carry):
        src_ref[dest_ref[0, 0, t]] = tok0 + t
        return carry

    lax.fori_loop(0, n, put, 0, unroll=16)


def _invert_call(dest, n_rows):
    t = dest.shape[1]
    blk = max(b for b in range(LANES, INVERT_BLOCK + 1, LANES) if t % b == 0)
    nb = t // blk
    return pl.pallas_call(
        _invert_kernel,
        out_shape=jax.ShapeDtypeStruct((n_rows,), jnp.int32),
        grid=(TOP_K, nb),
        in_specs=[pl.BlockSpec((1, 1, blk), lambda k, j: (k * nb + j, 0, 0), memory_space=pltpu.SMEM),
                  pl.BlockSpec(memory_space=pl.ANY)],
        out_specs=pl.BlockSpec(memory_space=pltpu.SMEM),
        compiler_params=_cparams(("arbitrary", "arbitrary")),
        name="moe_invert",
    )(dest.reshape(TOP_K * nb, 1, blk), jnp.zeros((n_rows,), jnp.int32))


def _moe(h2, logits, x, mod, gn, layer, w_gu, b_gu, w_dn, b_dn, cb_rows):
    t = logits.shape[0]
    tr = MOE_ROWS
    slab, counts = _router_call(logits)
    slab_t = slab[:, :2 * TOP_K].T
    e_idx = slab_t[:TOP_K].astype(jnp.int32)
    rank = slab_t[TOP_K:].astype(jnp.int32)
    counts = counts[0].astype(jnp.int32)
    padded = (counts + tr - 1) // tr * tr
    p_end = jnp.cumsum(padded)
    p_start = p_end - padded
    dest = rank
    for e in range(N_EXPERTS):
        dest = dest + jnp.where(e_idx == e, p_start[e], 0)
    a = t * TOP_K
    n_blk = (a + N_EXPERTS * (tr - 1) + tr - 1) // tr
    src_tok = _invert_call(dest, n_blk * tr)
    blk_start = jnp.arange(n_blk, dtype=jnp.int32) * tr
    blk_e = jnp.sum((p_end[None, :] <= blk_start[:, None]).astype(jnp.int32), axis=1)
    blk_e = jnp.minimum(blk_e, N_EXPERTS - 1)
    n_used = (p_end[-1] // tr).astype(jnp.int32).reshape(1)
    ys = _expert_call(blk_e, n_used, src_tok, h2, layer, w_gu, b_gu, w_dn, b_dn)
    return _combine_call(dest, ys, slab, x, mod, gn, cb_rows)


def _hgrn_in_kernel(x_ref, mod_ref, gn_ref, w_ref, lb_ref, q_ref, kf_ref, lf_ref, kb_ref, lbw_ref,
                    i_ref, gs_ref):
    x = x_ref[...]
    h = _rms(x, gn_ref[...]) * (1.0 + mod_ref[0, 1:2, :]) + mod_ref[0, 0:1, :]
    hb = h.astype(BF16)
    f = D_MODEL

    def proj(sec):
        return jnp.dot(hb, w_ref[:, sec * f:(sec + 1) * f], preferred_element_type=F32)

    z = proj(0)
    q_ref[...] = _silu(z).astype(q_ref.dtype)
    for sec, k_ref, l_ref in ((1, kf_ref, lf_ref), (2, kb_ref, lbw_ref)):
        z = proj(sec)
        lb = lb_ref[sec - 1:sec, :]
        sg = _sigmoid(z)
        l_ref[...] = jnp.log(lb + (1.0 - lb) * sg)
        k_ref[...] = ((1.0 - lb) * (1.0 - sg)).astype(k_ref.dtype)
    i_ref[...] = proj(3).astype(i_ref.dtype)
    z = proj(4)
    gs_ref[...] = _silu(z).astype(gs_ref.dtype)


def _hgrn_in_call(xs, mod, gn, w, lb, cb):
    t, d = xs.shape
    tm = ROW_BLOCK
    sel = lambda i: (jnp.minimum(i // cb, 1), 0, 0)
    row = pl.BlockSpec((tm, d), lambda i: (i, 0))
    dts = (BF16, BF16, F32, BF16, F32, BF16, BF16)
    return pl.pallas_call(
        _hgrn_in_kernel,
        out_shape=tuple(jax.ShapeDtypeStruct((t, d), dt) for dt in dts),
        grid=(t // tm,),
        in_specs=[row, pl.BlockSpec((1, 6, d), sel), pl.BlockSpec((1, d), lambda i: (0, 0)),
                  pl.BlockSpec(w.shape, lambda i: (0, 0)), pl.BlockSpec((2, d), lambda i: (0, 0))],
        out_specs=tuple(row for _ in dts),
        compiler_params=_cparams(("parallel",)),
        name="hgrn_in_proj",
    )(xs, mod, gn, w, lb)


def _split3(x):
    hi = x.astype(BF16)
    r1 = x - hi.astype(F32)
    mid = r1.astype(BF16)
    lo = (r1 - mid.astype(F32)).astype(BF16)
    return hi, mid, lo


def _scan_kernel(q_ref, k_ref, g_ref, v_ref, o_ref, st_ref, gcum_ref, *, reverse):
    j = pl.program_id(0)
    ch, sub = HGRN_CHUNK, HGRN_SUB
    ns = ch // sub
    dk = HGRN_DK

    @pl.when(j == 0)
    def _():
        st_ref[...] = jnp.zeros_like(st_ref)

    rr = lax.broadcasted_iota(jnp.int32, (ch, ch), 0)
    cc = lax.broadcasted_iota(jnp.int32, (ch, ch), 1)
    causal = (cc >= rr) if reverse else (cc <= rr)
    tri = jnp.where(causal, 1.0, 0.0).astype(BF16)
    hi, mid, lo = _split3(g_ref[...])
    gcum_ref[...] = LOG2E * (jnp.dot(tri, hi, preferred_element_type=F32)
                             + jnp.dot(tri, mid, preferred_element_type=F32)
                             + jnp.dot(tri, lo, preferred_element_type=F32))
    key_blocks = list(range(1, ns)) if reverse else list(range(ns - 1))
    nkb = len(key_blocks)
    kbd_mask = (lax.broadcasted_iota(jnp.int32, (ch, nkb * dk), 1) // dk + key_blocks[0]
                == lax.broadcasted_iota(jnp.int32, (ch, nkb * dk), 0) // sub)
    kdg_mask = (lax.broadcasted_iota(jnp.int32, (ch, sub * dk), 1) // dk
                == lax.broadcasted_iota(jnp.int32, (ch, sub * dk), 0) % sub)
    end_row = 0 if reverse else ch - 1
    edge = 0 if reverse else sub - 1
    nt = (((1,), (1,)), ((), ()))
    tn = (((0,), (0,)), ((), ()))

    grp = HGRN_GROUP
    rr_g = lax.broadcasted_iota(jnp.int32, (ch, grp * ch), 0)
    cc_g = lax.broadcasted_iota(jnp.int32, (ch, grp * ch), 1) % ch
    diag_mask_g = ((rr_g // sub) == (cc_g // sub)) & ((cc_g >= rr_g) if reverse else (cc_g <= rr_g))

    def place(x, i):
        w = x.shape[1]
        parts = []
        if i:
            parts.append(jnp.zeros((x.shape[0], i * w), x.dtype))
        parts.append(x)
        if grp - 1 - i:
            parts.append(jnp.zeros((x.shape[0], (grp - 1 - i) * w), x.dtype))
        return jnp.concatenate(parts, axis=1)

    for g0 in range(0, HGRN_HEADS, grp):
        qgs, q_cats, p_cats, kbd_rows, kdg_rows, st_rows, v_rows = [], [], [], [], [], [], []
        for i in range(grp):
            h = g0 + i
            hs = slice(h * dk, (h + 1) * dk)
            q = q_ref[:, hs].astype(F32)
            kb = k_ref[:, hs]
            k = kb.astype(F32)
            v = v_ref[:, hs]
            gc = gcum_ref[:, hs]
            st = st_ref[h]
            g_end = gc[end_row:end_row + 1, :]
            qgs.append((q * jnp.exp2(gc)).astype(BF16))
            st_rows.append(place(st.astype(BF16), i))
            v_rows.append(place(v, i))
            k_end = (k * jnp.exp2(g_end - gc)).astype(BF16)
            st_ref[h] = st * jnp.exp2(g_end) + lax.dot_general(v, k_end, tn, preferred_element_type=F32)
            gc4 = gc.reshape(ns, sub, dk)
            g_edge = jnp.broadcast_to(gc4[:, edge:edge + 1, :], (ns, sub, dk)).reshape(ch, dk)
            k_rel = (k * jnp.exp2(g_edge - gc)).astype(BF16)
            k_bd = jnp.where(kbd_mask, jnp.concatenate([k_rel] * nkb, axis=1), jnp.zeros((), BF16))
            kbd_rows.append(place(k_bd, i))
            q_parts = []
            for jb in key_blocks:
                row = jb * sub + edge
                rows = slice(0, jb * sub) if reverse else slice((jb + 1) * sub, ch)
                part = q[rows] * jnp.exp2(gc[rows] - gc[row:row + 1, :])
                pad = jnp.zeros((ch - part.shape[0], dk), F32)
                full = jnp.concatenate([part, pad] if reverse else [pad, part], axis=0)
                q_parts.append(full.astype(BF16))
            q_cats.append(jnp.concatenate(q_parts, axis=1))
            p_parts = []
            for s in range(sub):
                g_s = jnp.broadcast_to(gc4[:, s:s + 1, :], (ns, sub, dk)).reshape(ch, dk)
                p_parts.append((q * jnp.exp2(jnp.minimum(gc - g_s, 0.0))).astype(BF16))
            p_cats.append(jnp.concatenate(p_parts, axis=1))
            k_dg = jnp.where(kdg_mask, jnp.concatenate([kb] * sub, axis=1), jnp.zeros((), BF16))
            kdg_rows.append(place(k_dg, i))
        s_off = lax.dot_general(jnp.concatenate(q_cats, axis=1), jnp.concatenate(kbd_rows, axis=0), nt,
                                preferred_element_type=F32)
        s_diag = lax.dot_general(jnp.concatenate(p_cats, axis=1), jnp.concatenate(kdg_rows, axis=0), nt,
                                 preferred_element_type=F32)
        a = s_off + jnp.where(diag_mask_g, s_diag, 0.0)
        o = lax.dot_general(jnp.concatenate(qgs, axis=1), jnp.concatenate(st_rows, axis=0), nt,
                            preferred_element_type=F32)
        o = o + jnp.dot(a.astype(BF16), jnp.concatenate(v_rows, axis=0), preferred_element_type=F32)
        o_ref[:, g0 * dk:(g0 + grp) * dk] = o


def _scan_call(q, k, lg, v, c_len, reverse):
    t, d = q.shape
    ch = HGRN_CHUNK
    n_ch = t // ch
    cc = c_len // ch
    if reverse:
        idx = lambda j: (jnp.where(j < cc, cc - 1 - j, n_ch - 1 - (j - cc)), 0)
    else:
        idx = lambda j: (j, 0)
    blk = pl.BlockSpec((ch, d), idx)
    return pl.pallas_call(
        functools.partial(_scan_kernel, reverse=reverse),
        out_shape=jax.ShapeDtypeStruct((t, d), F32),
        grid=(n_ch,),
        in_specs=[blk, blk, blk, blk],
        out_specs=blk,
        scratch_shapes=[pltpu.VMEM((HGRN_HEADS, HGRN_DK, HGRN_DK), F32), pltpu.VMEM((ch, d), F32)],
        compiler_params=_cparams(("arbitrary",)),
        name="hgrn_scan_bwd" if reverse else "hgrn_scan_fwd",
    )(q, k, lg, v)


def _rope_tables(l, c_len):
    n = HEAD_DIM // 4
    inv_freq = ROPE_BASE ** (-jnp.arange(n, dtype=F32) / n)
    rows = l // GRID_W
    ang_row = jnp.arange(rows, dtype=F32)[:, None] * inv_freq[None, :]
    ang_col = jnp.arange(GRID_W, dtype=F32)[:, None] * inv_freq[None, :]
    cr, sr, cc, sc = jnp.cos(ang_row), jnp.sin(ang_row), jnp.cos(ang_col), jnp.sin(ang_col)
    zr, zc = jnp.zeros_like(cr), jnp.zeros_like(cc)
    cos_r = jnp.concatenate([cr, cr, zr, zr] * 2, axis=1)
    sin_r = jnp.concatenate([-sr, sr, zr, zr] * 2, axis=1)
    cos_c = jnp.concatenate([zc, zc, cc, cc] * 2, axis=1)
    sin_c = jnp.concatenate([zc, zc, -sc, sc] * 2, axis=1)
    cos = (cos_r[:, None, :] + cos_c[None, :, :]).reshape(l, LANES)
    sin = (sin_r[:, None, :] + sin_c[None, :, :]).reshape(l, LANES)
    cos = jnp.concatenate([jnp.ones((c_len, LANES), F32), cos], axis=0)
    sin = jnp.concatenate([jnp.zeros((c_len, LANES), F32), sin], axis=0)
    return cos, sin


def kernel(x, c, ctx, c_ctx, ada_w, ada_b, norm_g, attn_w_qkv, attn_b_qkv, attn_sink, attn_w_o,
           hgrn_w_in, hgrn_lb, hgrn_norm_g, hgrn_w_o, router_w, router_b, moe_w_gu, moe_b_gu,
           moe_w_dn, moe_b_dn):
    b, l, d = x.shape
    c_len = ctx.shape[1]
    depth = ada_w.shape[0]
    assert b == 1 and d == D_MODEL and depth == 2
    assert c_len % ROW_BLOCK == 0 and l % ROW_BLOCK == 0
    cb = c_len // ROW_BLOCK

    cs = jnp.zeros((8, d), F32).at[0].set(c_ctx).at[1].set(c[0])
    mods = _ada_call(cs, ada_w, ada_b)
    mods = mods[:, :2].reshape(depth, 2, 6, d)

    cos, sin = _rope_tables(l, c_len)

    q, k4, v4 = _qkv_call(ctx[0], x[0], mods[0], norm_g[0, 0:1], attn_w_qkv[0].astype(BF16),
                          attn_b_qkv[0].reshape(1, -1), cos, sin, cb)
    o = _attn_call(attn_sink[0], q, k4, v4, c_len)
    xs, h2, logits = _out_call("attn", (o,), (ctx[0], x[0]), mods[0], norm_g[0], attn_w_o[0].astype(BF16),
                               router_w[0], router_b[0].reshape(1, -1), None, cb, 0)
    xs = _moe(h2, logits, xs, mods[0], norm_g[0], 0, moe_w_gu, moe_b_gu, moe_w_dn, moe_b_dn, c_len)

    lb_soft = jax.nn.softmax(hgrn_lb.astype(F32), axis=0)
    lb = jnp.cumsum(lb_soft, axis=0)[1] - lb_soft[0]
    qh, kf, lf, kb, lbw, iv, gs = _hgrn_in_call(xs, mods[1], norm_g[1, 0:1], hgrn_w_in[0].astype(BF16),
                                                lb, cb)
    o_f = _scan_call(qh, kf, lf, iv, c_len, reverse=False)
    o_b = _scan_call(qh, kb, lbw, iv, c_len, reverse=True)
    x_lat, h2, logits = _out_call("hgrn", (o_f, o_b, gs), xs, mods[1], norm_g[1],
                                  hgrn_w_o[0].astype(BF16), router_w[1], router_b[1].reshape(1, -1),
                                  hgrn_norm_g[0].reshape(1, -1), cb, cb)
    out = _moe(h2, logits, x_lat, mods[1], norm_g[1], 1, moe_w_gu, moe_b_gu, moe_w_dn, moe_b_dn, 0)
    return out[None]
```

```python
import functools

import jax
import jax.numpy as jnp
from jax import lax
from jax.experimental import pallas as pl
from jax.experimental.pallas import tpu as pltpu

D_MODEL = 1024
GRID_W = 64
RMS_EPS = 1e-6

ATTN_HEADS = 16
ATTN_KV_HEADS = 2
HEAD_DIM = 64
Q_DIM = ATTN_HEADS * HEAD_DIM
KV_DIM = ATTN_KV_HEADS * HEAD_DIM
QKV_DIM = Q_DIM + 2 * KV_DIM
WINDOW = 128
ATTN_BLOCK = 128
ROPE_BASE = 10000.0

HGRN_HEADS = 8
HGRN_DK = 128
HGRN_CHUNK = 64
HGRN_SUB = 8
HGRN_GROUP = 1

N_EXPERTS = 32
TOP_K = 4
D_FF = 1024
SWIGLU_LIMIT = 7.0
SWIGLU_ALPHA = 1.702

LANES = 128
ROW_BLOCK = 256
MOE_ROWS = 256
COMBINE_ROWS = 256
VMEM_LIMIT = 56 * 1024 * 1024

F32 = jnp.float32
BF16 = jnp.bfloat16
NEG_BIG = -1e30
LOG2E = 1.4426950408889634


def _cparams(sem, row_dma=False):
    return pltpu.CompilerParams(dimension_semantics=sem, vmem_limit_bytes=VMEM_LIMIT,
                                disable_bounds_checks=row_dma)


def _rms(x, g):
    return x * lax.rsqrt(jnp.mean(x * x, axis=-1, keepdims=True) + RMS_EPS) * g


def _sigmoid(x):
    return 1.0 / (1.0 + jnp.exp(-x))


def _silu(x):
    return x * (0.5 * jnp.tanh(0.5 * x) + 0.5)


SEGS = D_MODEL // LANES


def _store_tiled(ref, val):
    n = val.shape[0]
    for s in range(SEGS):
        ref[pl.ds(s, n, stride=SEGS), :] = val[:, s * LANES:(s + 1) * LANES]


def _load_tiled(ref, n):
    return jnp.concatenate([ref[pl.ds(s, n, stride=SEGS), :] for s in range(SEGS)], axis=1)


def _row_tile(ref, r):
    return ref.at[pl.ds(pl.multiple_of(r * SEGS, SEGS), SEGS), :]


def _ada_kernel(c_ref, w_ref, b_ref, o_ref):
    c = c_ref[...]
    s = c * _sigmoid(c)
    o_ref[0] = jnp.dot(s, w_ref[0], precision=lax.Precision.HIGHEST,
                       preferred_element_type=F32) + b_ref[0]


def _ada_call(cs, ada_w, ada_b):
    depth, d, n = ada_w.shape
    tn = 1024
    return pl.pallas_call(
        _ada_kernel,
        out_shape=jax.ShapeDtypeStruct((depth, 8, n), F32),
        grid=(depth, n // tn),
        in_specs=[
            pl.BlockSpec((8, d), lambda i, j: (0, 0)),
            pl.BlockSpec((1, d, tn), lambda i, j: (i, 0, j)),
            pl.BlockSpec((1, 1, tn), lambda i, j: (i, 0, j)),
        ],
        out_specs=pl.BlockSpec((1, 8, tn), lambda i, j: (i, 0, j)),
        compiler_params=_cparams(("arbitrary", "arbitrary")),
        name="adaln",
    )(cs, ada_w, ada_b.reshape(depth, 1, n))


def _swap16(t):
    lane = lax.broadcasted_iota(jnp.int32, t.shape, 1)
    return jnp.where(lane % 32 < 16, pltpu.roll(t, LANES - 16, 1), pltpu.roll(t, 16, 1))


def _stream_rows(c_ref, x_ref, cb):
    return jnp.where(pl.program_id(0) < cb, c_ref[...], x_ref[...])


def _stream_specs(tm, d, cb):
    return [pl.BlockSpec((tm, d), lambda i: (jnp.minimum(i, cb - 1), 0)),
            pl.BlockSpec((tm, d), lambda i: (jnp.maximum(i - cb, 0), 0))]


def _qkv_kernel(c_ref, x_ref, mod_ref, gn_ref, w_ref, b_ref, cos_ref, sin_ref, q_ref, k4_ref, v4_ref, *, cb):
    x = _stream_rows(c_ref, x_ref, cb)
    h = _rms(x, gn_ref[...]) * (1.0 + mod_ref[0, 1:2, :]) + mod_ref[0, 0:1, :]
    hb = h.astype(BF16)
    cos = cos_ref[...]
    sin = sin_ref[...]
    nq = Q_DIM // LANES
    for j in range(nq + 1):
        sl = slice(j * LANES, (j + 1) * LANES)
        t = jnp.dot(hb, w_ref[:, sl], preferred_element_type=F32) + b_ref[:, sl]
        t = t * cos + _swap16(t) * sin
        if j < nq:
            q_ref[:, sl] = (t * (HEAD_DIM ** -0.5 * LOG2E)).astype(q_ref.dtype)
        else:
            kt = t
    sl = slice(Q_DIM + KV_DIM, QKV_DIM)
    vt = jnp.dot(hb, w_ref[:, sl], preferred_element_type=F32) + b_ref[:, sl]
    lo = lax.broadcasted_iota(jnp.int32, kt.shape, 1) < HEAD_DIM
    for t, ref in ((kt, k4_ref), (vt, v4_ref)):
        sw = pltpu.roll(t, HEAD_DIM, 1)
        ref[:, 0 * LANES:1 * LANES] = jnp.where(lo, t, 0.0).astype(ref.dtype)
        ref[:, 1 * LANES:2 * LANES] = jnp.where(lo, 0.0, sw).astype(ref.dtype)
        ref[:, 2 * LANES:3 * LANES] = jnp.where(lo, sw, 0.0).astype(ref.dtype)
        ref[:, 3 * LANES:4 * LANES] = jnp.where(lo, 0.0, t).astype(ref.dtype)


def _qkv_call(ctx, x, mod, gn, w, b, cos, sin, cb):
    d = x.shape[1]
    t = ctx.shape[0] + x.shape[0]
    tm = ROW_BLOCK
    sel = lambda i: (jnp.minimum(i // cb, 1), 0, 0)
    return pl.pallas_call(
        functools.partial(_qkv_kernel, cb=cb),
        out_shape=(jax.ShapeDtypeStruct((t, Q_DIM), BF16),
                   jax.ShapeDtypeStruct((t, 4 * LANES), BF16),
                   jax.ShapeDtypeStruct((t, 4 * LANES), BF16)),
        grid=(t // tm,),
        in_specs=_stream_specs(tm, d, cb) + [
            pl.BlockSpec((1, 6, d), sel),
            pl.BlockSpec((1, d), lambda i: (0, 0)),
            pl.BlockSpec((d, QKV_DIM), lambda i: (0, 0)),
            pl.BlockSpec((1, QKV_DIM), lambda i: (0, 0)),
            pl.BlockSpec((tm, LANES), lambda i: (i, 0)),
            pl.BlockSpec((tm, LANES), lambda i: (i, 0)),
        ],
        out_specs=(pl.BlockSpec((tm, Q_DIM), lambda i: (i, 0)),
                   pl.BlockSpec((tm, 4 * LANES), lambda i: (i, 0)),
                   pl.BlockSpec((tm, 4 * LANES), lambda i: (i, 0))),
        compiler_params=_cparams(("parallel",)),
        name="qkv_rope",
    )(ctx, x, mod, gn, w, b, cos, sin)


def _attn_kernel(sink_ref, q_ref, kp_ref, kc_ref, kn_ref, kx_ref, vp_ref, vc_ref, vn_ref, vx_ref,
                 o_ref, *, cb, n_lat):
    n = pl.program_id(0)
    blk = ATTN_BLOCK
    c = kx_ref.shape[0]
    nw = 3 * blk
    s = lax.broadcasted_iota(jnp.int32, (nw + c, blk), 0)
    r = lax.broadcasted_iota(jnp.int32, (nw + c, blk), 1)
    q_pos = (n - cb) * blk + r
    k_pos = (n - cb - 1) * blk + s
    win_ok = (jnp.abs(q_pos - k_pos) <= WINDOW) & (k_pos >= 0) & (k_pos < n_lat) & (n >= cb)
    valid = (s >= nw) | win_ok
    k_all = jnp.concatenate([kp_ref[...], kc_ref[...], kn_ref[...], kx_ref[...]], axis=0)
    v_all = jnp.concatenate([vp_ref[...], vc_ref[...], vn_ref[...], vx_ref[...]], axis=0)
    n_var = 2 * ATTN_KV_HEADS
    v_t = [v_all[:, j * LANES:(j + 1) * LANES].astype(F32).T.astype(BF16) for j in range(n_var)]
    nt = (((1,), (1,)), ((), ()))
    group = ATTN_HEADS // ATTN_KV_HEADS
    for p in range(ATTN_HEADS // 2):
        g = (2 * p) // group
        qp = q_ref[:, p * LANES:(p + 1) * LANES]
        o_t = jnp.zeros((LANES, blk), F32)
        for half in range(2):
            j = 2 * g + half
            sc = lax.dot_general(k_all[:, j * LANES:(j + 1) * LANES], qp, nt,
                                 preferred_element_type=F32)
            sc = jnp.where(valid, sc, NEG_BIG)
            sink = sink_ref[2 * p + half] * LOG2E
            m = jnp.maximum(jnp.max(sc, axis=0, keepdims=True), sink)
            e = jnp.exp2(sc - m)
            denom = jnp.sum(e, axis=0, keepdims=True) + jnp.exp2(sink - m)
            pv = jnp.dot(v_t[j], e.astype(BF16), preferred_element_type=F32)
            o_t = o_t + pv / denom
        o_ref[:, p * LANES:(p + 1) * LANES] = o_t.T.astype(o_ref.dtype)


def _attn_call(sink, q, k4, v4, c_len):
    t = q.shape[0]
    blk = ATTN_BLOCK
    cb = c_len // blk
    nb = t // blk
    n_lat = t - c_len
    last = nb - 1
    kw = 4 * LANES
    spec_q = pl.BlockSpec((blk, Q_DIM), lambda n: (n, 0))
    prev = pl.BlockSpec((blk, kw), lambda n: (jnp.maximum(n - 1, 0), 0))
    cur = pl.BlockSpec((blk, kw), lambda n: (n, 0))
    nxt = pl.BlockSpec((blk, kw), lambda n: (jnp.minimum(n + 1, last), 0))
    ctx = pl.BlockSpec((c_len, kw), lambda n: (0, 0))
    return pl.pallas_call(
        functools.partial(_attn_kernel, cb=cb, n_lat=n_lat),
        out_shape=jax.ShapeDtypeStruct((t, Q_DIM), BF16),
        grid=(nb,),
        in_specs=[pl.BlockSpec(memory_space=pltpu.SMEM), spec_q,
                  prev, cur, nxt, ctx, prev, cur, nxt, ctx],
        out_specs=pl.BlockSpec((blk, Q_DIM), lambda n: (n, 0)),
        compiler_params=_cparams(("parallel",)),
        name="window_attn",
    )(sink, q, k4, k4, k4, k4, v4, v4, v4, v4)


def _residual_tail(y, x, mod_ref, gn_ref, rw_ref, rb_ref, xo_ref, h2_ref, lg_ref):
    x_new = x + mod_ref[0, 2:3, :] * _rms(y, gn_ref[1:2, :])
    h2 = _rms(x_new, gn_ref[2:3, :]) * (1.0 + mod_ref[0, 4:5, :]) + mod_ref[0, 3:4, :]
    xo_ref[...] = x_new
    _store_tiled(h2_ref, h2)
    hi, mid, lo = _split3(h2)
    lane = lax.broadcasted_iota(jnp.int32, (h2.shape[0], LANES), 1)
    w = rw_ref[...]
    r = (jnp.dot(hi, w, preferred_element_type=F32)
         + jnp.where(lane < 2 * N_EXPERTS, jnp.dot(mid, w, preferred_element_type=F32), 0.0)
         + jnp.where(lane < N_EXPERTS, jnp.dot(lo, w, preferred_element_type=F32), 0.0))
    r = r + pltpu.roll(r, LANES - N_EXPERTS, 1) + pltpu.roll(r, LANES - 2 * N_EXPERTS, 1)
    lg_ref[...] = r[:, :N_EXPERTS] + rb_ref[...]


def _attn_out_kernel(a_ref, c_ref, x_ref, mod_ref, gn_ref, w_ref, rw_ref, rb_ref, xo_ref, h2_ref, lg_ref,
                     *, cb):
    y = jnp.dot(a_ref[...], w_ref[...], preferred_element_type=F32)
    _residual_tail(y, _stream_rows(c_ref, x_ref, cb), mod_ref, gn_ref, rw_ref, rb_ref, xo_ref, h2_ref, lg_ref)


def _hgrn_out_kernel(of_ref, ob_ref, gs_ref, hg_ref, x_ref, mod_ref, gn_ref, w_ref, rw_ref, rb_ref,
                     xo_ref, h2_ref, lg_ref):
    parts = []
    for h in range(HGRN_HEADS):
        sl = slice(h * HGRN_DK, (h + 1) * HGRN_DK)
        o = of_ref[:, sl] + ob_ref[:, sl]
        parts.append(_rms(o, hg_ref[:, sl]))
    a = jnp.concatenate(parts, axis=1) * gs_ref[...].astype(F32)
    y = jnp.dot(a.astype(BF16), w_ref[...], preferred_element_type=F32)
    _residual_tail(y, x_ref[...], mod_ref, gn_ref, rw_ref, rb_ref, xo_ref, h2_ref, lg_ref)


def _out_call(kind, acts, x, mod, gn, w, rw, rb, hg, cb, off):
    t_out = (x[0].shape[0] + x[1].shape[0]) if kind == "attn" else x.shape[0] - off * ROW_BLOCK
    d = D_MODEL
    tm = ROW_BLOCK
    w_hi, w_mid, w_lo = _split3(rw)
    rw = jnp.concatenate([w_hi, w_mid, w_lo, jnp.zeros_like(w_hi)], axis=1)
    row = lambda i: (i + off, 0)
    sel = lambda i: (jnp.minimum((i + off) // cb, 1), 0, 0)
    const = lambda i: (0, 0)
    common_specs = [
        pl.BlockSpec((1, 6, d), sel),
        pl.BlockSpec((4, d), const),
        pl.BlockSpec((d, d), const),
        pl.BlockSpec((d, LANES), const),
        pl.BlockSpec((1, N_EXPERTS), const),
    ]
    if kind == "attn":
        body = functools.partial(_attn_out_kernel, cb=cb)
        in_specs = [pl.BlockSpec((tm, d), row)] + _stream_specs(tm, d, cb) + common_specs
        args = (acts[0], x[0], x[1], mod, gn, w, rw, rb)
    else:
        body = _hgrn_out_kernel
        in_specs = ([pl.BlockSpec((tm, d), row)] * 3 + [pl.BlockSpec((1, d), const)]
                    + [pl.BlockSpec((tm, d), row)] + common_specs)
        args = (acts[0], acts[1], acts[2], hg, x, mod, gn, w, rw, rb)
    return pl.pallas_call(
        body,
        out_shape=(jax.ShapeDtypeStruct((t_out, d), F32),
                   jax.ShapeDtypeStruct((t_out * SEGS, LANES), F32),
                   jax.ShapeDtypeStruct((t_out, N_EXPERTS), F32)),
        grid=(t_out // tm,),
        in_specs=in_specs,
        out_specs=(pl.BlockSpec((tm, d), lambda i: (i, 0)),
                   pl.BlockSpec((tm * SEGS, LANES), lambda i: (i, 0)),
                   pl.BlockSpec((tm, N_EXPERTS), lambda i: (i, 0))),
        compiler_params=_cparams(("parallel",)),
        name=kind + "_out_residual",
    )(*args)


def _router_kernel(lg_ref, slab_ref, cnt_ref, run_ref):
    i = pl.program_id(0)

    @pl.when(i == 0)
    def _():
        run_ref[...] = jnp.zeros_like(run_ref)

    lg = lg_ref[...]
    tb = lg.shape[0]
    lane = lax.broadcasted_iota(jnp.int32, lg.shape, 1).astype(F32)
    sels, tops, idxs = [], [], []
    for _ in range(TOP_K):
        m = jnp.max(lg, axis=-1, keepdims=True)
        idx = jnp.min(jnp.where(lg == m, lane, float(N_EXPERTS)), axis=-1, keepdims=True)
        sel = lane == idx
        sels.append(sel)
        tops.append(m)
        idxs.append(idx)
        lg = jnp.where(sel, -jnp.inf, lg)
    ws = [jnp.exp(tk - tops[0]) for tk in tops]
    wsum = ws[0] + ws[1] + ws[2] + ws[3]
    chosen = (sels[0] | sels[1] | sels[2] | sels[3])
    onehot = jnp.where(chosen, 1.0, 0.0)
    rr = lax.broadcasted_iota(jnp.int32, (tb, tb), 0)
    cc = lax.broadcasted_iota(jnp.int32, (tb, tb), 1)
    tri = jnp.where(cc < rr, 1.0, 0.0).astype(BF16)
    before = jnp.dot(tri, onehot.astype(BF16), preferred_element_type=F32) + run_ref[...]
    out_lane = lax.broadcasted_iota(jnp.int32, (tb, LANES), 1)
    slab = jnp.zeros((tb, LANES), F32)
    for k in range(TOP_K):
        rank = jnp.sum(jnp.where(sels[k], before, 0.0), axis=-1, keepdims=True)
        slab = jnp.where(out_lane == k, idxs[k], slab)
        slab = jnp.where(out_lane == TOP_K + k, rank, slab)
        slab = jnp.where(out_lane == 2 * TOP_K + k, ws[k] / wsum, slab)
    slab_ref[...] = slab
    run_ref[...] = run_ref[...] + jnp.sum(onehot, axis=0, keepdims=True)
    cnt_ref[...] = run_ref[...]


def _router_call(logits):
    t = logits.shape[0]
    tb = ROW_BLOCK
    return pl.pallas_call(
        _router_kernel,
        out_shape=(jax.ShapeDtypeStruct((t, LANES), F32),
                   jax.ShapeDtypeStruct((1, N_EXPERTS), F32)),
        grid=(t // tb,),
        in_specs=[pl.BlockSpec((tb, N_EXPERTS), lambda i: (i, 0))],
        out_specs=(pl.BlockSpec((tb, LANES), lambda i: (i, 0)),
                   pl.BlockSpec((1, N_EXPERTS), lambda i: (0, 0))),
        scratch_shapes=[pltpu.VMEM((1, N_EXPERTS), F32)],
        compiler_params=_cparams(("arbitrary",)),
        name="router_topk",
    )(logits)


def _expert_kernel(be_ref, nu_ref, nx_ref, ws_ref, src0_ref, src1_ref, x_hbm, wgu_hbm, bgu_ref, wdn_hbm, bdn_ref,
                   o_ref, xbuf, sems, wgu_f, wdn_f, wsems, wgu_bf, wdn_bf, *, layer):
    i = pl.program_id(0)
    rows = o_ref.shape[0] // SEGS
    slot = i % 2
    n_used = nu_ref[0]

    def gather(idx_ref, s):
        def issue(r2, carry):
            for p in range(2):
                r = 2 * r2 + p
                pltpu.make_async_copy(_row_tile(x_hbm, idx_ref[0, 0, r]), _row_tile(xbuf.at[s], r),
                                      sems.at[s]).start()
            return carry

        lax.fori_loop(0, rows // 2, issue, 0, unroll=4)

    @pl.when(i == 0)
    def _():
        gather(src0_ref, 0)

    @pl.when(i + 1 < n_used)
    def _():
        gather(src1_ref, 1 - slot)

    prev = be_ref[jnp.maximum(i - 1, 0)]
    fresh = (i == 0) | (be_ref[i] != prev)

    e_cur = be_ref[i]
    wslot = ws_ref[i]

    def weight_copies(e, s):
        return (pltpu.make_async_copy(wgu_hbm.at[layer, e], wgu_f.at[s], wsems.at[s, 0]),
                pltpu.make_async_copy(wdn_hbm.at[layer, e], wdn_f.at[s], wsems.at[s, 1]))

    @pl.when(i == 0)
    def _():
        for c in weight_copies(e_cur, wslot):
            c.start(priority=1)

    @pl.when(fresh)
    def _():
        @pl.when(nx_ref[i] != e_cur)
        def _():
            for c in weight_copies(nx_ref[i], 1 - wslot):
                c.start(priority=1)

        for c in weight_copies(e_cur, wslot):
            c.wait()
        wgu_bf[...] = wgu_f[wslot].astype(BF16)
        wdn_bf[...] = wdn_f[wslot].astype(BF16)

    @pl.when(i < n_used)
    def _():
        pltpu.make_async_copy(x_hbm.at[pl.ds(0, rows * SEGS), :], xbuf.at[slot], sems.at[slot]).wait()
        xb = _load_tiled(xbuf.at[slot], rows).astype(BF16)
        gu = jnp.dot(xb, wgu_bf[...], preferred_element_type=F32) + bgu_ref[...]
        g = jnp.minimum(gu[:, :D_FF], SWIGLU_LIMIT)
        u = jnp.clip(gu[:, D_FF:], -SWIGLU_LIMIT, SWIGLU_LIMIT)
        hdn = g * _sigmoid(SWIGLU_ALPHA * g) * (u + 1.0)
        y = jnp.dot(hdn.astype(BF16), wdn_bf[...], preferred_element_type=F32) + bdn_ref[...]
        _store_tiled(o_ref, y)

    @pl.when(i >= n_used)
    def _():
        o_ref[...] = jnp.zeros_like(o_ref)


def _expert_call(blk_e, n_used, src_tok, h2, layer, w_gu, b_gu, w_dn, b_dn):
    d = D_MODEL
    tr = MOE_ROWS
    n_blk = blk_e.shape[0]
    depth, e, _, f2 = w_gu.shape
    last = n_blk - 1
    nxt = jnp.min(jnp.where(blk_e[None, :] > blk_e[:, None], blk_e[None, :], N_EXPERTS), axis=1)
    nxt = jnp.where(nxt == N_EXPERTS, blk_e, nxt).astype(jnp.int32)
    change = jnp.concatenate([jnp.zeros((1,), jnp.int32), (blk_e[1:] != blk_e[:-1]).astype(jnp.int32)])
    wslot = (jnp.cumsum(change) % 2).astype(jnp.int32)
    return pl.pallas_call(
        functools.partial(_expert_kernel, layer=layer),
        out_shape=jax.ShapeDtypeStruct((n_blk * tr * SEGS, LANES), F32),
        grid_spec=pltpu.PrefetchScalarGridSpec(
            num_scalar_prefetch=4,
            grid=(n_blk,),
            in_specs=[
                pl.BlockSpec((1, 1, tr), lambda i, *_: (0, 0, 0), memory_space=pltpu.SMEM),
                pl.BlockSpec((1, 1, tr), lambda i, *_: (jnp.minimum(i + 1, last), 0, 0),
                             memory_space=pltpu.SMEM),
                pl.BlockSpec(memory_space=pl.ANY),
                pl.BlockSpec(memory_space=pl.ANY),
                pl.BlockSpec((None, None, 1, f2), lambda i, be, *_: (layer, be[i], 0, 0)),
                pl.BlockSpec(memory_space=pl.ANY),
                pl.BlockSpec((None, None, 1, d), lambda i, be, *_: (layer, be[i], 0, 0)),
            ],
            out_specs=pl.BlockSpec((tr * SEGS, LANES), lambda i, *_: (i, 0)),
            scratch_shapes=[pltpu.VMEM((2, tr * SEGS, LANES), F32), pltpu.SemaphoreType.DMA((2,)),
                            pltpu.VMEM((2, d, f2), F32), pltpu.VMEM((2, D_FF, d), F32),
                            pltpu.SemaphoreType.DMA((2, 2)),
                            pltpu.VMEM((d, f2), BF16), pltpu.VMEM((D_FF, d), BF16)],
        ),
        compiler_params=_cparams(("arbitrary",), row_dma=True),
        name="moe_experts",
    )(blk_e, n_used, nxt, wslot, src_tok.reshape(n_blk, 1, tr), src_tok.reshape(n_blk, 1, tr), h2,
      w_gu, b_gu.reshape(depth, e, 1, f2), w_dn, b_dn.reshape(depth, e, 1, d))


def _combine_kernel(d0_ref, d1_ref, y_hbm, slab_ref, x_ref, mod_ref, gn_ref, o_ref, buf, sems):
    i = pl.program_id(0)
    tb = o_ref.shape[0]
    slot = i % 2

    def gather(idx_ref, s):
        def issue(r, carry):
            for k in range(TOP_K):
                row = idx_ref[0, 0, k * tb + r]
                pltpu.make_async_copy(_row_tile(y_hbm, row), _row_tile(buf.at[s, k], r),
                                      sems.at[s]).start(priority=k % 2)
            return carry

        lax.fori_loop(0, tb, issue, 0, unroll=4)

    @pl.when(i == 0)
    def _():
        gather(d0_ref, 0)

    @pl.when(i + 1 < pl.num_programs(0))
    def _():
        gather(d1_ref, 1 - slot)

    for k in range(TOP_K):
        pltpu.make_async_copy(y_hbm.at[pl.ds(0, tb * SEGS), :], buf.at[slot, k], sems.at[slot]).wait()
    slab = slab_ref[...]
    y2 = jnp.zeros(o_ref.shape, F32)
    for k in range(TOP_K):
        y2 = y2 + _load_tiled(buf.at[slot, k], tb) * slab[:, 2 * TOP_K + k:2 * TOP_K + k + 1]
    o_ref[...] = x_ref[...] + mod_ref[0, 5:6, :] * _rms(y2, gn_ref[3:4, :])


def _combine_call(dest, y_sorted, slab, x, mod, gn, cb_rows):
    t, d = x.shape
    tb = COMBINE_ROWS
    nb = t // tb
    cbb = max(cb_rows // tb, 1)
    sel = (lambda i: (jnp.minimum(i // cbb, 1), 0, 0)) if cb_rows else (lambda i: (1, 0, 0))
    dest3 = dest.reshape(TOP_K, nb, tb).transpose(1, 0, 2).reshape(nb, 1, TOP_K * tb)
    return pl.pallas_call(
        _combine_kernel,
        out_shape=jax.ShapeDtypeStruct((t, d), F32),
        grid=(nb,),
        in_specs=[
            pl.BlockSpec((1, 1, tb * TOP_K), lambda i: (0, 0, 0), memory_space=pltpu.SMEM),
            pl.BlockSpec((1, 1, tb * TOP_K), lambda i: (jnp.minimum(i + 1, nb - 1), 0, 0),
                         memory_space=pltpu.SMEM),
            pl.BlockSpec(memory_space=pl.ANY),
            pl.BlockSpec((tb, LANES), lambda i: (i, 0)),
            pl.BlockSpec((tb, d), lambda i: (i, 0)),
            pl.BlockSpec((1, 6, d), sel),
            pl.BlockSpec((4, d), lambda i: (0, 0)),
        ],
        out_specs=pl.BlockSpec((tb, d), lambda i: (i, 0)),
        scratch_shapes=[pltpu.VMEM((2, TOP_K, tb * SEGS, LANES), F32), pltpu.SemaphoreType.DMA((2,))],
        compiler_params=_cparams(("arbitrary",), row_dma=True),
        name="moe_combine",
    )(dest3, dest3, y_sorted, slab, x, mod, gn)


INVERT_BLOCK = 2048


def _invert_kernel(dest_ref, zeros_hbm, src_ref):
    k, j = pl.program_id(0), pl.program_id(1)
    n = dest_ref.shape[2]

    @pl.when((k == 0) & (j == 0))
    def _():
        pltpu.sync_copy(zeros_hbm, src_ref)

    tok0 = j * n

    def put(t, carry):
        src_ref[dest_ref[0, 0, t]] = tok0 + t
        return carry

    lax.fori_loop(0, n, put, 0, unroll=16)


def _invert_call(dest, n_rows):
    t = dest.shape[1]
    blk = max(b for b in range(LANES, INVERT_BLOCK + 1, LANES) if t % b == 0)
    nb = t // blk
    return pl.pallas_call(
        _invert_kernel,
        out_shape=jax.ShapeDtypeStruct((n_rows,), jnp.int32),
        grid=(TOP_K, nb),
        in_specs=[pl.BlockSpec((1, 1, blk), lambda k, j: (k * nb + j, 0, 0), memory_space=pltpu.SMEM),
                  pl.BlockSpec(memory_space=pl.ANY)],
        out_specs=pl.BlockSpec(memory_space=pltpu.SMEM),
        compiler_params=_cparams(("arbitrary", "arbitrary")),
        name="moe_invert",
    )(dest.reshape(TOP_K * nb, 1, blk), jnp.zeros((n_rows,), jnp.int32))


def _moe(h2, logits, x, mod, gn, layer, w_gu, b_gu, w_dn, b_dn, cb_rows):
    t = logits.shape[0]
    tr = MOE_ROWS
    slab, counts = _router_call(logits)
    slab_t = slab[:, :2 * TOP_K].T
    e_idx = slab_t[:TOP_K].astype(jnp.int32)
    rank = slab_t[TOP_K:].astype(jnp.int32)
    counts = counts[0].astype(jnp.int32)
    padded = (counts + tr - 1) // tr * tr
    p_end = jnp.cumsum(padded)
    p_start = p_end - padded
    dest = rank
    for e in range(N_EXPERTS):
        dest = dest + jnp.where(e_idx == e, p_start[e], 0)
    a = t * TOP_K
    n_blk = (a + N_EXPERTS * (tr - 1) + tr - 1) // tr
    src_tok = _invert_call(dest, n_blk * tr)
    blk_start = jnp.arange(n_blk, dtype=jnp.int32) * tr
    blk_e = jnp.sum((p_end[None, :] <= blk_start[:, None]).astype(jnp.int32), axis=1)
    blk_e = jnp.minimum(blk_e, N_EXPERTS - 1)
    n_used = (p_end[-1] // tr).astype(jnp.int32).reshape(1)
    ys = _expert_call(blk_e, n_used, src_tok, h2, layer, w_gu, b_gu, w_dn, b_dn)
    return _combine_call(dest, ys, slab, x, mod, gn, cb_rows)


def _hgrn_in_kernel(x_ref, mod_ref, gn_ref, w_ref, lb_ref, q_ref, kf_ref, lf_ref, kb_ref, lbw_ref,
                    i_ref, gs_ref):
    x = x_ref[...]
    h = _rms(x, gn_ref[...]) * (1.0 + mod_ref[0, 1:2, :]) + mod_ref[0, 0:1, :]
    hb = h.astype(BF16)
    f = D_MODEL

    def proj(sec):
        return jnp.dot(hb, w_ref[:, sec * f:(sec + 1) * f], preferred_element_type=F32)

    z = proj(0)
    q_ref[...] = _silu(z).astype(q_ref.dtype)
    for sec, k_ref, l_ref in ((1, kf_ref, lf_ref), (2, kb_ref, lbw_ref)):
        z = proj(sec)
        lb = lb_ref[sec - 1:sec, :]
        sg = _sigmoid(z)
        l_ref[...] = jnp.log(lb + (1.0 - lb) * sg)
        k_ref[...] = ((1.0 - lb) * (1.0 - sg)).astype(k_ref.dtype)
    i_ref[...] = proj(3).astype(i_ref.dtype)
    z = proj(4)
    gs_ref[...] = _silu(z).astype(gs_ref.dtype)


def _hgrn_in_call(xs, mod, gn, w, lb, cb):
    t, d = xs.shape
    tm = ROW_BLOCK
    sel = lambda i: (jnp.minimum(i // cb, 1), 0, 0)
    row = pl.BlockSpec((tm, d), lambda i: (i, 0))
    dts = (BF16, BF16, F32, BF16, F32, BF16, BF16)
    return pl.pallas_call(
        _hgrn_in_kernel,
        out_shape=tuple(jax.ShapeDtypeStruct((t, d), dt) for dt in dts),
        grid=(t // tm,),
        in_specs=[row, pl.BlockSpec((1, 6, d), sel), pl.BlockSpec((1, d), lambda i: (0, 0)),
                  pl.BlockSpec(w.shape, lambda i: (0, 0)), pl.BlockSpec((2, d), lambda i: (0, 0))],
        out_specs=tuple(row for _ in dts),
        compiler_params=_cparams(("parallel",)),
        name="hgrn_in_proj",
    )(xs, mod, gn, w, lb)


def _split3(x):
    hi = x.astype(BF16)
    r1 = x - hi.astype(F32)
    mid = r1.astype(BF16)
    lo = (r1 - mid.astype(F32)).astype(BF16)
    return hi, mid, lo


def _scan_kernel(q_ref, k_ref, g_ref, v_ref, o_ref, st_ref, gcum_ref, *, reverse):
    j = pl.program_id(0)
    ch, sub = HGRN_CHUNK, HGRN_SUB
    ns = ch // sub
    dk = HGRN_DK

    @pl.when(j == 0)
    def _():
        st_ref[...] = jnp.zeros_like(st_ref)

    rr = lax.broadcasted_iota(jnp.int32, (ch, ch), 0)
    cc = lax.broadcasted_iota(jnp.int32, (ch, ch), 1)
    causal = (cc >= rr) if reverse else (cc <= rr)
    tri = jnp.where(causal, 1.0, 0.0).astype(BF16)
    hi, mid, lo = _split3(g_ref[...])
    gcum_ref[...] = LOG2E * (jnp.dot(tri, hi, preferred_element_type=F32)
                             + jnp.dot(tri, mid, preferred_element_type=F32)
                             + jnp.dot(tri, lo, preferred_element_type=F32))
    key_blocks = list(range(1, ns)) if reverse else list(range(ns - 1))
    nkb = len(key_blocks)
    kbd_mask = (lax.broadcasted_iota(jnp.int32, (ch, nkb * dk), 1) // dk + key_blocks[0]
                == lax.broadcasted_iota(jnp.int32, (ch, nkb * dk), 0) // sub)
    kdg_mask = (lax.broadcasted_iota(jnp.int32, (ch, sub * dk), 1) // dk
                == lax.broadcasted_iota(jnp.int32, (ch, sub * dk), 0) % sub)
    end_row = 0 if reverse else ch - 1
    edge = 0 if reverse else sub - 1
    nt = (((1,), (1,)), ((), ()))
    tn = (((0,), (0,)), ((), ()))

    grp = HGRN_GROUP
    rr_g = lax.broadcasted_iota(jnp.int32, (ch, grp * ch), 0)
    cc_g = lax.broadcasted_iota(jnp.int32, (ch, grp * ch), 1) % ch
    diag_mask_g = ((rr_g // sub) == (cc_g // sub)) & ((cc_g >= rr_g) if reverse else (cc_g <= rr_g))

    def place(x, i):
        w = x.shape[1]
        parts = []
        if i:
            parts.append(jnp.zeros((x.shape[0], i * w), x.dtype))
        parts.append(x)
        if grp - 1 - i:
            parts.append(jnp.zeros((x.shape[0], (grp - 1 - i) * w), x.dtype))
        return jnp.concatenate(parts, axis=1)

    for g0 in range(0, HGRN_HEADS, grp):
        qgs, q_cats, p_cats, kbd_rows, kdg_rows, st_rows, v_rows = [], [], [], [], [], [], []
        for i in range(grp):
            h = g0 + i
            hs = slice(h * dk, (h + 1) * dk)
            q = q_ref[:, hs].astype(F32)
            kb = k_ref[:, hs]
            k = kb.astype(F32)
            v = v_ref[:, hs]
            gc = gcum_ref[:, hs]
            st = st_ref[h]
            g_end = gc[end_row:end_row + 1, :]
            qgs.append((q * jnp.exp2(gc)).astype(BF16))
            st_rows.append(place(st.astype(BF16), i))
            v_rows.append(place(v, i))
            k_end = (k * jnp.exp2(g_end - gc)).astype(BF16)
            st_ref[h] = st * jnp.exp2(g_end) + lax.dot_general(v, k_end, tn, preferred_element_type=F32)
            gc4 = gc.reshape(ns, sub, dk)
            g_edge = jnp.broadcast_to(gc4[:, edge:edge + 1, :], (ns, sub, dk)).reshape(ch, dk)
            k_rel = (k * jnp.exp2(g_edge - gc)).astype(BF16)
            k_bd = jnp.where(kbd_mask, jnp.concatenate([k_rel] * nkb, axis=1), jnp.zeros((), BF16))
            kbd_rows.append(place(k_bd, i))
            q_parts = []
            for jb in key_blocks:
                row = jb * sub + edge
                rows = slice(0, jb * sub) if reverse else slice((jb + 1) * sub, ch)
                part = q[rows] * jnp.exp2(gc[rows] - gc[row:row + 1, :])
                pad = jnp.zeros((ch - part.shape[0], dk), F32)
                full = jnp.concatenate([part, pad] if reverse else [pad, part], axis=0)
                q_parts.append(full.astype(BF16))
            q_cats.append(jnp.concatenate(q_parts, axis=1))
            p_parts = []
            for s in range(sub):
                g_s = jnp.broadcast_to(gc4[:, s:s + 1, :], (ns, sub, dk)).reshape(ch, dk)
                p_parts.append((q * jnp.exp2(jnp.minimum(gc - g_s, 0.0))).astype(BF16))
            p_cats.append(jnp.concatenate(p_parts, axis=1))
            k_dg = jnp.where(kdg_mask, jnp.concatenate([kb] * sub, axis=1), jnp.zeros((), BF16))
            kdg_rows.append(place(k_dg, i))
        s_off = lax.dot_general(jnp.concatenate(q_cats, axis=1), jnp.concatenate(kbd_rows, axis=0), nt,
                                preferred_element_type=F32)
        s_diag = lax.dot_general(jnp.concatenate(p_cats, axis=1), jnp.concatenate(kdg_rows, axis=0), nt,
                                 preferred_element_type=F32)
        a = s_off + jnp.where(diag_mask_g, s_diag, 0.0)
        o = lax.dot_general(jnp.concatenate(qgs, axis=1), jnp.concatenate(st_rows, axis=0), nt,
                            preferred_element_type=F32)
        o = o + jnp.dot(a.astype(BF16), jnp.concatenate(v_rows, axis=0), preferred_element_type=F32)
        o_ref[:, g0 * dk:(g0 + grp) * dk] = o


def _scan_call(q, k, lg, v, c_len, reverse):
    t, d = q.shape
    ch = HGRN_CHUNK
    n_ch = t // ch
    cc = c_len // ch
    if reverse:
        idx = lambda j: (jnp.where(j < cc, cc - 1 - j, n_ch - 1 - (j - cc)), 0)
    else:
        idx = lambda j: (j, 0)
    blk = pl.BlockSpec((ch, d), idx)
    return pl.pallas_call(
        functools.partial(_scan_kernel, reverse=reverse),
        out_shape=jax.ShapeDtypeStruct((t, d), F32),
        grid=(n_ch,),
        in_specs=[blk, blk, blk, blk],
        out_specs=blk,
        scratch_shapes=[pltpu.VMEM((HGRN_HEADS, HGRN_DK, HGRN_DK), F32), pltpu.VMEM((ch, d), F32)],
        compiler_params=_cparams(("arbitrary",)),
        name="hgrn_scan_bwd" if reverse else "hgrn_scan_fwd",
    )(q, k, lg, v)


def _rope_tables(l, c_len):
    n = HEAD_DIM // 4
    inv_freq = ROPE_BASE ** (-jnp.arange(n, dtype=F32) / n)
    rows = l // GRID_W
    ang_row = jnp.arange(rows, dtype=F32)[:, None] * inv_freq[None, :]
    ang_col = jnp.arange(GRID_W, dtype=F32)[:, None] * inv_freq[None, :]
    cr, sr, cc, sc = jnp.cos(ang_row), jnp.sin(ang_row), jnp.cos(ang_col), jnp.sin(ang_col)
    zr, zc = jnp.zeros_like(cr), jnp.zeros_like(cc)
    cos_r = jnp.concatenate([cr, cr, zr, zr] * 2, axis=1)
    sin_r = jnp.concatenate([-sr, sr, zr, zr] * 2, axis=1)
    cos_c = jnp.concatenate([zc, zc, cc, cc] * 2, axis=1)
    sin_c = jnp.concatenate([zc, zc, -sc, sc] * 2, axis=1)
    cos = (cos_r[:, None, :] + cos_c[None, :, :]).reshape(l, LANES)
    sin = (sin_r[:, None, :] + sin_c[None, :, :]).reshape(l, LANES)
    cos = jnp.concatenate([jnp.ones((c_len, LANES), F32), cos], axis=0)
    sin = jnp.concatenate([jnp.zeros((c_len, LANES), F32), sin], axis=0)
    return cos, sin


def kernel(x, c, ctx, c_ctx, ada_w, ada_b, norm_g, attn_w_qkv, attn_b_qkv, attn_sink, attn_w_o,
           hgrn_w_in, hgrn_lb, hgrn_norm_g, hgrn_w_o, router_w, router_b, moe_w_gu, moe_b_gu,
           moe_w_dn, moe_b_dn):
    b, l, d = x.shape
    c_len = ctx.shape[1]
    depth = ada_w.shape[0]
    assert b == 1 and d == D_MODEL and depth == 2
    assert c_len % ROW_BLOCK == 0 and l % ROW_BLOCK == 0
    cb = c_len // ROW_BLOCK

    cs = jnp.zeros((8, d), F32).at[0].set(c_ctx).at[1].set(c[0])
    mods = _ada_call(cs, ada_w, ada_b)
    mods = mods[:, :2].reshape(depth, 2, 6, d)

    cos, sin = _rope_tables(l, c_len)

    q, k4, v4 = _qkv_call(ctx[0], x[0], mods[0], norm_g[0, 0:1], attn_w_qkv[0].astype(BF16),
                          attn_b_qkv[0].reshape(1, -1), cos, sin, cb)
    o = _attn_call(attn_sink[0], q, k4, v4, c_len)
    xs, h2, logits = _out_call("attn", (o,), (ctx[0], x[0]), mods[0], norm_g[0], attn_w_o[0].astype(BF16),
                               router_w[0], router_b[0].reshape(1, -1), None, cb, 0)
    xs = _moe(h2, logits, xs, mods[0], norm_g[0], 0, moe_w_gu, moe_b_gu, moe_w_dn, moe_b_dn, c_len)

    lb_soft = jax.nn.softmax(hgrn_lb.astype(F32), axis=0)
    lb = jnp.cumsum(lb_soft, axis=0)[1] - lb_soft[0]
    qh, kf, lf, kb, lbw, iv, gs = _hgrn_in_call(xs, mods[1], norm_g[1, 0:1], hgrn_w_in[0].astype(BF16),
                                                lb, cb)
    o_f = _scan_call(qh, kf, lf, iv, c_len, reverse=False)
    o_b = _scan_call(qh, kb, lbw, iv, c_len, reverse=True)
    x_lat, h2, logits = _out_call("hgrn", (o_f, o_b, gs), xs, mods[1], norm_g[1],
                                  hgrn_w_o[0].astype(BF16), router_w[1], router_b[1].reshape(1, -1),
                                  hgrn_norm_g[0].reshape(1, -1), cb, cb)
    out = _moe(h2, logits, x_lat, mods[1], norm_g[1], 1, moe_w_gu, moe_b_gu, moe_w_dn, moe_b_dn, 0)
    return out[None]
```

```python
import functools

import jax
import jax.numpy as jnp
from jax import lax
from jax.experimental import pallas as pl
from jax.experimental.pallas import tpu as pltpu

D_MODEL = 1024
GRID_W = 64
RMS_EPS = 1e-6

ATTN_HEADS = 16
ATTN_KV_HEADS = 2
HEAD_DIM = 64
Q_DIM = ATTN_HEADS * HEAD_DIM
KV_DIM = ATTN_KV_HEADS * HEAD_DIM
QKV_DIM = Q_DIM + 2 * KV_DIM
WINDOW = 128
ATTN_BLOCK = 128
ROPE_BASE = 10000.0

HGRN_HEADS = 8
HGRN_DK = 128
HGRN_CHUNK = 64
HGRN_SUB = 8
HGRN_GROUP = 1

N_EXPERTS = 32
TOP_K = 4
D_FF = 1024
SWIGLU_LIMIT = 7.0
SWIGLU_ALPHA = 1.702

LANES = 128
SUBLANES = 8
ROW_BLOCK = 256
ADA_TILE = 1024
MOE_ROWS = 256
COMBINE_ROWS = 256
VMEM_LIMIT = 56 * 1024 * 1024

F32 = jnp.float32
BF16 = jnp.bfloat16
NEG_BIG = -1e30
LOG2E = 1.4426950408889634


def _cparams(sem, row_dma=False):
    return pltpu.CompilerParams(dimension_semantics=sem, vmem_limit_bytes=VMEM_LIMIT,
                                disable_bounds_checks=row_dma)


def _rms(x, g):
    return x * lax.rsqrt(jnp.mean(x * x, axis=-1, keepdims=True) + RMS_EPS) * g


def _sigmoid(x):
    return 1.0 / (1.0 + jnp.exp(-x))


def _silu(x):
    return x * (0.5 * jnp.tanh(0.5 * x) + 0.5)


SEGS = D_MODEL // LANES


def _store_tiled(ref, val):
    n = val.shape[0]
    for s in range(SEGS):
        ref[pl.ds(s, n, stride=SEGS), :] = val[:, s * LANES:(s + 1) * LANES]


def _load_tiled(ref, n):
    return jnp.concatenate([ref[pl.ds(s, n, stride=SEGS), :] for s in range(SEGS)], axis=1)


def _row_tile(ref, r):
    return ref.at[pl.ds(pl.multiple_of(r * SEGS, SEGS), SEGS), :]


def _ada_kernel(c_ref, w_ref, b_ref, o_ref):
    c = c_ref[...]
    s = c * _sigmoid(c)
    o_ref[0] = jnp.dot(s, w_ref[0], precision=lax.Precision.HIGHEST,
                       preferred_element_type=F32) + b_ref[0]


def _ada_call(cs, ada_w, ada_b):
    depth, d, n = ada_w.shape
    tn = ADA_TILE
    return pl.pallas_call(
        _ada_kernel,
        out_shape=jax.ShapeDtypeStruct((depth, SUBLANES, n), F32),
        grid=(depth, n // tn),
        in_specs=[
            pl.BlockSpec((SUBLANES, d), lambda i, j: (0, 0)),
            pl.BlockSpec((1, d, tn), lambda i, j: (i, 0, j)),
            pl.BlockSpec((1, 1, tn), lambda i, j: (i, 0, j)),
        ],
        out_specs=pl.BlockSpec((1, SUBLANES, tn), lambda i, j: (i, 0, j)),
        compiler_params=_cparams(("arbitrary", "arbitrary")),
        name="adaln",
    )(cs, ada_w, ada_b.reshape(depth, 1, n))


def _swap16(t):
    lane = lax.broadcasted_iota(jnp.int32, t.shape, 1)
    return jnp.where(lane % 32 < 16, pltpu.roll(t, LANES - 16, 1), pltpu.roll(t, 16, 1))


def _stream_rows(c_ref, x_ref, cb):
    return jnp.where(pl.program_id(0) < cb, c_ref[...], x_ref[...])


def _stream_specs(tm, d, cb):
    return [pl.BlockSpec((tm, d), lambda i: (jnp.minimum(i, cb - 1), 0)),
            pl.BlockSpec((tm, d), lambda i: (jnp.maximum(i - cb, 0), 0))]


def _qkv_kernel(c_ref, x_ref, mod_ref, gn_ref, w_ref, b_ref, cos_ref, sin_ref, q_ref, k4_ref, v4_ref, *, cb):
    x = _stream_rows(c_ref, x_ref, cb)
    h = _rms(x, gn_ref[...]) * (1.0 + mod_ref[0, 1:2, :]) + mod_ref[0, 0:1, :]
    hb = h.astype(BF16)
    cos = cos_ref[...]
    sin = sin_ref[...]
    nq = Q_DIM // LANES
    for j in range(nq + 1):
        sl = slice(j * LANES, (j + 1) * LANES)
        t = jnp.dot(hb, w_ref[:, sl], preferred_element_type=F32) + b_ref[:, sl]
        t = t * cos + _swap16(t) * sin
        if j < nq:
            q_ref[:, sl] = (t * (HEAD_DIM ** -0.5 * LOG2E)).astype(q_ref.dtype)
        else:
            kt = t
    sl = slice(Q_DIM + KV_DIM, QKV_DIM)
    vt = jnp.dot(hb, w_ref[:, sl], preferred_element_type=F32) + b_ref[:, sl]
    lo = lax.broadcasted_iota(jnp.int32, kt.shape, 1) < HEAD_DIM
    for t, ref in ((kt, k4_ref), (vt, v4_ref)):
        sw = pltpu.roll(t, HEAD_DIM, 1)
        ref[:, 0 * LANES:1 * LANES] = jnp.where(lo, t, 0.0).astype(ref.dtype)
        ref[:, 1 * LANES:2 * LANES] = jnp.where(lo, 0.0, sw).astype(ref.dtype)
        ref[:, 2 * LANES:3 * LANES] = jnp.where(lo, sw, 0.0).astype(ref.dtype)
        ref[:, 3 * LANES:4 * LANES] = jnp.where(lo, 0.0, t).astype(ref.dtype)


def _qkv_call(ctx, x, mod, gn, w, b, cos, sin, cb):
    d = x.shape[1]
    t = ctx.shape[0] + x.shape[0]
    tm = ROW_BLOCK
    sel = lambda i: (jnp.minimum(i // cb, 1), 0, 0)
    return pl.pallas_call(
        functools.partial(_qkv_kernel, cb=cb),
        out_shape=(jax.ShapeDtypeStruct((t, Q_DIM), BF16),
                   jax.ShapeDtypeStruct((t, 4 * LANES), BF16),
                   jax.ShapeDtypeStruct((t, 4 * LANES), BF16)),
        grid=(t // tm,),
        in_specs=_stream_specs(tm, d, cb) + [
            pl.BlockSpec((1, 6, d), sel),
            pl.BlockSpec((1, d), lambda i: (0, 0)),
            pl.BlockSpec((d, QKV_DIM), lambda i: (0, 0)),
            pl.BlockSpec((1, QKV_DIM), lambda i: (0, 0)),
            pl.BlockSpec((tm, LANES), lambda i: (i, 0)),
            pl.BlockSpec((tm, LANES), lambda i: (i, 0)),
        ],
        out_specs=(pl.BlockSpec((tm, Q_DIM), lambda i: (i, 0)),
                   pl.BlockSpec((tm, 4 * LANES), lambda i: (i, 0)),
                   pl.BlockSpec((tm, 4 * LANES), lambda i: (i, 0))),
        compiler_params=_cparams(("parallel",)),
        name="qkv_rope",
    )(ctx, x, mod, gn, w, b, cos, sin)


def _attn_kernel(sink_ref, q_ref, kp_ref, kc_ref, kn_ref, kx_ref, vp_ref, vc_ref, vn_ref, vx_ref,
                 o_ref, *, cb, n_lat):
    n = pl.program_id(0)
    blk = ATTN_BLOCK
    c = kx_ref.shape[0]
    nw = 3 * blk
    s = lax.broadcasted_iota(jnp.int32, (nw + c, blk), 0)
    r = lax.broadcasted_iota(jnp.int32, (nw + c, blk), 1)
    q_pos = (n - cb) * blk + r
    k_pos = (n - cb - 1) * blk + s
    win_ok = (jnp.abs(q_pos - k_pos) <= WINDOW) & (k_pos >= 0) & (k_pos < n_lat) & (n >= cb)
    valid = (s >= nw) | win_ok
    k_all = jnp.concatenate([kp_ref[...], kc_ref[...], kn_ref[...], kx_ref[...]], axis=0)
    v_all = jnp.concatenate([vp_ref[...], vc_ref[...], vn_ref[...], vx_ref[...]], axis=0)
    n_var = 2 * ATTN_KV_HEADS
    v_t = [v_all[:, j * LANES:(j + 1) * LANES].astype(F32).T.astype(BF16) for j in range(n_var)]
    nt = (((1,), (1,)), ((), ()))
    group = ATTN_HEADS // ATTN_KV_HEADS
    for p in range(ATTN_HEADS // 2):
        g = (2 * p) // group
        qp = q_ref[:, p * LANES:(p + 1) * LANES]
        o_t = jnp.zeros((LANES, blk), F32)
        for half in range(2):
            j = 2 * g + half
            sc = lax.dot_general(k_all[:, j * LANES:(j + 1) * LANES], qp, nt,
                                 preferred_element_type=F32)
            sc = jnp.where(valid, sc, NEG_BIG)
            sink = sink_ref[2 * p + half] * LOG2E
            m = jnp.maximum(jnp.max(sc, axis=0, keepdims=True), sink)
            e = jnp.exp2(sc - m)
            denom = jnp.sum(e, axis=0, keepdims=True) + jnp.exp2(sink - m)
            pv = jnp.dot(v_t[j], e.astype(BF16), preferred_element_type=F32)
            o_t = o_t + pv / denom
        o_ref[:, p * LANES:(p + 1) * LANES] = o_t.T.astype(o_ref.dtype)


def _attn_call(sink, q, k4, v4, c_len):
    t = q.shape[0]
    blk = ATTN_BLOCK
    cb = c_len // blk
    nb = t // blk
    n_lat = t - c_len
    last = nb - 1
    kw = 4 * LANES
    spec_q = pl.BlockSpec((blk, Q_DIM), lambda n: (n, 0))
    prev = pl.BlockSpec((blk, kw), lambda n: (jnp.maximum(n - 1, 0), 0))
    cur = pl.BlockSpec((blk, kw), lambda n: (n, 0))
    nxt = pl.BlockSpec((blk, kw), lambda n: (jnp.minimum(n + 1, last), 0))
    ctx = pl.BlockSpec((c_len, kw), lambda n: (0, 0))
    return pl.pallas_call(
        functools.partial(_attn_kernel, cb=cb, n_lat=n_lat),
        out_shape=jax.ShapeDtypeStruct((t, Q_DIM), BF16),
        grid=(nb,),
        in_specs=[pl.BlockSpec(memory_space=pltpu.SMEM), spec_q,
                  prev, cur, nxt, ctx, prev, cur, nxt, ctx],
        out_specs=pl.BlockSpec((blk, Q_DIM), lambda n: (n, 0)),
        compiler_params=_cparams(("parallel",)),
        name="window_attn",
    )(sink, q, k4, k4, k4, k4, v4, v4, v4, v4)


def _residual_tail(y, x, mod_ref, gn_ref, rw_ref, rb_ref, xo_ref, h2_ref, lg_ref):
    x_new = x + mod_ref[0, 2:3, :] * _rms(y, gn_ref[1:2, :])
    h2 = _rms(x_new, gn_ref[2:3, :]) * (1.0 + mod_ref[0, 4:5, :]) + mod_ref[0, 3:4, :]
    xo_ref[...] = x_new
    _store_tiled(h2_ref, h2)
    hi, mid, lo = _split3(h2)
    lane = lax.broadcasted_iota(jnp.int32, (h2.shape[0], LANES), 1)
    w = rw_ref[...]
    r = (jnp.dot(hi, w, preferred_element_type=F32)
         + jnp.where(lane < 2 * N_EXPERTS, jnp.dot(mid, w, preferred_element_type=F32), 0.0)
         + jnp.where(lane < N_EXPERTS, jnp.dot(lo, w, preferred_element_type=F32), 0.0))
    r = r + pltpu.roll(r, LANES - N_EXPERTS, 1) + pltpu.roll(r, LANES - 2 * N_EXPERTS, 1)
    lg_ref[...] = r[:, :N_EXPERTS] + rb_ref[...]


def _attn_out_kernel(a_ref, c_ref, x_ref, mod_ref, gn_ref, w_ref, rw_ref, rb_ref, xo_ref, h2_ref, lg_ref,
                     *, cb):
    y = jnp.dot(a_ref[...], w_ref[...], preferred_element_type=F32)
    _residual_tail(y, _stream_rows(c_ref, x_ref, cb), mod_ref, gn_ref, rw_ref, rb_ref, xo_ref, h2_ref, lg_ref)


def _hgrn_out_kernel(of_ref, ob_ref, gs_ref, hg_ref, x_ref, mod_ref, gn_ref, w_ref, rw_ref, rb_ref,
                     xo_ref, h2_ref, lg_ref):
    parts = []
    for h in range(HGRN_HEADS):
        sl = slice(h * HGRN_DK, (h + 1) * HGRN_DK)
        o = of_ref[:, sl] + ob_ref[:, sl]
        parts.append(_rms(o, hg_ref[:, sl]))
    a = jnp.concatenate(parts, axis=1) * gs_ref[...].astype(F32)
    y = jnp.dot(a.astype(BF16), w_ref[...], preferred_element_type=F32)
    _residual_tail(y, x_ref[...], mod_ref, gn_ref, rw_ref, rb_ref, xo_ref, h2_ref, lg_ref)


def _out_call(kind, acts, x, mod, gn, w, rw, rb, hg, cb, off):
    t_out = (x[0].shape[0] + x[1].shape[0]) if kind == "attn" else x.shape[0] - off * ROW_BLOCK
    d = D_MODEL
    tm = ROW_BLOCK
    w_hi, w_mid, w_lo = _split3(rw)
    rw = jnp.concatenate([w_hi, w_mid, w_lo, jnp.zeros_like(w_hi)], axis=1)
    row = lambda i: (i + off, 0)
    sel = lambda i: (jnp.minimum((i + off) // cb, 1), 0, 0)
    const = lambda i: (0, 0)
    common_specs = [
        pl.BlockSpec((1, 6, d), sel),
        pl.BlockSpec((4, d), const),
        pl.BlockSpec((d, d), const),
        pl.BlockSpec((d, LANES), const),
        pl.BlockSpec((1, N_EXPERTS), const),
    ]
    if kind == "attn":
        body = functools.partial(_attn_out_kernel, cb=cb)
        in_specs = [pl.BlockSpec((tm, d), row)] + _stream_specs(tm, d, cb) + common_specs
        args = (acts[0], x[0], x[1], mod, gn, w, rw, rb)
    else:
        body = _hgrn_out_kernel
        in_specs = ([pl.BlockSpec((tm, d), row)] * 3 + [pl.BlockSpec((1, d), const)]
                    + [pl.BlockSpec((tm, d), row)] + common_specs)
        args = (acts[0], acts[1], acts[2], hg, x, mod, gn, w, rw, rb)
    return pl.pallas_call(
        body,
        out_shape=(jax.ShapeDtypeStruct((t_out, d), F32),
                   jax.ShapeDtypeStruct((t_out * SEGS, LANES), F32),
                   jax.ShapeDtypeStruct((t_out, N_EXPERTS), F32)),
        grid=(t_out // tm,),
        in_specs=in_specs,
        out_specs=(pl.BlockSpec((tm, d), lambda i: (i, 0)),
                   pl.BlockSpec((tm * SEGS, LANES), lambda i: (i, 0)),
                   pl.BlockSpec((tm, N_EXPERTS), lambda i: (i, 0))),
        compiler_params=_cparams(("parallel",)),
        name=kind + "_out_residual",
    )(*args)


def _router_kernel(lg_ref, slab_ref, cnt_ref, run_ref):
    i = pl.program_id(0)

    @pl.when(i == 0)
    def _():
        run_ref[...] = jnp.zeros_like(run_ref)

    lg = lg_ref[...]
    tb = lg.shape[0]
    lane = lax.broadcasted_iota(jnp.int32, lg.shape, 1).astype(F32)
    sels, tops, idxs = [], [], []
    for _ in range(TOP_K):
        m = jnp.max(lg, axis=-1, keepdims=True)
        idx = jnp.min(jnp.where(lg == m, lane, float(N_EXPERTS)), axis=-1, keepdims=True)
        sel = lane == idx
        sels.append(sel)
        tops.append(m)
        idxs.append(idx)
        lg = jnp.where(sel, -jnp.inf, lg)
    ws = [jnp.exp(tk - tops[0]) for tk in tops]
    wsum = ws[0] + ws[1] + ws[2] + ws[3]
    chosen = (sels[0] | sels[1] | sels[2] | sels[3])
    onehot = jnp.where(chosen, 1.0, 0.0)
    rr = lax.broadcasted_iota(jnp.int32, (tb, tb), 0)
    cc = lax.broadcasted_iota(jnp.int32, (tb, tb), 1)
    tri = jnp.where(cc < rr, 1.0, 0.0).astype(BF16)
    before = jnp.dot(tri, onehot.astype(BF16), preferred_element_type=F32) + run_ref[...]
    out_lane = lax.broadcasted_iota(jnp.int32, (tb, LANES), 1)
    slab = jnp.zeros((tb, LANES), F32)
    for k in range(TOP_K):
        rank = jnp.sum(jnp.where(sels[k], before, 0.0), axis=-1, keepdims=True)
        slab = jnp.where(out_lane == k, idxs[k], slab)
        slab = jnp.where(out_lane == TOP_K + k, rank, slab)
        slab = jnp.where(out_lane == 2 * TOP_K + k, ws[k] / wsum, slab)
    slab_ref[...] = slab
    run_ref[...] = run_ref[...] + jnp.sum(onehot, axis=0, keepdims=True)
    cnt_ref[...] = run_ref[...]


def _router_call(logits):
    t = logits.shape[0]
    tb = ROW_BLOCK
    return pl.pallas_call(
        _router_kernel,
        out_shape=(jax.ShapeDtypeStruct((t, LANES), F32),
                   jax.ShapeDtypeStruct((1, N_EXPERTS), F32)),
        grid=(t // tb,),
        in_specs=[pl.BlockSpec((tb, N_EXPERTS), lambda i: (i, 0))],
        out_specs=(pl.BlockSpec((tb, LANES), lambda i: (i, 0)),
                   pl.BlockSpec((1, N_EXPERTS), lambda i: (0, 0))),
        scratch_shapes=[pltpu.VMEM((1, N_EXPERTS), F32)],
        compiler_params=_cparams(("arbitrary",)),
        name="router_topk",
    )(logits)


def _expert_kernel(be_ref, nu_ref, nx_ref, ws_ref, src0_ref, src1_ref, x_hbm, wgu_hbm, bgu_ref, wdn_hbm, bdn_ref,
                   o_ref, xbuf, sems, wgu_f, wdn_f, wsems, wgu_bf, wdn_bf, *, layer):
    i = pl.program_id(0)
    rows = o_ref.shape[0] // SEGS
    slot = i % 2
    n_used = nu_ref[0]

    def gather(idx_ref, s):
        def issue(r2, carry):
            for p in range(2):
                r = 2 * r2 + p
                pltpu.make_async_copy(_row_tile(x_hbm, idx_ref[0, 0, r]), _row_tile(xbuf.at[s], r),
                                      sems.at[s]).start()
            return carry

        lax.fori_loop(0, rows // 2, issue, 0, unroll=4)

    @pl.when(i == 0)
    def _():
        gather(src0_ref, 0)

    @pl.when(i + 1 < n_used)
    def _():
        gather(src1_ref, 1 - slot)

    prev = be_ref[jnp.maximum(i - 1, 0)]
    fresh = (i == 0) | (be_ref[i] != prev)

    e_cur = be_ref[i]
    wslot = ws_ref[i]

    def weight_copies(e, s):
        return (pltpu.make_async_copy(wgu_hbm.at[layer, e], wgu_f.at[s], wsems.at[s, 0]),
                pltpu.make_async_copy(wdn_hbm.at[layer, e], wdn_f.at[s], wsems.at[s, 1]))

    @pl.when(i == 0)
    def _():
        for c in weight_copies(e_cur, wslot):
            c.start(priority=1)

    @pl.when(fresh)
    def _():
        @pl.when(nx_ref[i] != e_cur)
        def _():
            for c in weight_copies(nx_ref[i], 1 - wslot):
                c.start(priority=1)

        for c in weight_copies(e_cur, wslot):
            c.wait()
        wgu_bf[...] = wgu_f[wslot].astype(BF16)
        wdn_bf[...] = wdn_f[wslot].astype(BF16)

    @pl.when(i < n_used)
    def _():
        pltpu.make_async_copy(x_hbm.at[pl.ds(0, rows * SEGS), :], xbuf.at[slot], sems.at[slot]).wait()
        xb = _load_tiled(xbuf.at[slot], rows).astype(BF16)
        gu = jnp.dot(xb, wgu_bf[...], preferred_element_type=F32) + bgu_ref[...]
        g = jnp.minimum(gu[:, :D_FF], SWIGLU_LIMIT)
        u = jnp.clip(gu[:, D_FF:], -SWIGLU_LIMIT, SWIGLU_LIMIT)
        hdn = g * _sigmoid(SWIGLU_ALPHA * g) * (u + 1.0)
        y = jnp.dot(hdn.astype(BF16), wdn_bf[...], preferred_element_type=F32) + bdn_ref[...]
        _store_tiled(o_ref, y)

    @pl.when(i >= n_used)
    def _():
        o_ref[...] = jnp.zeros_like(o_ref)


def _expert_call(blk_e, n_used, src_tok, h2, layer, w_gu, b_gu, w_dn, b_dn):
    d = D_MODEL
    tr = MOE_ROWS
    n_blk = blk_e.shape[0]
    depth, e, _, f2 = w_gu.shape
    last = n_blk - 1
    nxt = jnp.min(jnp.where(blk_e[None, :] > blk_e[:, None], blk_e[None, :], N_EXPERTS), axis=1)
    nxt = jnp.where(nxt == N_EXPERTS, blk_e, nxt).astype(jnp.int32)
    change = jnp.concatenate([jnp.zeros((1,), jnp.int32), (blk_e[1:] != blk_e[:-1]).astype(jnp.int32)])
    wslot = (jnp.cumsum(change) % 2).astype(jnp.int32)
    return pl.pallas_call(
        functools.partial(_expert_kernel, layer=layer),
        out_shape=jax.ShapeDtypeStruct((n_blk * tr * SEGS, LANES), F32),
        grid_spec=pltpu.PrefetchScalarGridSpec(
            num_scalar_prefetch=4,
            grid=(n_blk,),
            in_specs=[
                pl.BlockSpec((1, 1, tr), lambda i, *_: (0, 0, 0), memory_space=pltpu.SMEM),
                pl.BlockSpec((1, 1, tr), lambda i, *_: (jnp.minimum(i + 1, last), 0, 0),
                             memory_space=pltpu.SMEM),
                pl.BlockSpec(memory_space=pl.ANY),
                pl.BlockSpec(memory_space=pl.ANY),
                pl.BlockSpec((None, None, 1, f2), lambda i, be, *_: (layer, be[i], 0, 0)),
                pl.BlockSpec(memory_space=pl.ANY),
                pl.BlockSpec((None, None, 1, d), lambda i, be, *_: (layer, be[i], 0, 0)),
            ],
            out_specs=pl.BlockSpec((tr * SEGS, LANES), lambda i, *_: (i, 0)),
            scratch_shapes=[pltpu.VMEM((2, tr * SEGS, LANES), F32), pltpu.SemaphoreType.DMA((2,)),
                            pltpu.VMEM((2, d, f2), F32), pltpu.VMEM((2, D_FF, d), F32),
                            pltpu.SemaphoreType.DMA((2, 2)),
                            pltpu.VMEM((d, f2), BF16), pltpu.VMEM((D_FF, d), BF16)],
        ),
        compiler_params=_cparams(("arbitrary",), row_dma=True),
        name="moe_experts",
    )(blk_e, n_used, nxt, wslot, src_tok.reshape(n_blk, 1, tr), src_tok.reshape(n_blk, 1, tr), h2,
      w_gu, b_gu.reshape(depth, e, 1, f2), w_dn, b_dn.reshape(depth, e, 1, d))


def _combine_kernel(d0_ref, d1_ref, y_hbm, slab_ref, x_ref, mod_ref, gn_ref, o_ref, buf, sems):
    i = pl.program_id(0)
    tb = o_ref.shape[0]
    slot = i % 2

    def gather(idx_ref, s):
        def issue(r, carry):
            for k in range(TOP_K):
                row = idx_ref[0, 0, k * tb + r]
                pltpu.make_async_copy(_row_tile(y_hbm, row), _row_tile(buf.at[s, k], r),
                                      sems.at[s]).start(priority=k % 2)
            return carry

        lax.fori_loop(0, tb, issue, 0, unroll=4)

    @pl.when(i == 0)
    def _():
        gather(d0_ref, 0)

    @pl.when(i + 1 < pl.num_programs(0))
    def _():
        gather(d1_ref, 1 - slot)

    for k in range(TOP_K):
        pltpu.make_async_copy(y_hbm.at[pl.ds(0, tb * SEGS), :], buf.at[slot, k], sems.at[slot]).wait()
    slab = slab_ref[...]
    y2 = jnp.zeros(o_ref.shape, F32)
    for k in range(TOP_K):
        y2 = y2 + _load_tiled(buf.at[slot, k], tb) * slab[:, 2 * TOP_K + k:2 * TOP_K + k + 1]
    o_ref[...] = x_ref[...] + mod_ref[0, 5:6, :] * _rms(y2, gn_ref[3:4, :])


def _combine_call(dest, y_sorted, slab, x, mod, gn, cb_rows):
    t, d = x.shape
    tb = COMBINE_ROWS
    nb = t // tb
    cbb = max(cb_rows // tb, 1)
    sel = (lambda i: (jnp.minimum(i // cbb, 1), 0, 0)) if cb_rows else (lambda i: (1, 0, 0))
    dest3 = dest.reshape(TOP_K, nb, tb).transpose(1, 0, 2).reshape(nb, 1, TOP_K * tb)
    return pl.pallas_call(
        _combine_kernel,
        out_shape=jax.ShapeDtypeStruct((t, d), F32),
        grid=(nb,),
        in_specs=[
            pl.BlockSpec((1, 1, tb * TOP_K), lambda i: (0, 0, 0), memory_space=pltpu.SMEM),
            pl.BlockSpec((1, 1, tb * TOP_K), lambda i: (jnp.minimum(i + 1, nb - 1), 0, 0),
                         memory_space=pltpu.SMEM),
            pl.BlockSpec(memory_space=pl.ANY),
            pl.BlockSpec((tb, LANES), lambda i: (i, 0)),
            pl.BlockSpec((tb, d), lambda i: (i, 0)),
            pl.BlockSpec((1, 6, d), sel),
            pl.BlockSpec((4, d), lambda i: (0, 0)),
        ],
        out_specs=pl.BlockSpec((tb, d), lambda i: (i, 0)),
        scratch_shapes=[pltpu.VMEM((2, TOP_K, tb * SEGS, LANES), F32), pltpu.SemaphoreType.DMA((2,))],
        compiler_params=_cparams(("arbitrary",), row_dma=True),
        name="moe_combine",
    )(dest3, dest3, y_sorted, slab, x, mod, gn)


INVERT_BLOCK = 2048


def _invert_kernel(dest_ref, zeros_hbm, src_ref):
    k, j = pl.program_id(0), pl.program_id(1)
    n = dest_ref.shape[2]

    @pl.when((k == 0) & (j == 0))
    def _():
        pltpu.sync_copy(zeros_hbm, src_ref)

    tok0 = j * n

    def put(t, carry):
        src_ref[dest_ref[0, 0, t]] = tok0 + t
        return carry

    lax.fori_loop(0, n, put, 0, unroll=16)


def _invert_call(dest, n_rows):
    t = dest.shape[1]
    blk = max(b for b in range(LANES, INVERT_BLOCK + 1, LANES) if t % b == 0)
    nb = t // blk
    return pl.pallas_call(
        _invert_kernel,
        out_shape=jax.ShapeDtypeStruct((n_rows,), jnp.int32),
        grid=(TOP_K, nb),
        in_specs=[pl.BlockSpec((1, 1, blk), lambda k, j: (k * nb + j, 0, 0), memory_space=pltpu.SMEM),
                  pl.BlockSpec(memory_space=pl.ANY)],
        out_specs=pl.BlockSpec(memory_space=pltpu.SMEM),
        compiler_params=_cparams(("arbitrary", "arbitrary")),
        name="moe_invert",
    )(dest.reshape(TOP_K * nb, 1, blk), jnp.zeros((n_rows,), jnp.int32))


def _moe(h2, logits, x, mod, gn, layer, w_gu, b_gu, w_dn, b_dn, cb_rows):
    t = logits.shape[0]
    tr = MOE_ROWS
    slab, counts = _router_call(logits)
    slab_t = slab[:, :2 * TOP_K].T
    e_idx = slab_t[:TOP_K].astype(jnp.int32)
    rank = slab_t[TOP_K:].astype(jnp.int32)
    counts = counts[0].astype(jnp.int32)
    padded = (counts + tr - 1) // tr * tr
    p_end = jnp.cumsum(padded)
    p_start = p_end - padded
    e_flat = e_idx.reshape(-1, LANES)
    dest = rank.reshape(-1, LANES)
    for e in range(N_EXPERTS):
        dest = dest + jnp.where(e_flat == e, p_start[e], 0)
    dest = dest.reshape(TOP_K, t)
    a = t * TOP_K
    n_blk = (a + N_EXPERTS * (tr - 1) + tr - 1) // tr
    src_tok = _invert_call(dest, n_blk * tr)
    blk_start = jnp.arange(n_blk, dtype=jnp.int32) * tr
    blk_e = jnp.sum((p_end[None, :] <= blk_start[:, None]).astype(jnp.int32), axis=1)
    blk_e = jnp.minimum(blk_e, N_EXPERTS - 1)
    n_used = (p_end[-1] // tr).astype(jnp.int32).reshape(1)
    ys = _expert_call(blk_e, n_used, src_tok, h2, layer, w_gu, b_gu, w_dn, b_dn)
    return _combine_call(dest, ys, slab, x, mod, gn, cb_rows)


def _hgrn_in_kernel(x_ref, mod_ref, gn_ref, w_ref, lb_ref, q_ref, kf_ref, lf_ref, kb_ref, lbw_ref,
                    i_ref, gs_ref):
    x = x_ref[...]
    h = _rms(x, gn_ref[...]) * (1.0 + mod_ref[0, 1:2, :]) + mod_ref[0, 0:1, :]
    hb = h.astype(BF16)
    f = D_MODEL

    def proj(sec):
        return jnp.dot(hb, w_ref[:, sec * f:(sec + 1) * f], preferred_element_type=F32)

    z = proj(0)
    q_ref[...] = _silu(z).astype(q_ref.dtype)
    for sec, k_ref, l_ref in ((1, kf_ref, lf_ref), (2, kb_ref, lbw_ref)):
        z = proj(sec)
        lb = lb_ref[sec - 1:sec, :]
        sg = _sigmoid(z)
        l_ref[...] = jnp.log(lb + (1.0 - lb) * sg)
        k_ref[...] = ((1.0 - lb) * (1.0 - sg)).astype(k_ref.dtype)
    i_ref[...] = proj(3).astype(i_ref.dtype)
    z = proj(4)
    gs_ref[...] = _silu(z).astype(gs_ref.dtype)


def _hgrn_in_call(xs, mod, gn, w, lb, cb):
    t, d = xs.shape
    tm = ROW_BLOCK
    sel = lambda i: (jnp.minimum(i // cb, 1), 0, 0)
    row = pl.BlockSpec((tm, d), lambda i: (i, 0))
    dts = (BF16, BF16, F32, BF16, F32, BF16, BF16)
    return pl.pallas_call(
        _hgrn_in_kernel,
        out_shape=tuple(jax.ShapeDtypeStruct((t, d), dt) for dt in dts),
        grid=(t // tm,),
        in_specs=[row, pl.BlockSpec((1, 6, d), sel), pl.BlockSpec((1, d), lambda i: (0, 0)),
                  pl.BlockSpec(w.shape, lambda i: (0, 0)), pl.BlockSpec((2, d), lambda i: (0, 0))],
        out_specs=tuple(row for _ in dts),
        compiler_params=_cparams(("parallel",)),
        name="hgrn_in_proj",
    )(xs, mod, gn, w, lb)


def _split3(x):
    hi = x.astype(BF16)
    r1 = x - hi.astype(F32)
    mid = r1.astype(BF16)
    lo = (r1 - mid.astype(F32)).astype(BF16)
    return hi, mid, lo


def _scan_kernel(q_ref, k_ref, g_ref, v_ref, o_ref, st_ref, gcum_ref, *, reverse):
    j = pl.program_id(0)
    ch, sub = HGRN_CHUNK, HGRN_SUB
    ns = ch // sub
    dk = HGRN_DK

    @pl.when(j == 0)
    def _():
        st_ref[...] = jnp.zeros_like(st_ref)

    rr = lax.broadcasted_iota(jnp.int32, (ch, ch), 0)
    cc = lax.broadcasted_iota(jnp.int32, (ch, ch), 1)
    causal = (cc >= rr) if reverse else (cc <= rr)
    tri = jnp.where(causal, 1.0, 0.0).astype(BF16)
    hi, mid, lo = _split3(g_ref[...])
    gcum_ref[...] = LOG2E * (jnp.dot(tri, hi, preferred_element_type=F32)
                             + jnp.dot(tri, mid, preferred_element_type=F32)
                             + jnp.dot(tri, lo, preferred_element_type=F32))
    key_blocks = list(range(1, ns)) if reverse else list(range(ns - 1))
    nkb = len(key_blocks)
    kbd_mask = (lax.broadcasted_iota(jnp.int32, (ch, nkb * dk), 1) // dk + key_blocks[0]
                == lax.broadcasted_iota(jnp.int32, (ch, nkb * dk), 0) // sub)
    kdg_mask = (lax.broadcasted_iota(jnp.int32, (ch, sub * dk), 1) // dk
                == lax.broadcasted_iota(jnp.int32, (ch, sub * dk), 0) % sub)
    end_row = 0 if reverse else ch - 1
    edge = 0 if reverse else sub - 1
    nt = (((1,), (1,)), ((), ()))
    tn = (((0,), (0,)), ((), ()))

    grp = HGRN_GROUP
    rr_g = lax.broadcasted_iota(jnp.int32, (ch, grp * ch), 0)
    cc_g = lax.broadcasted_iota(jnp.int32, (ch, grp * ch), 1) % ch
    diag_mask_g = ((rr_g // sub) == (cc_g // sub)) & ((cc_g >= rr_g) if reverse else (cc_g <= rr_g))

    def place(x, i):
        w = x.shape[1]
        parts = []
        if i:
            parts.append(jnp.zeros((x.shape[0], i * w), x.dtype))
        parts.append(x)
        if grp - 1 - i:
            parts.append(jnp.zeros((x.shape[0], (grp - 1 - i) * w), x.dtype))
        return jnp.concatenate(parts, axis=1)

    for g0 in range(0, HGRN_HEADS, grp):
        qgs, q_cats, p_cats, kbd_rows, kdg_rows, st_rows, v_rows = [], [], [], [], [], [], []
        for i in range(grp):
            h = g0 + i
            hs = slice(h * dk, (h + 1) * dk)
            q = q_ref[:, hs].astype(F32)
            kb = k_ref[:, hs]
            k = kb.astype(F32)
            v = v_ref[:, hs]
            gc = gcum_ref[:, hs]
            st = st_ref[h]
            g_end = gc[end_row:end_row + 1, :]
            qgs.append((q * jnp.exp2(gc)).astype(BF16))
            st_rows.append(place(st.astype(BF16), i))
            v_rows.append(place(v, i))
            k_end = (k * jnp.exp2(g_end - gc)).astype(BF16)
            st_ref[h] = st * jnp.exp2(g_end) + lax.dot_general(v, k_end, tn, preferred_element_type=F32)
            gc4 = gc.reshape(ns, sub, dk)
            g_edge = jnp.broadcast_to(gc4[:, edge:edge + 1, :], (ns, sub, dk)).reshape(ch, dk)
            k_rel = (k * jnp.exp2(g_edge - gc)).astype(BF16)
            k_bd = jnp.where(kbd_mask, jnp.concatenate([k_rel] * nkb, axis=1), jnp.zeros((), BF16))
            kbd_rows.append(place(k_bd, i))
            q_parts = []
            for jb in key_blocks:
                row = jb * sub + edge
                rows = slice(0, jb * sub) if reverse else slice((jb + 1) * sub, ch)
                part = q[rows] * jnp.exp2(gc[rows] - gc[row:row + 1, :])
                pad = jnp.zeros((ch - part.shape[0], dk), F32)
                full = jnp.concatenate([part, pad] if reverse else [pad, part], axis=0)
                q_parts.append(full.astype(BF16))
            q_cats.append(jnp.concatenate(q_parts, axis=1))
            p_parts = []
            for s in range(sub):
                g_s = jnp.broadcast_to(gc4[:, s:s + 1, :], (ns, sub, dk)).reshape(ch, dk)
                p_parts.append((q * jnp.exp2(jnp.minimum(gc - g_s, 0.0))).astype(BF16))
            p_cats.append(jnp.concatenate(p_parts, axis=1))
            k_dg = jnp.where(kdg_mask, jnp.concatenate([kb] * sub, axis=1), jnp.zeros((), BF16))
            kdg_rows.append(place(k_dg, i))
        s_off = lax.dot_general(jnp.concatenate(q_cats, axis=1), jnp.concatenate(kbd_rows, axis=0), nt,
                                preferred_element_type=F32)
        s_diag = lax.dot_general(jnp.concatenate(p_cats, axis=1), jnp.concatenate(kdg_rows, axis=0), nt,
                                 preferred_element_type=F32)
        a = s_off + jnp.where(diag_mask_g, s_diag, 0.0)
        o = lax.dot_general(jnp.concatenate(qgs, axis=1), jnp.concatenate(st_rows, axis=0), nt,
                            preferred_element_type=F32)
        o = o + jnp.dot(a.astype(BF16), jnp.concatenate(v_rows, axis=0), preferred_element_type=F32)
        o_ref[:, g0 * dk:(g0 + grp) * dk] = o


def _scan_call(q, k, lg, v, c_len, reverse):
    t, d = q.shape
    ch = HGRN_CHUNK
    n_ch = t // ch
    cc = c_len // ch
    if reverse:
        idx = lambda j: (jnp.where(j < cc, cc - 1 - j, n_ch - 1 - (j - cc)), 0)
    else:
        idx = lambda j: (j, 0)
    blk = pl.BlockSpec((ch, d), idx)
    return pl.pallas_call(
        functools.partial(_scan_kernel, reverse=reverse),
        out_shape=jax.ShapeDtypeStruct((t, d), F32),
        grid=(n_ch,),
        in_specs=[blk, blk, blk, blk],
        out_specs=blk,
        scratch_shapes=[pltpu.VMEM((HGRN_HEADS, HGRN_DK, HGRN_DK), F32), pltpu.VMEM((ch, d), F32)],
        compiler_params=_cparams(("arbitrary",)),
        name="hgrn_scan_bwd" if reverse else "hgrn_scan_fwd",
    )(q, k, lg, v)


def _rope_tables(l, c_len):
    n = HEAD_DIM // 4
    inv_freq = ROPE_BASE ** (-jnp.arange(n, dtype=F32) / n)
    rows = l // GRID_W
    ang_row = jnp.arange(rows, dtype=F32)[:, None] * inv_freq[None, :]
    ang_col = jnp.arange(GRID_W, dtype=F32)[:, None] * inv_freq[None, :]
    cr, sr, cc, sc = jnp.cos(ang_row), jnp.sin(ang_row), jnp.cos(ang_col), jnp.sin(ang_col)
    zr, zc = jnp.zeros_like(cr), jnp.zeros_like(cc)
    cos_r = jnp.concatenate([cr, cr, zr, zr] * 2, axis=1)
    sin_r = jnp.concatenate([-sr, sr, zr, zr] * 2, axis=1)
    cos_c = jnp.concatenate([zc, zc, cc, cc] * 2, axis=1)
    sin_c = jnp.concatenate([zc, zc, -sc, sc] * 2, axis=1)
    cos = (cos_r[:, None, :] + cos_c[None, :, :]).reshape(l, LANES)
    sin = (sin_r[:, None, :] + sin_c[None, :, :]).reshape(l, LANES)
    cos = jnp.concatenate([jnp.ones((c_len, LANES), F32), cos], axis=0)
    sin = jnp.concatenate([jnp.zeros((c_len, LANES), F32), sin], axis=0)
    return cos, sin


def kernel(x, c, ctx, c_ctx, ada_w, ada_b, norm_g, attn_w_qkv, attn_b_qkv, attn_sink, attn_w_o,
           hgrn_w_in, hgrn_lb, hgrn_norm_g, hgrn_w_o, router_w, router_b, moe_w_gu, moe_b_gu,
           moe_w_dn, moe_b_dn):
    b, l, d = x.shape
    c_len = ctx.shape[1]
    depth = ada_w.shape[0]
    assert b == 1 and d == D_MODEL and depth == 2
    assert c_len % ROW_BLOCK == 0 and l % ROW_BLOCK == 0
    cb = c_len // ROW_BLOCK

    cs = jnp.zeros((SUBLANES, d), F32).at[0].set(c_ctx).at[1].set(c[0])
    mods = _ada_call(cs, ada_w, ada_b)
    mods = mods[:, :2].reshape(depth, 2, 6, d)

    cos, sin = _rope_tables(l, c_len)

    q, k4, v4 = _qkv_call(ctx[0], x[0], mods[0], norm_g[0, 0:1], attn_w_qkv[0].astype(BF16),
                          attn_b_qkv[0].reshape(1, -1), cos, sin, cb)
    o = _attn_call(attn_sink[0], q, k4, v4, c_len)
    xs, h2, logits = _out_call("attn", (o,), (ctx[0], x[0]), mods[0], norm_g[0], attn_w_o[0].astype(BF16),
                               router_w[0], router_b[0].reshape(1, -1), None, cb, 0)
    xs = _moe(h2, logits, xs, mods[0], norm_g[0], 0, moe_w_gu, moe_b_gu, moe_w_dn, moe_b_dn, c_len)

    lb_soft = jax.nn.softmax(hgrn_lb.astype(F32), axis=0)
    lb = jnp.cumsum(lb_soft, axis=0)[1] - lb_soft[0]
    qh, kf, lf, kb, lbw, iv, gs = _hgrn_in_call(xs, mods[1], norm_g[1, 0:1], hgrn_w_in[0].astype(BF16),
                                                lb, cb)
    o_f = _scan_call(qh, kf, lf, iv, c_len, reverse=False)
    o_b = _scan_call(qh, kb, lbw, iv, c_len, reverse=True)
    x_lat, h2, logits = _out_call("hgrn", (o_f, o_b, gs), xs, mods[1], norm_g[1],
                                  hgrn_w_o[0].astype(BF16), router_w[1], router_b[1].reshape(1, -1),
                                  hgrn_norm_g[0].reshape(1, -1), cb, cb)
    out = _moe(h2, logits, x_lat, mods[1], norm_g[1], 1, moe_w_gu, moe_b_gu, moe_w_dn, moe_b_dn, 0)
    return out[None]
```

```python
import functools

import jax
import jax.numpy as jnp
from jax import lax
from jax.experimental import pallas as pl
from jax.experimental.pallas import tpu as pltpu

D_MODEL = 1024
GRID_W = 64
RMS_EPS = 1e-6

ATTN_HEADS = 16
ATTN_KV_HEADS = 2
HEAD_DIM = 64
Q_DIM = ATTN_HEADS * HEAD_DIM
KV_DIM = ATTN_KV_HEADS * HEAD_DIM
QKV_DIM = Q_DIM + 2 * KV_DIM
WINDOW = 128
ATTN_BLOCK = 128
ROPE_BASE = 10000.0

HGRN_HEADS = 8
HGRN_DK = 128
HGRN_CHUNK = 64
HGRN_SUB = 8
HGRN_GROUP = 1

N_EXPERTS = 32
TOP_K = 4
D_FF = 1024
SWIGLU_LIMIT = 7.0
SWIGLU_ALPHA = 1.702

LANES = 128
ROW_BLOCK = 256
MOE_ROWS = 256
COMBINE_ROWS = 256
VMEM_LIMIT = 56 * 1024 * 1024

F32 = jnp.float32
BF16 = jnp.bfloat16
NEG_BIG = -1e30
LOG2E = 1.4426950408889634


def _cparams(sem, row_dma=False):
    return pltpu.CompilerParams(dimension_semantics=sem, vmem_limit_bytes=VMEM_LIMIT,
                                disable_bounds_checks=row_dma)


def _rms(x, g):
    return x * lax.rsqrt(jnp.mean(x * x, axis=-1, keepdims=True) + RMS_EPS) * g


def _sigmoid(x):
    return 1.0 / (1.0 + jnp.exp(-x))


def _silu(x):
    return x * (0.5 * jnp.tanh(0.5 * x) + 0.5)


SEGS = D_MODEL // LANES


def _store_tiled(ref, val):
    n = val.shape[0]
    for s in range(SEGS):
        ref[pl.ds(s, n, stride=SEGS), :] = val[:, s * LANES:(s + 1) * LANES]


def _load_tiled(ref, n):
    return jnp.concatenate([ref[pl.ds(s, n, stride=SEGS), :] for s in range(SEGS)], axis=1)


def _row_tile(ref, r):
    return ref.at[pl.ds(pl.multiple_of(r * SEGS, SEGS), SEGS), :]


def _ada_kernel(c_ref, w_ref, b_ref, o_ref):
    c = c_ref[...]
    s = c * _sigmoid(c)
    o_ref[0] = jnp.dot(s, w_ref[0], precision=lax.Precision.HIGHEST,
                       preferred_element_type=F32) + b_ref[0]


def _ada_call(cs, ada_w, ada_b):
    depth, d, n = ada_w.shape
    tn = 1024
    return pl.pallas_call(
        _ada_kernel,
        out_shape=jax.ShapeDtypeStruct((depth, 8, n), F32),
        grid=(depth, n // tn),
        in_specs=[
            pl.BlockSpec((8, d), lambda i, j: (0, 0)),
            pl.BlockSpec((1, d, tn), lambda i, j: (i, 0, j)),
            pl.BlockSpec((1, 1, tn), lambda i, j: (i, 0, j)),
        ],
        out_specs=pl.BlockSpec((1, 8, tn), lambda i, j: (i, 0, j)),
        compiler_params=_cparams(("arbitrary", "arbitrary")),
        name="adaln",
    )(cs, ada_w, ada_b.reshape(depth, 1, n))


def _swap16(t):
    lane = lax.broadcasted_iota(jnp.int32, t.shape, 1)
    return jnp.where(lane % 32 < 16, pltpu.roll(t, LANES - 16, 1), pltpu.roll(t, 16, 1))


def _stream_rows(c_ref, x_ref, cb):
    return jnp.where(pl.program_id(0) < cb, c_ref[...], x_ref[...])


def _stream_specs(tm, d, cb):
    return [pl.BlockSpec((tm, d), lambda i: (jnp.minimum(i, cb - 1), 0)),
            pl.BlockSpec((tm, d), lambda i: (jnp.maximum(i - cb, 0), 0))]


def _qkv_kernel(c_ref, x_ref, mod_ref, gn_ref, w_ref, b_ref, cos_ref, sin_ref, q_ref, k4_ref, v4_ref, *, cb):
    x = _stream_rows(c_ref, x_ref, cb)
    h = _rms(x, gn_ref[...]) * (1.0 + mod_ref[0, 1:2, :]) + mod_ref[0, 0:1, :]
    hb = h.astype(BF16)
    cos = cos_ref[...]
    sin = sin_ref[...]
    nq = Q_DIM // LANES
    for j in range(nq + 1):
        sl = slice(j * LANES, (j + 1) * LANES)
        t = jnp.dot(hb, w_ref[:, sl], preferred_element_type=F32) + b_ref[:, sl]
        t = t * cos + _swap16(t) * sin
        if j < nq:
            q_ref[:, sl] = (t * (HEAD_DIM ** -0.5 * LOG2E)).astype(q_ref.dtype)
        else:
            kt = t
    sl = slice(Q_DIM + KV_DIM, QKV_DIM)
    vt = jnp.dot(hb, w_ref[:, sl], preferred_element_type=F32) + b_ref[:, sl]
    lo = lax.broadcasted_iota(jnp.int32, kt.shape, 1) < HEAD_DIM
    for t, ref in ((kt, k4_ref), (vt, v4_ref)):
        sw = pltpu.roll(t, HEAD_DIM, 1)
        ref[:, 0 * LANES:1 * LANES] = jnp.where(lo, t, 0.0).astype(ref.dtype)
        ref[:, 1 * LANES:2 * LANES] = jnp.where(lo, 0.0, sw).astype(ref.dtype)
        ref[:, 2 * LANES:3 * LANES] = jnp.where(lo, sw, 0.0).astype(ref.dtype)
        ref[:, 3 * LANES:4 * LANES] = jnp.where(lo, 0.0, t).astype(ref.dtype)


def _qkv_call(ctx, x, mod, gn, w, b, cos, sin, cb):
    d = x.shape[1]
    t = ctx.shape[0] + x.shape[0]
    tm = ROW_BLOCK
    sel = lambda i: (jnp.minimum(i // cb, 1), 0, 0)
    return pl.pallas_call(
        functools.partial(_qkv_kernel, cb=cb),
        out_shape=(jax.ShapeDtypeStruct((t, Q_DIM), BF16),
                   jax.ShapeDtypeStruct((t, 4 * LANES), BF16),
                   jax.ShapeDtypeStruct((t, 4 * LANES), BF16)),
        grid=(t // tm,),
        in_specs=_stream_specs(tm, d, cb) + [
            pl.BlockSpec((1, 6, d), sel),
            pl.BlockSpec((1, d), lambda i: (0, 0)),
            pl.BlockSpec((d, QKV_DIM), lambda i: (0, 0)),
            pl.BlockSpec((1, QKV_DIM), lambda i: (0, 0)),
            pl.BlockSpec((tm, LANES), lambda i: (i, 0)),
            pl.BlockSpec((tm, LANES), lambda i: (i, 0)),
        ],
        out_specs=(pl.BlockSpec((tm, Q_DIM), lambda i: (i, 0)),
                   pl.BlockSpec((tm, 4 * LANES), lambda i: (i, 0)),
                   pl.BlockSpec((tm, 4 * LANES), lambda i: (i, 0))),
        compiler_params=_cparams(("parallel",)),
        name="qkv_rope",
    )(ctx, x, mod, gn, w, b, cos, sin)


def _attn_kernel(sink_ref, q_ref, kp_ref, kc_ref, kn_ref, kx_ref, vp_ref, vc_ref, vn_ref, vx_ref,
                 o_ref, *, cb, n_lat):
    n = pl.program_id(0)
    blk = ATTN_BLOCK
    c = kx_ref.shape[0]
    nw = 3 * blk
    s = lax.broadcasted_iota(jnp.int32, (nw + c, blk), 0)
    r = lax.broadcasted_iota(jnp.int32, (nw + c, blk), 1)
    q_pos = (n - cb) * blk + r
    k_pos = (n - cb - 1) * blk + s
    win_ok = (jnp.abs(q_pos - k_pos) <= WINDOW) & (k_pos >= 0) & (k_pos < n_lat) & (n >= cb)
    valid = (s >= nw) | win_ok
    k_all = jnp.concatenate([kp_ref[...], kc_ref[...], kn_ref[...], kx_ref[...]], axis=0)
    v_all = jnp.concatenate([vp_ref[...], vc_ref[...], vn_ref[...], vx_ref[...]], axis=0)
    n_var = 2 * ATTN_KV_HEADS
    v_t = [v_all[:, j * LANES:(j + 1) * LANES].astype(F32).T.astype(BF16) for j in range(n_var)]
    nt = (((1,), (1,)), ((), ()))
    group = ATTN_HEADS // ATTN_KV_HEADS
    for p in range(ATTN_HEADS // 2):
        g = (2 * p) // group
        qp = q_ref[:, p * LANES:(p + 1) * LANES]
        o_t = jnp.zeros((LANES, blk), F32)
        for half in range(2):
            j = 2 * g + half
            sc = lax.dot_general(k_all[:, j * LANES:(j + 1) * LANES], qp, nt,
                                 preferred_element_type=F32)
            sc = jnp.where(valid, sc, NEG_BIG)
            sink = sink_ref[2 * p + half] * LOG2E
            m = jnp.maximum(jnp.max(sc, axis=0, keepdims=True), sink)
            e = jnp.exp2(sc - m)
            denom = jnp.sum(e, axis=0, keepdims=True) + jnp.exp2(sink - m)
            pv = jnp.dot(v_t[j], e.astype(BF16), preferred_element_type=F32)
            o_t = o_t + pv / denom
        o_ref[:, p * LANES:(p + 1) * LANES] = o_t.T.astype(o_ref.dtype)


def _attn_call(sink, q, k4, v4, c_len):
    t = q.shape[0]
    blk = ATTN_BLOCK
    cb = c_len // blk
    nb = t // blk
    n_lat = t - c_len
    last = nb - 1
    kw = 4 * LANES
    spec_q = pl.BlockSpec((blk, Q_DIM), lambda n: (n, 0))
    prev = pl.BlockSpec((blk, kw), lambda n: (jnp.maximum(n - 1, 0), 0))
    cur = pl.BlockSpec((blk, kw), lambda n: (n, 0))
    nxt = pl.BlockSpec((blk, kw), lambda n: (jnp.minimum(n + 1, last), 0))
    ctx = pl.BlockSpec((c_len, kw), lambda n: (0, 0))
    return pl.pallas_call(
        functools.partial(_attn_kernel, cb=cb, n_lat=n_lat),
        out_shape=jax.ShapeDtypeStruct((t, Q_DIM), BF16),
        grid=(nb,),
        in_specs=[pl.BlockSpec(memory_space=pltpu.SMEM), spec_q,
                  prev, cur, nxt, ctx, prev, cur, nxt, ctx],
        out_specs=pl.BlockSpec((blk, Q_DIM), lambda n: (n, 0)),
        compiler_params=_cparams(("parallel",)),
        name="window_attn",
    )(sink, q, k4, k4, k4, k4, v4, v4, v4, v4)


def _residual_tail(y, x, mod_ref, gn_ref, rw_ref, rb_ref, xo_ref, h2_ref, lg_ref):
    x_new = x + mod_ref[0, 2:3, :] * _rms(y, gn_ref[1:2, :])
    h2 = _rms(x_new, gn_ref[2:3, :]) * (1.0 + mod_ref[0, 4:5, :]) + mod_ref[0, 3:4, :]
    xo_ref[...] = x_new
    _store_tiled(h2_ref, h2)
    hi, mid, lo = _split3(h2)
    lane = lax.broadcasted_iota(jnp.int32, (h2.shape[0], LANES), 1)
    w = rw_ref[...]
    r = (jnp.dot(hi, w, preferred_element_type=F32)
         + jnp.where(lane < 2 * N_EXPERTS, jnp.dot(mid, w, preferred_element_type=F32), 0.0)
         + jnp.where(lane < N_EXPERTS, jnp.dot(lo, w, preferred_element_type=F32), 0.0))
    r = r + pltpu.roll(r, LANES - N_EXPERTS, 1) + pltpu.roll(r, LANES - 2 * N_EXPERTS, 1)
    lg_ref[...] = r[:, :N_EXPERTS] + rb_ref[...]


def _attn_out_kernel(a_ref, c_ref, x_ref, mod_ref, gn_ref, w_ref, rw_ref, rb_ref, xo_ref, h2_ref, lg_ref,
                     *, cb):
    y = jnp.dot(a_ref[...], w_ref[...], preferred_element_type=F32)
    _residual_tail(y, _stream_rows(c_ref, x_ref, cb), mod_ref, gn_ref, rw_ref, rb_ref, xo_ref, h2_ref, lg_ref)


def _hgrn_out_kernel(of_ref, ob_ref, gs_ref, hg_ref, x_ref, mod_ref, gn_ref, w_ref, rw_ref, rb_ref,
                     xo_ref, h2_ref, lg_ref):
    parts = []
    for h in range(HGRN_HEADS):
        sl = slice(h * HGRN_DK, (h + 1) * HGRN_DK)
        o = of_ref[:, sl] + ob_ref[:, sl]
        parts.append(_rms(o, hg_ref[:, sl]))
    a = jnp.concatenate(parts, axis=1) * gs_ref[...].astype(F32)
    y = jnp.dot(a.astype(BF16), w_ref[...], preferred_element_type=F32)
    _residual_tail(y, x_ref[...], mod_ref, gn_ref, rw_ref, rb_ref, xo_ref, h2_ref, lg_ref)


def _out_call(kind, acts, x, mod, gn, w, rw, rb, hg, cb, off):
    t_out = (x[0].shape[0] + x[1].shape[0]) if kind == "attn" else x.shape[0] - off * ROW_BLOCK
    d = D_MODEL
    tm = ROW_BLOCK
    w_hi, w_mid, w_lo = _split3(rw)
    rw = jnp.concatenate([w_hi, w_mid, w_lo, jnp.zeros_like(w_hi)], axis=1)
    row = lambda i: (i + off, 0)
    sel = lambda i: (jnp.minimum((i + off) // cb, 1), 0, 0)
    const = lambda i: (0, 0)
    common_specs = [
        pl.BlockSpec((1, 6, d), sel),
        pl.BlockSpec((4, d), const),
        pl.BlockSpec((d, d), const),
        pl.BlockSpec((d, LANES), const),
        pl.BlockSpec((1, N_EXPERTS), const),
    ]
    if kind == "attn":
        body = functools.partial(_attn_out_kernel, cb=cb)
        in_specs = [pl.BlockSpec((tm, d), row)] + _stream_specs(tm, d, cb) + common_specs
        args = (acts[0], x[0], x[1], mod, gn, w, rw, rb)
    else:
        body = _hgrn_out_kernel
        in_specs = ([pl.BlockSpec((tm, d), row)] * 3 + [pl.BlockSpec((1, d), const)]
                    + [pl.BlockSpec((tm, d), row)] + common_specs)
        args = (acts[0], acts[1], acts[2], hg, x, mod, gn, w, rw, rb)
    return pl.pallas_call(
        body,
        out_shape=(jax.ShapeDtypeStruct((t_out, d), F32),
                   jax.ShapeDtypeStruct((t_out * SEGS, LANES), F32),
                   jax.ShapeDtypeStruct((t_out, N_EXPERTS), F32)),
        grid=(t_out // tm,),
        in_specs=in_specs,
        out_specs=(pl.BlockSpec((tm, d), lambda i: (i, 0)),
                   pl.BlockSpec((tm * SEGS, LANES), lambda i: (i, 0)),
                   pl.BlockSpec((tm, N_EXPERTS), lambda i: (i, 0))),
        compiler_params=_cparams(("parallel",)),
        name=kind + "_out_residual",
    )(*args)


def _router_kernel(lg_ref, meta_ref, cnt_ref, run_ref):
    i = pl.program_id(0)

    @pl.when(i == 0)
    def _():
        run_ref[...] = jnp.zeros_like(run_ref)

    lg = lg_ref[...].T
    ne, tb = lg.shape
    eid = lax.broadcasted_iota(jnp.int32, lg.shape, 0).astype(F32)
    sels, tops, idxs = [], [], []
    for _ in range(TOP_K):
        m = jnp.max(lg, axis=0, keepdims=True)
        idx = jnp.min(jnp.where(lg == m, eid, float(N_EXPERTS)), axis=0, keepdims=True)
        sel = eid == idx
        sels.append(sel)
        tops.append(m)
        idxs.append(idx)
        lg = jnp.where(sel, -jnp.inf, lg)
    ws = [jnp.exp(tk - tops[0]) for tk in tops]
    wsum = ws[0] + ws[1] + ws[2] + ws[3]
    chosen = (sels[0] | sels[1] | sels[2] | sels[3])
    onehot = jnp.where(chosen, 1.0, 0.0)
    rr = lax.broadcasted_iota(jnp.int32, (tb, tb), 0)
    cc = lax.broadcasted_iota(jnp.int32, (tb, tb), 1)
    tri = jnp.where(rr < cc, 1.0, 0.0).astype(BF16)
    before = jnp.dot(onehot.astype(BF16), tri, preferred_element_type=F32) + run_ref[:, 0:1]
    rows = list(idxs)
    rows += [jnp.sum(jnp.where(sels[k], before, 0.0), axis=0, keepdims=True) for k in range(TOP_K)]
    rows += [ws[k] / wsum for k in range(TOP_K)]
    rows.append(jnp.zeros((meta_ref.shape[0] - len(rows), tb), F32))
    meta_ref[...] = jnp.concatenate(rows, axis=0)
    run_ref[...] = run_ref[...] + jnp.sum(onehot, axis=1, keepdims=True)
    cnt_ref[...] = run_ref[...]


ROUTE_ROWS = 16


def _router_call(logits):
    t = logits.shape[0]
    tb = ROW_BLOCK
    meta, counts = pl.pallas_call(
        _router_kernel,
        out_shape=(jax.ShapeDtypeStruct((ROUTE_ROWS, t), F32),
                   jax.ShapeDtypeStruct((N_EXPERTS, LANES), F32)),
        grid=(t // tb,),
        in_specs=[pl.BlockSpec((tb, N_EXPERTS), lambda i: (i, 0))],
        out_specs=(pl.BlockSpec((ROUTE_ROWS, tb), lambda i: (0, i)),
                   pl.BlockSpec((N_EXPERTS, LANES), lambda i: (0, 0))),
        scratch_shapes=[pltpu.VMEM((N_EXPERTS, LANES), F32)],
        compiler_params=_cparams(("arbitrary",)),
        name="router_topk",
    )(logits)
    return meta, counts[:, 0]


def _expert_kernel(be_ref, nu_ref, nx_ref, ws_ref, src0_ref, src1_ref, x_hbm, wgu_hbm, bgu_ref, wdn_hbm, bdn_ref,
                   o_ref, xbuf, sems, wgu_f, wdn_f, wsems, wgu_bf, wdn_bf, *, layer):
    i = pl.program_id(0)
    rows = o_ref.shape[0] // SEGS
    slot = i % 2
    n_used = nu_ref[0]

    def gather(idx_ref, s):
        def issue(r2, carry):
            for p in range(2):
                r = 2 * r2 + p
                pltpu.make_async_copy(_row_tile(x_hbm, idx_ref[0, 0, r]), _row_tile(xbuf.at[s], r),
                                      sems.at[s]).start()
            return carry

        lax.fori_loop(0, rows // 2, issue, 0, unroll=4)

    @pl.when(i == 0)
    def _():
        gather(src0_ref, 0)

    @pl.when(i + 1 < n_used)
    def _():
        gather(src1_ref, 1 - slot)

    prev = be_ref[jnp.maximum(i - 1, 0)]
    fresh = (i == 0) | (be_ref[i] != prev)

    e_cur = be_ref[i]
    wslot = ws_ref[i]

    def weight_copies(e, s):
        return (pltpu.make_async_copy(wgu_hbm.at[layer, e], wgu_f.at[s], wsems.at[s, 0]),
                pltpu.make_async_copy(wdn_hbm.at[layer, e], wdn_f.at[s], wsems.at[s, 1]))

    @pl.when(i == 0)
    def _():
        for c in weight_copies(e_cur, wslot):
            c.start(priority=1)

    @pl.when(fresh)
    def _():
        @pl.when(nx_ref[i] != e_cur)
        def _():
            for c in weight_copies(nx_ref[i], 1 - wslot):
                c.start(priority=1)

        for c in weight_copies(e_cur, wslot):
            c.wait()
        wgu_bf[...] = wgu_f[wslot].astype(BF16)
        wdn_bf[...] = wdn_f[wslot].astype(BF16)

    @pl.when(i < n_used)
    def _():
        pltpu.make_async_copy(x_hbm.at[pl.ds(0, rows * SEGS), :], xbuf.at[slot], sems.at[slot]).wait()
        xb = _load_tiled(xbuf.at[slot], rows).astype(BF16)
        gu = jnp.dot(xb, wgu_bf[...], preferred_element_type=F32) + bgu_ref[...]
        g = jnp.minimum(gu[:, :D_FF], SWIGLU_LIMIT)
        u = jnp.clip(gu[:, D_FF:], -SWIGLU_LIMIT, SWIGLU_LIMIT)
        hdn = g * _sigmoid(SWIGLU_ALPHA * g) * (u + 1.0)
        y = jnp.dot(hdn.astype(BF16), wdn_bf[...], preferred_element_type=F32) + bdn_ref[...]
        _store_tiled(o_ref, y)

    @pl.when(i >= n_used)
    def _():
        o_ref[...] = jnp.zeros_like(o_ref)


def _expert_call(blk_e, n_used, src_tok, h2, layer, w_gu, b_gu, w_dn, b_dn):
    d = D_MODEL
    tr = MOE_ROWS
    n_blk = blk_e.shape[0]
    depth, e, _, f2 = w_gu.shape
    last = n_blk - 1
    nxt = jnp.min(jnp.where(blk_e[None, :] > blk_e[:, None], blk_e[None, :], N_EXPERTS), axis=1)
    nxt = jnp.where(nxt == N_EXPERTS, blk_e, nxt).astype(jnp.int32)
    change = jnp.concatenate([jnp.zeros((1,), jnp.int32), (blk_e[1:] != blk_e[:-1]).astype(jnp.int32)])
    wslot = (jnp.cumsum(change) % 2).astype(jnp.int32)
    return pl.pallas_call(
        functools.partial(_expert_kernel, layer=layer),
        out_shape=jax.ShapeDtypeStruct((n_blk * tr * SEGS, LANES), F32),
        grid_spec=pltpu.PrefetchScalarGridSpec(
            num_scalar_prefetch=4,
            grid=(n_blk,),
            in_specs=[
                pl.BlockSpec((1, 1, tr), lambda i, *_: (0, 0, 0), memory_space=pltpu.SMEM),
                pl.BlockSpec((1, 1, tr), lambda i, *_: (jnp.minimum(i + 1, last), 0, 0),
                             memory_space=pltpu.SMEM),
                pl.BlockSpec(memory_space=pl.ANY),
                pl.BlockSpec(memory_space=pl.ANY),
                pl.BlockSpec((None, None, 1, f2), lambda i, be, *_: (layer, be[i], 0, 0)),
                pl.BlockSpec(memory_space=pl.ANY),
                pl.BlockSpec((None, None, 1, d), lambda i, be, *_: (layer, be[i], 0, 0)),
            ],
            out_specs=pl.BlockSpec((tr * SEGS, LANES), lambda i, *_: (i, 0)),
            scratch_shapes=[pltpu.VMEM((2, tr * SEGS, LANES), F32), pltpu.SemaphoreType.DMA((2,)),
                            pltpu.VMEM((2, d, f2), F32), pltpu.VMEM((2, D_FF, d), F32),
                            pltpu.SemaphoreType.DMA((2, 2)),
                            pltpu.VMEM((d, f2), BF16), pltpu.VMEM((D_FF, d), BF16)],
        ),
        compiler_params=_cparams(("arbitrary",), row_dma=True),
        name="moe_experts",
    )(blk_e, n_used, nxt, wslot, src_tok.reshape(n_blk, 1, tr), src_tok.reshape(n_blk, 1, tr), h2,
      w_gu, b_gu.reshape(depth, e, 1, f2), w_dn, b_dn.reshape(depth, e, 1, d))


def _combine_kernel(d0_ref, d1_ref, y_hbm, meta_ref, x_ref, mod_ref, gn_ref, o_ref, buf, sems):
    i = pl.program_id(0)
    tb = o_ref.shape[0]
    slot = i % 2

    def gather(idx_ref, s):
        def issue(r, carry):
            for k in range(TOP_K):
                row = idx_ref[0, 0, k * tb + r]
                pltpu.make_async_copy(_row_tile(y_hbm, row), _row_tile(buf.at[s, k], r),
                                      sems.at[s]).start(priority=k % 2)
            return carry

        lax.fori_loop(0, tb, issue, 0, unroll=4)

    @pl.when(i == 0)
    def _():
        gather(d0_ref, 0)

    @pl.when(i + 1 < pl.num_programs(0))
    def _():
        gather(d1_ref, 1 - slot)

    for k in range(TOP_K):
        pltpu.make_async_copy(y_hbm.at[pl.ds(0, tb * SEGS), :], buf.at[slot, k], sems.at[slot]).wait()
    gates = meta_ref[...].T
    y2 = jnp.zeros(o_ref.shape, F32)
    for k in range(TOP_K):
        y2 = y2 + _load_tiled(buf.at[slot, k], tb) * gates[:, 2 * TOP_K + k:2 * TOP_K + k + 1]
    o_ref[...] = x_ref[...] + mod_ref[0, 5:6, :] * _rms(y2, gn_ref[3:4, :])


def _combine_call(dest, y_sorted, meta, x, mod, gn, cb_rows):
    t, d = x.shape
    tb = COMBINE_ROWS
    nb = t // tb
    cbb = max(cb_rows // tb, 1)
    sel = (lambda i: (jnp.minimum(i // cbb, 1), 0, 0)) if cb_rows else (lambda i: (1, 0, 0))
    dest3 = dest.reshape(TOP_K, nb, tb).transpose(1, 0, 2).reshape(nb, 1, TOP_K * tb)
    return pl.pallas_call(
        _combine_kernel,
        out_shape=jax.ShapeDtypeStruct((t, d), F32),
        grid=(nb,),
        in_specs=[
            pl.BlockSpec((1, 1, tb * TOP_K), lambda i: (0, 0, 0), memory_space=pltpu.SMEM),
            pl.BlockSpec((1, 1, tb * TOP_K), lambda i: (jnp.minimum(i + 1, nb - 1), 0, 0),
                         memory_space=pltpu.SMEM),
            pl.BlockSpec(memory_space=pl.ANY),
            pl.BlockSpec((ROUTE_ROWS, tb), lambda i: (0, i)),
            pl.BlockSpec((tb, d), lambda i: (i, 0)),
            pl.BlockSpec((1, 6, d), sel),
            pl.BlockSpec((4, d), lambda i: (0, 0)),
        ],
        out_specs=pl.BlockSpec((tb, d), lambda i: (i, 0)),
        scratch_shapes=[pltpu.VMEM((2, TOP_K, tb * SEGS, LANES), F32), pltpu.SemaphoreType.DMA((2,))],
        compiler_params=_cparams(("arbitrary",), row_dma=True),
        name="moe_combine",
    )(dest3, dest3, y_sorted, meta, x, mod, gn)


INVERT_BLOCK = 2048


def _invert_kernel(dest_ref, zeros_hbm, src_ref):
    k, j = pl.program_id(0), pl.program_id(1)
    n = dest_ref.shape[2]

    @pl.when((k == 0) & (j == 0))
    def _():
        pltpu.sync_copy(zeros_hbm, src_ref)

    tok0 = j * n

    def put(t, carry):
        src_ref[dest_ref[0, 0, t]] = tok0 + t
        return carry

    lax.fori_loop(0, n, put, 0, unroll=32)


def _invert_call(dest, n_rows):
    t = dest.shape[1]
    blk = max(b for b in range(LANES, INVERT_BLOCK + 1, LANES) if t % b == 0)
    nb = t // blk
    return pl.pallas_call(
        _invert_kernel,
        out_shape=jax.ShapeDtypeStruct((n_rows,), jnp.int32),
        grid=(TOP_K, nb),
        in_specs=[pl.BlockSpec((1, 1, blk), lambda k, j: (k * nb + j, 0, 0), memory_space=pltpu.SMEM),
                  pl.BlockSpec(memory_space=pl.ANY)],
        out_specs=pl.BlockSpec(memory_space=pltpu.SMEM),
        compiler_params=_cparams(("arbitrary", "arbitrary")),
        name="moe_invert",
    )(dest.reshape(TOP_K * nb, 1, blk), jnp.zeros((n_rows,), jnp.int32))


def _moe(h2, logits, x, mod, gn, layer, w_gu, b_gu, w_dn, b_dn, cb_rows):
    t = logits.shape[0]
    tr = MOE_ROWS
    meta, counts = _router_call(logits)
    e_idx = meta[:TOP_K].astype(jnp.int32)
    rank = meta[TOP_K:2 * TOP_K].astype(jnp.int32)
    counts = counts.astype(jnp.int32)
    padded = (counts + tr - 1) // tr * tr
    p_end = jnp.cumsum(padded)
    p_start = p_end - padded
    dest = rank
    for e in range(N_EXPERTS):
        dest = dest + jnp.where(e_idx == e, p_start[e], 0)
    a = t * TOP_K
    n_blk = (a + N_EXPERTS * (tr - 1) + tr - 1) // tr
    src_tok = _invert_call(dest, n_blk * tr)
    blk_start = jnp.arange(n_blk, dtype=jnp.int32) * tr
    blk_e = jnp.sum((p_end[None, :] <= blk_start[:, None]).astype(jnp.int32), axis=1)
    blk_e = jnp.minimum(blk_e, N_EXPERTS - 1)
    n_used = (p_end[-1] // tr).astype(jnp.int32).reshape(1)
    ys = _expert_call(blk_e, n_used, src_tok, h2, layer, w_gu, b_gu, w_dn, b_dn)
    return _combine_call(dest, ys, meta, x, mod, gn, cb_rows)


def _hgrn_in_kernel(x_ref, mod_ref, gn_ref, w_ref, lb_ref, q_ref, kf_ref, lf_ref, kb_ref, lbw_ref,
                    i_ref, gs_ref):
    x = x_ref[...]
    h = _rms(x, gn_ref[...]) * (1.0 + mod_ref[0, 1:2, :]) + mod_ref[0, 0:1, :]
    hb = h.astype(BF16)
    f = D_MODEL

    def proj(sec):
        return jnp.dot(hb, w_ref[:, sec * f:(sec + 1) * f], preferred_element_type=F32)

    z = proj(0)
    q_ref[...] = _silu(z).astype(q_ref.dtype)
    for sec, k_ref, l_ref in ((1, kf_ref, lf_ref), (2, kb_ref, lbw_ref)):
        z = proj(sec)
        lb = lb_ref[sec - 1:sec, :]
        sg = _sigmoid(z)
        l_ref[...] = jnp.log(lb + (1.0 - lb) * sg)
        k_ref[...] = ((1.0 - lb) * (1.0 - sg)).astype(k_ref.dtype)
    i_ref[...] = proj(3).astype(i_ref.dtype)
    z = proj(4)
    gs_ref[...] = _silu(z).astype(gs_ref.dtype)


def _hgrn_in_call(xs, mod, gn, w, lb, cb):
    t, d = xs.shape
    tm = ROW_BLOCK
    sel = lambda i: (jnp.minimum(i // cb, 1), 0, 0)
    row = pl.BlockSpec((tm, d), lambda i: (i, 0))
    dts = (BF16, BF16, F32, BF16, F32, BF16, BF16)
    return pl.pallas_call(
        _hgrn_in_kernel,
        out_shape=tuple(jax.ShapeDtypeStruct((t, d), dt) for dt in dts),
        grid=(t // tm,),
        in_specs=[row, pl.BlockSpec((1, 6, d), sel), pl.BlockSpec((1, d), lambda i: (0, 0)),
                  pl.BlockSpec(w.shape, lambda i: (0, 0)), pl.BlockSpec((2, d), lambda i: (0, 0))],
        out_specs=tuple(row for _ in dts),
        compiler_params=_cparams(("parallel",)),
        name="hgrn_in_proj",
    )(xs, mod, gn, w, lb)


def _split3(x):
    hi = x.astype(BF16)
    r1 = x - hi.astype(F32)
    mid = r1.astype(BF16)
    lo = (r1 - mid.astype(F32)).astype(BF16)
    return hi, mid, lo


def _scan_kernel(q_ref, k_ref, g_ref, v_ref, o_ref, st_ref, gcum_ref, *, reverse):
    j = pl.program_id(0)
    ch, sub = HGRN_CHUNK, HGRN_SUB
    ns = ch // sub
    dk = HGRN_DK

    @pl.when(j == 0)
    def _():
        st_ref[...] = jnp.zeros_like(st_ref)

    rr = lax.broadcasted_iota(jnp.int32, (ch, ch), 0)
    cc = lax.broadcasted_iota(jnp.int32, (ch, ch), 1)
    causal = (cc >= rr) if reverse else (cc <= rr)
    tri = jnp.where(causal, 1.0, 0.0).astype(BF16)
    hi, mid, lo = _split3(g_ref[...])
    gcum_ref[...] = LOG2E * (jnp.dot(tri, hi, preferred_element_type=F32)
                             + jnp.dot(tri, mid, preferred_element_type=F32)
                             + jnp.dot(tri, lo, preferred_element_type=F32))
    key_blocks = list(range(1, ns)) if reverse else list(range(ns - 1))
    nkb = len(key_blocks)
    kbd_mask = (lax.broadcasted_iota(jnp.int32, (ch, nkb * dk), 1) // dk + key_blocks[0]
                == lax.broadcasted_iota(jnp.int32, (ch, nkb * dk), 0) // sub)
    kdg_mask = (lax.broadcasted_iota(jnp.int32, (ch, sub * dk), 1) // dk
                == lax.broadcasted_iota(jnp.int32, (ch, sub * dk), 0) % sub)
    end_row = 0 if reverse else ch - 1
    edge = 0 if reverse else sub - 1
    nt = (((1,), (1,)), ((), ()))
    tn = (((0,), (0,)), ((), ()))

    grp = HGRN_GROUP
    rr_g = lax.broadcasted_iota(jnp.int32, (ch, grp * ch), 0)
    cc_g = lax.broadcasted_iota(jnp.int32, (ch, grp * ch), 1) % ch
    diag_mask_g = ((rr_g // sub) == (cc_g // sub)) & ((cc_g >= rr_g) if reverse else (cc_g <= rr_g))

    def place(x, i):
        w = x.shape[1]
        parts = []
        if i:
            parts.append(jnp.zeros((x.shape[0], i * w), x.dtype))
        parts.append(x)
        if grp - 1 - i:
            parts.append(jnp.zeros((x.shape[0], (grp - 1 - i) * w), x.dtype))
        return jnp.concatenate(parts, axis=1)

    for g0 in range(0, HGRN_HEADS, grp):
        qgs, q_cats, p_cats, kbd_rows, kdg_rows, st_rows, v_rows = [], [], [], [], [], [], []
        for i in range(grp):
            h = g0 + i
            hs = slice(h * dk, (h + 1) * dk)
            q = q_ref[:, hs].astype(F32)
            kb = k_ref[:, hs]
            k = kb.astype(F32)
            v = v_ref[:, hs]
            gc = gcum_ref[:, hs]
            st = st_ref[h]
            g_end = gc[end_row:end_row + 1, :]
            qgs.append((q * jnp.exp2(gc)).astype(BF16))
            st_rows.append(place(st.astype(BF16), i))
            v_rows.append(place(v, i))
            k_end = (k * jnp.exp2(g_end - gc)).astype(BF16)
            st_ref[h] = st * jnp.exp2(g_end) + lax.dot_general(v, k_end, tn, preferred_element_type=F32)
            gc4 = gc.reshape(ns, sub, dk)
            g_edge = jnp.broadcast_to(gc4[:, edge:edge + 1, :], (ns, sub, dk)).reshape(ch, dk)
            k_rel = (k * jnp.exp2(g_edge - gc)).astype(BF16)
            k_bd = jnp.where(kbd_mask, jnp.concatenate([k_rel] * nkb, axis=1), jnp.zeros((), BF16))
            kbd_rows.append(place(k_bd, i))
            q_parts = []
            for jb in key_blocks:
                row = jb * sub + edge
                rows = slice(0, jb * sub) if reverse else slice((jb + 1) * sub, ch)
                part = q[rows] * jnp.exp2(gc[rows] - gc[row:row + 1, :])
                pad = jnp.zeros((ch - part.shape[0], dk), F32)
                full = jnp.concatenate([part, pad] if reverse else [pad, part], axis=0)
                q_parts.append(full.astype(BF16))
            q_cats.append(jnp.concatenate(q_parts, axis=1))
            p_parts = []
            for s in range(sub):
                g_s = jnp.broadcast_to(gc4[:, s:s + 1, :], (ns, sub, dk)).reshape(ch, dk)
                p_parts.append((q * jnp.exp2(jnp.minimum(gc - g_s, 0.0))).astype(BF16))
            p_cats.append(jnp.concatenate(p_parts, axis=1))
            k_dg = jnp.where(kdg_mask, jnp.concatenate([kb] * sub, axis=1), jnp.zeros((), BF16))
            kdg_rows.append(place(k_dg, i))
        s_off = lax.dot_general(jnp.concatenate(q_cats, axis=1), jnp.concatenate(kbd_rows, axis=0), nt,
                                preferred_element_type=F32)
        s_diag = lax.dot_general(jnp.concatenate(p_cats, axis=1), jnp.concatenate(kdg_rows, axis=0), nt,
                                 preferred_element_type=F32)
        a = s_off + jnp.where(diag_mask_g, s_diag, 0.0)
        o = lax.dot_general(jnp.concatenate(qgs, axis=1), jnp.concatenate(st_rows, axis=0), nt,
                            preferred_element_type=F32)
        o = o + jnp.dot(a.astype(BF16), jnp.concatenate(v_rows, axis=0), preferred_element_type=F32)
        o_ref[:, g0 * dk:(g0 + grp) * dk] = o


def _scan_call(q, k, lg, v, c_len, reverse):
    t, d = q.shape
    ch = HGRN_CHUNK
    n_ch = t // ch
    cc = c_len // ch
    if reverse:
        idx = lambda j: (jnp.where(j < cc, cc - 1 - j, n_ch - 1 - (j - cc)), 0)
    else:
        idx = lambda j: (j, 0)
    blk = pl.BlockSpec((ch, d), idx)
    return pl.pallas_call(
        functools.partial(_scan_kernel, reverse=reverse),
        out_shape=jax.ShapeDtypeStruct((t, d), F32),
        grid=(n_ch,),
        in_specs=[blk, blk, blk, blk],
        out_specs=blk,
        scratch_shapes=[pltpu.VMEM((HGRN_HEADS, HGRN_DK, HGRN_DK), F32), pltpu.VMEM((ch, d), F32)],
        compiler_params=_cparams(("arbitrary",)),
        name="hgrn_scan_bwd" if reverse else "hgrn_scan_fwd",
    )(q, k, lg, v)


def _rope_tables(l, c_len):
    n = HEAD_DIM // 4
    inv_freq = ROPE_BASE ** (-jnp.arange(n, dtype=F32) / n)
    rows = l // GRID_W
    ang_row = jnp.arange(rows, dtype=F32)[:, None] * inv_freq[None, :]
    ang_col = jnp.arange(GRID_W, dtype=F32)[:, None] * inv_freq[None, :]
    cr, sr, cc, sc = jnp.cos(ang_row), jnp.sin(ang_row), jnp.cos(ang_col), jnp.sin(ang_col)
    zr, zc = jnp.zeros_like(cr), jnp.zeros_like(cc)
    cos_r = jnp.concatenate([cr, cr, zr, zr] * 2, axis=1)
    sin_r = jnp.concatenate([-sr, sr, zr, zr] * 2, axis=1)
    cos_c = jnp.concatenate([zc, zc, cc, cc] * 2, axis=1)
    sin_c = jnp.concatenate([zc, zc, -sc, sc] * 2, axis=1)
    cos = (cos_r[:, None, :] + cos_c[None, :, :]).reshape(l, LANES)
    sin = (sin_r[:, None, :] + sin_c[None, :, :]).reshape(l, LANES)
    cos = jnp.concatenate([jnp.ones((c_len, LANES), F32), cos], axis=0)
    sin = jnp.concatenate([jnp.zeros((c_len, LANES), F32), sin], axis=0)
    return cos, sin


def kernel(x, c, ctx, c_ctx, ada_w, ada_b, norm_g, attn_w_qkv, attn_b_qkv, attn_sink, attn_w_o,
           hgrn_w_in, hgrn_lb, hgrn_norm_g, hgrn_w_o, router_w, router_b, moe_w_gu, moe_b_gu,
           moe_w_dn, moe_b_dn):
    b, l, d = x.shape
    c_len = ctx.shape[1]
    depth = ada_w.shape[0]
    assert b == 1 and d == D_MODEL and depth == 2
    assert c_len % ROW_BLOCK == 0 and l % ROW_BLOCK == 0
    cb = c_len // ROW_BLOCK

    cs = jnp.zeros((8, d), F32).at[0].set(c_ctx).at[1].set(c[0])
    mods = _ada_call(cs, ada_w, ada_b)
    mods = mods[:, :2].reshape(depth, 2, 6, d)

    cos, sin = _rope_tables(l, c_len)

    q, k4, v4 = _qkv_call(ctx[0], x[0], mods[0], norm_g[0, 0:1], attn_w_qkv[0].astype(BF16),
                          attn_b_qkv[0].reshape(1, -1), cos, sin, cb)
    o = _attn_call(attn_sink[0], q, k4, v4, c_len)
    xs, h2, logits = _out_call("attn", (o,), (ctx[0], x[0]), mods[0], norm_g[0], attn_w_o[0].astype(BF16),
                               router_w[0], router_b[0].reshape(1, -1), None, cb, 0)
    xs = _moe(h2, logits, xs, mods[0], norm_g[0], 0, moe_w_gu, moe_b_gu, moe_w_dn, moe_b_dn, c_len)

    lb_soft = jax.nn.softmax(hgrn_lb.astype(F32), axis=0)
    lb = jnp.cumsum(lb_soft, axis=0)[1] - lb_soft[0]
    qh, kf, lf, kb, lbw, iv, gs = _hgrn_in_call(xs, mods[1], norm_g[1, 0:1], hgrn_w_in[0].astype(BF16),
                                                lb, cb)
    o_f = _scan_call(qh, kf, lf, iv, c_len, reverse=False)
    o_b = _scan_call(qh, kb, lbw, iv, c_len, reverse=True)
    x_lat, h2, logits = _out_call("hgrn", (o_f, o_b, gs), xs, mods[1], norm_g[1],
                                  hgrn_w_o[0].astype(BF16), router_w[1], router_b[1].reshape(1, -1),
                                  hgrn_norm_g[0].reshape(1, -1), cb, cb)
    out = _moe(h2, logits, x_lat, mods[1], norm_g[1], 1, moe_w_gu, moe_b_gu, moe_w_dn, moe_b_dn, 0)
    return out[None]
```

```python
import functools

import jax
import jax.numpy as jnp
from jax import lax
from jax.experimental import pallas as pl
from jax.experimental.pallas import tpu as pltpu

D_MODEL = 1024
GRID_W = 64
RMS_EPS = 1e-6

ATTN_HEADS = 16
ATTN_KV_HEADS = 2
HEAD_DIM = 64
Q_DIM = ATTN_HEADS * HEAD_DIM
KV_DIM = ATTN_KV_HEADS * HEAD_DIM
QKV_DIM = Q_DIM + 2 * KV_DIM
WINDOW = 128
ATTN_BLOCK = 128
ROPE_BASE = 10000.0

HGRN_HEADS = 8
HGRN_DK = 128
HGRN_CHUNK = 64
HGRN_SUB = 8
HGRN_GROUP = 1

N_EXPERTS = 32
TOP_K = 4
D_FF = 1024
SWIGLU_LIMIT = 7.0
SWIGLU_ALPHA = 1.702

LANES = 128
ROW_BLOCK = 256
MOE_ROWS = 256
COMBINE_ROWS = 256
VMEM_LIMIT = 56 * 1024 * 1024

F32 = jnp.float32
BF16 = jnp.bfloat16
NEG_BIG = -1e30
LOG2E = 1.4426950408889634


def _cparams(sem, row_dma=False):
    return pltpu.CompilerParams(dimension_semantics=sem, vmem_limit_bytes=VMEM_LIMIT,
                                disable_bounds_checks=row_dma)


def _rms(x, g):
    return x * lax.rsqrt(jnp.mean(x * x, axis=-1, keepdims=True) + RMS_EPS) * g


def _sigmoid(x):
    return 1.0 / (1.0 + jnp.exp(-x))


def _silu(x):
    return x * (0.5 * jnp.tanh(0.5 * x) + 0.5)


SEGS = D_MODEL // LANES


def _store_tiled(ref, val):
    n = val.shape[0]
    for s in range(SEGS):
        ref[pl.ds(s, n, stride=SEGS), :] = val[:, s * LANES:(s + 1) * LANES]


def _load_tiled(ref, n):
    return jnp.concatenate([ref[pl.ds(s, n, stride=SEGS), :] for s in range(SEGS)], axis=1)


def _row_tile(ref, r):
    return ref.at[pl.ds(pl.multiple_of(r * SEGS, SEGS), SEGS), :]


def _ada_kernel(c_ref, w_ref, b_ref, o_ref):
    c = c_ref[...]
    s = c * _sigmoid(c)
    o_ref[0] = jnp.dot(s, w_ref[0], precision=lax.Precision.HIGHEST,
                       preferred_element_type=F32) + b_ref[0]


def _ada_call(cs, ada_w, ada_b):
    depth, d, n = ada_w.shape
    tn = 1024
    return pl.pallas_call(
        _ada_kernel,
        out_shape=jax.ShapeDtypeStruct((depth, 8, n), F32),
        grid=(depth, n // tn),
        in_specs=[
            pl.BlockSpec((8, d), lambda i, j: (0, 0)),
            pl.BlockSpec((1, d, tn), lambda i, j: (i, 0, j)),
            pl.BlockSpec((1, 1, tn), lambda i, j: (i, 0, j)),
        ],
        out_specs=pl.BlockSpec((1, 8, tn), lambda i, j: (i, 0, j)),
        compiler_params=_cparams(("arbitrary", "arbitrary")),
        name="adaln",
    )(cs, ada_w, ada_b.reshape(depth, 1, n))


def _swap16(t):
    lane = lax.broadcasted_iota(jnp.int32, t.shape, 1)
    return jnp.where(lane % 32 < 16, pltpu.roll(t, LANES - 16, 1), pltpu.roll(t, 16, 1))


def _stream_rows(c_ref, x_ref, cb):
    return jnp.where(pl.program_id(0) < cb, c_ref[...], x_ref[...])


def _stream_specs(tm, d, cb):
    return [pl.BlockSpec((tm, d), lambda i: (jnp.minimum(i, cb - 1), 0)),
            pl.BlockSpec((tm, d), lambda i: (jnp.maximum(i - cb, 0), 0))]


def _qkv_kernel(c_ref, x_ref, mod_ref, gn_ref, w_ref, b_ref, cos_ref, sin_ref, q_ref, k4_ref, v4_ref, *, cb):
    x = _stream_rows(c_ref, x_ref, cb)
    h = _rms(x, gn_ref[...]) * (1.0 + mod_ref[0, 1:2, :]) + mod_ref[0, 0:1, :]
    hb = h.astype(BF16)
    cos = cos_ref[...]
    sin = sin_ref[...]
    nq = Q_DIM // LANES
    for j in range(nq + 1):
        sl = slice(j * LANES, (j + 1) * LANES)
        t = jnp.dot(hb, w_ref[:, sl], preferred_element_type=F32) + b_ref[:, sl]
        t = t * cos + _swap16(t) * sin
        if j < nq:
            q_ref[:, sl] = (t * (HEAD_DIM ** -0.5 * LOG2E)).astype(q_ref.dtype)
        else:
            kt = t
    sl = slice(Q_DIM + KV_DIM, QKV_DIM)
    vt = jnp.dot(hb, w_ref[:, sl], preferred_element_type=F32) + b_ref[:, sl]
    lo = lax.broadcasted_iota(jnp.int32, kt.shape, 1) < HEAD_DIM
    for t, ref in ((kt, k4_ref), (vt, v4_ref)):
        sw = pltpu.roll(t, HEAD_DIM, 1)
        ref[:, 0 * LANES:1 * LANES] = jnp.where(lo, t, 0.0).astype(ref.dtype)
        ref[:, 1 * LANES:2 * LANES] = jnp.where(lo, 0.0, sw).astype(ref.dtype)
        ref[:, 2 * LANES:3 * LANES] = jnp.where(lo, sw, 0.0).astype(ref.dtype)
        ref[:, 3 * LANES:4 * LANES] = jnp.where(lo, 0.0, t).astype(ref.dtype)


def _qkv_call(ctx, x, mod, gn, w, b, cos, sin, cb):
    d = x.shape[1]
    t = ctx.shape[0] + x.shape[0]
    tm = ROW_BLOCK
    sel = lambda i: (jnp.minimum(i // cb, 1), 0, 0)
    return pl.pallas_call(
        functools.partial(_qkv_kernel, cb=cb),
        out_shape=(jax.ShapeDtypeStruct((t, Q_DIM), BF16),
                   jax.ShapeDtypeStruct((t, 4 * LANES), BF16),
                   jax.ShapeDtypeStruct((t, 4 * LANES), BF16)),
        grid=(t // tm,),
        in_specs=_stream_specs(tm, d, cb) + [
            pl.BlockSpec((1, 6, d), sel),
            pl.BlockSpec((1, d), lambda i: (0, 0)),
            pl.BlockSpec((d, QKV_DIM), lambda i: (0, 0)),
            pl.BlockSpec((1, QKV_DIM), lambda i: (0, 0)),
            pl.BlockSpec((tm, LANES), lambda i: (i, 0)),
            pl.BlockSpec((tm, LANES), lambda i: (i, 0)),
        ],
        out_specs=(pl.BlockSpec((tm, Q_DIM), lambda i: (i, 0)),
                   pl.BlockSpec((tm, 4 * LANES), lambda i: (i, 0)),
                   pl.BlockSpec((tm, 4 * LANES), lambda i: (i, 0))),
        compiler_params=_cparams(("parallel",)),
        name="qkv_rope",
    )(ctx, x, mod, gn, w, b, cos, sin)


def _attn_kernel(sink_ref, q_ref, kp_ref, kc_ref, kn_ref, kx_ref, vp_ref, vc_ref, vn_ref, vx_ref,
                 o_ref, *, cb, n_lat):
    n = pl.program_id(0)
    blk = ATTN_BLOCK
    c = kx_ref.shape[0]
    nw = 3 * blk
    s = lax.broadcasted_iota(jnp.int32, (nw + c, blk), 0)
    r = lax.broadcasted_iota(jnp.int32, (nw + c, blk), 1)
    q_pos = (n - cb) * blk + r
    k_pos = (n - cb - 1) * blk + s
    win_ok = (jnp.abs(q_pos - k_pos) <= WINDOW) & (k_pos >= 0) & (k_pos < n_lat) & (n >= cb)
    valid = (s >= nw) | win_ok
    k_all = jnp.concatenate([kp_ref[...], kc_ref[...], kn_ref[...], kx_ref[...]], axis=0)
    v_all = jnp.concatenate([vp_ref[...], vc_ref[...], vn_ref[...], vx_ref[...]], axis=0)
    n_var = 2 * ATTN_KV_HEADS
    v_t = [v_all[:, j * LANES:(j + 1) * LANES].astype(F32).T.astype(BF16) for j in range(n_var)]
    nt = (((1,), (1,)), ((), ()))
    group = ATTN_HEADS // ATTN_KV_HEADS
    for p in range(ATTN_HEADS // 2):
        g = (2 * p) // group
        qp = q_ref[:, p * LANES:(p + 1) * LANES]
        o_t = jnp.zeros((LANES, blk), F32)
        for half in range(2):
            j = 2 * g + half
            sc = lax.dot_general(k_all[:, j * LANES:(j + 1) * LANES], qp, nt,
                                 preferred_element_type=F32)
            sc = jnp.where(valid, sc, NEG_BIG)
            sink = sink_ref[2 * p + half] * LOG2E
            m = jnp.maximum(jnp.max(sc, axis=0, keepdims=True), sink)
            e = jnp.exp2(sc - m)
            denom = jnp.sum(e, axis=0, keepdims=True) + jnp.exp2(sink - m)
            pv = jnp.dot(v_t[j], e.astype(BF16), preferred_element_type=F32)
            o_t = o_t + pv / denom
        o_ref[:, p * LANES:(p + 1) * LANES] = o_t.T.astype(o_ref.dtype)


def _attn_call(sink, q, k4, v4, c_len):
    t = q.shape[0]
    blk = ATTN_BLOCK
    cb = c_len // blk
    nb = t // blk
    n_lat = t - c_len
    last = nb - 1
    kw = 4 * LANES
    spec_q = pl.BlockSpec((blk, Q_DIM), lambda n: (n, 0))
    prev = pl.BlockSpec((blk, kw), lambda n: (jnp.maximum(n - 1, 0), 0))
    cur = pl.BlockSpec((blk, kw), lambda n: (n, 0))
    nxt = pl.BlockSpec((blk, kw), lambda n: (jnp.minimum(n + 1, last), 0))
    ctx = pl.BlockSpec((c_len, kw), lambda n: (0, 0))
    return pl.pallas_call(
        functools.partial(_attn_kernel, cb=cb, n_lat=n_lat),
        out_shape=jax.ShapeDtypeStruct((t, Q_DIM), BF16),
        grid=(nb,),
        in_specs=[pl.BlockSpec(memory_space=pltpu.SMEM), spec_q,
                  prev, cur, nxt, ctx, prev, cur, nxt, ctx],
        out_specs=pl.BlockSpec((blk, Q_DIM), lambda n: (n, 0)),
        compiler_params=_cparams(("parallel",)),
        name="window_attn",
    )(sink, q, k4, k4, k4, k4, v4, v4, v4, v4)


def _residual_tail(y, x, mod_ref, gn_ref, rw_ref, rb_ref, xo_ref, h2_ref, lg_ref):
    x_new = x + mod_ref[0, 2:3, :] * _rms(y, gn_ref[1:2, :])
    h2 = _rms(x_new, gn_ref[2:3, :]) * (1.0 + mod_ref[0, 4:5, :]) + mod_ref[0, 3:4, :]
    xo_ref[...] = x_new
    _store_tiled(h2_ref, h2)
    hi, mid, lo = _split3(h2)
    lane = lax.broadcasted_iota(jnp.int32, (h2.shape[0], LANES), 1)
    w = rw_ref[...]
    r = (jnp.dot(hi, w, preferred_element_type=F32)
         + jnp.where(lane < 2 * N_EXPERTS, jnp.dot(mid, w, preferred_element_type=F32), 0.0)
         + jnp.where(lane < N_EXPERTS, jnp.dot(lo, w, preferred_element_type=F32), 0.0))
    r = r + pltpu.roll(r, LANES - N_EXPERTS, 1) + pltpu.roll(r, LANES - 2 * N_EXPERTS, 1)
    lg_ref[...] = r[:, :N_EXPERTS] + rb_ref[...]


def _attn_out_kernel(a_ref, c_ref, x_ref, mod_ref, gn_ref, w_ref, rw_ref, rb_ref, xo_ref, h2_ref, lg_ref,
                     *, cb):
    y = jnp.dot(a_ref[...], w_ref[...], preferred_element_type=F32)
    _residual_tail(y, _stream_rows(c_ref, x_ref, cb), mod_ref, gn_ref, rw_ref, rb_ref, xo_ref, h2_ref, lg_ref)


def _hgrn_out_kernel(of_ref, ob_ref, gs_ref, hg_ref, x_ref, mod_ref, gn_ref, w_ref, rw_ref, rb_ref,
                     xo_ref, h2_ref, lg_ref):
    parts = []
    for h in range(HGRN_HEADS):
        sl = slice(h * HGRN_DK, (h + 1) * HGRN_DK)
        o = of_ref[:, sl] + ob_ref[:, sl]
        parts.append(_rms(o, hg_ref[:, sl]))
    a = jnp.concatenate(parts, axis=1) * gs_ref[...].astype(F32)
    y = jnp.dot(a.astype(BF16), w_ref[...], preferred_element_type=F32)
    _residual_tail(y, x_ref[...], mod_ref, gn_ref, rw_ref, rb_ref, xo_ref, h2_ref, lg_ref)


def _out_call(kind, acts, x, mod, gn, w, rw, rb, hg, cb, off):
    t_out = (x[0].shape[0] + x[1].shape[0]) if kind == "attn" else x.shape[0] - off * ROW_BLOCK
    d = D_MODEL
    tm = ROW_BLOCK
    w_hi, w_mid, w_lo = _split3(rw)
    rw = jnp.concatenate([w_hi, w_mid, w_lo, jnp.zeros_like(w_hi)], axis=1)
    row = lambda i: (i + off, 0)
    sel = lambda i: (jnp.minimum((i + off) // cb, 1), 0, 0)
    const = lambda i: (0, 0)
    common_specs = [
        pl.BlockSpec((1, 6, d), sel),
        pl.BlockSpec((4, d), const),
        pl.BlockSpec((d, d), const),
        pl.BlockSpec((d, LANES), const),
        pl.BlockSpec((1, N_EXPERTS), const),
    ]
    if kind == "attn":
        body = functools.partial(_attn_out_kernel, cb=cb)
        in_specs = [pl.BlockSpec((tm, d), row)] + _stream_specs(tm, d, cb) + common_specs
        args = (acts[0], x[0], x[1], mod, gn, w, rw, rb)
    else:
        body = _hgrn_out_kernel
        in_specs = ([pl.BlockSpec((tm, d), row)] * 3 + [pl.BlockSpec((1, d), const)]
                    + [pl.BlockSpec((tm, d), row)] + common_specs)
        args = (acts[0], acts[1], acts[2], hg, x, mod, gn, w, rw, rb)
    return pl.pallas_call(
        body,
        out_shape=(jax.ShapeDtypeStruct((t_out, d), F32),
                   jax.ShapeDtypeStruct((t_out * SEGS, LANES), F32),
                   jax.ShapeDtypeStruct((t_out, N_EXPERTS), F32)),
        grid=(t_out // tm,),
        in_specs=in_specs,
        out_specs=(pl.BlockSpec((tm, d), lambda i: (i, 0)),
                   pl.BlockSpec((tm * SEGS, LANES), lambda i: (i, 0)),
                   pl.BlockSpec((tm, N_EXPERTS), lambda i: (i, 0))),
        compiler_params=_cparams(("parallel",)),
        name=kind + "_out_residual",
    )(*args)


def _router_kernel(lg_ref, meta_ref, cnt_ref, run_ref):
    i = pl.program_id(0)

    @pl.when(i == 0)
    def _():
        run_ref[...] = jnp.zeros_like(run_ref)

    lg = lg_ref[...].T
    ne, tb = lg.shape
    eid = lax.broadcasted_iota(jnp.int32, lg.shape, 0).astype(F32)
    sels, tops, idxs = [], [], []
    for _ in range(TOP_K):
        m = jnp.max(lg, axis=0, keepdims=True)
        idx = jnp.min(jnp.where(lg == m, eid, float(N_EXPERTS)), axis=0, keepdims=True)
        sel = eid == idx
        sels.append(sel)
        tops.append(m)
        idxs.append(idx)
        lg = jnp.where(sel, -jnp.inf, lg)
    ws = [jnp.exp(tk - tops[0]) for tk in tops]
    wsum = ws[0] + ws[1] + ws[2] + ws[3]
    chosen = (sels[0] | sels[1] | sels[2] | sels[3])
    onehot = jnp.where(chosen, 1.0, 0.0)
    rr = lax.broadcasted_iota(jnp.int32, (tb, tb), 0)
    cc = lax.broadcasted_iota(jnp.int32, (tb, tb), 1)
    tri = jnp.where(rr < cc, 1.0, 0.0).astype(BF16)
    before = jnp.dot(onehot.astype(BF16), tri, preferred_element_type=F32) + run_ref[:, 0:1]
    rows = list(idxs)
    rows += [jnp.sum(jnp.where(sels[k], before, 0.0), axis=0, keepdims=True) for k in range(TOP_K)]
    rows += [ws[k] / wsum for k in range(TOP_K)]
    rows.append(jnp.zeros((meta_ref.shape[0] - len(rows), tb), F32))
    meta_ref[...] = jnp.concatenate(rows, axis=0)
    run_ref[...] = run_ref[...] + jnp.sum(onehot, axis=1, keepdims=True)
    cnt_ref[...] = run_ref[...]


ROUTE_ROWS = 16


def _router_call(logits):
    t = logits.shape[0]
    tb = ROW_BLOCK
    meta, counts = pl.pallas_call(
        _router_kernel,
        out_shape=(jax.ShapeDtypeStruct((ROUTE_ROWS, t), F32),
                   jax.ShapeDtypeStruct((N_EXPERTS, LANES), F32)),
        grid=(t // tb,),
        in_specs=[pl.BlockSpec((tb, N_EXPERTS), lambda i: (i, 0))],
        out_specs=(pl.BlockSpec((ROUTE_ROWS, tb), lambda i: (0, i)),
                   pl.BlockSpec((N_EXPERTS, LANES), lambda i: (0, 0))),
        scratch_shapes=[pltpu.VMEM((N_EXPERTS, LANES), F32)],
        compiler_params=_cparams(("arbitrary",)),
        name="router_topk",
    )(logits)
    return meta, counts[:, 0]


def _expert_kernel(be_ref, nu_ref, nx_ref, ws_ref, src_first_ref, src_b_ref, src_c_ref, x_hbm, wgu_hbm, bgu_a_ref,
                   bgu_b_ref, wdn_hbm, bdn_a_ref, bdn_b_ref, o_ref, xbuf, sems, wgu_f, wdn_f, wsems, wgu_bf, wdn_bf,
                   *, layer):
    i = pl.program_id(0)
    rows = o_ref.shape[0] // (2 * SEGS)
    n_used = nu_ref[0]

    def gather(idx_ref, s):
        def issue(r2, carry):
            for p in range(2):
                r = 2 * r2 + p
                pltpu.make_async_copy(_row_tile(x_hbm, idx_ref[0, 0, r]), _row_tile(xbuf.at[s], r),
                                      sems.at[s]).start()
            return carry

        lax.fori_loop(0, rows // 2, issue, 0, unroll=4)

    def weight_copies(e, s):
        return (pltpu.make_async_copy(wgu_hbm.at[layer, e], wgu_f.at[s], wsems.at[s, 0]),
                pltpu.make_async_copy(wdn_hbm.at[layer, e], wdn_f.at[s], wsems.at[s, 1]))

    @pl.when(i == 0)
    def _():
        gather(src_first_ref, 0)
        for c in weight_copies(be_ref[0], ws_ref[0]):
            c.start(priority=1)

    for s, next_idx_ref, bgu_ref, bdn_ref in ((0, src_b_ref, bgu_a_ref, bdn_a_ref),
                                              (1, src_c_ref, bgu_b_ref, bdn_b_ref)):
        blk = 2 * i + s
        out = o_ref.at[pl.ds(s * rows * SEGS, rows * SEGS), :]

        @pl.when(blk + 1 < n_used)
        def _():
            gather(next_idx_ref, 1 - s)

        e_cur = be_ref[blk]
        wslot = ws_ref[blk]
        fresh = (blk == 0) | (e_cur != be_ref[jnp.maximum(blk - 1, 0)])

        @pl.when(fresh)
        def _():
            @pl.when(nx_ref[blk] != e_cur)
            def _():
                for c in weight_copies(nx_ref[blk], 1 - wslot):
                    c.start(priority=1)

            for c in weight_copies(e_cur, wslot):
                c.wait()
            wgu_bf[...] = wgu_f[wslot].astype(BF16)
            wdn_bf[...] = wdn_f[wslot].astype(BF16)

        @pl.when(blk < n_used)
        def _():
            pltpu.make_async_copy(x_hbm.at[pl.ds(0, rows * SEGS), :], xbuf.at[s], sems.at[s]).wait()
            xb = _load_tiled(xbuf.at[s], rows).astype(BF16)
            gu = jnp.dot(xb, wgu_bf[...], preferred_element_type=F32) + bgu_ref[...]
            g = jnp.minimum(gu[:, :D_FF], SWIGLU_LIMIT)
            u = jnp.clip(gu[:, D_FF:], -SWIGLU_LIMIT, SWIGLU_LIMIT)
            hdn = g * _sigmoid(SWIGLU_ALPHA * g) * (u + 1.0)
            y = jnp.dot(hdn.astype(BF16), wdn_bf[...], preferred_element_type=F32) + bdn_ref[...]
            _store_tiled(out, y)

        @pl.when(blk >= n_used)
        def _():
            out[...] = jnp.zeros(out.shape, F32)


def _expert_call(blk_e, n_used, src_tok, h2, layer, w_gu, b_gu, w_dn, b_dn):
    d = D_MODEL
    tr = MOE_ROWS
    n_blk = blk_e.shape[0]
    depth, e, _, f2 = w_gu.shape
    last = n_blk - 1
    nxt = jnp.min(jnp.where(blk_e[None, :] > blk_e[:, None], blk_e[None, :], N_EXPERTS), axis=1)
    nxt = jnp.where(nxt == N_EXPERTS, blk_e, nxt).astype(jnp.int32)
    change = jnp.concatenate([jnp.zeros((1,), jnp.int32), (blk_e[1:] != blk_e[:-1]).astype(jnp.int32)])
    wslot = (jnp.cumsum(change) % 2).astype(jnp.int32)
    src3 = src_tok.reshape(n_blk, 1, tr)
    smem_idx = lambda f: pl.BlockSpec((1, 1, tr), f, memory_space=pltpu.SMEM)
    bias = lambda width, s: pl.BlockSpec((None, None, 1, width), lambda i, be, *_: (layer, be[2 * i + s], 0, 0))
    return pl.pallas_call(
        functools.partial(_expert_kernel, layer=layer),
        out_shape=jax.ShapeDtypeStruct((n_blk * tr * SEGS, LANES), F32),
        grid_spec=pltpu.PrefetchScalarGridSpec(
            num_scalar_prefetch=4,
            grid=(n_blk // 2,),
            in_specs=[
                smem_idx(lambda i, *_: (0, 0, 0)),
                smem_idx(lambda i, *_: (2 * i + 1, 0, 0)),
                smem_idx(lambda i, *_: (jnp.minimum(2 * i + 2, last), 0, 0)),
                pl.BlockSpec(memory_space=pl.ANY),
                pl.BlockSpec(memory_space=pl.ANY),
                bias(f2, 0), bias(f2, 1),
                pl.BlockSpec(memory_space=pl.ANY),
                bias(d, 0), bias(d, 1),
            ],
            out_specs=pl.BlockSpec((2 * tr * SEGS, LANES), lambda i, *_: (i, 0)),
            scratch_shapes=[pltpu.VMEM((2, tr * SEGS, LANES), F32), pltpu.SemaphoreType.DMA((2,)),
                            pltpu.VMEM((2, d, f2), F32), pltpu.VMEM((2, D_FF, d), F32),
                            pltpu.SemaphoreType.DMA((2, 2)),
                            pltpu.VMEM((d, f2), BF16), pltpu.VMEM((D_FF, d), BF16)],
        ),
        compiler_params=_cparams(("arbitrary",), row_dma=True),
        name="moe_experts",
    )(blk_e, n_used, nxt, wslot, src3, src3, src3, h2,
      w_gu, b_gu.reshape(depth, e, 1, f2), b_gu.reshape(depth, e, 1, f2),
      w_dn, b_dn.reshape(depth, e, 1, d), b_dn.reshape(depth, e, 1, d))


def _combine_kernel(d0_ref, d1_ref, y_hbm, meta_ref, x_ref, mod_ref, gn_ref, o_ref, buf, sems):
    i = pl.program_id(0)
    tb = o_ref.shape[0]
    slot = i % 2

    def gather(idx_ref, s):
        def issue(r, carry):
            for k in range(TOP_K):
                row = idx_ref[0, 0, k * tb + r]
                pltpu.make_async_copy(_row_tile(y_hbm, row), _row_tile(buf.at[s, k], r),
                                      sems.at[s]).start(priority=k % 2)
            return carry

        lax.fori_loop(0, tb, issue, 0, unroll=4)

    @pl.when(i == 0)
    def _():
        gather(d0_ref, 0)

    @pl.when(i + 1 < pl.num_programs(0))
    def _():
        gather(d1_ref, 1 - slot)

    for k in range(TOP_K):
        pltpu.make_async_copy(y_hbm.at[pl.ds(0, tb * SEGS), :], buf.at[slot, k], sems.at[slot]).wait()
    gates = meta_ref[...].T
    y2 = jnp.zeros(o_ref.shape, F32)
    for k in range(TOP_K):
        y2 = y2 + _load_tiled(buf.at[slot, k], tb) * gates[:, 2 * TOP_K + k:2 * TOP_K + k + 1]
    o_ref[...] = x_ref[...] + mod_ref[0, 5:6, :] * _rms(y2, gn_ref[3:4, :])


def _combine_call(dest, y_sorted, meta, x, mod, gn, cb_rows):
    t, d = x.shape
    tb = COMBINE_ROWS
    nb = t // tb
    cbb = max(cb_rows // tb, 1)
    sel = (lambda i: (jnp.minimum(i // cbb, 1), 0, 0)) if cb_rows else (lambda i: (1, 0, 0))
    dest3 = dest.reshape(TOP_K, nb, tb).transpose(1, 0, 2).reshape(nb, 1, TOP_K * tb)
    return pl.pallas_call(
        _combine_kernel,
        out_shape=jax.ShapeDtypeStruct((t, d), F32),
        grid=(nb,),
        in_specs=[
            pl.BlockSpec((1, 1, tb * TOP_K), lambda i: (0, 0, 0), memory_space=pltpu.SMEM),
            pl.BlockSpec((1, 1, tb * TOP_K), lambda i: (jnp.minimum(i + 1, nb - 1), 0, 0),
                         memory_space=pltpu.SMEM),
            pl.BlockSpec(memory_space=pl.ANY),
            pl.BlockSpec((ROUTE_ROWS, tb), lambda i: (0, i)),
            pl.BlockSpec((tb, d), lambda i: (i, 0)),
            pl.BlockSpec((1, 6, d), sel),
            pl.BlockSpec((4, d), lambda i: (0, 0)),
        ],
        out_specs=pl.BlockSpec((tb, d), lambda i: (i, 0)),
        scratch_shapes=[pltpu.VMEM((2, TOP_K, tb * SEGS, LANES), F32), pltpu.SemaphoreType.DMA((2,))],
        compiler_params=_cparams(("arbitrary",), row_dma=True),
        name="moe_combine",
    )(dest3, dest3, y_sorted, meta, x, mod, gn)


INVERT_BLOCK = 2048


def _invert_kernel(dest_ref, zeros_hbm, src_ref):
    k, j = pl.program_id(0), pl.program_id(1)
    n = dest_ref.shape[2]

    @pl.when((k == 0) & (j == 0))
    def _():
        pltpu.sync_copy(zeros_hbm, src_ref)

    tok0 = j * n

    def put(t, carry):
        src_ref[dest_ref[0, 0, t]] = tok0 + t
        return carry

    lax.fori_loop(0, n, put, 0, unroll=32)


def _invert_call(dest, n_rows):
    t = dest.shape[1]
    blk = max(b for b in range(LANES, INVERT_BLOCK + 1, LANES) if t % b == 0)
    nb = t // blk
    return pl.pallas_call(
        _invert_kernel,
        out_shape=jax.ShapeDtypeStruct((n_rows,), jnp.int32),
        grid=(TOP_K, nb),
        in_specs=[pl.BlockSpec((1, 1, blk), lambda k, j: (k * nb + j, 0, 0), memory_space=pltpu.SMEM),
                  pl.BlockSpec(memory_space=pl.ANY)],
        out_specs=pl.BlockSpec(memory_space=pltpu.SMEM),
        compiler_params=_cparams(("arbitrary", "arbitrary")),
        name="moe_invert",
    )(dest.reshape(TOP_K * nb, 1, blk), jnp.zeros((n_rows,), jnp.int32))


def _moe(h2, logits, x, mod, gn, layer, w_gu, b_gu, w_dn, b_dn, cb_rows):
    t = logits.shape[0]
    tr = MOE_ROWS
    meta, counts = _router_call(logits)
    e_idx = meta[:TOP_K].astype(jnp.int32)
    rank = meta[TOP_K:2 * TOP_K].astype(jnp.int32)
    counts = counts.astype(jnp.int32)
    padded = (counts + tr - 1) // tr * tr
    p_end = jnp.cumsum(padded)
    p_start = p_end - padded
    dest = rank
    for e in range(N_EXPERTS):
        dest = dest + jnp.where(e_idx == e, p_start[e], 0)
    a = t * TOP_K
    n_blk = (a + N_EXPERTS * (tr - 1) + tr - 1) // tr
    n_blk += n_blk % 2
    src_tok = _invert_call(dest, n_blk * tr)
    blk_start = jnp.arange(n_blk, dtype=jnp.int32) * tr
    blk_e = jnp.sum((p_end[None, :] <= blk_start[:, None]).astype(jnp.int32), axis=1)
    blk_e = jnp.minimum(blk_e, N_EXPERTS - 1)
    n_used = (p_end[-1] // tr).astype(jnp.int32).reshape(1)
    ys = _expert_call(blk_e, n_used, src_tok, h2, layer, w_gu, b_gu, w_dn, b_dn)
    return _combine_call(dest, ys, meta, x, mod, gn, cb_rows)


def _hgrn_in_kernel(x_ref, mod_ref, gn_ref, w_ref, lb_ref, q_ref, kf_ref, lf_ref, kb_ref, lbw_ref,
                    i_ref, gs_ref):
    x = x_ref[...]
    h = _rms(x, gn_ref[...]) * (1.0 + mod_ref[0, 1:2, :]) + mod_ref[0, 0:1, :]
    hb = h.astype(BF16)
    f = D_MODEL

    def proj(sec):
        return jnp.dot(hb, w_ref[:, sec * f:(sec + 1) * f], preferred_element_type=F32)

    z = proj(0)
    q_ref[...] = _silu(z).astype(q_ref.dtype)
    for sec, k_ref, l_ref in ((1, kf_ref, lf_ref), (2, kb_ref, lbw_ref)):
        z = proj(sec)
        lb = lb_ref[sec - 1:sec, :]
        sg = _sigmoid(z)
        l_ref[...] = jnp.log(lb + (1.0 - lb) * sg)
        k_ref[...] = ((1.0 - lb) * (1.0 - sg)).astype(k_ref.dtype)
    i_ref[...] = proj(3).astype(i_ref.dtype)
    z = proj(4)
    gs_ref[...] = _silu(z).astype(gs_ref.dtype)


def _hgrn_in_call(xs, mod, gn, w, lb, cb):
    t, d = xs.shape
    tm = ROW_BLOCK
    sel = lambda i: (jnp.minimum(i // cb, 1), 0, 0)
    row = pl.BlockSpec((tm, d), lambda i: (i, 0))
    dts = (BF16, BF16, F32, BF16, F32, BF16, BF16)
    return pl.pallas_call(
        _hgrn_in_kernel,
        out_shape=tuple(jax.ShapeDtypeStruct((t, d), dt) for dt in dts),
        grid=(t // tm,),
        in_specs=[row, pl.BlockSpec((1, 6, d), sel), pl.BlockSpec((1, d), lambda i: (0, 0)),
                  pl.BlockSpec(w.shape, lambda i: (0, 0)), pl.BlockSpec((2, d), lambda i: (0, 0))],
        out_specs=tuple(row for _ in dts),
        compiler_params=_cparams(("parallel",)),
        name="hgrn_in_proj",
    )(xs, mod, gn, w, lb)


def _split3(x):
    hi = x.astype(BF16)
    r1 = x - hi.astype(F32)
    mid = r1.astype(BF16)
    lo = (r1 - mid.astype(F32)).astype(BF16)
    return hi, mid, lo


def _scan_kernel(q_ref, k_ref, g_ref, v_ref, o_ref, st_ref, gcum_ref, *, reverse):
    j = pl.program_id(0)
    ch, sub = HGRN_CHUNK, HGRN_SUB
    ns = ch // sub
    dk = HGRN_DK

    @pl.when(j == 0)
    def _():
        st_ref[...] = jnp.zeros_like(st_ref)

    rr = lax.broadcasted_iota(jnp.int32, (ch, ch), 0)
    cc = lax.broadcasted_iota(jnp.int32, (ch, ch), 1)
    causal = (cc >= rr) if reverse else (cc <= rr)
    tri = jnp.where(causal, 1.0, 0.0).astype(BF16)
    hi, mid, lo = _split3(g_ref[...])
    gcum_ref[...] = LOG2E * (jnp.dot(tri, hi, preferred_element_type=F32)
                             + jnp.dot(tri, mid, preferred_element_type=F32)
                             + jnp.dot(tri, lo, preferred_element_type=F32))
    key_blocks = list(range(1, ns)) if reverse else list(range(ns - 1))
    nkb = len(key_blocks)
    kbd_mask = (lax.broadcasted_iota(jnp.int32, (ch, nkb * dk), 1) // dk + key_blocks[0]
                == lax.broadcasted_iota(jnp.int32, (ch, nkb * dk), 0) // sub)
    kdg_mask = (lax.broadcasted_iota(jnp.int32, (ch, sub * dk), 1) // dk
                == lax.broadcasted_iota(jnp.int32, (ch, sub * dk), 0) % sub)
    end_row = 0 if reverse else ch - 1
    edge = 0 if reverse else sub - 1
    nt = (((1,), (1,)), ((), ()))
    tn = (((0,), (0,)), ((), ()))

    grp = HGRN_GROUP
    rr_g = lax.broadcasted_iota(jnp.int32, (ch, grp * ch), 0)
    cc_g = lax.broadcasted_iota(jnp.int32, (ch, grp * ch), 1) % ch
    diag_mask_g = ((rr_g // sub) == (cc_g // sub)) & ((cc_g >= rr_g) if reverse else (cc_g <= rr_g))

    def place(x, i):
        w = x.shape[1]
        parts = []
        if i:
            parts.append(jnp.zeros((x.shape[0], i * w), x.dtype))
        parts.append(x)
        if grp - 1 - i:
            parts.append(jnp.zeros((x.shape[0], (grp - 1 - i) * w), x.dtype))
        return jnp.concatenate(parts, axis=1)

    for g0 in range(0, HGRN_HEADS, grp):
        qgs, q_cats, p_cats, kbd_rows, kdg_rows, st_rows, v_rows = [], [], [], [], [], [], []
        for i in range(grp):
            h = g0 + i
            hs = slice(h * dk, (h + 1) * dk)
            q = q_ref[:, hs].astype(F32)
            kb = k_ref[:, hs]
            k = kb.astype(F32)
            v = v_ref[:, hs]
            gc = gcum_ref[:, hs]
            st = st_ref[h]
            g_end = gc[end_row:end_row + 1, :]
            qgs.append((q * jnp.exp2(gc)).astype(BF16))
            st_rows.append(place(st.astype(BF16), i))
            v_rows.append(place(v, i))
            k_end = (k * jnp.exp2(g_end - gc)).astype(BF16)
            st_ref[h] = st * jnp.exp2(g_end) + lax.dot_general(v, k_end, tn, preferred_element_type=F32)
            gc4 = gc.reshape(ns, sub, dk)
            g_edge = jnp.broadcast_to(gc4[:, edge:edge + 1, :], (ns, sub, dk)).reshape(ch, dk)
            k_rel = (k * jnp.exp2(g_edge - gc)).astype(BF16)
            k_bd = jnp.where(kbd_mask, jnp.concatenate([k_rel] * nkb, axis=1), jnp.zeros((), BF16))
            kbd_rows.append(place(k_bd, i))
            q_parts = []
            for jb in key_blocks:
                row = jb * sub + edge
                rows = slice(0, jb * sub) if reverse else slice((jb + 1) * sub, ch)
                part = q[rows] * jnp.exp2(gc[rows] - gc[row:row + 1, :])
                pad = jnp.zeros((ch - part.shape[0], dk), F32)
                full = jnp.concatenate([part, pad] if reverse else [pad, part], axis=0)
                q_parts.append(full.astype(BF16))
            q_cats.append(jnp.concatenate(q_parts, axis=1))
            p_parts = []
            for s in range(sub):
                g_s = jnp.broadcast_to(gc4[:, s:s + 1, :], (ns, sub, dk)).reshape(ch, dk)
                p_parts.append((q * jnp.exp2(jnp.minimum(gc - g_s, 0.0))).astype(BF16))
            p_cats.append(jnp.concatenate(p_parts, axis=1))
            k_dg = jnp.where(kdg_mask, jnp.concatenate([kb] * sub, axis=1), jnp.zeros((), BF16))
            kdg_rows.append(place(k_dg, i))
        s_off = lax.dot_general(jnp.concatenate(q_cats, axis=1), jnp.concatenate(kbd_rows, axis=0), nt,
                                preferred_element_type=F32)
        s_diag = lax.dot_general(jnp.concatenate(p_cats, axis=1), jnp.concatenate(kdg_rows, axis=0), nt,
                                 preferred_element_type=F32)
        a = s_off + jnp.where(diag_mask_g, s_diag, 0.0)
        o = lax.dot_general(jnp.concatenate(qgs, axis=1), jnp.concatenate(st_rows, axis=0), nt,
                            preferred_element_type=F32)
        o = o + jnp.dot(a.astype(BF16), jnp.concatenate(v_rows, axis=0), preferred_element_type=F32)
        o_ref[:, g0 * dk:(g0 + grp) * dk] = o


def _scan_call(q, k, lg, v, c_len, reverse):
    t, d = q.shape
    ch = HGRN_CHUNK
    n_ch = t // ch
    cc = c_len // ch
    if reverse:
        idx = lambda j: (jnp.where(j < cc, cc - 1 - j, n_ch - 1 - (j - cc)), 0)
    else:
        idx = lambda j: (j, 0)
    blk = pl.BlockSpec((ch, d), idx)
    return pl.pallas_call(
        functools.partial(_scan_kernel, reverse=reverse),
        out_shape=jax.ShapeDtypeStruct((t, d), F32),
        grid=(n_ch,),
        in_specs=[blk, blk, blk, blk],
        out_specs=blk,
        scratch_shapes=[pltpu.VMEM((HGRN_HEADS, HGRN_DK, HGRN_DK), F32), pltpu.VMEM((ch, d), F32)],
        compiler_params=_cparams(("arbitrary",)),
        name="hgrn_scan_bwd" if reverse else "hgrn_scan_fwd",
    )(q, k, lg, v)


def _rope_tables(l, c_len):
    n = HEAD_DIM // 4
    inv_freq = ROPE_BASE ** (-jnp.arange(n, dtype=F32) / n)
    rows = l // GRID_W
    ang_row = jnp.arange(rows, dtype=F32)[:, None] * inv_freq[None, :]
    ang_col = jnp.arange(GRID_W, dtype=F32)[:, None] * inv_freq[None, :]
    cr, sr, cc, sc = jnp.cos(ang_row), jnp.sin(ang_row), jnp.cos(ang_col), jnp.sin(ang_col)
    zr, zc = jnp.zeros_like(cr), jnp.zeros_like(cc)
    cos_r = jnp.concatenate([cr, cr, zr, zr] * 2, axis=1)
    sin_r = jnp.concatenate([-sr, sr, zr, zr] * 2, axis=1)
    cos_c = jnp.concatenate([zc, zc, cc, cc] * 2, axis=1)
    sin_c = jnp.concatenate([zc, zc, -sc, sc] * 2, axis=1)
    cos = (cos_r[:, None, :] + cos_c[None, :, :]).reshape(l, LANES)
    sin = (sin_r[:, None, :] + sin_c[None, :, :]).reshape(l, LANES)
    cos = jnp.concatenate([jnp.ones((c_len, LANES), F32), cos], axis=0)
    sin = jnp.concatenate([jnp.zeros((c_len, LANES), F32), sin], axis=0)
    return cos, sin


def kernel(x, c, ctx, c_ctx, ada_w, ada_b, norm_g, attn_w_qkv, attn_b_qkv, attn_sink, attn_w_o,
           hgrn_w_in, hgrn_lb, hgrn_norm_g, hgrn_w_o, router_w, router_b, moe_w_gu, moe_b_gu,
           moe_w_dn, moe_b_dn):
    b, l, d = x.shape
    c_len = ctx.shape[1]
    depth = ada_w.shape[0]
    assert b == 1 and d == D_MODEL and depth == 2
    assert c_len % ROW_BLOCK == 0 and l % ROW_BLOCK == 0
    cb = c_len // ROW_BLOCK

    cs = jnp.zeros((8, d), F32).at[0].set(c_ctx).at[1].set(c[0])
    mods = _ada_call(cs, ada_w, ada_b)
    mods = mods[:, :2].reshape(depth, 2, 6, d)

    cos, sin = _rope_tables(l, c_len)

    q, k4, v4 = _qkv_call(ctx[0], x[0], mods[0], norm_g[0, 0:1], attn_w_qkv[0].astype(BF16),
                          attn_b_qkv[0].reshape(1, -1), cos, sin, cb)
    o = _attn_call(attn_sink[0], q, k4, v4, c_len)
    xs, h2, logits = _out_call("attn", (o,), (ctx[0], x[0]), mods[0], norm_g[0], attn_w_o[0].astype(BF16),
                               router_w[0], router_b[0].reshape(1, -1), None, cb, 0)
    xs = _moe(h2, logits, xs, mods[0], norm_g[0], 0, moe_w_gu, moe_b_gu, moe_w_dn, moe_b_dn, c_len)

    lb_soft = jax.nn.softmax(hgrn_lb.astype(F32), axis=0)
    lb = jnp.cumsum(lb_soft, axis=0)[1] - lb_soft[0]
    qh, kf, lf, kb, lbw, iv, gs = _hgrn_in_call(xs, mods[1], norm_g[1, 0:1], hgrn_w_in[0].astype(BF16),
                                                lb, cb)
    o_f = _scan_call(qh, kf, lf, iv, c_len, reverse=False)
    o_b = _scan_call(qh, kb, lbw, iv, c_len, reverse=True)
    x_lat, h2, logits = _out_call("hgrn", (o_f, o_b, gs), xs, mods[1], norm_g[1],
                                  hgrn_w_o[0].astype(BF16), router_w[1], router_b[1].reshape(1, -1),
                                  hgrn_norm_g[0].reshape(1, -1), cb, cb)
    out = _moe(h2, logits, x_lat, mods[1], norm_g[1], 1, moe_w_gu, moe_b_gu, moe_w_dn, moe_b_dn, 0)
    return out[None]
```

```python
import functools

import jax
import jax.numpy as jnp
from jax import lax
from jax.experimental import pallas as pl
from jax.experimental.pallas import tpu as pltpu

D_MODEL = 1024
GRID_W = 64
RMS_EPS = 1e-6

ATTN_HEADS = 16
ATTN_KV_HEADS = 2
HEAD_DIM = 64
Q_DIM = ATTN_HEADS * HEAD_DIM
KV_DIM = ATTN_KV_HEADS * HEAD_DIM
QKV_DIM = Q_DIM + 2 * KV_DIM
WINDOW = 128
ATTN_BLOCK = 128
ROPE_BASE = 10000.0

HGRN_HEADS = 8
HGRN_DK = 128
HGRN_CHUNK = 64
HGRN_SUB = 8
HGRN_GROUP = 1

N_EXPERTS = 32
TOP_K = 4
D_FF = 1024
SWIGLU_LIMIT = 7.0
SWIGLU_ALPHA = 1.702

LANES = 128
ROW_BLOCK = 256
MOE_ROWS = 256
COMBINE_ROWS = 256
VMEM_LIMIT = 56 * 1024 * 1024

F32 = jnp.float32
BF16 = jnp.bfloat16
NEG_BIG = -1e30
LOG2E = 1.4426950408889634


def _cparams(sem, row_dma=False):
    return pltpu.CompilerParams(dimension_semantics=sem, vmem_limit_bytes=VMEM_LIMIT,
                                disable_bounds_checks=row_dma)


def _rms(x, g):
    return x * lax.rsqrt(jnp.mean(x * x, axis=-1, keepdims=True) + RMS_EPS) * g


def _sigmoid(x):
    return 1.0 / (1.0 + jnp.exp(-x))


def _silu(x):
    return x * (0.5 * jnp.tanh(0.5 * x) + 0.5)


SEGS = D_MODEL // LANES


def _store_tiled(ref, val):
    n = val.shape[0]
    for s in range(SEGS):
        ref[pl.ds(s, n, stride=SEGS), :] = val[:, s * LANES:(s + 1) * LANES]


def _load_tiled(ref, n):
    return jnp.concatenate([ref[pl.ds(s, n, stride=SEGS), :] for s in range(SEGS)], axis=1)


def _row_tile(ref, r):
    return ref.at[pl.ds(pl.multiple_of(r * SEGS, SEGS), SEGS), :]


def _ada_kernel(c_ref, w_ref, b_ref, o_ref):
    c = c_ref[...]
    s = c * _sigmoid(c)
    o_ref[0] = jnp.dot(s, w_ref[0], precision=lax.Precision.HIGHEST,
                       preferred_element_type=F32) + b_ref[0]


def _ada_call(cs, ada_w, ada_b):
    depth, d, n = ada_w.shape
    tn = 1024
    return pl.pallas_call(
        _ada_kernel,
        out_shape=jax.ShapeDtypeStruct((depth, 8, n), F32),
        grid=(depth, n // tn),
        in_specs=[
            pl.BlockSpec((8, d), lambda i, j: (0, 0)),
            pl.BlockSpec((1, d, tn), lambda i, j: (i, 0, j)),
            pl.BlockSpec((1, 1, tn), lambda i, j: (i, 0, j)),
        ],
        out_specs=pl.BlockSpec((1, 8, tn), lambda i, j: (i, 0, j)),
        compiler_params=_cparams(("arbitrary", "arbitrary")),
        name="adaln",
    )(cs, ada_w, ada_b.reshape(depth, 1, n))


def _swap16(t):
    lane = lax.broadcasted_iota(jnp.int32, t.shape, 1)
    return jnp.where(lane % 32 < 16, pltpu.roll(t, LANES - 16, 1), pltpu.roll(t, 16, 1))


def _stream_rows(c_ref, x_ref, cb):
    return jnp.where(pl.program_id(0) < cb, c_ref[...], x_ref[...])


def _stream_specs(tm, d, cb):
    return [pl.BlockSpec((tm, d), lambda i: (jnp.minimum(i, cb - 1), 0)),
            pl.BlockSpec((tm, d), lambda i: (jnp.maximum(i - cb, 0), 0))]


def _qkv_kernel(c_ref, x_ref, mod_ref, gn_ref, w_ref, b_ref, cos_ref, sin_ref, q_ref, k4_ref, v4_ref, *, cb):
    x = _stream_rows(c_ref, x_ref, cb)
    h = _rms(x, gn_ref[...]) * (1.0 + mod_ref[0, 1:2, :]) + mod_ref[0, 0:1, :]
    hb = h.astype(BF16)
    cos = cos_ref[...]
    sin = sin_ref[...]
    nq = Q_DIM // LANES
    for j in range(nq + 1):
        sl = slice(j * LANES, (j + 1) * LANES)
        t = jnp.dot(hb, w_ref[:, sl], preferred_element_type=F32) + b_ref[:, sl]
        t = t * cos + _swap16(t) * sin
        if j < nq:
            q_ref[:, sl] = (t * (HEAD_DIM ** -0.5 * LOG2E)).astype(q_ref.dtype)
        else:
            kt = t
    sl = slice(Q_DIM + KV_DIM, QKV_DIM)
    vt = jnp.dot(hb, w_ref[:, sl], preferred_element_type=F32) + b_ref[:, sl]
    lo = lax.broadcasted_iota(jnp.int32, kt.shape, 1) < HEAD_DIM
    for t, ref in ((kt, k4_ref), (vt, v4_ref)):
        sw = pltpu.roll(t, HEAD_DIM, 1)
        ref[:, 0 * LANES:1 * LANES] = jnp.where(lo, t, 0.0).astype(ref.dtype)
        ref[:, 1 * LANES:2 * LANES] = jnp.where(lo, 0.0, sw).astype(ref.dtype)
        ref[:, 2 * LANES:3 * LANES] = jnp.where(lo, sw, 0.0).astype(ref.dtype)
        ref[:, 3 * LANES:4 * LANES] = jnp.where(lo, 0.0, t).astype(ref.dtype)


def _qkv_call(ctx, x, mod, gn, w, b, cos, sin, cb):
    d = x.shape[1]
    t = ctx.shape[0] + x.shape[0]
    tm = ROW_BLOCK
    sel = lambda i: (jnp.minimum(i // cb, 1), 0, 0)
    return pl.pallas_call(
        functools.partial(_qkv_kernel, cb=cb),
        out_shape=(jax.ShapeDtypeStruct((t, Q_DIM), BF16),
                   jax.ShapeDtypeStruct((t, 4 * LANES), BF16),
                   jax.ShapeDtypeStruct((t, 4 * LANES), BF16)),
        grid=(t // tm,),
        in_specs=_stream_specs(tm, d, cb) + [
            pl.BlockSpec((1, 6, d), sel),
            pl.BlockSpec((1, d), lambda i: (0, 0)),
            pl.BlockSpec((d, QKV_DIM), lambda i: (0, 0)),
            pl.BlockSpec((1, QKV_DIM), lambda i: (0, 0)),
            pl.BlockSpec((tm, LANES), lambda i: (i, 0)),
            pl.BlockSpec((tm, LANES), lambda i: (i, 0)),
        ],
        out_specs=(pl.BlockSpec((tm, Q_DIM), lambda i: (i, 0)),
                   pl.BlockSpec((tm, 4 * LANES), lambda i: (i, 0)),
                   pl.BlockSpec((tm, 4 * LANES), lambda i: (i, 0))),
        compiler_params=_cparams(("parallel",)),
        name="qkv_rope",
    )(ctx, x, mod, gn, w, b, cos, sin)


def _attn_kernel(sink_ref, q_ref, kp_ref, kc_ref, kn_ref, kx_ref, vp_ref, vc_ref, vn_ref, vx_ref,
                 o_ref, *, cb, n_lat):
    n = pl.program_id(0)
    blk = ATTN_BLOCK
    c = kx_ref.shape[0]
    nw = 3 * blk
    s = lax.broadcasted_iota(jnp.int32, (nw + c, blk), 0)
    r = lax.broadcasted_iota(jnp.int32, (nw + c, blk), 1)
    q_pos = (n - cb) * blk + r
    k_pos = (n - cb - 1) * blk + s
    win_ok = (jnp.abs(q_pos - k_pos) <= WINDOW) & (k_pos >= 0) & (k_pos < n_lat) & (n >= cb)
    valid = (s >= nw) | win_ok
    k_all = jnp.concatenate([kp_ref[...], kc_ref[...], kn_ref[...], kx_ref[...]], axis=0)
    v_all = jnp.concatenate([vp_ref[...], vc_ref[...], vn_ref[...], vx_ref[...]], axis=0)
    n_var = 2 * ATTN_KV_HEADS
    v_t = [v_all[:, j * LANES:(j + 1) * LANES].astype(F32).T.astype(BF16) for j in range(n_var)]
    nt = (((1,), (1,)), ((), ()))
    group = ATTN_HEADS // ATTN_KV_HEADS
    for p in range(ATTN_HEADS // 2):
        g = (2 * p) // group
        qp = q_ref[:, p * LANES:(p + 1) * LANES]
        o_t = jnp.zeros((LANES, blk), F32)
        for half in range(2):
            j = 2 * g + half
            sc = lax.dot_general(k_all[:, j * LANES:(j + 1) * LANES], qp, nt,
                                 preferred_element_type=F32)
            sc = jnp.where(valid, sc, NEG_BIG)
            sink = sink_ref[2 * p + half] * LOG2E
            m = jnp.maximum(jnp.max(sc, axis=0, keepdims=True), sink)
            e = jnp.exp2(sc - m)
            denom = jnp.sum(e, axis=0, keepdims=True) + jnp.exp2(sink - m)
            pv = jnp.dot(v_t[j], e.astype(BF16), preferred_element_type=F32)
            o_t = o_t + pv / denom
        o_ref[:, p * LANES:(p + 1) * LANES] = o_t.T.astype(o_ref.dtype)


def _attn_call(sink, q, k4, v4, c_len):
    t = q.shape[0]
    blk = ATTN_BLOCK
    cb = c_len // blk
    nb = t // blk
    n_lat = t - c_len
    last = nb - 1
    kw = 4 * LANES
    spec_q = pl.BlockSpec((blk, Q_DIM), lambda n: (n, 0))
    prev = pl.BlockSpec((blk, kw), lambda n: (jnp.maximum(n - 1, 0), 0))
    cur = pl.BlockSpec((blk, kw), lambda n: (n, 0))
    nxt = pl.BlockSpec((blk, kw), lambda n: (jnp.minimum(n + 1, last), 0))
    ctx = pl.BlockSpec((c_len, kw), lambda n: (0, 0))
    return pl.pallas_call(
        functools.partial(_attn_kernel, cb=cb, n_lat=n_lat),
        out_shape=jax.ShapeDtypeStruct((t, Q_DIM), BF16),
        grid=(nb,),
        in_specs=[pl.BlockSpec(memory_space=pltpu.SMEM), spec_q,
                  prev, cur, nxt, ctx, prev, cur, nxt, ctx],
        out_specs=pl.BlockSpec((blk, Q_DIM), lambda n: (n, 0)),
        compiler_params=_cparams(("parallel",)),
        name="window_attn",
    )(sink, q, k4, k4, k4, k4, v4, v4, v4, v4)


def _residual_tail(y, x, mod_ref, gn_ref, rw_ref, rb_ref, xo_ref, h2_ref, lg_ref):
    x_new = x + mod_ref[0, 2:3, :] * _rms(y, gn_ref[1:2, :])
    h2 = _rms(x_new, gn_ref[2:3, :]) * (1.0 + mod_ref[0, 4:5, :]) + mod_ref[0, 3:4, :]
    xo_ref[...] = x_new
    _store_tiled(h2_ref, h2)
    hi, mid, lo = _split3(h2)
    lane = lax.broadcasted_iota(jnp.int32, (h2.shape[0], LANES), 1)
    w = rw_ref[...]
    r = (jnp.dot(hi, w, preferred_element_type=F32)
         + jnp.where(lane < 2 * N_EXPERTS, jnp.dot(mid, w, preferred_element_type=F32), 0.0)
         + jnp.where(lane < N_EXPERTS, jnp.dot(lo, w, preferred_element_type=F32), 0.0))
    r = r + pltpu.roll(r, LANES - N_EXPERTS, 1) + pltpu.roll(r, LANES - 2 * N_EXPERTS, 1)
    lg_ref[...] = r[:, :N_EXPERTS] + rb_ref[...]


def _attn_out_kernel(a_ref, c_ref, x_ref, mod_ref, gn_ref, w_ref, rw_ref, rb_ref, xo_ref, h2_ref, lg_ref,
                     *, cb):
    y = jnp.dot(a_ref[...], w_ref[...], preferred_element_type=F32)
    _residual_tail(y, _stream_rows(c_ref, x_ref, cb), mod_ref, gn_ref, rw_ref, rb_ref, xo_ref, h2_ref, lg_ref)


def _hgrn_out_kernel(of_ref, ob_ref, gs_ref, hg_ref, x_ref, mod_ref, gn_ref, w_ref, rw_ref, rb_ref,
                     xo_ref, h2_ref, lg_ref):
    parts = []
    for h in range(HGRN_HEADS):
        sl = slice(h * HGRN_DK, (h + 1) * HGRN_DK)
        o = of_ref[:, sl] + ob_ref[:, sl]
        parts.append(_rms(o, hg_ref[:, sl]))
    a = jnp.concatenate(parts, axis=1) * gs_ref[...].astype(F32)
    y = jnp.dot(a.astype(BF16), w_ref[...], preferred_element_type=F32)
    _residual_tail(y, x_ref[...], mod_ref, gn_ref, rw_ref, rb_ref, xo_ref, h2_ref, lg_ref)


def _out_call(kind, acts, x, mod, gn, w, rw, rb, hg, cb, off):
    t_out = (x[0].shape[0] + x[1].shape[0]) if kind == "attn" else x.shape[0] - off * ROW_BLOCK
    d = D_MODEL
    tm = ROW_BLOCK
    w_hi, w_mid, w_lo = _split3(rw)
    rw = jnp.concatenate([w_hi, w_mid, w_lo, jnp.zeros_like(w_hi)], axis=1)
    row = lambda i: (i + off, 0)
    sel = lambda i: (jnp.minimum((i + off) // cb, 1), 0, 0)
    const = lambda i: (0, 0)
    common_specs = [
        pl.BlockSpec((1, 6, d), sel),
        pl.BlockSpec((4, d), const),
        pl.BlockSpec((d, d), const),
        pl.BlockSpec((d, LANES), const),
        pl.BlockSpec((1, N_EXPERTS), const),
    ]
    if kind == "attn":
        body = functools.partial(_attn_out_kernel, cb=cb)
        in_specs = [pl.BlockSpec((tm, d), row)] + _stream_specs(tm, d, cb) + common_specs
        args = (acts[0], x[0], x[1], mod, gn, w, rw, rb)
    else:
        body = _hgrn_out_kernel
        in_specs = ([pl.BlockSpec((tm, d), row)] * 3 + [pl.BlockSpec((1, d), const)]
                    + [pl.BlockSpec((tm, d), row)] + common_specs)
        args = (acts[0], acts[1], acts[2], hg, x, mod, gn, w, rw, rb)
    return pl.pallas_call(
        body,
        out_shape=(jax.ShapeDtypeStruct((t_out, d), F32),
                   jax.ShapeDtypeStruct((t_out * SEGS, LANES), F32),
                   jax.ShapeDtypeStruct((t_out, N_EXPERTS), F32)),
        grid=(t_out // tm,),
        in_specs=in_specs,
        out_specs=(pl.BlockSpec((tm, d), lambda i: (i, 0)),
                   pl.BlockSpec((tm * SEGS, LANES), lambda i: (i, 0)),
                   pl.BlockSpec((tm, N_EXPERTS), lambda i: (i, 0))),
        compiler_params=_cparams(("parallel",)),
        name=kind + "_out_residual",
    )(*args)


def _router_kernel(lg_ref, meta_ref, cnt_ref, run_ref):
    i = pl.program_id(0)

    @pl.when(i == 0)
    def _():
        run_ref[...] = jnp.zeros_like(run_ref)

    lg = lg_ref[...].T
    ne, tb = lg.shape
    eid = lax.broadcasted_iota(jnp.int32, lg.shape, 0).astype(F32)
    sels, tops, idxs = [], [], []
    for _ in range(TOP_K):
        m = jnp.max(lg, axis=0, keepdims=True)
        idx = jnp.min(jnp.where(lg == m, eid, float(N_EXPERTS)), axis=0, keepdims=True)
        sel = eid == idx
        sels.append(sel)
        tops.append(m)
        idxs.append(idx)
        lg = jnp.where(sel, -jnp.inf, lg)
    ws = [jnp.exp(tk - tops[0]) for tk in tops]
    wsum = ws[0] + ws[1] + ws[2] + ws[3]
    chosen = (sels[0] | sels[1] | sels[2] | sels[3])
    onehot = jnp.where(chosen, 1.0, 0.0)
    rr = lax.broadcasted_iota(jnp.int32, (tb, tb), 0)
    cc = lax.broadcasted_iota(jnp.int32, (tb, tb), 1)
    tri = jnp.where(rr < cc, 1.0, 0.0).astype(BF16)
    before = jnp.dot(onehot.astype(BF16), tri, preferred_element_type=F32) + run_ref[:, 0:1]
    rows = list(idxs)
    rows += [jnp.sum(jnp.where(sels[k], before, 0.0), axis=0, keepdims=True) for k in range(TOP_K)]
    rows += [ws[k] / wsum for k in range(TOP_K)]
    rows.append(jnp.zeros((meta_ref.shape[0] - len(rows), tb), F32))
    meta_ref[...] = jnp.concatenate(rows, axis=0)
    run_ref[...] = run_ref[...] + jnp.sum(onehot, axis=1, keepdims=True)
    cnt_ref[...] = run_ref[...]


ROUTE_ROWS = 16


def _router_call(logits):
    t = logits.shape[0]
    tb = ROW_BLOCK
    meta, counts = pl.pallas_call(
        _router_kernel,
        out_shape=(jax.ShapeDtypeStruct((ROUTE_ROWS, t), F32),
                   jax.ShapeDtypeStruct((N_EXPERTS, LANES), F32)),
        grid=(t // tb,),
        in_specs=[pl.BlockSpec((tb, N_EXPERTS), lambda i: (i, 0))],
        out_specs=(pl.BlockSpec((ROUTE_ROWS, tb), lambda i: (0, i)),
                   pl.BlockSpec((N_EXPERTS, LANES), lambda i: (0, 0))),
        scratch_shapes=[pltpu.VMEM((N_EXPERTS, LANES), F32)],
        compiler_params=_cparams(("arbitrary",)),
        name="router_topk",
    )(logits)
    return meta, counts[:, 0]


def _expert_kernel(be_ref, nu_ref, nx_ref, ws_ref, src_first_ref, src_b_ref, src_c_ref, x_hbm, wgu_hbm, bgu_a_ref,
                   bgu_b_ref, wdn_hbm, bdn_a_ref, bdn_b_ref, o_ref, xbuf, sems, wgu_f, wdn_f, wsems, wgu_bf, wdn_bf,
                   *, layer):
    i = pl.program_id(0)
    rows = o_ref.shape[0] // (2 * SEGS)
    n_used = nu_ref[0]

    def gather(idx_ref, s):
        def issue(r2, carry):
            for p in range(2):
                r = 2 * r2 + p
                pltpu.make_async_copy(_row_tile(x_hbm, idx_ref[0, 0, r]), _row_tile(xbuf.at[s], r),
                                      sems.at[s]).start()
            return carry

        lax.fori_loop(0, rows // 2, issue, 0, unroll=4)

    def weight_copies(e, s):
        return (pltpu.make_async_copy(wgu_hbm.at[layer, e], wgu_f.at[s], wsems.at[s, 0]),
                pltpu.make_async_copy(wdn_hbm.at[layer, e], wdn_f.at[s], wsems.at[s, 1]))

    @pl.when(i == 0)
    def _():
        gather(src_first_ref, 0)
        for c in weight_copies(be_ref[0], ws_ref[0]):
            c.start(priority=1)

    for s, next_idx_ref, bgu_ref, bdn_ref in ((0, src_b_ref, bgu_a_ref, bdn_a_ref),
                                              (1, src_c_ref, bgu_b_ref, bdn_b_ref)):
        blk = 2 * i + s
        out = o_ref.at[pl.ds(s * rows * SEGS, rows * SEGS), :]

        @pl.when(blk + 1 < n_used)
        def _():
            gather(next_idx_ref, 1 - s)

        e_cur = be_ref[blk]
        wslot = ws_ref[blk]
        fresh = (blk == 0) | (e_cur != be_ref[jnp.maximum(blk - 1, 0)])

        @pl.when(fresh)
        def _():
            @pl.when(nx_ref[blk] != e_cur)
            def _():
                for c in weight_copies(nx_ref[blk], 1 - wslot):
                    c.start(priority=1)

            for c in weight_copies(e_cur, wslot):
                c.wait()
            wgu_bf[...] = wgu_f[wslot].astype(BF16)
            wdn_bf[...] = wdn_f[wslot].astype(BF16)

        @pl.when(blk < n_used)
        def _():
            pltpu.make_async_copy(x_hbm.at[pl.ds(0, rows * SEGS), :], xbuf.at[s], sems.at[s]).wait()
            xb = _load_tiled(xbuf.at[s], rows).astype(BF16)
            gu = jnp.dot(xb, wgu_bf[...], preferred_element_type=F32) + bgu_ref[...]
            g = jnp.minimum(gu[:, :D_FF], SWIGLU_LIMIT)
            u = jnp.clip(gu[:, D_FF:], -SWIGLU_LIMIT, SWIGLU_LIMIT)
            hdn = g * _sigmoid(SWIGLU_ALPHA * g) * (u + 1.0)
            y = jnp.dot(hdn.astype(BF16), wdn_bf[...], preferred_element_type=F32) + bdn_ref[...]
            _store_tiled(out, y)

        @pl.when(blk >= n_used)
        def _():
            out[...] = jnp.zeros(out.shape, F32)


def _expert_call(blk_e, n_used, src_tok, h2, layer, w_gu, b_gu, w_dn, b_dn):
    d = D_MODEL
    tr = MOE_ROWS
    n_blk = blk_e.shape[0]
    depth, e, _, f2 = w_gu.shape
    last = n_blk - 1
    nxt = jnp.min(jnp.where(blk_e[None, :] > blk_e[:, None], blk_e[None, :], N_EXPERTS), axis=1)
    nxt = jnp.where(nxt == N_EXPERTS, blk_e, nxt).astype(jnp.int32)
    change = jnp.concatenate([jnp.zeros((1,), jnp.int32), (blk_e[1:] != blk_e[:-1]).astype(jnp.int32)])
    wslot = (jnp.cumsum(change) % 2).astype(jnp.int32)
    src3 = src_tok.reshape(n_blk, 1, tr)
    smem_idx = lambda f: pl.BlockSpec((1, 1, tr), f, memory_space=pltpu.SMEM)
    bias = lambda width, s: pl.BlockSpec((None, None, 1, width), lambda i, be, *_: (layer, be[2 * i + s], 0, 0))
    return pl.pallas_call(
        functools.partial(_expert_kernel, layer=layer),
        out_shape=jax.ShapeDtypeStruct((n_blk * tr * SEGS, LANES), F32),
        grid_spec=pltpu.PrefetchScalarGridSpec(
            num_scalar_prefetch=4,
            grid=(n_blk // 2,),
            in_specs=[
                smem_idx(lambda i, *_: (0, 0, 0)),
                smem_idx(lambda i, *_: (2 * i + 1, 0, 0)),
                smem_idx(lambda i, *_: (jnp.minimum(2 * i + 2, last), 0, 0)),
                pl.BlockSpec(memory_space=pl.ANY),
                pl.BlockSpec(memory_space=pl.ANY),
                bias(f2, 0), bias(f2, 1),
                pl.BlockSpec(memory_space=pl.ANY),
                bias(d, 0), bias(d, 1),
            ],
            out_specs=pl.BlockSpec((2 * tr * SEGS, LANES), lambda i, *_: (i, 0)),
            scratch_shapes=[pltpu.VMEM((2, tr * SEGS, LANES), F32), pltpu.SemaphoreType.DMA((2,)),
                            pltpu.VMEM((2, d, f2), F32), pltpu.VMEM((2, D_FF, d), F32),
                            pltpu.SemaphoreType.DMA((2, 2)),
                            pltpu.VMEM((d, f2), BF16), pltpu.VMEM((D_FF, d), BF16)],
        ),
        compiler_params=_cparams(("arbitrary",), row_dma=True),
        name="moe_experts",
    )(blk_e, n_used, nxt, wslot, src3, src3, src3, h2,
      w_gu, b_gu.reshape(depth, e, 1, f2), b_gu.reshape(depth, e, 1, f2),
      w_dn, b_dn.reshape(depth, e, 1, d), b_dn.reshape(depth, e, 1, d))


def _combine_body(d0_ref, d1_ref, y_hbm, meta_ref, x_ref, mod_ref, gn_ref, buf, sems):
    i = pl.program_id(0)
    tb = x_ref.shape[0]
    slot = i % 2

    def gather(idx_ref, s):
        def issue(r, carry):
            for k in range(TOP_K):
                row = idx_ref[0, 0, k * tb + r]
                pltpu.make_async_copy(_row_tile(y_hbm, row), _row_tile(buf.at[s, k], r),
                                      sems.at[s]).start(priority=k % 2)
            return carry

        lax.fori_loop(0, tb, issue, 0, unroll=4)

    @pl.when(i == 0)
    def _():
        gather(d0_ref, 0)

    @pl.when(i + 1 < pl.num_programs(0))
    def _():
        gather(d1_ref, 1 - slot)

    for k in range(TOP_K):
        pltpu.make_async_copy(y_hbm.at[pl.ds(0, tb * SEGS), :], buf.at[slot, k], sems.at[slot]).wait()
    gates = meta_ref[...].T
    y2 = jnp.zeros(x_ref.shape, F32)
    for k in range(TOP_K):
        y2 = y2 + _load_tiled(buf.at[slot, k], tb) * gates[:, 2 * TOP_K + k:2 * TOP_K + k + 1]
    return x_ref[...] + mod_ref[0, 5:6, :] * _rms(y2, gn_ref[3:4, :])


def _combine_kernel(d0_ref, d1_ref, y_hbm, meta_ref, x_ref, mod_ref, gn_ref, o_ref, buf, sems):
    o_ref[...] = _combine_body(d0_ref, d1_ref, y_hbm, meta_ref, x_ref, mod_ref, gn_ref, buf, sems)


def _combine_in_kernel(d0_ref, d1_ref, y_hbm, meta_ref, x_ref, mod_ref, gn_ref, mod1_ref, gn1_ref, w_ref, lb_ref,
                       o_ref, q_ref, kf_ref, lf_ref, kb_ref, lbw_ref, i_ref, gs_ref, buf, sems):
    x_new = _combine_body(d0_ref, d1_ref, y_hbm, meta_ref, x_ref, mod_ref, gn_ref, buf, sems)
    o_ref[...] = x_new
    _hgrn_in_body(x_new, mod1_ref, gn1_ref, w_ref, lb_ref, q_ref, kf_ref, lf_ref, kb_ref, lbw_ref, i_ref, gs_ref)


def _combine_call(dest, y_sorted, meta, x, mod, gn, cb_rows, next_in=None):
    t, d = x.shape
    tb = COMBINE_ROWS
    nb = t // tb
    cbb = max(cb_rows // tb, 1)
    sel = (lambda i: (jnp.minimum(i // cbb, 1), 0, 0)) if cb_rows else (lambda i: (1, 0, 0))
    dest3 = dest.reshape(TOP_K, nb, tb).transpose(1, 0, 2).reshape(nb, 1, TOP_K * tb)
    row = pl.BlockSpec((tb, d), lambda i: (i, 0))
    const = lambda i: (0, 0)
    in_specs = [
        pl.BlockSpec((1, 1, tb * TOP_K), lambda i: (0, 0, 0), memory_space=pltpu.SMEM),
        pl.BlockSpec((1, 1, tb * TOP_K), lambda i: (jnp.minimum(i + 1, nb - 1), 0, 0),
                     memory_space=pltpu.SMEM),
        pl.BlockSpec(memory_space=pl.ANY),
        pl.BlockSpec((ROUTE_ROWS, tb), lambda i: (0, i)),
        row,
        pl.BlockSpec((1, 6, d), sel),
        pl.BlockSpec((4, d), const),
    ]
    args = (dest3, dest3, y_sorted, meta, x, mod, gn)
    out_shape = (jax.ShapeDtypeStruct((t, d), F32),)
    body = _combine_kernel
    if next_in is not None:
        mod1, gn1, w, lb = next_in
        in_specs += [pl.BlockSpec((1, 6, d), sel), pl.BlockSpec((1, d), const),
                     pl.BlockSpec(w.shape, const), pl.BlockSpec((2, d), const)]
        args += (mod1, gn1, w, lb)
        out_shape += tuple(jax.ShapeDtypeStruct((t, d), dt) for dt in HGRN_IN_DTYPES)
        body = _combine_in_kernel
    outs = pl.pallas_call(
        body,
        out_shape=out_shape,
        grid=(nb,),
        in_specs=in_specs,
        out_specs=tuple(row for _ in out_shape),
        scratch_shapes=[pltpu.VMEM((2, TOP_K, tb * SEGS, LANES), F32), pltpu.SemaphoreType.DMA((2,))],
        compiler_params=_cparams(("arbitrary",), row_dma=True),
        name="moe_combine" if next_in is None else "moe_combine_hgrn_in",
    )(*args)
    return outs[0] if next_in is None else outs


INVERT_BLOCK = 2048


def _invert_kernel(dest_ref, zeros_hbm, src_ref):
    k, j = pl.program_id(0), pl.program_id(1)
    n = dest_ref.shape[2]

    @pl.when((k == 0) & (j == 0))
    def _():
        pltpu.sync_copy(zeros_hbm, src_ref)

    tok0 = j * n

    def put(t, carry):
        src_ref[dest_ref[0, 0, t]] = tok0 + t
        return carry

    lax.fori_loop(0, n, put, 0, unroll=32)


def _invert_call(dest, n_rows):
    t = dest.shape[1]
    blk = max(b for b in range(LANES, INVERT_BLOCK + 1, LANES) if t % b == 0)
    nb = t // blk
    return pl.pallas_call(
        _invert_kernel,
        out_shape=jax.ShapeDtypeStruct((n_rows,), jnp.int32),
        grid=(TOP_K, nb),
        in_specs=[pl.BlockSpec((1, 1, blk), lambda k, j: (k * nb + j, 0, 0), memory_space=pltpu.SMEM),
                  pl.BlockSpec(memory_space=pl.ANY)],
        out_specs=pl.BlockSpec(memory_space=pltpu.SMEM),
        compiler_params=_cparams(("arbitrary", "arbitrary")),
        name="moe_invert",
    )(dest.reshape(TOP_K * nb, 1, blk), jnp.zeros((n_rows,), jnp.int32))


def _moe(h2, logits, x, mod, gn, layer, w_gu, b_gu, w_dn, b_dn, cb_rows, next_in=None):
    t = logits.shape[0]
    tr = MOE_ROWS
    meta, counts = _router_call(logits)
    e_idx = meta[:TOP_K].astype(jnp.int32)
    rank = meta[TOP_K:2 * TOP_K].astype(jnp.int32)
    counts = counts.astype(jnp.int32)
    padded = (counts + tr - 1) // tr * tr
    p_end = jnp.cumsum(padded)
    p_start = p_end - padded
    dest = rank
    for e in range(N_EXPERTS):
        dest = dest + jnp.where(e_idx == e, p_start[e], 0)
    a = t * TOP_K
    n_blk = (a + N_EXPERTS * (tr - 1) + tr - 1) // tr
    n_blk += n_blk % 2
    src_tok = _invert_call(dest, n_blk * tr)
    blk_start = jnp.arange(n_blk, dtype=jnp.int32) * tr
    blk_e = jnp.sum((p_end[None, :] <= blk_start[:, None]).astype(jnp.int32), axis=1)
    blk_e = jnp.minimum(blk_e, N_EXPERTS - 1)
    n_used = (p_end[-1] // tr).astype(jnp.int32).reshape(1)
    ys = _expert_call(blk_e, n_used, src_tok, h2, layer, w_gu, b_gu, w_dn, b_dn)
    return _combine_call(dest, ys, meta, x, mod, gn, cb_rows, next_in)


HGRN_IN_DTYPES = (BF16, BF16, F32, BF16, F32, BF16, BF16)


def _hgrn_in_body(x, mod_ref, gn_ref, w_ref, lb_ref, q_ref, kf_ref, lf_ref, kb_ref, lbw_ref, i_ref, gs_ref):
    h = _rms(x, gn_ref[...]) * (1.0 + mod_ref[0, 1:2, :]) + mod_ref[0, 0:1, :]
    hb = h.astype(BF16)
    f = D_MODEL

    def proj(sec):
        return jnp.dot(hb, w_ref[:, sec * f:(sec + 1) * f], preferred_element_type=F32)

    z = proj(0)
    q_ref[...] = _silu(z).astype(q_ref.dtype)
    for sec, k_ref, l_ref in ((1, kf_ref, lf_ref), (2, kb_ref, lbw_ref)):
        z = proj(sec)
        lb = lb_ref[sec - 1:sec, :]
        sg = _sigmoid(z)
        l_ref[...] = jnp.log(lb + (1.0 - lb) * sg)
        k_ref[...] = ((1.0 - lb) * (1.0 - sg)).astype(k_ref.dtype)
    i_ref[...] = proj(3).astype(i_ref.dtype)
    z = proj(4)
    gs_ref[...] = _silu(z).astype(gs_ref.dtype)


def _split3(x):
    hi = x.astype(BF16)
    r1 = x - hi.astype(F32)
    mid = r1.astype(BF16)
    lo = (r1 - mid.astype(F32)).astype(BF16)
    return hi, mid, lo


def _scan_kernel(q_ref, k_ref, g_ref, v_ref, o_ref, st_ref, gcum_ref, *, reverse):
    j = pl.program_id(0)
    ch, sub = HGRN_CHUNK, HGRN_SUB
    ns = ch // sub
    dk = HGRN_DK

    @pl.when(j == 0)
    def _():
        st_ref[...] = jnp.zeros_like(st_ref)

    rr = lax.broadcasted_iota(jnp.int32, (ch, ch), 0)
    cc = lax.broadcasted_iota(jnp.int32, (ch, ch), 1)
    causal = (cc >= rr) if reverse else (cc <= rr)
    tri = jnp.where(causal, 1.0, 0.0).astype(BF16)
    hi, mid, lo = _split3(g_ref[...])
    gcum_ref[...] = LOG2E * (jnp.dot(tri, hi, preferred_element_type=F32)
                             + jnp.dot(tri, mid, preferred_element_type=F32)
                             + jnp.dot(tri, lo, preferred_element_type=F32))
    key_blocks = list(range(1, ns)) if reverse else list(range(ns - 1))
    nkb = len(key_blocks)
    kbd_mask = (lax.broadcasted_iota(jnp.int32, (ch, nkb * dk), 1) // dk + key_blocks[0]
                == lax.broadcasted_iota(jnp.int32, (ch, nkb * dk), 0) // sub)
    kdg_mask = (lax.broadcasted_iota(jnp.int32, (ch, sub * dk), 1) // dk
                == lax.broadcasted_iota(jnp.int32, (ch, sub * dk), 0) % sub)
    end_row = 0 if reverse else ch - 1
    edge = 0 if reverse else sub - 1
    nt = (((1,), (1,)), ((), ()))
    tn = (((0,), (0,)), ((), ()))

    grp = HGRN_GROUP
    rr_g = lax.broadcasted_iota(jnp.int32, (ch, grp * ch), 0)
    cc_g = lax.broadcasted_iota(jnp.int32, (ch, grp * ch), 1) % ch
    diag_mask_g = ((rr_g // sub) == (cc_g // sub)) & ((cc_g >= rr_g) if reverse else (cc_g <= rr_g))

    def place(x, i):
        w = x.shape[1]
        parts = []
        if i:
            parts.append(jnp.zeros((x.shape[0], i * w), x.dtype))
        parts.append(x)
        if grp - 1 - i:
            parts.append(jnp.zeros((x.shape[0], (grp - 1 - i) * w), x.dtype))
        return jnp.concatenate(parts, axis=1)

    for g0 in range(0, HGRN_HEADS, grp):
        qgs, q_cats, p_cats, kbd_rows, kdg_rows, st_rows, v_rows = [], [], [], [], [], [], []
        for i in range(grp):
            h = g0 + i
            hs = slice(h * dk, (h + 1) * dk)
            q = q_ref[:, hs].astype(F32)
            kb = k_ref[:, hs]
            k = kb.astype(F32)
            v = v_ref[:, hs]
            gc = gcum_ref[:, hs]
            st = st_ref[h]
            g_end = gc[end_row:end_row + 1, :]
            qgs.append((q * jnp.exp2(gc)).astype(BF16))
            st_rows.append(place(st.astype(BF16), i))
            v_rows.append(place(v, i))
            k_end = (k * jnp.exp2(g_end - gc)).astype(BF16)
            st_ref[h] = st * jnp.exp2(g_end) + lax.dot_general(v, k_end, tn, preferred_element_type=F32)
            gc4 = gc.reshape(ns, sub, dk)
            g_edge = jnp.broadcast_to(gc4[:, edge:edge + 1, :], (ns, sub, dk)).reshape(ch, dk)
            k_rel = (k * jnp.exp2(g_edge - gc)).astype(BF16)
            k_bd = jnp.where(kbd_mask, jnp.concatenate([k_rel] * nkb, axis=1), jnp.zeros((), BF16))
            kbd_rows.append(place(k_bd, i))
            q_parts = []
            for jb in key_blocks:
                row = jb * sub + edge
                rows = slice(0, jb * sub) if reverse else slice((jb + 1) * sub, ch)
                part = q[rows] * jnp.exp2(gc[rows] - gc[row:row + 1, :])
                pad = jnp.zeros((ch - part.shape[0], dk), F32)
                full = jnp.concatenate([part, pad] if reverse else [pad, part], axis=0)
                q_parts.append(full.astype(BF16))
            q_cats.append(jnp.concatenate(q_parts, axis=1))
            p_parts = []
            for s in range(sub):
                g_s = jnp.broadcast_to(gc4[:, s:s + 1, :], (ns, sub, dk)).reshape(ch, dk)
                p_parts.append((q * jnp.exp2(jnp.minimum(gc - g_s, 0.0))).astype(BF16))
            p_cats.append(jnp.concatenate(p_parts, axis=1))
            k_dg = jnp.where(kdg_mask, jnp.concatenate([kb] * sub, axis=1), jnp.zeros((), BF16))
            kdg_rows.append(place(k_dg, i))
        s_off = lax.dot_general(jnp.concatenate(q_cats, axis=1), jnp.concatenate(kbd_rows, axis=0), nt,
                                preferred_element_type=F32)
        s_diag = lax.dot_general(jnp.concatenate(p_cats, axis=1), jnp.concatenate(kdg_rows, axis=0), nt,
                                 preferred_element_type=F32)
        a = s_off + jnp.where(diag_mask_g, s_diag, 0.0)
        o = lax.dot_general(jnp.concatenate(qgs, axis=1), jnp.concatenate(st_rows, axis=0), nt,
                            preferred_element_type=F32)
        o = o + jnp.dot(a.astype(BF16), jnp.concatenate(v_rows, axis=0), preferred_element_type=F32)
        o_ref[:, g0 * dk:(g0 + grp) * dk] = o


def _scan_call(q, k, lg, v, c_len, reverse):
    t, d = q.shape
    ch = HGRN_CHUNK
    n_ch = t // ch
    cc = c_len // ch
    if reverse:
        idx = lambda j: (jnp.where(j < cc, cc - 1 - j, n_ch - 1 - (j - cc)), 0)
    else:
        idx = lambda j: (j, 0)
    blk = pl.BlockSpec((ch, d), idx)
    return pl.pallas_call(
        functools.partial(_scan_kernel, reverse=reverse),
        out_shape=jax.ShapeDtypeStruct((t, d), F32),
        grid=(n_ch,),
        in_specs=[blk, blk, blk, blk],
        out_specs=blk,
        scratch_shapes=[pltpu.VMEM((HGRN_HEADS, HGRN_DK, HGRN_DK), F32), pltpu.VMEM((ch, d), F32)],
        compiler_params=_cparams(("arbitrary",)),
        name="hgrn_scan_bwd" if reverse else "hgrn_scan_fwd",
    )(q, k, lg, v)


def _rope_tables(l, c_len):
    n = HEAD_DIM // 4
    inv_freq = ROPE_BASE ** (-jnp.arange(n, dtype=F32) / n)
    rows = l // GRID_W
    ang_row = jnp.arange(rows, dtype=F32)[:, None] * inv_freq[None, :]
    ang_col = jnp.arange(GRID_W, dtype=F32)[:, None] * inv_freq[None, :]
    cr, sr, cc, sc = jnp.cos(ang_row), jnp.sin(ang_row), jnp.cos(ang_col), jnp.sin(ang_col)
    zr, zc = jnp.zeros_like(cr), jnp.zeros_like(cc)
    cos_r = jnp.concatenate([cr, cr, zr, zr] * 2, axis=1)
    sin_r = jnp.concatenate([-sr, sr, zr, zr] * 2, axis=1)
    cos_c = jnp.concatenate([zc, zc, cc, cc] * 2, axis=1)
    sin_c = jnp.concatenate([zc, zc, -sc, sc] * 2, axis=1)
    cos = (cos_r[:, None, :] + cos_c[None, :, :]).reshape(l, LANES)
    sin = (sin_r[:, None, :] + sin_c[None, :, :]).reshape(l, LANES)
    cos = jnp.concatenate([jnp.ones((c_len, LANES), F32), cos], axis=0)
    sin = jnp.concatenate([jnp.zeros((c_len, LANES), F32), sin], axis=0)
    return cos, sin


def kernel(x, c, ctx, c_ctx, ada_w, ada_b, norm_g, attn_w_qkv, attn_b_qkv, attn_sink, attn_w_o,
           hgrn_w_in, hgrn_lb, hgrn_norm_g, hgrn_w_o, router_w, router_b, moe_w_gu, moe_b_gu,
           moe_w_dn, moe_b_dn):
    b, l, d = x.shape
    c_len = ctx.shape[1]
    depth = ada_w.shape[0]
    assert b == 1 and d == D_MODEL and depth == 2
    assert c_len % ROW_BLOCK == 0 and l % ROW_BLOCK == 0
    cb = c_len // ROW_BLOCK

    cs = jnp.zeros((8, d), F32).at[0].set(c_ctx).at[1].set(c[0])
    mods = _ada_call(cs, ada_w, ada_b)
    mods = mods[:, :2].reshape(depth, 2, 6, d)

    cos, sin = _rope_tables(l, c_len)

    q, k4, v4 = _qkv_call(ctx[0], x[0], mods[0], norm_g[0, 0:1], attn_w_qkv[0].astype(BF16),
                          attn_b_qkv[0].reshape(1, -1), cos, sin, cb)
    o = _attn_call(attn_sink[0], q, k4, v4, c_len)
    xs, h2, logits = _out_call("attn", (o,), (ctx[0], x[0]), mods[0], norm_g[0], attn_w_o[0].astype(BF16),
                               router_w[0], router_b[0].reshape(1, -1), None, cb, 0)
    lb_soft = jax.nn.softmax(hgrn_lb.astype(F32), axis=0)
    lb = jnp.cumsum(lb_soft, axis=0)[1] - lb_soft[0]
    xs, qh, kf, lf, kb, lbw, iv, gs = _moe(
        h2, logits, xs, mods[0], norm_g[0], 0, moe_w_gu, moe_b_gu, moe_w_dn, moe_b_dn, c_len,
        next_in=(mods[1], norm_g[1, 0:1], hgrn_w_in[0].astype(BF16), lb))
    o_f = _scan_call(qh, kf, lf, iv, c_len, reverse=False)
    o_b = _scan_call(qh, kb, lbw, iv, c_len, reverse=True)
    x_lat, h2, logits = _out_call("hgrn", (o_f, o_b, gs), xs, mods[1], norm_g[1],
                                  hgrn_w_o[0].astype(BF16), router_w[1], router_b[1].reshape(1, -1),
                                  hgrn_norm_g[0].reshape(1, -1), cb, cb)
    out = _moe(h2, logits, x_lat, mods[1], norm_g[1], 1, moe_w_gu, moe_b_gu, moe_w_dn, moe_b_dn, 0)
    return out[None]
```

```python
import functools

import jax
import jax.numpy as jnp
from jax import lax
from jax.experimental import pallas as pl
from jax.experimental.pallas import tpu as pltpu

D_MODEL = 1024
GRID_W = 64
RMS_EPS = 1e-6

ATTN_HEADS = 16
ATTN_KV_HEADS = 2
HEAD_DIM = 64
Q_DIM = ATTN_HEADS * HEAD_DIM
KV_DIM = ATTN_KV_HEADS * HEAD_DIM
QKV_DIM = Q_DIM + 2 * KV_DIM
WINDOW = 128
ATTN_BLOCK = 128
ROPE_BASE = 10000.0

HGRN_HEADS = 8
HGRN_DK = 128
HGRN_CHUNK = 64
HGRN_SUB = 8
HGRN_GROUP = 1

N_EXPERTS = 32
TOP_K = 4
D_FF = 1024
SWIGLU_LIMIT = 7.0
SWIGLU_ALPHA = 1.702

LANES = 128
ROW_BLOCK = 256
MOE_ROWS = 256
COMBINE_ROWS = 256
VMEM_LIMIT = 56 * 1024 * 1024

F32 = jnp.float32
BF16 = jnp.bfloat16
NEG_BIG = -1e30
LOG2E = 1.4426950408889634


def _cparams(sem, row_dma=False):
    return pltpu.CompilerParams(dimension_semantics=sem, vmem_limit_bytes=VMEM_LIMIT,
                                disable_bounds_checks=row_dma)


def _rms(x, g):
    return x * lax.rsqrt(jnp.mean(x * x, axis=-1, keepdims=True) + RMS_EPS) * g


def _sigmoid(x):
    return 1.0 / (1.0 + jnp.exp(-x))


def _silu(x):
    return x * (0.5 * jnp.tanh(0.5 * x) + 0.5)


SEGS = D_MODEL // LANES


def _store_tiled(ref, val):
    n = val.shape[0]
    for s in range(SEGS):
        ref[pl.ds(s, n, stride=SEGS), :] = val[:, s * LANES:(s + 1) * LANES]


def _load_tiled(ref, n):
    return jnp.concatenate([ref[pl.ds(s, n, stride=SEGS), :] for s in range(SEGS)], axis=1)


def _row_tile(ref, r):
    return ref.at[pl.ds(pl.multiple_of(r * SEGS, SEGS), SEGS), :]


def _ada_kernel(c_ref, w_ref, b_ref, o_ref):
    c = c_ref[...]
    s = c * _sigmoid(c)
    o_ref[0] = jnp.dot(s, w_ref[0], precision=lax.Precision.HIGHEST,
                       preferred_element_type=F32) + b_ref[0]


def _ada_call(cs, ada_w, ada_b):
    depth, d, n = ada_w.shape
    tn = 1024
    return pl.pallas_call(
        _ada_kernel,
        out_shape=jax.ShapeDtypeStruct((depth, 8, n), F32),
        grid=(depth, n // tn),
        in_specs=[
            pl.BlockSpec((8, d), lambda i, j: (0, 0)),
            pl.BlockSpec((1, d, tn), lambda i, j: (i, 0, j)),
            pl.BlockSpec((1, 1, tn), lambda i, j: (i, 0, j)),
        ],
        out_specs=pl.BlockSpec((1, 8, tn), lambda i, j: (i, 0, j)),
        compiler_params=_cparams(("arbitrary", "arbitrary")),
        name="adaln",
    )(cs, ada_w, ada_b.reshape(depth, 1, n))


def _swap16(t):
    lane = lax.broadcasted_iota(jnp.int32, t.shape, 1)
    return jnp.where(lane % 32 < 16, pltpu.roll(t, LANES - 16, 1), pltpu.roll(t, 16, 1))


def _stream_rows(c_ref, x_ref, cb):
    return jnp.where(pl.program_id(0) < cb, c_ref[...], x_ref[...])


def _stream_specs(tm, d, cb):
    return [pl.BlockSpec((tm, d), lambda i: (jnp.minimum(i, cb - 1), 0)),
            pl.BlockSpec((tm, d), lambda i: (jnp.maximum(i - cb, 0), 0))]


def _qkv_kernel(c_ref, x_ref, mod_ref, gn_ref, w_ref, b_ref, cos_ref, sin_ref, q_ref, k4_ref, v4_ref, *, cb):
    x = _stream_rows(c_ref, x_ref, cb)
    h = _rms(x, gn_ref[...]) * (1.0 + mod_ref[0, 1:2, :]) + mod_ref[0, 0:1, :]
    hb = h.astype(BF16)
    cos = cos_ref[...]
    sin = sin_ref[...]
    nq = Q_DIM // LANES
    for j in range(nq + 1):
        sl = slice(j * LANES, (j + 1) * LANES)
        t = jnp.dot(hb, w_ref[:, sl], preferred_element_type=F32) + b_ref[:, sl]
        t = t * cos + _swap16(t) * sin
        if j < nq:
            q_ref[:, sl] = (t * (HEAD_DIM ** -0.5 * LOG2E)).astype(q_ref.dtype)
        else:
            kt = t
    sl = slice(Q_DIM + KV_DIM, QKV_DIM)
    vt = jnp.dot(hb, w_ref[:, sl], preferred_element_type=F32) + b_ref[:, sl]
    lo = lax.broadcasted_iota(jnp.int32, kt.shape, 1) < HEAD_DIM
    for t, ref in ((kt, k4_ref), (vt, v4_ref)):
        sw = pltpu.roll(t, HEAD_DIM, 1)
        ref[:, 0 * LANES:1 * LANES] = jnp.where(lo, t, 0.0).astype(ref.dtype)
        ref[:, 1 * LANES:2 * LANES] = jnp.where(lo, 0.0, sw).astype(ref.dtype)
        ref[:, 2 * LANES:3 * LANES] = jnp.where(lo, sw, 0.0).astype(ref.dtype)
        ref[:, 3 * LANES:4 * LANES] = jnp.where(lo, 0.0, t).astype(ref.dtype)


def _qkv_call(ctx, x, mod, gn, w, b, cos, sin, cb):
    d = x.shape[1]
    t = ctx.shape[0] + x.shape[0]
    tm = ROW_BLOCK
    sel = lambda i: (jnp.minimum(i // cb, 1), 0, 0)
    return pl.pallas_call(
        functools.partial(_qkv_kernel, cb=cb),
        out_shape=(jax.ShapeDtypeStruct((t, Q_DIM), BF16),
                   jax.ShapeDtypeStruct((t, 4 * LANES), BF16),
                   jax.ShapeDtypeStruct((t, 4 * LANES), BF16)),
        grid=(t // tm,),
        in_specs=_stream_specs(tm, d, cb) + [
            pl.BlockSpec((1, 6, d), sel),
            pl.BlockSpec((1, d), lambda i: (0, 0)),
            pl.BlockSpec((d, QKV_DIM), lambda i: (0, 0)),
            pl.BlockSpec((1, QKV_DIM), lambda i: (0, 0)),
            pl.BlockSpec((tm, LANES), lambda i: (i, 0)),
            pl.BlockSpec((tm, LANES), lambda i: (i, 0)),
        ],
        out_specs=(pl.BlockSpec((tm, Q_DIM), lambda i: (i, 0)),
                   pl.BlockSpec((tm, 4 * LANES), lambda i: (i, 0)),
                   pl.BlockSpec((tm, 4 * LANES), lambda i: (i, 0))),
        compiler_params=_cparams(("parallel",)),
        name="qkv_rope",
    )(ctx, x, mod, gn, w, b, cos, sin)


def _attn_kernel(sink_ref, q_ref, kp_ref, kc_ref, kn_ref, kx_ref, vp_ref, vc_ref, vn_ref, vx_ref,
                 o_ref, *, cb, n_lat):
    n = pl.program_id(0)
    blk = ATTN_BLOCK
    c = kx_ref.shape[0]
    nw = 3 * blk
    s = lax.broadcasted_iota(jnp.int32, (nw + c, blk), 0)
    r = lax.broadcasted_iota(jnp.int32, (nw + c, blk), 1)
    q_pos = (n - cb) * blk + r
    k_pos = (n - cb - 1) * blk + s
    win_ok = (jnp.abs(q_pos - k_pos) <= WINDOW) & (k_pos >= 0) & (k_pos < n_lat) & (n >= cb)
    valid = (s >= nw) | win_ok
    k_all = jnp.concatenate([kp_ref[...], kc_ref[...], kn_ref[...], kx_ref[...]], axis=0)
    v_all = jnp.concatenate([vp_ref[...], vc_ref[...], vn_ref[...], vx_ref[...]], axis=0)
    n_var = 2 * ATTN_KV_HEADS
    v_t = [v_all[:, j * LANES:(j + 1) * LANES].astype(F32).T.astype(BF16) for j in range(n_var)]
    nt = (((1,), (1,)), ((), ()))
    group = ATTN_HEADS // ATTN_KV_HEADS
    for p in range(ATTN_HEADS // 2):
        g = (2 * p) // group
        qp = q_ref[:, p * LANES:(p + 1) * LANES]
        o_t = jnp.zeros((LANES, blk), F32)
        for half in range(2):
            j = 2 * g + half
            sc = lax.dot_general(k_all[:, j * LANES:(j + 1) * LANES], qp, nt,
                                 preferred_element_type=F32)
            sc = jnp.where(valid, sc, NEG_BIG)
            sink = sink_ref[2 * p + half] * LOG2E
            m = jnp.maximum(jnp.max(sc, axis=0, keepdims=True), sink)
            e = jnp.exp2(sc - m)
            denom = jnp.sum(e, axis=0, keepdims=True) + jnp.exp2(sink - m)
            pv = jnp.dot(v_t[j], e.astype(BF16), preferred_element_type=F32)
            o_t = o_t + pv / denom
        o_ref[:, p * LANES:(p + 1) * LANES] = o_t.T.astype(o_ref.dtype)


def _attn_call(sink, q, k4, v4, c_len):
    t = q.shape[0]
    blk = ATTN_BLOCK
    cb = c_len // blk
    nb = t // blk
    n_lat = t - c_len
    last = nb - 1
    kw = 4 * LANES
    spec_q = pl.BlockSpec((blk, Q_DIM), lambda n: (n, 0))
    prev = pl.BlockSpec((blk, kw), lambda n: (jnp.maximum(n - 1, 0), 0))
    cur = pl.BlockSpec((blk, kw), lambda n: (n, 0))
    nxt = pl.BlockSpec((blk, kw), lambda n: (jnp.minimum(n + 1, last), 0))
    ctx = pl.BlockSpec((c_len, kw), lambda n: (0, 0))
    return pl.pallas_call(
        functools.partial(_attn_kernel, cb=cb, n_lat=n_lat),
        out_shape=jax.ShapeDtypeStruct((t, Q_DIM), BF16),
        grid=(nb,),
        in_specs=[pl.BlockSpec(memory_space=pltpu.SMEM), spec_q,
                  prev, cur, nxt, ctx, prev, cur, nxt, ctx],
        out_specs=pl.BlockSpec((blk, Q_DIM), lambda n: (n, 0)),
        compiler_params=_cparams(("parallel",)),
        name="window_attn",
    )(sink, q, k4, k4, k4, k4, v4, v4, v4, v4)


def _residual_tail(y, x, mod_ref, gn_ref, rw_ref, rb_ref, xo_ref, h2_ref, lg_ref):
    x_new = x + mod_ref[0, 2:3, :] * _rms(y, gn_ref[1:2, :])
    h2 = _rms(x_new, gn_ref[2:3, :]) * (1.0 + mod_ref[0, 4:5, :]) + mod_ref[0, 3:4, :]
    xo_ref[...] = x_new
    _store_tiled(h2_ref, h2)
    hi, mid, lo = _split3(h2)
    lane = lax.broadcasted_iota(jnp.int32, (h2.shape[0], LANES), 1)
    w = rw_ref[...]
    r = (jnp.dot(hi, w, preferred_element_type=F32)
         + jnp.where(lane < 2 * N_EXPERTS, jnp.dot(mid, w, preferred_element_type=F32), 0.0)
         + jnp.where(lane < N_EXPERTS, jnp.dot(lo, w, preferred_element_type=F32), 0.0))
    r = r + pltpu.roll(r, LANES - N_EXPERTS, 1) + pltpu.roll(r, LANES - 2 * N_EXPERTS, 1)
    lg_ref[...] = r[:, :N_EXPERTS] + rb_ref[...]


def _attn_out_kernel(a_ref, c_ref, x_ref, mod_ref, gn_ref, w_ref, rw_ref, rb_ref, xo_ref, h2_ref, lg_ref,
                     *, cb):
    y = jnp.dot(a_ref[...], w_ref[...], preferred_element_type=F32)
    _residual_tail(y, _stream_rows(c_ref, x_ref, cb), mod_ref, gn_ref, rw_ref, rb_ref, xo_ref, h2_ref, lg_ref)


def _hgrn_out_kernel(of_ref, ob_ref, gs_ref, hg_ref, x_ref, mod_ref, gn_ref, w_ref, rw_ref, rb_ref,
                     xo_ref, h2_ref, lg_ref):
    parts = []
    for h in range(HGRN_HEADS):
        sl = slice(h * HGRN_DK, (h + 1) * HGRN_DK)
        o = of_ref[:, sl] + ob_ref[:, sl]
        parts.append(_rms(o, hg_ref[:, sl]))
    a = jnp.concatenate(parts, axis=1) * gs_ref[...].astype(F32)
    y = jnp.dot(a.astype(BF16), w_ref[...], preferred_element_type=F32)
    _residual_tail(y, x_ref[...], mod_ref, gn_ref, rw_ref, rb_ref, xo_ref, h2_ref, lg_ref)


def _out_call(kind, acts, x, mod, gn, w, rw, rb, hg, cb, off):
    t_out = (x[0].shape[0] + x[1].shape[0]) if kind == "attn" else x.shape[0] - off * ROW_BLOCK
    d = D_MODEL
    tm = ROW_BLOCK
    w_hi, w_mid, w_lo = _split3(rw)
    rw = jnp.concatenate([w_hi, w_mid, w_lo, jnp.zeros_like(w_hi)], axis=1)
    row = lambda i: (i + off, 0)
    sel = lambda i: (jnp.minimum((i + off) // cb, 1), 0, 0)
    const = lambda i: (0, 0)
    common_specs = [
        pl.BlockSpec((1, 6, d), sel),
        pl.BlockSpec((4, d), const),
        pl.BlockSpec((d, d), const),
        pl.BlockSpec((d, LANES), const),
        pl.BlockSpec((1, N_EXPERTS), const),
    ]
    if kind == "attn":
        body = functools.partial(_attn_out_kernel, cb=cb)
        in_specs = [pl.BlockSpec((tm, d), row)] + _stream_specs(tm, d, cb) + common_specs
        args = (acts[0], x[0], x[1], mod, gn, w, rw, rb)
    else:
        body = _hgrn_out_kernel
        in_specs = ([pl.BlockSpec((tm, d), row)] * 3 + [pl.BlockSpec((1, d), const)]
                    + [pl.BlockSpec((tm, d), row)] + common_specs)
        args = (acts[0], acts[1], acts[2], hg, x, mod, gn, w, rw, rb)
    return pl.pallas_call(
        body,
        out_shape=(jax.ShapeDtypeStruct((t_out, d), F32),
                   jax.ShapeDtypeStruct((t_out * SEGS, LANES), F32),
                   jax.ShapeDtypeStruct((t_out, N_EXPERTS), F32)),
        grid=(t_out // tm,),
        in_specs=in_specs,
        out_specs=(pl.BlockSpec((tm, d), lambda i: (i, 0)),
                   pl.BlockSpec((tm * SEGS, LANES), lambda i: (i, 0)),
                   pl.BlockSpec((tm, N_EXPERTS), lambda i: (i, 0))),
        compiler_params=_cparams(("parallel",)),
        name=kind + "_out_residual",
    )(*args)


def _router_kernel(lg_ref, meta_ref, cnt_ref, run_ref):
    i = pl.program_id(0)

    @pl.when(i == 0)
    def _():
        run_ref[...] = jnp.zeros_like(run_ref)

    lg = lg_ref[...].T
    ne, tb = lg.shape
    eid = lax.broadcasted_iota(jnp.int32, lg.shape, 0).astype(F32)
    sels, tops, idxs = [], [], []
    for _ in range(TOP_K):
        m = jnp.max(lg, axis=0, keepdims=True)
        idx = jnp.min(jnp.where(lg == m, eid, float(N_EXPERTS)), axis=0, keepdims=True)
        sel = eid == idx
        sels.append(sel)
        tops.append(m)
        idxs.append(idx)
        lg = jnp.where(sel, -jnp.inf, lg)
    ws = [jnp.exp(tk - tops[0]) for tk in tops]
    wsum = ws[0] + ws[1] + ws[2] + ws[3]
    chosen = (sels[0] | sels[1] | sels[2] | sels[3])
    onehot = jnp.where(chosen, 1.0, 0.0)
    rr = lax.broadcasted_iota(jnp.int32, (tb, tb), 0)
    cc = lax.broadcasted_iota(jnp.int32, (tb, tb), 1)
    tri = jnp.where(rr < cc, 1.0, 0.0).astype(BF16)
    before = jnp.dot(onehot.astype(BF16), tri, preferred_element_type=F32) + run_ref[:, 0:1]
    rows = list(idxs)
    rows += [jnp.sum(jnp.where(sels[k], before, 0.0), axis=0, keepdims=True) for k in range(TOP_K)]
    rows += [ws[k] / wsum for k in range(TOP_K)]
    rows.append(jnp.zeros((meta_ref.shape[0] - len(rows), tb), F32))
    meta_ref[...] = jnp.concatenate(rows, axis=0)
    run_ref[...] = run_ref[...] + jnp.sum(onehot, axis=1, keepdims=True)
    cnt_ref[...] = run_ref[...]


ROUTE_ROWS = 16


def _router_call(logits):
    t = logits.shape[0]
    tb = ROW_BLOCK
    meta, counts = pl.pallas_call(
        _router_kernel,
        out_shape=(jax.ShapeDtypeStruct((ROUTE_ROWS, t), F32),
                   jax.ShapeDtypeStruct((N_EXPERTS, LANES), F32)),
        grid=(t // tb,),
        in_specs=[pl.BlockSpec((tb, N_EXPERTS), lambda i: (i, 0))],
        out_specs=(pl.BlockSpec((ROUTE_ROWS, tb), lambda i: (0, i)),
                   pl.BlockSpec((N_EXPERTS, LANES), lambda i: (0, 0))),
        scratch_shapes=[pltpu.VMEM((N_EXPERTS, LANES), F32)],
        compiler_params=_cparams(("arbitrary",)),
        name="router_topk",
    )(logits)
    return meta, counts[:, 0]


def _expert_kernel(be_ref, nu_ref, nx_ref, ws_ref, src_first_ref, src_b_ref, src_c_ref, x_hbm, wgu_hbm, bgu_a_ref,
                   bgu_b_ref, wdn_hbm, bdn_a_ref, bdn_b_ref, o_ref, xbuf, sems, wgu_f, wdn_f, wsems, wgu_bf, wdn_bf,
                   *, layer):
    i = pl.program_id(0)
    rows = o_ref.shape[0] // (2 * SEGS)
    n_used = nu_ref[0]

    def gather(idx_ref, s):
        for r in range(rows):
            pltpu.make_async_copy(_row_tile(x_hbm, idx_ref[0, 0, r]),
                                  xbuf.at[s, pl.ds(r * SEGS, SEGS), :], sems.at[s]).start()

    def weight_copies(e, s):
        return (pltpu.make_async_copy(wgu_hbm.at[layer, e], wgu_f.at[s], wsems.at[s, 0]),
                pltpu.make_async_copy(wdn_hbm.at[layer, e], wdn_f.at[s], wsems.at[s, 1]))

    @pl.when(i == 0)
    def _():
        gather(src_first_ref, 0)
        for c in weight_copies(be_ref[0], ws_ref[0]):
            c.start(priority=1)

    for s, next_idx_ref, bgu_ref, bdn_ref in ((0, src_b_ref, bgu_a_ref, bdn_a_ref),
                                              (1, src_c_ref, bgu_b_ref, bdn_b_ref)):
        blk = 2 * i + s
        out = o_ref.at[pl.ds(s * rows * SEGS, rows * SEGS), :]

        @pl.when(blk + 1 < n_used)
        def _():
            gather(next_idx_ref, 1 - s)

        e_cur = be_ref[blk]
        wslot = ws_ref[blk]
        fresh = (blk == 0) | (e_cur != be_ref[jnp.maximum(blk - 1, 0)])

        @pl.when(fresh)
        def _():
            @pl.when(nx_ref[blk] != e_cur)
            def _():
                for c in weight_copies(nx_ref[blk], 1 - wslot):
                    c.start(priority=1)

            for c in weight_copies(e_cur, wslot):
                c.wait()
            wgu_bf[...] = wgu_f[wslot].astype(BF16)
            wdn_bf[...] = wdn_f[wslot].astype(BF16)

        @pl.when(blk < n_used)
        def _():
            pltpu.make_async_copy(x_hbm.at[pl.ds(0, rows * SEGS), :], xbuf.at[s], sems.at[s]).wait()
            xb = _load_tiled(xbuf.at[s], rows).astype(BF16)
            gu = jnp.dot(xb, wgu_bf[...], preferred_element_type=F32) + bgu_ref[...]
            g = jnp.minimum(gu[:, :D_FF], SWIGLU_LIMIT)
            u = jnp.clip(gu[:, D_FF:], -SWIGLU_LIMIT, SWIGLU_LIMIT)
            hdn = g * _sigmoid(SWIGLU_ALPHA * g) * (u + 1.0)
            y = jnp.dot(hdn.astype(BF16), wdn_bf[...], preferred_element_type=F32) + bdn_ref[...]
            _store_tiled(out, y)

        @pl.when(blk >= n_used)
        def _():
            out[...] = jnp.zeros(out.shape, F32)


def _expert_call(blk_e, n_used, src_tok, h2, layer, w_gu, b_gu, w_dn, b_dn):
    d = D_MODEL
    tr = MOE_ROWS
    n_blk = blk_e.shape[0]
    depth, e, _, f2 = w_gu.shape
    last = n_blk - 1
    nxt = jnp.min(jnp.where(blk_e[None, :] > blk_e[:, None], blk_e[None, :], N_EXPERTS), axis=1)
    nxt = jnp.where(nxt == N_EXPERTS, blk_e, nxt).astype(jnp.int32)
    change = jnp.concatenate([jnp.zeros((1,), jnp.int32), (blk_e[1:] != blk_e[:-1]).astype(jnp.int32)])
    wslot = (jnp.cumsum(change) % 2).astype(jnp.int32)
    src3 = src_tok.reshape(n_blk, 1, tr)
    smem_idx = lambda f: pl.BlockSpec((1, 1, tr), f, memory_space=pltpu.SMEM)
    bias = lambda width, s: pl.BlockSpec((None, None, 1, width), lambda i, be, *_: (layer, be[2 * i + s], 0, 0))
    return pl.pallas_call(
        functools.partial(_expert_kernel, layer=layer),
        out_shape=jax.ShapeDtypeStruct((n_blk * tr * SEGS, LANES), F32),
        grid_spec=pltpu.PrefetchScalarGridSpec(
            num_scalar_prefetch=4,
            grid=(n_blk // 2,),
            in_specs=[
                smem_idx(lambda i, *_: (0, 0, 0)),
                smem_idx(lambda i, *_: (2 * i + 1, 0, 0)),
                smem_idx(lambda i, *_: (jnp.minimum(2 * i + 2, last), 0, 0)),
                pl.BlockSpec(memory_space=pl.ANY),
                pl.BlockSpec(memory_space=pl.ANY),
                bias(f2, 0), bias(f2, 1),
                pl.BlockSpec(memory_space=pl.ANY),
                bias(d, 0), bias(d, 1),
            ],
            out_specs=pl.BlockSpec((2 * tr * SEGS, LANES), lambda i, *_: (i, 0)),
            scratch_shapes=[pltpu.VMEM((2, tr * SEGS, LANES), F32), pltpu.SemaphoreType.DMA((2,)),
                            pltpu.VMEM((2, d, f2), F32), pltpu.VMEM((2, D_FF, d), F32),
                            pltpu.SemaphoreType.DMA((2, 2)),
                            pltpu.VMEM((d, f2), BF16), pltpu.VMEM((D_FF, d), BF16)],
        ),
        compiler_params=_cparams(("arbitrary",), row_dma=True),
        name="moe_experts",
    )(blk_e, n_used, nxt, wslot, src3, src3, src3, h2,
      w_gu, b_gu.reshape(depth, e, 1, f2), b_gu.reshape(depth, e, 1, f2),
      w_dn, b_dn.reshape(depth, e, 1, d), b_dn.reshape(depth, e, 1, d))


def _combine_kernel(d0_ref, d1_ref, y_hbm, meta_ref, x_ref, mod_ref, gn_ref, o_ref, buf, sems):
    i = pl.program_id(0)
    tb = o_ref.shape[0]
    slot = i % 2

    def gather(idx_ref, s):
        def issue(r, carry):
            for k in range(TOP_K):
                row = idx_ref[0, 0, k * tb + r]
                pltpu.make_async_copy(_row_tile(y_hbm, row), _row_tile(buf.at[s, k], r),
                                      sems.at[s]).start(priority=k % 2)
            return carry

        lax.fori_loop(0, tb, issue, 0, unroll=4)

    @pl.when(i == 0)
    def _():
        gather(d0_ref, 0)

    @pl.when(i + 1 < pl.num_programs(0))
    def _():
        gather(d1_ref, 1 - slot)

    for k in range(TOP_K):
        pltpu.make_async_copy(y_hbm.at[pl.ds(0, tb * SEGS), :], buf.at[slot, k], sems.at[slot]).wait()
    gates = meta_ref[...].T
    y2 = jnp.zeros(o_ref.shape, F32)
    for k in range(TOP_K):
        y2 = y2 + _load_tiled(buf.at[slot, k], tb) * gates[:, 2 * TOP_K + k:2 * TOP_K + k + 1]
    o_ref[...] = x_ref[...] + mod_ref[0, 5:6, :] * _rms(y2, gn_ref[3:4, :])


def _combine_call(dest, y_sorted, meta, x, mod, gn, cb_rows):
    t, d = x.shape
    tb = COMBINE_ROWS
    nb = t // tb
    cbb = max(cb_rows // tb, 1)
    sel = (lambda i: (jnp.minimum(i // cbb, 1), 0, 0)) if cb_rows else (lambda i: (1, 0, 0))
    dest3 = dest.reshape(TOP_K, nb, tb).transpose(1, 0, 2).reshape(nb, 1, TOP_K * tb)
    return pl.pallas_call(
        _combine_kernel,
        out_shape=jax.ShapeDtypeStruct((t, d), F32),
        grid=(nb,),
        in_specs=[
            pl.BlockSpec((1, 1, tb * TOP_K), lambda i: (0, 0, 0), memory_space=pltpu.SMEM),
            pl.BlockSpec((1, 1, tb * TOP_K), lambda i: (jnp.minimum(i + 1, nb - 1), 0, 0),
                         memory_space=pltpu.SMEM),
            pl.BlockSpec(memory_space=pl.ANY),
            pl.BlockSpec((ROUTE_ROWS, tb), lambda i: (0, i)),
            pl.BlockSpec((tb, d), lambda i: (i, 0)),
            pl.BlockSpec((1, 6, d), sel),
            pl.BlockSpec((4, d), lambda i: (0, 0)),
        ],
        out_specs=pl.BlockSpec((tb, d), lambda i: (i, 0)),
        scratch_shapes=[pltpu.VMEM((2, TOP_K, tb * SEGS, LANES), F32), pltpu.SemaphoreType.DMA((2,))],
        compiler_params=_cparams(("arbitrary",), row_dma=True),
        name="moe_combine",
    )(dest3, dest3, y_sorted, meta, x, mod, gn)


INVERT_BLOCK = 2048


def _invert_kernel(dest_ref, zeros_hbm, src_ref):
    k, j = pl.program_id(0), pl.program_id(1)
    n = dest_ref.shape[2]

    @pl.when((k == 0) & (j == 0))
    def _():
        pltpu.sync_copy(zeros_hbm, src_ref)

    tok0 = j * n

    def put(t, carry):
        src_ref[dest_ref[0, 0, t]] = tok0 + t
        return carry

    lax.fori_loop(0, n, put, 0, unroll=32)


def _invert_call(dest, n_rows):
    t = dest.shape[1]
    blk = max(b for b in range(LANES, INVERT_BLOCK + 1, LANES) if t % b == 0)
    nb = t // blk
    return pl.pallas_call(
        _invert_kernel,
        out_shape=jax.ShapeDtypeStruct((n_rows,), jnp.int32),
        grid=(TOP_K, nb),
        in_specs=[pl.BlockSpec((1, 1, blk), lambda k, j: (k * nb + j, 0, 0), memory_space=pltpu.SMEM),
                  pl.BlockSpec(memory_space=pl.ANY)],
        out_specs=pl.BlockSpec(memory_space=pltpu.SMEM),
        compiler_params=_cparams(("arbitrary", "arbitrary")),
        name="moe_invert",
    )(dest.reshape(TOP_K * nb, 1, blk), jnp.zeros((n_rows,), jnp.int32))


def _moe(h2, logits, x, mod, gn, layer, w_gu, b_gu, w_dn, b_dn, cb_rows):
    t = logits.shape[0]
    tr = MOE_ROWS
    meta, counts = _router_call(logits)
    e_idx = meta[:TOP_K].astype(jnp.int32)
    rank = meta[TOP_K:2 * TOP_K].astype(jnp.int32)
    counts = counts.astype(jnp.int32)
    padded = (counts + tr - 1) // tr * tr
    p_end = jnp.cumsum(padded)
    p_start = p_end - padded
    dest = rank
    for e in range(N_EXPERTS):
        dest = dest + jnp.where(e_idx == e, p_start[e], 0)
    a = t * TOP_K
    n_blk = (a + N_EXPERTS * (tr - 1) + tr - 1) // tr
    n_blk += n_blk % 2
    src_tok = _invert_call(dest, n_blk * tr)
    blk_start = jnp.arange(n_blk, dtype=jnp.int32) * tr
    blk_e = jnp.sum((p_end[None, :] <= blk_start[:, None]).astype(jnp.int32), axis=1)
    blk_e = jnp.minimum(blk_e, N_EXPERTS - 1)
    n_used = (p_end[-1] // tr).astype(jnp.int32).reshape(1)
    ys = _expert_call(blk_e, n_used, src_tok, h2, layer, w_gu, b_gu, w_dn, b_dn)
    return _combine_call(dest, ys, meta, x, mod, gn, cb_rows)


def _hgrn_in_kernel(x_ref, mod_ref, gn_ref, w_ref, lb_ref, q_ref, kf_ref, lf_ref, kb_ref, lbw_ref,
                    i_ref, gs_ref):
    x = x_ref[...]
    h = _rms(x, gn_ref[...]) * (1.0 + mod_ref[0, 1:2, :]) + mod_ref[0, 0:1, :]
    hb = h.astype(BF16)
    f = D_MODEL

    def proj(sec):
        return jnp.dot(hb, w_ref[:, sec * f:(sec + 1) * f], preferred_element_type=F32)

    z = proj(0)
    q_ref[...] = _silu(z).astype(q_ref.dtype)
    for sec, k_ref, l_ref in ((1, kf_ref, lf_ref), (2, kb_ref, lbw_ref)):
        z = proj(sec)
        lb = lb_ref[sec - 1:sec, :]
        sg = _sigmoid(z)
        l_ref[...] = jnp.log(lb + (1.0 - lb) * sg)
        k_ref[...] = ((1.0 - lb) * (1.0 - sg)).astype(k_ref.dtype)
    i_ref[...] = proj(3).astype(i_ref.dtype)
    z = proj(4)
    gs_ref[...] = _silu(z).astype(gs_ref.dtype)


def _hgrn_in_call(xs, mod, gn, w, lb, cb):
    t, d = xs.shape
    tm = ROW_BLOCK
    sel = lambda i: (jnp.minimum(i // cb, 1), 0, 0)
    row = pl.BlockSpec((tm, d), lambda i: (i, 0))
    dts = (BF16, BF16, F32, BF16, F32, BF16, BF16)
    return pl.pallas_call(
        _hgrn_in_kernel,
        out_shape=tuple(jax.ShapeDtypeStruct((t, d), dt) for dt in dts),
        grid=(t // tm,),
        in_specs=[row, pl.BlockSpec((1, 6, d), sel), pl.BlockSpec((1, d), lambda i: (0, 0)),
                  pl.BlockSpec(w.shape, lambda i: (0, 0)), pl.BlockSpec((2, d), lambda i: (0, 0))],
        out_specs=tuple(row for _ in dts),
        compiler_params=_cparams(("parallel",)),
        name="hgrn_in_proj",
    )(xs, mod, gn, w, lb)


def _split3(x):
    hi = x.astype(BF16)
    r1 = x - hi.astype(F32)
    mid = r1.astype(BF16)
    lo = (r1 - mid.astype(F32)).astype(BF16)
    return hi, mid, lo


def _scan_kernel(q_ref, k_ref, g_ref, v_ref, o_ref, st_ref, gcum_ref, *, reverse):
    j = pl.program_id(0)
    ch, sub = HGRN_CHUNK, HGRN_SUB
    ns = ch // sub
    dk = HGRN_DK

    @pl.when(j == 0)
    def _():
        st_ref[...] = jnp.zeros_like(st_ref)

    rr = lax.broadcasted_iota(jnp.int32, (ch, ch), 0)
    cc = lax.broadcasted_iota(jnp.int32, (ch, ch), 1)
    causal = (cc >= rr) if reverse else (cc <= rr)
    tri = jnp.where(causal, 1.0, 0.0).astype(BF16)
    hi, mid, lo = _split3(g_ref[...])
    gcum_ref[...] = LOG2E * (jnp.dot(tri, hi, preferred_element_type=F32)
                             + jnp.dot(tri, mid, preferred_element_type=F32)
                             + jnp.dot(tri, lo, preferred_element_type=F32))
    key_blocks = list(range(1, ns)) if reverse else list(range(ns - 1))
    nkb = len(key_blocks)
    kbd_mask = (lax.broadcasted_iota(jnp.int32, (ch, nkb * dk), 1) // dk + key_blocks[0]
                == lax.broadcasted_iota(jnp.int32, (ch, nkb * dk), 0) // sub)
    kdg_mask = (lax.broadcasted_iota(jnp.int32, (ch, sub * dk), 1) // dk
                == lax.broadcasted_iota(jnp.int32, (ch, sub * dk), 0) % sub)
    end_row = 0 if reverse else ch - 1
    edge = 0 if reverse else sub - 1
    nt = (((1,), (1,)), ((), ()))
    tn = (((0,), (0,)), ((), ()))

    grp = HGRN_GROUP
    rr_g = lax.broadcasted_iota(jnp.int32, (ch, grp * ch), 0)
    cc_g = lax.broadcasted_iota(jnp.int32, (ch, grp * ch), 1) % ch
    diag_mask_g = ((rr_g // sub) == (cc_g // sub)) & ((cc_g >= rr_g) if reverse else (cc_g <= rr_g))

    def place(x, i):
        w = x.shape[1]
        parts = []
        if i:
            parts.append(jnp.zeros((x.shape[0], i * w), x.dtype))
        parts.append(x)
        if grp - 1 - i:
            parts.append(jnp.zeros((x.shape[0], (grp - 1 - i) * w), x.dtype))
        return jnp.concatenate(parts, axis=1)

    for g0 in range(0, HGRN_HEADS, grp):
        qgs, q_cats, p_cats, kbd_rows, kdg_rows, st_rows, v_rows = [], [], [], [], [], [], []
        for i in range(grp):
            h = g0 + i
            hs = slice(h * dk, (h + 1) * dk)
            q = q_ref[:, hs].astype(F32)
            kb = k_ref[:, hs]
            k = kb.astype(F32)
            v = v_ref[:, hs]
            gc = gcum_ref[:, hs]
            st = st_ref[h]
            g_end = gc[end_row:end_row + 1, :]
            qgs.append((q * jnp.exp2(gc)).astype(BF16))
            st_rows.append(place(st.astype(BF16), i))
            v_rows.append(place(v, i))
            k_end = (k * jnp.exp2(g_end - gc)).astype(BF16)
            st_ref[h] = st * jnp.exp2(g_end) + lax.dot_general(v, k_end, tn, preferred_element_type=F32)
            gc4 = gc.reshape(ns, sub, dk)
            g_edge = jnp.broadcast_to(gc4[:, edge:edge + 1, :], (ns, sub, dk)).reshape(ch, dk)
            k_rel = (k * jnp.exp2(g_edge - gc)).astype(BF16)
            k_bd = jnp.where(kbd_mask, jnp.concatenate([k_rel] * nkb, axis=1), jnp.zeros((), BF16))
            kbd_rows.append(place(k_bd, i))
            q_parts = []
            for jb in key_blocks:
                row = jb * sub + edge
                rows = slice(0, jb * sub) if reverse else slice((jb + 1) * sub, ch)
                part = q[rows] * jnp.exp2(gc[rows] - gc[row:row + 1, :])
                pad = jnp.zeros((ch - part.shape[0], dk), F32)
                full = jnp.concatenate([part, pad] if reverse else [pad, part], axis=0)
                q_parts.append(full.astype(BF16))
            q_cats.append(jnp.concatenate(q_parts, axis=1))
            p_parts = []
            for s in range(sub):
                g_s = jnp.broadcast_to(gc4[:, s:s + 1, :], (ns, sub, dk)).reshape(ch, dk)
                p_parts.append((q * jnp.exp2(jnp.minimum(gc - g_s, 0.0))).astype(BF16))
            p_cats.append(jnp.concatenate(p_parts, axis=1))
            k_dg = jnp.where(kdg_mask, jnp.concatenate([kb] * sub, axis=1), jnp.zeros((), BF16))
            kdg_rows.append(place(k_dg, i))
        s_off = lax.dot_general(jnp.concatenate(q_cats, axis=1), jnp.concatenate(kbd_rows, axis=0), nt,
                                preferred_element_type=F32)
        s_diag = lax.dot_general(jnp.concatenate(p_cats, axis=1), jnp.concatenate(kdg_rows, axis=0), nt,
                                 preferred_element_type=F32)
        a = s_off + jnp.where(diag_mask_g, s_diag, 0.0)
        o = lax.dot_general(jnp.concatenate(qgs, axis=1), jnp.concatenate(st_rows, axis=0), nt,
                            preferred_element_type=F32)
        o = o + jnp.dot(a.astype(BF16), jnp.concatenate(v_rows, axis=0), preferred_element_type=F32)
        o_ref[:, g0 * dk:(g0 + grp) * dk] = o


def _scan_call(q, k, lg, v, c_len, reverse):
    t, d = q.shape
    ch = HGRN_CHUNK
    n_ch = t // ch
    cc = c_len // ch
    if reverse:
        idx = lambda j: (jnp.where(j < cc, cc - 1 - j, n_ch - 1 - (j - cc)), 0)
    else:
        idx = lambda j: (j, 0)
    blk = pl.BlockSpec((ch, d), idx)
    return pl.pallas_call(
        functools.partial(_scan_kernel, reverse=reverse),
        out_shape=jax.ShapeDtypeStruct((t, d), F32),
        grid=(n_ch,),
        in_specs=[blk, blk, blk, blk],
        out_specs=blk,
        scratch_shapes=[pltpu.VMEM((HGRN_HEADS, HGRN_DK, HGRN_DK), F32), pltpu.VMEM((ch, d), F32)],
        compiler_params=_cparams(("arbitrary",)),
        name="hgrn_scan_bwd" if reverse else "hgrn_scan_fwd",
    )(q, k, lg, v)


def _rope_tables(l, c_len):
    n = HEAD_DIM // 4
    inv_freq = ROPE_BASE ** (-jnp.arange(n, dtype=F32) / n)
    rows = l // GRID_W
    ang_row = jnp.arange(rows, dtype=F32)[:, None] * inv_freq[None, :]
    ang_col = jnp.arange(GRID_W, dtype=F32)[:, None] * inv_freq[None, :]
    cr, sr, cc, sc = jnp.cos(ang_row), jnp.sin(ang_row), jnp.cos(ang_col), jnp.sin(ang_col)
    zr, zc = jnp.zeros_like(cr), jnp.zeros_like(cc)
    cos_r = jnp.concatenate([cr, cr, zr, zr] * 2, axis=1)
    sin_r = jnp.concatenate([-sr, sr, zr, zr] * 2, axis=1)
    cos_c = jnp.concatenate([zc, zc, cc, cc] * 2, axis=1)
    sin_c = jnp.concatenate([zc, zc, -sc, sc] * 2, axis=1)
    cos = (cos_r[:, None, :] + cos_c[None, :, :]).reshape(l, LANES)
    sin = (sin_r[:, None, :] + sin_c[None, :, :]).reshape(l, LANES)
    cos = jnp.concatenate([jnp.ones((c_len, LANES), F32), cos], axis=0)
    sin = jnp.concatenate([jnp.zeros((c_len, LANES), F32), sin], axis=0)
    return cos, sin


def kernel(x, c, ctx, c_ctx, ada_w, ada_b, norm_g, attn_w_qkv, attn_b_qkv, attn_sink, attn_w_o,
           hgrn_w_in, hgrn_lb, hgrn_norm_g, hgrn_w_o, router_w, router_b, moe_w_gu, moe_b_gu,
           moe_w_dn, moe_b_dn):
    b, l, d = x.shape
    c_len = ctx.shape[1]
    depth = ada_w.shape[0]
    assert b == 1 and d == D_MODEL and depth == 2
    assert c_len % ROW_BLOCK == 0 and l % ROW_BLOCK == 0
    cb = c_len // ROW_BLOCK

    cs = jnp.zeros((8, d), F32).at[0].set(c_ctx).at[1].set(c[0])
    mods = _ada_call(cs, ada_w, ada_b)
    mods = mods[:, :2].reshape(depth, 2, 6, d)

    cos, sin = _rope_tables(l, c_len)

    q, k4, v4 = _qkv_call(ctx[0], x[0], mods[0], norm_g[0, 0:1], attn_w_qkv[0].astype(BF16),
                          attn_b_qkv[0].reshape(1, -1), cos, sin, cb)
    o = _attn_call(attn_sink[0], q, k4, v4, c_len)
    xs, h2, logits = _out_call("attn", (o,), (ctx[0], x[0]), mods[0], norm_g[0], attn_w_o[0].astype(BF16),
                               router_w[0], router_b[0].reshape(1, -1), None, cb, 0)
    xs = _moe(h2, logits, xs, mods[0], norm_g[0], 0, moe_w_gu, moe_b_gu, moe_w_dn, moe_b_dn, c_len)

    lb_soft = jax.nn.softmax(hgrn_lb.astype(F32), axis=0)
    lb = jnp.cumsum(lb_soft, axis=0)[1] - lb_soft[0]
    qh, kf, lf, kb, lbw, iv, gs = _hgrn_in_call(xs, mods[1], norm_g[1, 0:1], hgrn_w_in[0].astype(BF16),
                                                lb, cb)
    o_f = _scan_call(qh, kf, lf, iv, c_len, reverse=False)
    o_b = _scan_call(qh, kb, lbw, iv, c_len, reverse=True)
    x_lat, h2, logits = _out_call("hgrn", (o_f, o_b, gs), xs, mods[1], norm_g[1],
                                  hgrn_w_o[0].astype(BF16), router_w[1], router_b[1].reshape(1, -1),
                                  hgrn_norm_g[0].reshape(1, -1), cb, cb)
    out = _moe(h2, logits, x_lat, mods[1], norm_g[1], 1, moe_w_gu, moe_b_gu, moe_w_dn, moe_b_dn, 0)
    return out[None]
```

```python
import functools

import jax
import jax.numpy as jnp
from jax import lax
from jax.experimental import pallas as pl
from jax.experimental.pallas import tpu as pltpu

D_MODEL = 1024
GRID_W = 64
RMS_EPS = 1e-6

ATTN_HEADS = 16
ATTN_KV_HEADS = 2
HEAD_DIM = 64
Q_DIM = ATTN_HEADS * HEAD_DIM
KV_DIM = ATTN_KV_HEADS * HEAD_DIM
QKV_DIM = Q_DIM + 2 * KV_DIM
WINDOW = 128
ATTN_BLOCK = 128
ROPE_BASE = 10000.0

HGRN_HEADS = 8
HGRN_DK = 128
HGRN_CHUNK = 64
HGRN_SUB = 8
HGRN_GROUP = 1

N_EXPERTS = 32
TOP_K = 4
D_FF = 1024
SWIGLU_LIMIT = 7.0
SWIGLU_ALPHA = 1.702

LANES = 128
ROW_BLOCK = 256
MOE_ROWS = 256
COMBINE_ROWS = 256
VMEM_LIMIT = 56 * 1024 * 1024

F32 = jnp.float32
BF16 = jnp.bfloat16
NEG_BIG = -1e30
LOG2E = 1.4426950408889634


def _cparams(sem, row_dma=False):
    return pltpu.CompilerParams(dimension_semantics=sem, vmem_limit_bytes=VMEM_LIMIT,
                                disable_bounds_checks=row_dma)


def _rms(x, g):
    return x * lax.rsqrt(jnp.mean(x * x, axis=-1, keepdims=True) + RMS_EPS) * g


def _sigmoid(x):
    return 1.0 / (1.0 + jnp.exp(-x))


def _silu(x):
    return x * (0.5 * jnp.tanh(0.5 * x) + 0.5)


SEGS = D_MODEL // LANES


def _store_tiled(ref, val):
    n = val.shape[0]
    for s in range(SEGS):
        ref[pl.ds(s, n, stride=SEGS), :] = val[:, s * LANES:(s + 1) * LANES]


def _load_tiled(ref, n):
    return jnp.concatenate([ref[pl.ds(s, n, stride=SEGS), :] for s in range(SEGS)], axis=1)


def _row_tile(ref, r):
    return ref.at[pl.ds(pl.multiple_of(r * SEGS, SEGS), SEGS), :]


def _ada_kernel(c_ref, w_ref, b_ref, o_ref):
    c = c_ref[...]
    s = c * _sigmoid(c)
    o_ref[0] = jnp.dot(s, w_ref[0], precision=lax.Precision.HIGHEST,
                       preferred_element_type=F32) + b_ref[0]


def _ada_call(cs, ada_w, ada_b):
    depth, d, n = ada_w.shape
    tn = 1024
    return pl.pallas_call(
        _ada_kernel,
        out_shape=jax.ShapeDtypeStruct((depth, 8, n), F32),
        grid=(depth, n // tn),
        in_specs=[
            pl.BlockSpec((8, d), lambda i, j: (0, 0)),
            pl.BlockSpec((1, d, tn), lambda i, j: (i, 0, j)),
            pl.BlockSpec((1, 1, tn), lambda i, j: (i, 0, j)),
        ],
        out_specs=pl.BlockSpec((1, 8, tn), lambda i, j: (i, 0, j)),
        compiler_params=_cparams(("arbitrary", "arbitrary")),
        name="adaln",
    )(cs, ada_w, ada_b.reshape(depth, 1, n))


def _swap16(t):
    lane = lax.broadcasted_iota(jnp.int32, t.shape, 1)
    return jnp.where(lane % 32 < 16, pltpu.roll(t, LANES - 16, 1), pltpu.roll(t, 16, 1))


def _stream_rows(c_ref, x_ref, cb):
    return jnp.where(pl.program_id(0) < cb, c_ref[...], x_ref[...])


def _stream_specs(tm, d, cb):
    return [pl.BlockSpec((tm, d), lambda i: (jnp.minimum(i, cb - 1), 0)),
            pl.BlockSpec((tm, d), lambda i: (jnp.maximum(i - cb, 0), 0))]


def _qkv_kernel(c_ref, x_ref, mod_ref, gn_ref, w_ref, b_ref, cos_ref, sin_ref, q_ref, k4_ref, v4_ref, *, cb):
    x = _stream_rows(c_ref, x_ref, cb)
    h = _rms(x, gn_ref[...]) * (1.0 + mod_ref[0, 1:2, :]) + mod_ref[0, 0:1, :]
    hb = h.astype(BF16)
    cos = cos_ref[...]
    sin = sin_ref[...]
    nq = Q_DIM // LANES
    for j in range(nq + 1):
        sl = slice(j * LANES, (j + 1) * LANES)
        t = jnp.dot(hb, w_ref[:, sl], preferred_element_type=F32) + b_ref[:, sl]
        t = t * cos + _swap16(t) * sin
        if j < nq:
            q_ref[:, sl] = (t * (HEAD_DIM ** -0.5 * LOG2E)).astype(q_ref.dtype)
        else:
            kt = t
    sl = slice(Q_DIM + KV_DIM, QKV_DIM)
    vt = jnp.dot(hb, w_ref[:, sl], preferred_element_type=F32) + b_ref[:, sl]
    lo = lax.broadcasted_iota(jnp.int32, kt.shape, 1) < HEAD_DIM
    for t, ref in ((kt, k4_ref), (vt, v4_ref)):
        sw = pltpu.roll(t, HEAD_DIM, 1)
        ref[:, 0 * LANES:1 * LANES] = jnp.where(lo, t, 0.0).astype(ref.dtype)
        ref[:, 1 * LANES:2 * LANES] = jnp.where(lo, 0.0, sw).astype(ref.dtype)
        ref[:, 2 * LANES:3 * LANES] = jnp.where(lo, sw, 0.0).astype(ref.dtype)
        ref[:, 3 * LANES:4 * LANES] = jnp.where(lo, 0.0, t).astype(ref.dtype)


def _qkv_call(ctx, x, mod, gn, w, b, cos, sin, cb):
    d = x.shape[1]
    t = ctx.shape[0] + x.shape[0]
    tm = ROW_BLOCK
    sel = lambda i: (jnp.minimum(i // cb, 1), 0, 0)
    return pl.pallas_call(
        functools.partial(_qkv_kernel, cb=cb),
        out_shape=(jax.ShapeDtypeStruct((t, Q_DIM), BF16),
                   jax.ShapeDtypeStruct((t, 4 * LANES), BF16),
                   jax.ShapeDtypeStruct((t, 4 * LANES), BF16)),
        grid=(t // tm,),
        in_specs=_stream_specs(tm, d, cb) + [
            pl.BlockSpec((1, 6, d), sel),
            pl.BlockSpec((1, d), lambda i: (0, 0)),
            pl.BlockSpec((d, QKV_DIM), lambda i: (0, 0)),
            pl.BlockSpec((1, QKV_DIM), lambda i: (0, 0)),
            pl.BlockSpec((tm, LANES), lambda i: (i, 0)),
            pl.BlockSpec((tm, LANES), lambda i: (i, 0)),
        ],
        out_specs=(pl.BlockSpec((tm, Q_DIM), lambda i: (i, 0)),
                   pl.BlockSpec((tm, 4 * LANES), lambda i: (i, 0)),
                   pl.BlockSpec((tm, 4 * LANES), lambda i: (i, 0))),
        compiler_params=_cparams(("parallel",)),
        name="qkv_rope",
    )(ctx, x, mod, gn, w, b, cos, sin)


def _attn_kernel(sink_ref, q_ref, kp_ref, kc_ref, kn_ref, kx_ref, vp_ref, vc_ref, vn_ref, vx_ref,
                 o_ref, *, cb, n_lat):
    n = pl.program_id(0)
    blk = ATTN_BLOCK
    c = kx_ref.shape[0]
    nw = 3 * blk
    s = lax.broadcasted_iota(jnp.int32, (nw + c, blk), 0)
    r = lax.broadcasted_iota(jnp.int32, (nw + c, blk), 1)
    q_pos = (n - cb) * blk + r
    k_pos = (n - cb - 1) * blk + s
    win_ok = (jnp.abs(q_pos - k_pos) <= WINDOW) & (k_pos >= 0) & (k_pos < n_lat) & (n >= cb)
    valid = (s >= nw) | win_ok
    k_all = jnp.concatenate([kp_ref[...], kc_ref[...], kn_ref[...], kx_ref[...]], axis=0)
    v_all = jnp.concatenate([vp_ref[...], vc_ref[...], vn_ref[...], vx_ref[...]], axis=0)
    n_var = 2 * ATTN_KV_HEADS
    v_t = [v_all[:, j * LANES:(j + 1) * LANES].astype(F32).T.astype(BF16) for j in range(n_var)]
    nt = (((1,), (1,)), ((), ()))
    group = ATTN_HEADS // ATTN_KV_HEADS
    for p in range(ATTN_HEADS // 2):
        g = (2 * p) // group
        qp = q_ref[:, p * LANES:(p + 1) * LANES]
        o_t = jnp.zeros((LANES, blk), F32)
        for half in range(2):
            j = 2 * g + half
            sc = lax.dot_general(k_all[:, j * LANES:(j + 1) * LANES], qp, nt,
                                 preferred_element_type=F32)
            sc = jnp.where(valid, sc, NEG_BIG)
            sink = sink_ref[2 * p + half] * LOG2E
            m = jnp.maximum(jnp.max(sc, axis=0, keepdims=True), sink)
            e = jnp.exp2(sc - m)
            denom = jnp.sum(e, axis=0, keepdims=True) + jnp.exp2(sink - m)
            pv = jnp.dot(v_t[j], e.astype(BF16), preferred_element_type=F32)
            o_t = o_t + pv / denom
        o_ref[:, p * LANES:(p + 1) * LANES] = o_t.T.astype(o_ref.dtype)


def _attn_call(sink, q, k4, v4, c_len):
    t = q.shape[0]
    blk = ATTN_BLOCK
    cb = c_len // blk
    nb = t // blk
    n_lat = t - c_len
    last = nb - 1
    kw = 4 * LANES
    spec_q = pl.BlockSpec((blk, Q_DIM), lambda n: (n, 0))
    prev = pl.BlockSpec((blk, kw), lambda n: (jnp.maximum(n - 1, 0), 0))
    cur = pl.BlockSpec((blk, kw), lambda n: (n, 0))
    nxt = pl.BlockSpec((blk, kw), lambda n: (jnp.minimum(n + 1, last), 0))
    ctx = pl.BlockSpec((c_len, kw), lambda n: (0, 0))
    return pl.pallas_call(
        functools.partial(_attn_kernel, cb=cb, n_lat=n_lat),
        out_shape=jax.ShapeDtypeStruct((t, Q_DIM), BF16),
        grid=(nb,),
        in_specs=[pl.BlockSpec(memory_space=pltpu.SMEM), spec_q,
                  prev, cur, nxt, ctx, prev, cur, nxt, ctx],
        out_specs=pl.BlockSpec((blk, Q_DIM), lambda n: (n, 0)),
        compiler_params=_cparams(("parallel",)),
        name="window_attn",
    )(sink, q, k4, k4, k4, k4, v4, v4, v4, v4)


def _residual_tail(y, x, mod_ref, gn_ref, rw_ref, rb_ref, xo_ref, h2_ref, lg_ref):
    x_new = x + mod_ref[0, 2:3, :] * _rms(y, gn_ref[1:2, :])
    h2 = _rms(x_new, gn_ref[2:3, :]) * (1.0 + mod_ref[0, 4:5, :]) + mod_ref[0, 3:4, :]
    xo_ref[...] = x_new
    _store_tiled(h2_ref, h2)
    hi, mid, lo = _split3(h2)
    lane = lax.broadcasted_iota(jnp.int32, (h2.shape[0], LANES), 1)
    w = rw_ref[...]
    r = (jnp.dot(hi, w, preferred_element_type=F32)
         + jnp.where(lane < 2 * N_EXPERTS, jnp.dot(mid, w, preferred_element_type=F32), 0.0)
         + jnp.where(lane < N_EXPERTS, jnp.dot(lo, w, preferred_element_type=F32), 0.0))
    r = r + pltpu.roll(r, LANES - N_EXPERTS, 1) + pltpu.roll(r, LANES - 2 * N_EXPERTS, 1)
    lg_ref[...] = r[:, :N_EXPERTS] + rb_ref[...]


def _attn_out_kernel(a_ref, c_ref, x_ref, mod_ref, gn_ref, w_ref, rw_ref, rb_ref, xo_ref, h2_ref, lg_ref,
                     *, cb):
    y = jnp.dot(a_ref[...], w_ref[...], preferred_element_type=F32)
    _residual_tail(y, _stream_rows(c_ref, x_ref, cb), mod_ref, gn_ref, rw_ref, rb_ref, xo_ref, h2_ref, lg_ref)


def _hgrn_out_kernel(of_ref, ob_ref, gs_ref, hg_ref, x_ref, mod_ref, gn_ref, w_ref, rw_ref, rb_ref,
                     xo_ref, h2_ref, lg_ref):
    parts = []
    for h in range(HGRN_HEADS):
        sl = slice(h * HGRN_DK, (h + 1) * HGRN_DK)
        o = of_ref[:, sl] + ob_ref[:, sl]
        parts.append(_rms(o, hg_ref[:, sl]))
    a = jnp.concatenate(parts, axis=1) * gs_ref[...].astype(F32)
    y = jnp.dot(a.astype(BF16), w_ref[...], preferred_element_type=F32)
    _residual_tail(y, x_ref[...], mod_ref, gn_ref, rw_ref, rb_ref, xo_ref, h2_ref, lg_ref)


def _out_call(kind, acts, x, mod, gn, w, rw, rb, hg, cb, off):
    t_out = (x[0].shape[0] + x[1].shape[0]) if kind == "attn" else x.shape[0] - off * ROW_BLOCK
    d = D_MODEL
    tm = ROW_BLOCK
    w_hi, w_mid, w_lo = _split3(rw)
    rw = jnp.concatenate([w_hi, w_mid, w_lo, jnp.zeros_like(w_hi)], axis=1)
    row = lambda i: (i + off, 0)
    sel = lambda i: (jnp.minimum((i + off) // cb, 1), 0, 0)
    const = lambda i: (0, 0)
    common_specs = [
        pl.BlockSpec((1, 6, d), sel),
        pl.BlockSpec((4, d), const),
        pl.BlockSpec((d, d), const),
        pl.BlockSpec((d, LANES), const),
        pl.BlockSpec((1, N_EXPERTS), const),
    ]
    if kind == "attn":
        body = functools.partial(_attn_out_kernel, cb=cb)
        in_specs = [pl.BlockSpec((tm, d), row)] + _stream_specs(tm, d, cb) + common_specs
        args = (acts[0], x[0], x[1], mod, gn, w, rw, rb)
    else:
        body = _hgrn_out_kernel
        in_specs = ([pl.BlockSpec((tm, d), row)] * 3 + [pl.BlockSpec((1, d), const)]
                    + [pl.BlockSpec((tm, d), row)] + common_specs)
        args = (acts[0], acts[1], acts[2], hg, x, mod, gn, w, rw, rb)
    return pl.pallas_call(
        body,
        out_shape=(jax.ShapeDtypeStruct((t_out, d), F32),
                   jax.ShapeDtypeStruct((t_out * SEGS, LANES), F32),
                   jax.ShapeDtypeStruct((t_out, N_EXPERTS), F32)),
        grid=(t_out // tm,),
        in_specs=in_specs,
        out_specs=(pl.BlockSpec((tm, d), lambda i: (i, 0)),
                   pl.BlockSpec((tm * SEGS, LANES), lambda i: (i, 0)),
                   pl.BlockSpec((tm, N_EXPERTS), lambda i: (i, 0))),
        compiler_params=_cparams(("parallel",)),
        name=kind + "_out_residual",
    )(*args)


def _router_kernel(lg_ref, meta_ref, cnt_ref, run_ref):
    i = pl.program_id(0)

    @pl.when(i == 0)
    def _():
        run_ref[...] = jnp.zeros_like(run_ref)

    lg = lg_ref[...].T
    ne, tb = lg.shape
    eid = lax.broadcasted_iota(jnp.int32, lg.shape, 0).astype(F32)
    sels, tops, idxs = [], [], []
    for _ in range(TOP_K):
        m = jnp.max(lg, axis=0, keepdims=True)
        idx = jnp.min(jnp.where(lg == m, eid, float(N_EXPERTS)), axis=0, keepdims=True)
        sel = eid == idx
        sels.append(sel)
        tops.append(m)
        idxs.append(idx)
        lg = jnp.where(sel, -jnp.inf, lg)
    ws = [jnp.exp(tk - tops[0]) for tk in tops]
    wsum = ws[0] + ws[1] + ws[2] + ws[3]
    chosen = (sels[0] | sels[1] | sels[2] | sels[3])
    onehot = jnp.where(chosen, 1.0, 0.0)
    rr = lax.broadcasted_iota(jnp.int32, (tb, tb), 0)
    cc = lax.broadcasted_iota(jnp.int32, (tb, tb), 1)
    tri = jnp.where(rr < cc, 1.0, 0.0).astype(BF16)
    before = jnp.dot(onehot.astype(BF16), tri, preferred_element_type=F32) + run_ref[:, 0:1]
    rows = list(idxs)
    rows += [jnp.sum(jnp.where(sels[k], before, 0.0), axis=0, keepdims=True) for k in range(TOP_K)]
    rows += [ws[k] / wsum for k in range(TOP_K)]
    rows.append(jnp.zeros((meta_ref.shape[0] - len(rows), tb), F32))
    meta_ref[...] = jnp.concatenate(rows, axis=0)
    run_ref[...] = run_ref[...] + jnp.sum(onehot, axis=1, keepdims=True)
    cnt_ref[...] = run_ref[...]


ROUTE_ROWS = 16


def _router_call(logits):
    t = logits.shape[0]
    tb = ROW_BLOCK
    meta, counts = pl.pallas_call(
        _router_kernel,
        out_shape=(jax.ShapeDtypeStruct((ROUTE_ROWS, t), F32),
                   jax.ShapeDtypeStruct((N_EXPERTS, LANES), F32)),
        grid=(t // tb,),
        in_specs=[pl.BlockSpec((tb, N_EXPERTS), lambda i: (i, 0))],
        out_specs=(pl.BlockSpec((ROUTE_ROWS, tb), lambda i: (0, i)),
                   pl.BlockSpec((N_EXPERTS, LANES), lambda i: (0, 0))),
        scratch_shapes=[pltpu.VMEM((N_EXPERTS, LANES), F32)],
        compiler_params=_cparams(("arbitrary",)),
        name="router_topk",
    )(logits)
    return meta, counts[:, 0]


def _expert_kernel(be_ref, nu_ref, nx_ref, ws_ref, src_first_ref, src_b_ref, src_c_ref, x_hbm, wgu_hbm, bgu_a_ref,
                   bgu_b_ref, wdn_hbm, bdn_a_ref, bdn_b_ref, o_ref, xbuf, sems, wgu_f, wdn_f, wsems, wgu_bf, wdn_bf,
                   *, layer):
    i = pl.program_id(0)
    rows = o_ref.shape[0] // (2 * SEGS)
    n_used = nu_ref[0]

    def gather(idx_ref, s):
        for r in range(rows):
            pltpu.make_async_copy(_row_tile(x_hbm, idx_ref[0, 0, r]),
                                  xbuf.at[s, pl.ds(r * SEGS, SEGS), :], sems.at[s]).start()

    def weight_copies(e, s):
        return (pltpu.make_async_copy(wgu_hbm.at[layer, e], wgu_f.at[s], wsems.at[s, 0]),
                pltpu.make_async_copy(wdn_hbm.at[layer, e], wdn_f.at[s], wsems.at[s, 1]))

    @pl.when(i == 0)
    def _():
        gather(src_first_ref, 0)
        for c in weight_copies(be_ref[0], ws_ref[0]):
            c.start(priority=1)

    for s, next_idx_ref, bgu_ref, bdn_ref in ((0, src_b_ref, bgu_a_ref, bdn_a_ref),
                                              (1, src_c_ref, bgu_b_ref, bdn_b_ref)):
        blk = 2 * i + s
        out = o_ref.at[pl.ds(s * rows * SEGS, rows * SEGS), :]

        @pl.when(blk + 1 < n_used)
        def _():
            gather(next_idx_ref, 1 - s)

        e_cur = be_ref[blk]
        wslot = ws_ref[blk]
        fresh = (blk == 0) | (e_cur != be_ref[jnp.maximum(blk - 1, 0)])

        @pl.when(fresh)
        def _():
            @pl.when(nx_ref[blk] != e_cur)
            def _():
                for c in weight_copies(nx_ref[blk], 1 - wslot):
                    c.start(priority=1)

            for c in weight_copies(e_cur, wslot):
                c.wait()
            wgu_bf[...] = wgu_f[wslot].astype(BF16)
            wdn_bf[...] = wdn_f[wslot].astype(BF16)

        @pl.when(blk < n_used)
        def _():
            pltpu.make_async_copy(x_hbm.at[pl.ds(0, rows * SEGS), :], xbuf.at[s], sems.at[s]).wait()
            xb = _load_tiled(xbuf.at[s], rows).astype(BF16)
            gu = jnp.dot(xb, wgu_bf[...], preferred_element_type=F32) + bgu_ref[...]
            g = jnp.minimum(gu[:, :D_FF], SWIGLU_LIMIT)
            u = jnp.clip(gu[:, D_FF:], -SWIGLU_LIMIT, SWIGLU_LIMIT)
            hdn = g * _sigmoid(SWIGLU_ALPHA * g) * (u + 1.0)
            y = jnp.dot(hdn.astype(BF16), wdn_bf[...], preferred_element_type=F32) + bdn_ref[...]
            _store_tiled(out, y)

        @pl.when(blk >= n_used)
        def _():
            out[...] = jnp.zeros(out.shape, F32)


def _expert_call(blk_e, n_used, src_tok, h2, layer, w_gu, b_gu, w_dn, b_dn):
    d = D_MODEL
    tr = MOE_ROWS
    n_blk = blk_e.shape[0]
    depth, e, _, f2 = w_gu.shape
    last = n_blk - 1
    nxt = jnp.min(jnp.where(blk_e[None, :] > blk_e[:, None], blk_e[None, :], N_EXPERTS), axis=1)
    nxt = jnp.where(nxt == N_EXPERTS, blk_e, nxt).astype(jnp.int32)
    change = jnp.concatenate([jnp.zeros((1,), jnp.int32), (blk_e[1:] != blk_e[:-1]).astype(jnp.int32)])
    wslot = (jnp.cumsum(change) % 2).astype(jnp.int32)
    src3 = src_tok.reshape(n_blk, 1, tr)
    smem_idx = lambda f: pl.BlockSpec((1, 1, tr), f, memory_space=pltpu.SMEM)
    bias = lambda width, s: pl.BlockSpec((None, None, 1, width), lambda i, be, *_: (layer, be[2 * i + s], 0, 0))
    return pl.pallas_call(
        functools.partial(_expert_kernel, layer=layer),
        out_shape=jax.ShapeDtypeStruct((n_blk * tr * SEGS, LANES), F32),
        grid_spec=pltpu.PrefetchScalarGridSpec(
            num_scalar_prefetch=4,
            grid=(n_blk // 2,),
            in_specs=[
                smem_idx(lambda i, *_: (0, 0, 0)),
                smem_idx(lambda i, *_: (2 * i + 1, 0, 0)),
                smem_idx(lambda i, *_: (jnp.minimum(2 * i + 2, last), 0, 0)),
                pl.BlockSpec(memory_space=pl.ANY),
                pl.BlockSpec(memory_space=pl.ANY),
                bias(f2, 0), bias(f2, 1),
                pl.BlockSpec(memory_space=pl.ANY),
                bias(d, 0), bias(d, 1),
            ],
            out_specs=pl.BlockSpec((2 * tr * SEGS, LANES), lambda i, *_: (i, 0)),
            scratch_shapes=[pltpu.VMEM((2, tr * SEGS, LANES), F32), pltpu.SemaphoreType.DMA((2,)),
                            pltpu.VMEM((2, d, f2), F32), pltpu.VMEM((2, D_FF, d), F32),
                            pltpu.SemaphoreType.DMA((2, 2)),
                            pltpu.VMEM((d, f2), BF16), pltpu.VMEM((D_FF, d), BF16)],
        ),
        compiler_params=_cparams(("arbitrary",), row_dma=True),
        name="moe_experts",
    )(blk_e, n_used, nxt, wslot, src3, src3, src3, h2,
      w_gu, b_gu.reshape(depth, e, 1, f2), b_gu.reshape(depth, e, 1, f2),
      w_dn, b_dn.reshape(depth, e, 1, d), b_dn.reshape(depth, e, 1, d))


def _combine_kernel(d0_ref, d1_ref, y_hbm, meta_ref, x_ref, mod_ref, gn_ref, o_ref, buf, sems):
    i = pl.program_id(0)
    tb = o_ref.shape[0]
    slot = i % 2

    def gather(idx_ref, s):
        for r in range(tb):
            for k in range(TOP_K):
                row = idx_ref[0, 0, k * tb + r]
                pltpu.make_async_copy(_row_tile(y_hbm, row), buf.at[s, k, pl.ds(r * SEGS, SEGS), :],
                                      sems.at[s]).start(priority=k % 2)

    @pl.when(i == 0)
    def _():
        gather(d0_ref, 0)

    @pl.when(i + 1 < pl.num_programs(0))
    def _():
        gather(d1_ref, 1 - slot)

    for k in range(TOP_K):
        pltpu.make_async_copy(y_hbm.at[pl.ds(0, tb * SEGS), :], buf.at[slot, k], sems.at[slot]).wait()
    gates = meta_ref[...].T
    y2 = jnp.zeros(o_ref.shape, F32)
    for k in range(TOP_K):
        y2 = y2 + _load_tiled(buf.at[slot, k], tb) * gates[:, 2 * TOP_K + k:2 * TOP_K + k + 1]
    o_ref[...] = x_ref[...] + mod_ref[0, 5:6, :] * _rms(y2, gn_ref[3:4, :])


def _combine_call(dest, y_sorted, meta, x, mod, gn, cb_rows):
    t, d = x.shape
    tb = COMBINE_ROWS
    nb = t // tb
    cbb = max(cb_rows // tb, 1)
    sel = (lambda i: (jnp.minimum(i // cbb, 1), 0, 0)) if cb_rows else (lambda i: (1, 0, 0))
    dest3 = dest.reshape(TOP_K, nb, tb).transpose(1, 0, 2).reshape(nb, 1, TOP_K * tb)
    return pl.pallas_call(
        _combine_kernel,
        out_shape=jax.ShapeDtypeStruct((t, d), F32),
        grid=(nb,),
        in_specs=[
            pl.BlockSpec((1, 1, tb * TOP_K), lambda i: (0, 0, 0), memory_space=pltpu.SMEM),
            pl.BlockSpec((1, 1, tb * TOP_K), lambda i: (jnp.minimum(i + 1, nb - 1), 0, 0),
                         memory_space=pltpu.SMEM),
            pl.BlockSpec(memory_space=pl.ANY),
            pl.BlockSpec((ROUTE_ROWS, tb), lambda i: (0, i)),
            pl.BlockSpec((tb, d), lambda i: (i, 0)),
            pl.BlockSpec((1, 6, d), sel),
            pl.BlockSpec((4, d), lambda i: (0, 0)),
        ],
        out_specs=pl.BlockSpec((tb, d), lambda i: (i, 0)),
        scratch_shapes=[pltpu.VMEM((2, TOP_K, tb * SEGS, LANES), F32), pltpu.SemaphoreType.DMA((2,))],
        compiler_params=_cparams(("arbitrary",), row_dma=True),
        name="moe_combine",
    )(dest3, dest3, y_sorted, meta, x, mod, gn)


INVERT_BLOCK = 2048


def _invert_kernel(dest_ref, zeros_hbm, src_ref):
    k, j = pl.program_id(0), pl.program_id(1)
    n = dest_ref.shape[2]

    @pl.when((k == 0) & (j == 0))
    def _():
        pltpu.sync_copy(zeros_hbm, src_ref)

    tok0 = j * n

    def put(t, carry):
        src_ref[dest_ref[0, 0, t]] = tok0 + t
        return carry

    lax.fori_loop(0, n, put, 0, unroll=32)


def _invert_call(dest, n_rows):
    t = dest.shape[1]
    blk = max(b for b in range(LANES, INVERT_BLOCK + 1, LANES) if t % b == 0)
    nb = t // blk
    return pl.pallas_call(
        _invert_kernel,
        out_shape=jax.ShapeDtypeStruct((n_rows,), jnp.int32),
        grid=(TOP_K, nb),
        in_specs=[pl.BlockSpec((1, 1, blk), lambda k, j: (k * nb + j, 0, 0), memory_space=pltpu.SMEM),
                  pl.BlockSpec(memory_space=pl.ANY)],
        out_specs=pl.BlockSpec(memory_space=pltpu.SMEM),
        compiler_params=_cparams(("arbitrary", "arbitrary")),
        name="moe_invert",
    )(dest.reshape(TOP_K * nb, 1, blk), jnp.zeros((n_rows,), jnp.int32))


def _moe(h2, logits, x, mod, gn, layer, w_gu, b_gu, w_dn, b_dn, cb_rows):
    t = logits.shape[0]
    tr = MOE_ROWS
    meta, counts = _router_call(logits)
    e_idx = meta[:TOP_K].astype(jnp.int32)
    rank = meta[TOP_K:2 * TOP_K].astype(jnp.int32)
    counts = counts.astype(jnp.int32)
    padded = (counts + tr - 1) // tr * tr
    p_end = jnp.cumsum(padded)
    p_start = p_end - padded
    dest = rank
    for e in range(N_EXPERTS):
        dest = dest + jnp.where(e_idx == e, p_start[e], 0)
    a = t * TOP_K
    n_blk = (a + N_EXPERTS * (tr - 1) + tr - 1) // tr
    n_blk += n_blk % 2
    src_tok = _invert_call(dest, n_blk * tr)
    blk_start = jnp.arange(n_blk, dtype=jnp.int32) * tr
    blk_e = jnp.sum((p_end[None, :] <= blk_start[:, None]).astype(jnp.int32), axis=1)
    blk_e = jnp.minimum(blk_e, N_EXPERTS - 1)
    n_used = (p_end[-1] // tr).astype(jnp.int32).reshape(1)
    ys = _expert_call(blk_e, n_used, src_tok, h2, layer, w_gu, b_gu, w_dn, b_dn)
    return _combine_call(dest, ys, meta, x, mod, gn, cb_rows)


def _hgrn_in_kernel(x_ref, mod_ref, gn_ref, w_ref, lb_ref, q_ref, kf_ref, lf_ref, kb_ref, lbw_ref,
                    i_ref, gs_ref):
    x = x_ref[...]
    h = _rms(x, gn_ref[...]) * (1.0 + mod_ref[0, 1:2, :]) + mod_ref[0, 0:1, :]
    hb = h.astype(BF16)
    f = D_MODEL

    def proj(sec):
        return jnp.dot(hb, w_ref[:, sec * f:(sec + 1) * f], preferred_element_type=F32)

    z = proj(0)
    q_ref[...] = _silu(z).astype(q_ref.dtype)
    for sec, k_ref, l_ref in ((1, kf_ref, lf_ref), (2, kb_ref, lbw_ref)):
        z = proj(sec)
        lb = lb_ref[sec - 1:sec, :]
        sg = _sigmoid(z)
        l_ref[...] = jnp.log(lb + (1.0 - lb) * sg)
        k_ref[...] = ((1.0 - lb) * (1.0 - sg)).astype(k_ref.dtype)
    i_ref[...] = proj(3).astype(i_ref.dtype)
    z = proj(4)
    gs_ref[...] = _silu(z).astype(gs_ref.dtype)


def _hgrn_in_call(xs, mod, gn, w, lb, cb):
    t, d = xs.shape
    tm = ROW_BLOCK
    sel = lambda i: (jnp.minimum(i // cb, 1), 0, 0)
    row = pl.BlockSpec((tm, d), lambda i: (i, 0))
    dts = (BF16, BF16, F32, BF16, F32, BF16, BF16)
    return pl.pallas_call(
        _hgrn_in_kernel,
        out_shape=tuple(jax.ShapeDtypeStruct((t, d), dt) for dt in dts),
        grid=(t // tm,),
        in_specs=[row, pl.BlockSpec((1, 6, d), sel), pl.BlockSpec((1, d), lambda i: (0, 0)),
                  pl.BlockSpec(w.shape, lambda i: (0, 0)), pl.BlockSpec((2, d), lambda i: (0, 0))],
        out_specs=tuple(row for _ in dts),
        compiler_params=_cparams(("parallel",)),
        name="hgrn_in_proj",
    )(xs, mod, gn, w, lb)


def _split3(x):
    hi = x.astype(BF16)
    r1 = x - hi.astype(F32)
    mid = r1.astype(BF16)
    lo = (r1 - mid.astype(F32)).astype(BF16)
    return hi, mid, lo


def _scan_kernel(q_ref, k_ref, g_ref, v_ref, o_ref, st_ref, gcum_ref, *, reverse):
    j = pl.program_id(0)
    ch, sub = HGRN_CHUNK, HGRN_SUB
    ns = ch // sub
    dk = HGRN_DK

    @pl.when(j == 0)
    def _():
        st_ref[...] = jnp.zeros_like(st_ref)

    rr = lax.broadcasted_iota(jnp.int32, (ch, ch), 0)
    cc = lax.broadcasted_iota(jnp.int32, (ch, ch), 1)
    causal = (cc >= rr) if reverse else (cc <= rr)
    tri = jnp.where(causal, 1.0, 0.0).astype(BF16)
    hi, mid, lo = _split3(g_ref[...])
    gcum_ref[...] = LOG2E * (jnp.dot(tri, hi, preferred_element_type=F32)
                             + jnp.dot(tri, mid, preferred_element_type=F32)
                             + jnp.dot(tri, lo, preferred_element_type=F32))
    key_blocks = list(range(1, ns)) if reverse else list(range(ns - 1))
    nkb = len(key_blocks)
    kbd_mask = (lax.broadcasted_iota(jnp.int32, (ch, nkb * dk), 1) // dk + key_blocks[0]
                == lax.broadcasted_iota(jnp.int32, (ch, nkb * dk), 0) // sub)
    kdg_mask = (lax.broadcasted_iota(jnp.int32, (ch, sub * dk), 1) // dk
                == lax.broadcasted_iota(jnp.int32, (ch, sub * dk), 0) % sub)
    end_row = 0 if reverse else ch - 1
    edge = 0 if reverse else sub - 1
    nt = (((1,), (1,)), ((), ()))
    tn = (((0,), (0,)), ((), ()))

    grp = HGRN_GROUP
    rr_g = lax.broadcasted_iota(jnp.int32, (ch, grp * ch), 0)
    cc_g = lax.broadcasted_iota(jnp.int32, (ch, grp * ch), 1) % ch
    diag_mask_g = ((rr_g // sub) == (cc_g // sub)) & ((cc_g >= rr_g) if reverse else (cc_g <= rr_g))

    def place(x, i):
        w = x.shape[1]
        parts = []
        if i:
            parts.append(jnp.zeros((x.shape[0], i * w), x.dtype))
        parts.append(x)
        if grp - 1 - i:
            parts.append(jnp.zeros((x.shape[0], (grp - 1 - i) * w), x.dtype))
        return jnp.concatenate(parts, axis=1)

    for g0 in range(0, HGRN_HEADS, grp):
        qgs, q_cats, p_cats, kbd_rows, kdg_rows, st_rows, v_rows = [], [], [], [], [], [], []
        for i in range(grp):
            h = g0 + i
            hs = slice(h * dk, (h + 1) * dk)
            q = q_ref[:, hs].astype(F32)
            kb = k_ref[:, hs]
            k = kb.astype(F32)
            v = v_ref[:, hs]
            gc = gcum_ref[:, hs]
            st = st_ref[h]
            g_end = gc[end_row:end_row + 1, :]
            qgs.append((q * jnp.exp2(gc)).astype(BF16))
            st_rows.append(place(st.astype(BF16), i))
            v_rows.append(place(v, i))
            k_end = (k * jnp.exp2(g_end - gc)).astype(BF16)
            st_ref[h] = st * jnp.exp2(g_end) + lax.dot_general(v, k_end, tn, preferred_element_type=F32)
            gc4 = gc.reshape(ns, sub, dk)
            g_edge = jnp.broadcast_to(gc4[:, edge:edge + 1, :], (ns, sub, dk)).reshape(ch, dk)
            k_rel = (k * jnp.exp2(g_edge - gc)).astype(BF16)
            k_bd = jnp.where(kbd_mask, jnp.concatenate([k_rel] * nkb, axis=1), jnp.zeros((), BF16))
            kbd_rows.append(place(k_bd, i))
            q_parts = []
            for jb in key_blocks:
                row = jb * sub + edge
                rows = slice(0, jb * sub) if reverse else slice((jb + 1) * sub, ch)
                part = q[rows] * jnp.exp2(gc[rows] - gc[row:row + 1, :])
                pad = jnp.zeros((ch - part.shape[0], dk), F32)
                full = jnp.concatenate([part, pad] if reverse else [pad, part], axis=0)
                q_parts.append(full.astype(BF16))
            q_cats.append(jnp.concatenate(q_parts, axis=1))
            p_parts = []
            for s in range(sub):
                g_s = jnp.broadcast_to(gc4[:, s:s + 1, :], (ns, sub, dk)).reshape(ch, dk)
                p_parts.append((q * jnp.exp2(jnp.minimum(gc - g_s, 0.0))).astype(BF16))
            p_cats.append(jnp.concatenate(p_parts, axis=1))
            k_dg = jnp.where(kdg_mask, jnp.concatenate([kb] * sub, axis=1), jnp.zeros((), BF16))
            kdg_rows.append(place(k_dg, i))
        s_off = lax.dot_general(jnp.concatenate(q_cats, axis=1), jnp.concatenate(kbd_rows, axis=0), nt,
                                preferred_element_type=F32)
        s_diag = lax.dot_general(jnp.concatenate(p_cats, axis=1), jnp.concatenate(kdg_rows, axis=0), nt,
                                 preferred_element_type=F32)
        a = s_off + jnp.where(diag_mask_g, s_diag, 0.0)
        o = lax.dot_general(jnp.concatenate(qgs, axis=1), jnp.concatenate(st_rows, axis=0), nt,
                            preferred_element_type=F32)
        o = o + jnp.dot(a.astype(BF16), jnp.concatenate(v_rows, axis=0), preferred_element_type=F32)
        o_ref[:, g0 * dk:(g0 + grp) * dk] = o


def _scan_call(q, k, lg, v, c_len, reverse):
    t, d = q.shape
    ch = HGRN_CHUNK
    n_ch = t // ch
    cc = c_len // ch
    if reverse:
        idx = lambda j: (jnp.where(j < cc, cc - 1 - j, n_ch - 1 - (j - cc)), 0)
    else:
        idx = lambda j: (j, 0)
    blk = pl.BlockSpec((ch, d), idx)
    return pl.pallas_call(
        functools.partial(_scan_kernel, reverse=reverse),
        out_shape=jax.ShapeDtypeStruct((t, d), F32),
        grid=(n_ch,),
        in_specs=[blk, blk, blk, blk],
        out_specs=blk,
        scratch_shapes=[pltpu.VMEM((HGRN_HEADS, HGRN_DK, HGRN_DK), F32), pltpu.VMEM((ch, d), F32)],
        compiler_params=_cparams(("arbitrary",)),
        name="hgrn_scan_bwd" if reverse else "hgrn_scan_fwd",
    )(q, k, lg, v)


def _rope_tables(l, c_len):
    n = HEAD_DIM // 4
    inv_freq = ROPE_BASE ** (-jnp.arange(n, dtype=F32) / n)
    rows = l // GRID_W
    ang_row = jnp.arange(rows, dtype=F32)[:, None] * inv_freq[None, :]
    ang_col = jnp.arange(GRID_W, dtype=F32)[:, None] * inv_freq[None, :]
    cr, sr, cc, sc = jnp.cos(ang_row), jnp.sin(ang_row), jnp.cos(ang_col), jnp.sin(ang_col)
    zr, zc = jnp.zeros_like(cr), jnp.zeros_like(cc)
    cos_r = jnp.concatenate([cr, cr, zr, zr] * 2, axis=1)
    sin_r = jnp.concatenate([-sr, sr, zr, zr] * 2, axis=1)
    cos_c = jnp.concatenate([zc, zc, cc, cc] * 2, axis=1)
    sin_c = jnp.concatenate([zc, zc, -sc, sc] * 2, axis=1)
    cos = (cos_r[:, None, :] + cos_c[None, :, :]).reshape(l, LANES)
    sin = (sin_r[:, None, :] + sin_c[None, :, :]).reshape(l, LANES)
    cos = jnp.concatenate([jnp.ones((c_len, LANES), F32), cos], axis=0)
    sin = jnp.concatenate([jnp.zeros((c_len, LANES), F32), sin], axis=0)
    return cos, sin


def kernel(x, c, ctx, c_ctx, ada_w, ada_b, norm_g, attn_w_qkv, attn_b_qkv, attn_sink, attn_w_o,
           hgrn_w_in, hgrn_lb, hgrn_norm_g, hgrn_w_o, router_w, router_b, moe_w_gu, moe_b_gu,
           moe_w_dn, moe_b_dn):
    b, l, d = x.shape
    c_len = ctx.shape[1]
    depth = ada_w.shape[0]
    assert b == 1 and d == D_MODEL and depth == 2
    assert c_len % ROW_BLOCK == 0 and l % ROW_BLOCK == 0
    cb = c_len // ROW_BLOCK

    cs = jnp.zeros((8, d), F32).at[0].set(c_ctx).at[1].set(c[0])
    mods = _ada_call(cs, ada_w, ada_b)
    mods = mods[:, :2].reshape(depth, 2, 6, d)

    cos, sin = _rope_tables(l, c_len)

    q, k4, v4 = _qkv_call(ctx[0], x[0], mods[0], norm_g[0, 0:1], attn_w_qkv[0].astype(BF16),
                          attn_b_qkv[0].reshape(1, -1), cos, sin, cb)
    o = _attn_call(attn_sink[0], q, k4, v4, c_len)
    xs, h2, logits = _out_call("attn", (o,), (ctx[0], x[0]), mods[0], norm_g[0], attn_w_o[0].astype(BF16),
                               router_w[0], router_b[0].reshape(1, -1), None, cb, 0)
    xs = _moe(h2, logits, xs, mods[0], norm_g[0], 0, moe_w_gu, moe_b_gu, moe_w_dn, moe_b_dn, c_len)

    lb_soft = jax.nn.softmax(hgrn_lb.astype(F32), axis=0)
    lb = jnp.cumsum(lb_soft, axis=0)[1] - lb_soft[0]
    qh, kf, lf, kb, lbw, iv, gs = _hgrn_in_call(xs, mods[1], norm_g[1, 0:1], hgrn_w_in[0].astype(BF16),
                                                lb, cb)
    o_f = _scan_call(qh, kf, lf, iv, c_len, reverse=False)
    o_b = _scan_call(qh, kb, lbw, iv, c_len, reverse=True)
    x_lat, h2, logits = _out_call("hgrn", (o_f, o_b, gs), xs, mods[1], norm_g[1],
                                  hgrn_w_o[0].astype(BF16), router_w[1], router_b[1].reshape(1, -1),
                                  hgrn_norm_g[0].reshape(1, -1), cb, cb)
    out = _moe(h2, logits, x_lat, mods[1], norm_g[1], 1, moe_w_gu, moe_b_gu, moe_w_dn, moe_b_dn, 0)
    return out[None]
```
